```python
import jax
import jax.numpy as jnp
from jax import lax
import numpy as np


D_MODEL = 1024
BATCH = 2
SEQ = 8192
DEPTH = 2

GRID_W = 64
CTX_LEN = 256
EPS = 1e-6

A_HEADS = 4
A_HEAD_DIM = 128
A_WIDTH = A_HEADS * A_HEAD_DIM
A_IN = 5 * A_WIDTH
A_CHUNK = 64
POOL_WINDOWS = (2, 4, 8, 16)
B_GROUP = 128
B_WIDTH = B_GROUP * len(POOL_WINDOWS)
AB_IN = A_IN + B_WIDTH
AB_MIX = A_WIDTH + B_WIDTH
C_HEADS = 8
C_HEAD_DIM = 64
C_WIDTH = C_HEADS * C_HEAD_DIM
NA_ROWS_MAX = 8
NA_COLS = 16
D_HEADS = 8
MLA_Q_RANK = 256
MLA_KV_RANK = 128
MLA_NOPE = 64
MLA_ROPE = 32
MLA_V = 64
D_WIDTH = D_HEADS * MLA_V
CD_IN = 3 * C_WIDTH + MLA_Q_RANK + MLA_KV_RANK + MLA_ROPE
CD_MIX = C_WIDTH + D_WIDTH
ROPE_THETA = 10000.0
ATTN_BLOCK = 128
MOE_GROUPS = 4
MOE_EXPERTS_PER_GROUP = 8
MOE_TOPK = 2
MOE_HIDDEN = 256

N_AB_LAYERS = (DEPTH + 1) // 2
N_CD_LAYERS = DEPTH // 2

kernel_name = 'hybrid_hgrn2_pool_natten_mla_hmoe_diffusion'


def rmsnorm(x, g):
    xf = x.astype(jnp.float32)
    y = xf * lax.rsqrt(jnp.mean(xf * xf, axis=-1, keepdims=True) + EPS)
    return (y * g.astype(jnp.float32)).astype(x.dtype)


def ada_modulate(x, g, shift, scale):
    return rmsnorm(x, g) * (1.0 + scale) + shift


def _heads(a, n_heads):
    b, n, _ = a.shape
    return a.reshape(b, n, n_heads, -1).transpose(0, 2, 1, 3)


def _merge(a):
    b, h, n, d = a.shape
    return a.transpose(0, 2, 1, 3).reshape(b, n, h * d)


def axial_rope(x, pos_r, pos_c):
    half = x.shape[-1] // 2
    n_freq = half // 2
    inv = ROPE_THETA ** (-jnp.arange(n_freq, dtype=jnp.float32) / n_freq)

    def rot(xa, pos):
        ang = pos[:, None] * inv[None, :]
        cos, sin = jnp.cos(ang), jnp.sin(ang)
        x1, x2 = xa[..., :n_freq], xa[..., n_freq:]
        return jnp.concatenate([x1 * cos - x2 * sin, x1 * sin + x2 * cos], axis=-1)

    xf = x.astype(jnp.float32)
    return jnp.concatenate([rot(xf[..., :half], pos_r), rot(xf[..., half:], pos_c)], axis=-1).astype(x.dtype)


def gla_chunk_scan(q, k, v, logf, s0):
    b_, h, n, _ = q.shape
    nc = n // A_CHUNK

    def chunks(a):
        return jnp.moveaxis(a.reshape(b_, h, nc, A_CHUNK, a.shape[-1]), 2, 0)

    incl = jnp.tril(jnp.ones((A_CHUNK, A_CHUNK), dtype=bool))

    def step(s, inp):
        qc, kc, vc, gc = inp
        bcum = jnp.cumsum(gc, axis=2)
        o_inter = jnp.einsum('bhtd,bhde->bhte', qc * jnp.exp(bcum), s)
        diff = bcum[:, :, :, None, :] - bcum[:, :, None, :, :]
        decay = jnp.exp(jnp.where(incl[:, :, None], diff, -jnp.inf))
        att = jnp.einsum('bhtd,bhsd,bhtsd->bhts', qc, kc, decay)
        o = o_inter + jnp.einsum('bhts,bhse->bhte', att, vc)
        b_last = bcum[:, :, -1:, :]
        s_new = jnp.exp(b_last[:, :, 0, :, None]) * s + jnp.einsum('bhsd,bhse->bhde', kc * jnp.exp(b_last - bcum), vc)
        return s_new, o

    s_fin, o = lax.scan(step, s0, (chunks(q), chunks(k), chunks(v), chunks(logf)))
    o = jnp.moveaxis(o, 0, 2).reshape(b_, h, n, v.shape[-1])
    return o, s_fin


def hgrn2_inputs(u, lb):
    uf = u.astype(jnp.float32)
    q, f_fwd, f_bwd, i, g = jnp.split(uf, 5, axis=-1)
    dirs = []
    for d, fz in enumerate((f_fwd, f_bwd)):
        f = lb[d] + (1.0 - lb[d]) * jax.nn.sigmoid(fz)
        dirs.append((_heads(1.0 - f, A_HEADS), _heads(jnp.log(f), A_HEADS)))
    return _heads(jax.nn.silu(q), A_HEADS), _heads(i, A_HEADS), dirs, g


def hgrn2_bidir(u_ctx, u_lat, lb, onorm_g):
    q_c, i_c, dirs_c, g_c = hgrn2_inputs(u_ctx, lb)
    q_l, i_l, dirs_l, g_l = hgrn2_inputs(u_lat, lb)
    (kf_c, gf_c), (kb_c, gb_c) = dirs_c
    (kf_l, gf_l), (kb_l, gb_l) = dirs_l
    s0 = jnp.zeros(q_c.shape[:2] + (A_HEAD_DIM, A_HEAD_DIM), jnp.float32)

    def flip(a):
        return jnp.flip(a, axis=2)

    o_cf, s_cf = gla_chunk_scan(q_c, kf_c, i_c, gf_c, s0)
    o_lf, _ = gla_chunk_scan(q_l, kf_l, i_l, gf_l, s_cf)
    o_cb, s_cb = gla_chunk_scan(flip(q_c), flip(kb_c), flip(i_c), flip(gb_c), s0)
    o_lb, _ = gla_chunk_scan(flip(q_l), flip(kb_l), flip(i_l), flip(gb_l), s_cb)

    def readout(o, g, dtype):
        return (_merge(rmsnorm(o, onorm_g)) * jax.nn.silu(g)).astype(dtype)

    return readout(o_cf + flip(o_cb), g_c, u_ctx.dtype), readout(o_lf + flip(o_lb), g_l, u_lat.dtype)


def multiscale_pool(u, w, scale):
    b_, n, _ = u.shape
    uf = u.astype(jnp.float32)
    cs = jnp.concatenate([jnp.zeros((b_, 1, B_WIDTH), jnp.float32), jnp.cumsum(uf, axis=1)], axis=1)
    t = jnp.arange(n)
    outs = []
    for gi, win in enumerate(POOL_WINDOWS):
        sl = slice(gi * B_GROUP, (gi + 1) * B_GROUP)
        lo = jnp.clip(t - win // 2, 0, n)
        hi = jnp.clip(t + win - win // 2, 0, n)
        mean = (cs[:, hi, sl] - cs[:, lo, sl]) / (hi - lo).astype(jnp.float32)[None, :, None]
        outs.append(jnp.einsum('bnc,cd->bnd', mean - uf[:, :, sl], w[gi].astype(jnp.float32)))
    return (jnp.concatenate(outs, axis=-1) * scale.astype(jnp.float32)).astype(u.dtype)


def ab_mixer(h_ctx, h_lat, need_ctx, w_in, w_out, lb, onorm_g, pool_w, pool_scale):
    u_ctx = h_ctx @ w_in
    u_lat = h_lat @ w_in
    a_ctx, a_lat = hgrn2_bidir(u_ctx[..., :A_IN], u_lat[..., :A_IN], lb, onorm_g)
    o_lat = jnp.concatenate([a_lat, multiscale_pool(u_lat[..., A_IN:], pool_w, pool_scale)], axis=-1) @ w_out
    o_ctx = None
    if need_ctx:
        o_ctx = jnp.concatenate([a_ctx, multiscale_pool(u_ctx[..., A_IN:], pool_w, pool_scale)], axis=-1) @ w_out
    return o_ctx, o_lat


def dense_attention(q, k, v, scale):
    s = jnp.einsum('bhqd,bhkd->bhqk', q, k).astype(jnp.float32) * scale
    p = jax.nn.softmax(s, axis=-1).astype(v.dtype)
    return jnp.einsum('bhqk,bhkd->bhqd', p, v)


def neighbourhood_attention(q_lat, k_lat, v_lat, k_ctx, v_ctx, rpb):
    b_, h, n, dh = q_lat.shape
    rows = n // GRID_W
    wr = min(NA_ROWS_MAX, rows)
    scale = dh ** -0.5
    kg = k_lat.reshape(b_, h, rows, GRID_W, dh)
    vg = v_lat.reshape(b_, h, rows, GRID_W, dh)
    qg = jnp.moveaxis(q_lat.reshape(b_, h, rows, GRID_W, dh), 2, 0)
    col = jnp.arange(GRID_W)
    col_start = jnp.clip(col - NA_COLS // 2, 0, GRID_W - NA_COLS)
    col_idx = col_start[:, None] + jnp.arange(NA_COLS)[None, :]
    dc = col_idx - col[:, None] + (NA_COLS - 1)
    rpb_f = rpb.astype(jnp.float32)
    n_loc = wr * NA_COLS

    def row_block(args):
        r, q_row = args
        rs = jnp.clip(r - wr // 2, 0, rows - wr)
        k_win = lax.dynamic_slice_in_dim(kg, rs, wr, axis=2)[:, :, :, col_idx]
        v_win = lax.dynamic_slice_in_dim(vg, rs, wr, axis=2)[:, :, :, col_idx]
        dr = rs + jnp.arange(wr) - r + (NA_ROWS_MAX - 1)
        bias = rpb_f[:, dr[None, :, None], dc[:, None, :]]
        s_loc = jnp.einsum('bhqd,bhrqjd->bhqrj', q_row, k_win).astype(jnp.float32) * scale + bias
        s_ctx = jnp.einsum('bhqd,bhkd->bhqk', q_row, k_ctx).astype(jnp.float32) * scale
        s = jnp.concatenate([s_loc.reshape(b_, h, GRID_W, n_loc), s_ctx], axis=-1)
        p = jax.nn.softmax(s, axis=-1).astype(v_lat.dtype)
        p_loc = p[..., :n_loc].reshape(b_, h, GRID_W, wr, NA_COLS)
        return (jnp.einsum('bhqrj,bhrqjd->bhqd', p_loc, v_win)
                + jnp.einsum('bhqk,bhkd->bhqd', p[..., n_loc:], v_ctx))

    o = lax.map(row_block, (jnp.arange(rows), qg))
    return jnp.moveaxis(o, 0, 2).reshape(b_, h, n, dh)


def mla_q(cq, q_norm_g, w_uq, pos):
    q = _heads(rmsnorm(cq, q_norm_g) @ w_uq, D_HEADS)
    q_nope, q_rope = q[..., :MLA_NOPE], q[..., MLA_NOPE:]
    if pos is not None:
        q_rope = axial_rope(q_rope, pos[0], pos[1])
    return jnp.concatenate([q_nope, q_rope], axis=-1)


def mla_kv(ckv, k_rope, kv_norm_g, w_ukv, pos):
    kv = _heads(rmsnorm(ckv, kv_norm_g) @ w_ukv, D_HEADS)
    k_nope, v = kv[..., :MLA_NOPE], kv[..., MLA_NOPE:]
    kr = k_rope[:, None]
    if pos is not None:
        kr = axial_rope(kr, pos[0], pos[1])
    kr = jnp.broadcast_to(kr, k_nope.shape[:3] + (MLA_ROPE,))
    return jnp.concatenate([k_nope, kr], axis=-1), v


def blocked_attention(q, k_all, v_all, scale):
    b_, h, n, dq = q.shape
    nb = n // ATTN_BLOCK
    qb = jnp.moveaxis(q.reshape(b_, h, nb, ATTN_BLOCK, dq), 2, 0)
    o = lax.map(lambda q_blk: dense_attention(q_blk, k_all, v_all, scale), qb)
    return jnp.moveaxis(o, 0, 2).reshape(b_, h, n, v_all.shape[-1])


def split_cd(u):
    b_, n, _ = u.shape
    qkv = u[..., :3 * C_WIDTH].reshape(b_, n, 3, C_HEADS, C_HEAD_DIM).transpose(2, 0, 3, 1, 4)
    o = 3 * C_WIDTH
    cq = u[..., o:o + MLA_Q_RANK]
    ckv = u[..., o + MLA_Q_RANK:o + MLA_Q_RANK + MLA_KV_RANK]
    kr = u[..., o + MLA_Q_RANK + MLA_KV_RANK:]
    return qkv[0], qkv[1], qkv[2], cq, ckv, kr


def cd_mixer(h_ctx, h_lat, need_ctx, w_in, w_out, rpb, q_norm_g, w_uq, kv_norm_g, w_ukv):
    n_lat = h_lat.shape[1]
    t = jnp.arange(n_lat)
    pos = ((t // GRID_W).astype(jnp.float32), (t % GRID_W).astype(jnp.float32))
    nq_c, nk_c, nv_c, cq_c, ckv_c, kr_c = split_cd(h_ctx @ w_in)
    nq_l, nk_l, nv_l, cq_l, ckv_l, kr_l = split_cd(h_lat @ w_in)
    d_scale = (MLA_NOPE + MLA_ROPE) ** -0.5
    c_lat = neighbourhood_attention(nq_l, nk_l, nv_l, nk_c, nv_c, rpb)
    dk_c, dv_c = mla_kv(ckv_c, kr_c, kv_norm_g, w_ukv, None)
    dk_l, dv_l = mla_kv(ckv_l, kr_l, kv_norm_g, w_ukv, pos)
    dq_l = mla_q(cq_l, q_norm_g, w_uq, pos)
    d_lat = blocked_attention(dq_l, jnp.concatenate([dk_c, dk_l], axis=2), jnp.concatenate([dv_c, dv_l], axis=2), d_scale)
    o_lat = jnp.concatenate([_merge(c_lat), _merge(d_lat)], axis=-1) @ w_out
    o_ctx = None
    if need_ctx:
        c_c = dense_attention(nq_c, nk_c, nv_c, C_HEAD_DIM ** -0.5)
        d_c = dense_attention(mla_q(cq_c, q_norm_g, w_uq, None), dk_c, dv_c, d_scale)
        o_ctx = jnp.concatenate([_merge(c_c), _merge(d_c)], axis=-1) @ w_out
    return o_ctx, o_lat


def hier_moe(h, w_rg, b_rg, w_re, b_re, w_gate, w_up, w_down):
    b_, n, d = h.shape
    x = h.reshape(-1, d)
    tok = x.shape[0]
    g_prob = jax.nn.softmax((x @ w_rg + b_rg).astype(jnp.float32), axis=-1)
    g_p, g_idx = lax.top_k(g_prob, 1)
    e_logits = (x @ w_re + b_re).astype(jnp.float32).reshape(tok, MOE_GROUPS, MOE_EXPERTS_PER_GROUP)
    e_logits = jnp.take_along_axis(e_logits, g_idx[:, :, None], axis=1)[:, 0]
    e_p, e_idx = lax.top_k(jax.nn.softmax(e_logits, axis=-1), MOE_TOPK)
    e_p = e_p / jnp.sum(e_p, axis=-1, keepdims=True)
    e_w = jnp.sum(jax.nn.one_hot(e_idx, MOE_EXPERTS_PER_GROUP, dtype=jnp.float32) * e_p[..., None], axis=1)
    comb = (g_p[:, :, None] * jax.nn.one_hot(g_idx[:, 0], MOE_GROUPS, dtype=jnp.float32)[:, :, None]
            * e_w[:, None, :]).astype(h.dtype)
    y = jnp.zeros_like(x)
    for g in range(MOE_GROUPS):
        a = jax.nn.silu(jnp.einsum('td,edf->tef', x, w_gate[g])) * jnp.einsum('td,edf->tef', x, w_up[g])
        y = y + jnp.einsum('tef,efd->td', a * comb[:, g, :, None], w_down[g])
    return y.reshape(b_, n, d)


def setup_inputs(seed: int = 0) -> dict:
    key = jax.random.key(seed)
    ks = jax.random.split(key, 32)
    counter = [0]

    def nrm(shape, scale):
        k = ks[counter[0]]
        counter[0] += 1
        return jax.random.normal(k, shape, jnp.float32) * scale

    def gain(shape):
        return 1.0 + nrm(shape, 0.02)

    G, E, F = MOE_GROUPS, MOE_EXPERTS_PER_GROUP, MOE_HIDDEN
    return {
        'x': nrm((BATCH, SEQ, D_MODEL), 1.0),
        'c': nrm((BATCH, D_MODEL), 1.0),
        'ctx': nrm((BATCH, CTX_LEN, D_MODEL), 1.0),
        'c_ctx': nrm((D_MODEL,), 1.0),
        'ada_w': nrm((DEPTH, D_MODEL, 6 * D_MODEL), 0.5 * D_MODEL ** -0.5),
        'ada_b': nrm((DEPTH, 6 * D_MODEL), 0.02),
        'norm1_g': gain((DEPTH, D_MODEL)),
        'norm2_g': gain((DEPTH, D_MODEL)),
        'ab_w_in': nrm((N_AB_LAYERS, D_MODEL, AB_IN), D_MODEL ** -0.5),
        'ab_w_out': nrm((N_AB_LAYERS, AB_MIX, D_MODEL), AB_MIX ** -0.5),
        'hgrn_lb_logits': nrm((2, N_AB_LAYERS + 1, A_WIDTH), 0.1),
        'hgrn_onorm_g': gain((N_AB_LAYERS, A_HEAD_DIM)),
        'pool_w': nrm((N_AB_LAYERS, len(POOL_WINDOWS), B_GROUP, B_GROUP), B_GROUP ** -0.5),
        'pool_scale': gain((N_AB_LAYERS, B_WIDTH)),
        'cd_w_in': nrm((N_CD_LAYERS, D_MODEL, CD_IN), D_MODEL ** -0.5),
        'cd_w_out': nrm((N_CD_LAYERS, CD_MIX, D_MODEL), CD_MIX ** -0.5),
        'na_rpb': nrm((N_CD_LAYERS, C_HEADS, 2 * NA_ROWS_MAX - 1, 2 * NA_COLS - 1), 0.1),
        'mla_q_norm_g': gain((N_CD_LAYERS, MLA_Q_RANK)),
        'mla_w_uq': nrm((N_CD_LAYERS, MLA_Q_RANK, D_HEADS * (MLA_NOPE + MLA_ROPE)), MLA_Q_RANK ** -0.5),
        'mla_kv_norm_g': gain((N_CD_LAYERS, MLA_KV_RANK)),
        'mla_w_ukv': nrm((N_CD_LAYERS, MLA_KV_RANK, D_HEADS * (MLA_NOPE + MLA_V)), MLA_KV_RANK ** -0.5),
        'moe_w_rg': nrm((DEPTH, D_MODEL, G), D_MODEL ** -0.5),
        'moe_b_rg': nrm((DEPTH, G), 0.01),
        'moe_w_re': nrm((DEPTH, D_MODEL, G * E), D_MODEL ** -0.5),
        'moe_b_re': nrm((DEPTH, G * E), 0.01),
        'moe_w_gate': nrm((DEPTH, G, E, D_MODEL, F), D_MODEL ** -0.5),
        'moe_w_up': nrm((DEPTH, G, E, D_MODEL, F), D_MODEL ** -0.5),
        'moe_w_down': nrm((DEPTH, G, E, F, D_MODEL), F ** -0.5),
        'final_norm_g': gain((D_MODEL,)),
    }


def reference(x, c, ctx, c_ctx, ada_w, ada_b, norm1_g, norm2_g, ab_w_in, ab_w_out, hgrn_lb_logits,
              hgrn_onorm_g, pool_w, pool_scale, cd_w_in, cd_w_out, na_rpb, mla_q_norm_g, mla_w_uq,
              mla_kv_norm_g, mla_w_ukv, moe_w_rg, moe_b_rg, moe_w_re, moe_b_re, moe_w_gate, moe_w_up,
              moe_w_down, final_norm_g):
    lb_all = jnp.cumsum(jax.nn.softmax(hgrn_lb_logits.astype(jnp.float32), axis=1), axis=1)
    xc = ctx
    for layer in range(DEPTH):
        last = layer == DEPTH - 1
        k = layer // 2
        m_lat = (jax.nn.silu(c) @ ada_w[layer] + ada_b[layer])[:, None, :]
        m_ctx = (jax.nn.silu(c_ctx) @ ada_w[layer] + ada_b[layer])[None, None, :]
        sh1, sc1, g1, sh2, sc2, g2 = jnp.split(m_lat, 6, axis=-1)
        sh1c, sc1c, g1c, sh2c, sc2c, g2c = jnp.split(m_ctx, 6, axis=-1)
        h_lat = ada_modulate(x, norm1_g[layer], sh1, sc1)
        h_ctx = ada_modulate(xc, norm1_g[layer], sh1c, sc1c)
        if layer % 2 == 0:
            o_ctx, o_lat = ab_mixer(h_ctx, h_lat, not last, ab_w_in[k], ab_w_out[k], lb_all[:, k],
                                    hgrn_onorm_g[k], pool_w[k], pool_scale[k])
        else:
            o_ctx, o_lat = cd_mixer(h_ctx, h_lat, not last, cd_w_in[k], cd_w_out[k], na_rpb[k],
                                    mla_q_norm_g[k], mla_w_uq[k], mla_kv_norm_g[k], mla_w_ukv[k])
        moe_p = (moe_w_rg[layer], moe_b_rg[layer], moe_w_re[layer], moe_b_re[layer],
                 moe_w_gate[layer], moe_w_up[layer], moe_w_down[layer])
        x = x + g1 * o_lat
        x = x + g2 * hier_moe(ada_modulate(x, norm2_g[layer], sh2, sc2), *moe_p)
        if not last:
            xc = xc + g1c * o_ctx
            xc = xc + g2c * hier_moe(ada_modulate(xc, norm2_g[layer], sh2c, sc2c), *moe_p)
    return rmsnorm(x, final_norm_g)
```

```python
import functools

import numpy as np
import jax
import jax.numpy as jnp
from jax import lax
from jax.experimental import pallas as pl
from jax.experimental.pallas import tpu as pltpu

F32 = jnp.float32
BF16 = jnp.bfloat16

EPS = 1e-6
NEG = -1e30

GRID_W = 64
A_HEADS = 4
A_HEAD_DIM = 128
A_WIDTH = A_HEADS * A_HEAD_DIM
POOL_WINDOWS = (2, 4, 8, 16)
B_GROUP = 128
B_WIDTH = B_GROUP * len(POOL_WINDOWS)
POOL_HALO = 16
C_HEADS = 8
C_HEAD_DIM = 64
C_WIDTH = C_HEADS * C_HEAD_DIM
NA_ROWS = 8
NA_COLS = 16
NA_QROWS = 4
D_HEADS = 8
MLA_Q_RANK = 256
MLA_KV_RANK = 128
MLA_NOPE = 64
MLA_ROPE = 32
MLA_V = 64
MLA_PAD = 128
ROPE_THETA = 10000.0
MOE_GROUPS = 4
MOE_EPG = 8
MOE_EXPERTS = MOE_GROUPS * MOE_EPG
MOE_HIDDEN = 256
LANES = 128
SLOT_BLOCK = 16
STEP_BLOCKS = 16
VMEM_LIMIT = 56 * 1024 * 1024

NT = (((1,), (1,)), ((), ()))
TN = (((0,), (0,)), ((), ()))


def _cparams(*sem):
    return pltpu.CompilerParams(dimension_semantics=sem, vmem_limit_bytes=VMEM_LIMIT)


def _sigmoid(x):
    return 1.0 / (1.0 + jnp.exp(-x))


def _silu(x):
    return x * _sigmoid(x)


def _dot(a, b):
    return jnp.dot(a, b, preferred_element_type=F32)


def _rmsnorm(x, g):
    return x * lax.rsqrt(jnp.mean(x * x, axis=-1, keepdims=True) + EPS) * g


def _ada_kernel(c_ref, w_ref, b_ref, o_ref):
    s = _silu(c_ref[...])
    o_ref[0] = jnp.dot(s, w_ref[0], precision=lax.Precision.HIGHEST, preferred_element_type=F32) + b_ref[0]


def _ada(cc, ada_w, ada_b):
    depth, d, n6 = ada_w.shape
    tn = n6 // 4
    return pl.pallas_call(
        _ada_kernel,
        grid=(depth, n6 // tn),
        in_specs=[
            pl.BlockSpec((8, d), lambda l, j: (0, 0)),
            pl.BlockSpec((1, d, tn), lambda l, j: (l, 0, j)),
            pl.BlockSpec((1, 1, tn), lambda l, j: (l, 0, j)),
        ],
        out_specs=pl.BlockSpec((1, 8, tn), lambda l, j: (l, 0, j)),
        out_shape=jax.ShapeDtypeStruct((depth, 8, n6), F32),
        compiler_params=_cparams("parallel", "parallel"),
        name="ada_mod",
    )(cc, ada_w, ada_b.reshape(depth, 1, n6))


def _in_kernel(x_ref, g_ref, m_ref, w_ref, *o_refs, splits):
    h = _rmsnorm(x_ref[0], g_ref[...]) * (1.0 + m_ref[0, 1:2, :]) + m_ref[0, 0:1, :]
    hb = h.astype(BF16)
    for o_ref, (a, b) in zip(o_refs, splits):
        o_ref[0] = _dot(hb, w_ref[:, a:b]).astype(o_ref.dtype)


def _in_proj(x, gain, mods, w, splits, dtypes, tm):
    b, n, d = x.shape
    tm = min(tm, n)
    outs = [jax.ShapeDtypeStruct((b, n, hi - lo), dt) for (lo, hi), dt in zip(splits, dtypes)]
    return pl.pallas_call(
        functools.partial(_in_kernel, splits=splits),
        grid=(b, n // tm),
        in_specs=[
            pl.BlockSpec((1, tm, d), lambda bi, i: (bi, i, 0)),
            pl.BlockSpec((1, d), lambda bi, i: (0, 0)),
            pl.BlockSpec((1, 6, d), lambda bi, i: (bi, 0, 0)),
            pl.BlockSpec(w.shape, lambda bi, i: (0, 0)),
        ],
        out_specs=[pl.BlockSpec((1, tm, hi - lo), lambda bi, i: (bi, i, 0)) for lo, hi in splits],
        out_shape=outs,
        compiler_params=_cparams("parallel", "parallel"),
        name="in_proj",
    )(x, gain.reshape(1, d), mods, w)


HG_SUB = 64


def _hgrn_direction(q_raw, fz, v, lb, st_ref, d, o_ref, reverse):
    rows = q_raw.shape[0]
    c = HG_SUB
    f = lb + (1.0 - lb) * _sigmoid(fz)
    k = 1.0 - f
    g = jnp.log(f)
    q = _silu(q_raw)
    r_i = lax.broadcasted_iota(jnp.int32, (c, c), 0)
    c_i = lax.broadcasted_iota(jnp.int32, (c, c), 1)
    keep = (c_i >= r_i) if reverse else (c_i <= r_i)
    tri = jnp.where(keep, 1.0, 0.0).astype(BF16)
    order = range(rows // c - 1, -1, -1) if reverse else range(rows // c)
    for ci in order:
        sl = slice(ci * c, (ci + 1) * c)
        gc = g[sl]
        g_hi = gc.astype(BF16)
        g_lo = (gc - g_hi.astype(F32)).astype(BF16)
        bc = _dot(tri, g_hi) + _dot(tri, g_lo)
        ref = bc[c // 2:c // 2 + 1]
        tot = bc[0:1] if reverse else bc[c - 1:c]
        qt = q[sl] * jnp.exp(bc - ref)
        kt = k[sl] * jnp.exp(ref - bc)
        qd = (qt * jnp.exp(ref)).astype(BF16)
        kd = (kt * jnp.exp(tot - ref)).astype(BF16)
        qt = qt.astype(BF16)
        kt = kt.astype(BF16)
        vb = v[sl].astype(BF16)
        dec = jnp.exp(tot)
        for h in range(A_HEADS):
            hs = slice(h * A_HEAD_DIM, (h + 1) * A_HEAD_DIM)
            att = lax.dot_general(qt[:, hs], kt[:, hs], NT, preferred_element_type=F32)
            att = jnp.where(keep, att, 0.0).astype(BF16)
            st = st_ref[d, h]
            o = _dot(att, vb[:, hs]) + lax.dot_general(qd[:, hs], st.astype(BF16), NT, preferred_element_type=F32)
            o_ref[0, sl, hs] = o
            st_ref[d, h] = st * dec[:, hs] + lax.dot_general(vb[:, hs], kd[:, hs], TN, preferred_element_type=F32)


def _hgrn_kernel(qf_ref, ff_ref, vf_ref, qb_ref, fb_ref, vb_ref, lb_ref, s0_ref, of_ref, ob_ref, sfin_ref, st_ref):
    j = pl.program_id(1)

    @pl.when(j == 0)
    def _():
        st_ref[...] = s0_ref[0]

    _hgrn_direction(qf_ref[0], ff_ref[0], vf_ref[0], lb_ref[0:1], st_ref, 0, of_ref, False)
    _hgrn_direction(qb_ref[0], fb_ref[0], vb_ref[0], lb_ref[1:2], st_ref, 1, ob_ref, True)

    @pl.when(j == pl.num_programs(1) - 1)
    def _():
        sfin_ref[0] = st_ref[...]


def _hgrn_scan(u, lb, s0, rows):
    b, n, _ = u.shape
    rows = min(rows, n)
    nb = n // rows
    w = A_WIDTH

    def fwd(col):
        return pl.BlockSpec((1, rows, w), lambda bi, j: (bi, j, col))

    def bwd(col):
        return pl.BlockSpec((1, rows, w), lambda bi, j: (bi, nb - 1 - j, col))

    st_spec = pl.BlockSpec((1, 2, A_HEADS, A_HEAD_DIM, A_HEAD_DIM), lambda bi, j: (bi, 0, 0, 0, 0))
    return pl.pallas_call(
        _hgrn_kernel,
        grid=(b, nb),
        in_specs=[fwd(0), fwd(1), fwd(3), bwd(0), bwd(2), bwd(3), pl.BlockSpec((2, w), lambda bi, j: (0, 0)), st_spec],
        out_specs=[
            pl.BlockSpec((1, rows, w), lambda bi, j: (bi, j, 0)),
            pl.BlockSpec((1, rows, w), lambda bi, j: (bi, nb - 1 - j, 0)),
            st_spec,
        ],
        out_shape=[
            jax.ShapeDtypeStruct((b, n, w), F32),
            jax.ShapeDtypeStruct((b, n, w), F32),
            jax.ShapeDtypeStruct(s0.shape, F32),
        ],
        scratch_shapes=[pltpu.VMEM((2, A_HEADS, A_HEAD_DIM, A_HEAD_DIM), F32)],
        compiler_params=_cparams("parallel", "arbitrary"),
        name="hgrn_scan",
    )(u, u, u, u, u, u, lb, s0)


def _ab_out_kernel(of_ref, ob_ref, ug_ref, up_ref, pprev_ref, pnext_ref, x_ref, m_ref, on_ref, pw_ref, ps_ref,
                   wo_ref, o_ref, *, n):
    i = pl.program_id(1)
    tm = x_ref.shape[1]
    o = of_ref[0] + ob_ref[0]
    gate = _silu(ug_ref[0])
    parts = []
    for h in range(A_HEADS):
        hs = slice(h * A_HEAD_DIM, (h + 1) * A_HEAD_DIM)
        parts.append(_rmsnorm(o[:, hs], on_ref[...]) * gate[:, hs])
    main = up_ref[0]
    prev = jnp.where(i > 0, pprev_ref[0], 0.0)
    nxt = jnp.where(i < pl.num_programs(1) - 1, pnext_ref[0], 0.0)
    ext = jnp.concatenate([prev, main, nxt], axis=0)
    ext_rows = tm + 2 * POOL_HALO
    t = i * tm + lax.broadcasted_iota(jnp.int32, (tm, 1), 0)
    for gi, win in enumerate(POOL_WINDOWS):
        gs = slice(gi * B_GROUP, (gi + 1) * B_GROUP)
        acc = ext[:, gs]
        acc = acc + pltpu.roll(acc, 1, 0)
        half = 1
        while 2 * half < win:
            acc = pltpu.roll(acc, half, 0) + pltpu.roll(acc, ext_rows - half, 0)
            half *= 2
        cnt = jnp.minimum(t + (win - win // 2), n) - jnp.maximum(t - win // 2, 0)
        mean = acc[POOL_HALO:POOL_HALO + tm] / cnt.astype(F32)
        pooled = _dot((mean - main[:, gs]).astype(BF16), pw_ref[gi])
        parts.append(pooled * ps_ref[:, gs])
    mix = jnp.concatenate(parts, axis=-1).astype(BF16)
    o_ref[0] = x_ref[0] + m_ref[0, 2:3, :] * _dot(mix, wo_ref[...])


def _ab_out(o_f, o_b, u, x, mods, onorm_g, pool_w, pool_scale, w_out, tm):
    b, n, d = x.shape
    tm = min(tm, n)
    nt = n // tm
    hb = tm // POOL_HALO
    last_halo = n // POOL_HALO - 1
    w = A_WIDTH
    tile = lambda col: pl.BlockSpec((1, tm, w), lambda bi, i: (bi, i, col))
    return pl.pallas_call(
        functools.partial(_ab_out_kernel, n=n),
        grid=(b, nt),
        in_specs=[
            tile(0), tile(0), tile(4), tile(5),
            pl.BlockSpec((1, POOL_HALO, w), lambda bi, i: (bi, jnp.maximum(i * hb - 1, 0), 5)),
            pl.BlockSpec((1, POOL_HALO, w), lambda bi, i: (bi, jnp.minimum((i + 1) * hb, last_halo), 5)),
            pl.BlockSpec((1, tm, d), lambda bi, i: (bi, i, 0)),
            pl.BlockSpec((1, 6, d), lambda bi, i: (bi, 0, 0)),
            pl.BlockSpec((1, A_HEAD_DIM), lambda bi, i: (0, 0)),
            pl.BlockSpec(pool_w.shape, lambda bi, i: (0, 0, 0)),
            pl.BlockSpec((1, B_WIDTH), lambda bi, i: (0, 0)),
            pl.BlockSpec(w_out.shape, lambda bi, i: (0, 0)),
        ],
        out_specs=pl.BlockSpec((1, tm, d), lambda bi, i: (bi, i, 0)),
        out_shape=jax.ShapeDtypeStruct((b, n, d), F32),
        compiler_params=_cparams("parallel", "parallel"),
        name="ab_out",
    )(o_f, o_b, u, u, u, u, x, mods, onorm_g.reshape(1, A_HEAD_DIM), pool_w, pool_scale.reshape(1, B_WIDTH), w_out)


def _slot_rows(tr):
    rows = 2 * tr + MOE_EXPERTS * (SLOT_BLOCK - 1)
    assert rows % SLOT_BLOCK == 0
    return rows


def _route_kernel(x_ref, g_ref, m_ref, whi_ref, wlo_ref, br_ref, xs_ref, info_ref, cnt_ref, *, slot_rows):
    tr = x_ref.shape[0]
    h = _rmsnorm(x_ref[...], g_ref[...]) * (1.0 + m_ref[0, 4:5, :]) + m_ref[0, 3:4, :]
    hb = h.astype(BF16)
    hl = (h - hb.astype(F32)).astype(BF16)
    logits = _dot(hb, whi_ref[...]) + _dot(hb, wlo_ref[...]) + _dot(hl, whi_ref[...]) + br_ref[...]
    lane = lax.broadcasted_iota(jnp.int32, (tr, LANES), 1)
    lanef = lane.astype(F32)
    lg = jnp.where(lane < MOE_GROUPS, logits, NEG)
    mg = jnp.max(lg, axis=-1, keepdims=True)
    g_p = 1.0 / jnp.sum(jnp.exp(lg - mg), axis=-1, keepdims=True)
    gidx = jnp.min(jnp.where(lg == mg, lanef, float(LANES)), axis=-1, keepdims=True)
    lo = MOE_GROUPS + MOE_EPG * gidx
    le = jnp.where((lanef >= lo) & (lanef < lo + MOE_EPG), logits, NEG)
    m1 = jnp.max(le, axis=-1, keepdims=True)
    i1 = jnp.min(jnp.where(le == m1, lanef, float(LANES)), axis=-1, keepdims=True)
    le2 = jnp.where(lanef == i1, NEG, le)
    m2 = jnp.max(le2, axis=-1, keepdims=True)
    i2 = jnp.min(jnp.where(le2 == m2, lanef, float(LANES)), axis=-1, keepdims=True)
    ratio = jnp.exp(m2 - m1)
    w1 = g_p / (1.0 + ratio)
    w2 = g_p * ratio / (1.0 + ratio)
    hot1 = lanef == i1
    hot2 = lanef == i2
    hot = jnp.where(hot1, 1.0, jnp.where(hot2, 1.0, 0.0))
    r_i = lax.broadcasted_iota(jnp.int32, (tr, tr), 0)
    c_i = lax.broadcasted_iota(jnp.int32, (tr, tr), 1)
    rank = _dot(jnp.where(c_i < r_i, 1.0, 0.0).astype(BF16), hot.astype(BF16))
    cnt = jnp.sum(hot, axis=0, keepdims=True)
    nblk = jnp.floor((cnt + (SLOT_BLOCK - 1)) * (1.0 / SLOT_BLOCK))
    l_r = lax.broadcasted_iota(jnp.int32, (LANES, LANES), 0)
    l_c = lax.broadcasted_iota(jnp.int32, (LANES, LANES), 1)
    before = jnp.where(l_r < l_c, 1.0, 0.0).astype(BF16)
    off = SLOT_BLOCK * _dot(jnp.broadcast_to(nblk, (8, LANES)).astype(BF16), before)[0:1]
    posm = off + rank
    pos1 = jnp.sum(jnp.where(hot1, posm, 0.0), axis=-1, keepdims=True)
    pos2 = jnp.sum(jnp.where(hot2, posm, 0.0), axis=-1, keepdims=True)
    col = lax.broadcasted_iota(jnp.int32, (tr, slot_rows), 1)
    sel = jnp.where(col == pos1.astype(jnp.int32), 1.0, jnp.where(col == pos2.astype(jnp.int32), 1.0, 0.0))
    xs_ref[...] = lax.dot_general(sel.astype(BF16), hb, TN, preferred_element_type=F32).astype(BF16)
    info_ref[...] = jnp.where(lane == 0, pos1, jnp.where(lane == 1, pos2, jnp.where(lane == 2, w1,
                                                                                       jnp.where(lane == 3, w2, 0.0))))
    cnt_ref[0] = jnp.broadcast_to(cnt, (8, LANES))


def _moe_route(x2d, gain, mods, tiles_per_mod, w_hi, w_lo, b_r, tr):
    t, d = x2d.shape
    nt = t // tr
    sr = _slot_rows(tr)
    return pl.pallas_call(
        functools.partial(_route_kernel, slot_rows=sr),
        grid=(nt,),
        in_specs=[
            pl.BlockSpec((tr, d), lambda i: (i, 0)),
            pl.BlockSpec((1, d), lambda i: (0, 0)),
            pl.BlockSpec((1, 6, d), lambda i: (i // tiles_per_mod, 0, 0)),
            pl.BlockSpec((d, LANES), lambda i: (0, 0)),
            pl.BlockSpec((d, LANES), lambda i: (0, 0)),
            pl.BlockSpec((1, LANES), lambda i: (0, 0)),
        ],
        out_specs=[
            pl.BlockSpec((sr, d), lambda i: (i, 0)),
            pl.BlockSpec((tr, LANES), lambda i: (i, 0)),
            pl.BlockSpec((1, 8, LANES), lambda i: (i, 0, 0)),
        ],
        out_shape=[
            jax.ShapeDtypeStruct((nt * sr, d), BF16),
            jax.ShapeDtypeStruct((t, LANES), F32),
            jax.ShapeDtypeStruct((nt, 8, LANES), F32),
        ],
        compiler_params=_cparams("parallel"),
        name="moe_route",
    )(x2d, gain.reshape(1, d), mods, w_hi, w_lo, b_r)


def _expert_tables(cnt, blocks_per_tile, chunk_blocks, nsteps):
    ntiles = cnt.shape[0]
    nb = (cnt + (SLOT_BLOCK - 1)) // SLOT_BLOCK
    first = jnp.cumsum(nb, axis=1) - nb
    cum = jnp.cumsum(nb.T, axis=1)
    tot = cum[:, -1]
    totp = (tot + (STEP_BLOCKS - 1)) // STEP_BLOCKS * STEP_BLOCKS
    ends = jnp.cumsum(totp)
    j = jnp.arange(nsteps * STEP_BLOCKS, dtype=jnp.int32)
    e_raw = jnp.sum(ends[None, :] <= j[:, None], axis=1).astype(jnp.int32)
    e = jnp.minimum(e_raw, MOE_EXPERTS - 1)
    q = j - (ends - totp)[e]
    real = (e_raw < MOE_EXPERTS) & (q < tot[e])
    ti = jnp.minimum(jnp.sum(cum[e] <= q[:, None], axis=1), ntiles - 1).astype(jnp.int32)
    prev = jnp.where(ti > 0, cum[e, jnp.maximum(ti - 1, 0)], 0)
    loc = first[ti, e] + (q - prev)
    src = jnp.where(real, ti * blocks_per_tile + loc, 0).astype(jnp.int32)
    n_inv = ntiles * chunk_blocks
    inv = jnp.zeros((n_inv + 1,), jnp.int32).at[jnp.where(real, ti * chunk_blocks + loc, n_inv)].set(j)[:n_inv]
    return src, e[::STEP_BLOCKS], real[::STEP_BLOCKS].astype(jnp.int32), inv


def _experts_kernel(src_ref, exp_ref, valid_ref, *refs):
    x_refs = refs[:STEP_BLOCKS]
    wgu_ref, wd_ref, y_ref = refs[STEP_BLOCKS:]
    s = pl.program_id(0)

    @pl.when(valid_ref[s] > 0)
    def _():
        x = jnp.concatenate([r[0] for r in x_refs], axis=0)
        hgu = _dot(x, wgu_ref[0])
        a = _silu(hgu[:, :MOE_HIDDEN]) * hgu[:, MOE_HIDDEN:]
        y_ref[...] = _dot(a.astype(BF16), wd_ref[0]).astype(BF16)

    @pl.when(valid_ref[s] == 0)
    def _():
        y_ref[...] = jnp.zeros(y_ref.shape, y_ref.dtype)


def _moe_experts(xs, src, step_e, valid, w_gu, w_d, nsteps):
    rows, d = xs.shape
    xs3 = xs.reshape(rows // SLOT_BLOCK, SLOT_BLOCK, d)
    step_rows = STEP_BLOCKS * SLOT_BLOCK
    in_blk = lambda kk: pl.BlockSpec((1, SLOT_BLOCK, d), lambda s, sr, ex, va: (sr[s * STEP_BLOCKS + kk], 0, 0))
    grid_spec = pltpu.PrefetchScalarGridSpec(
        num_scalar_prefetch=3,
        grid=(nsteps,),
        in_specs=[in_blk(kk) for kk in range(STEP_BLOCKS)] + [
            pl.BlockSpec((1,) + w_gu.shape[1:], lambda s, sr, ex, va: (ex[s], 0, 0)),
            pl.BlockSpec((1,) + w_d.shape[1:], lambda s, sr, ex, va: (ex[s], 0, 0)),
        ],
        out_specs=pl.BlockSpec((step_rows, d), lambda s, sr, ex, va: (s, 0)),
    )
    return pl.pallas_call(
        _experts_kernel,
        grid_spec=grid_spec,
        out_shape=jax.ShapeDtypeStruct((nsteps * step_rows, d), BF16),
        compiler_params=_cparams("arbitrary"),
        name="moe_experts",
    )(src, step_e, valid, *([xs3] * STEP_BLOCKS), w_gu, w_d)


def _combine_kernel(inv_ref, x_ref, info_ref, m_ref, fg_ref, *refs, final):
    y_refs = refs[:STEP_BLOCKS]
    o_ref, acc_ref = refs[STEP_BLOCKS:]
    ck = pl.program_id(1)
    tr = x_ref.shape[0]
    rows = STEP_BLOCKS * SLOT_BLOCK

    @pl.when(ck == 0)
    def _():
        acc_ref[...] = jnp.zeros(acc_ref.shape, F32)

    info = info_ref[...]
    col = ck * rows + lax.broadcasted_iota(jnp.int32, (tr, rows), 1)
    wsel = jnp.where(col == info[:, 0:1].astype(jnp.int32), info[:, 2:3],
                     jnp.where(col == info[:, 1:2].astype(jnp.int32), info[:, 3:4], 0.0))
    acc_ref[...] += _dot(wsel.astype(BF16), jnp.concatenate([r[0] for r in y_refs], axis=0))

    @pl.when(ck == pl.num_programs(1) - 1)
    def _():
        out = x_ref[...] + m_ref[0, 5:6, :] * acc_ref[...]
        if final:
            out = _rmsnorm(out, fg_ref[...])
        o_ref[...] = out


def _moe_combine(x2d, ys, inv, info, mods, tiles_per_mod, final_g, tr, nchunks, final):
    t, d = x2d.shape
    ys3 = ys.reshape(ys.shape[0] // SLOT_BLOCK, SLOT_BLOCK, d)
    y_blk = lambda kk: pl.BlockSpec((1, SLOT_BLOCK, d),
                                    lambda i, ck, iv: (iv[(i * nchunks + ck) * STEP_BLOCKS + kk], 0, 0))
    grid_spec = pltpu.PrefetchScalarGridSpec(
        num_scalar_prefetch=1,
        grid=(t // tr, nchunks),
        in_specs=[
            pl.BlockSpec((tr, d), lambda i, ck, iv: (i, 0)),
            pl.BlockSpec((tr, LANES), lambda i, ck, iv: (i, 0)),
            pl.BlockSpec((1, 6, d), lambda i, ck, iv: (i // tiles_per_mod, 0, 0)),
            pl.BlockSpec((1, d), lambda i, ck, iv: (0, 0)),
        ] + [y_blk(kk) for kk in range(STEP_BLOCKS)],
        out_specs=pl.BlockSpec((tr, d), lambda i, ck, iv: (i, 0)),
        scratch_shapes=[pltpu.VMEM((tr, d), F32)],
    )
    return pl.pallas_call(
        functools.partial(_combine_kernel, final=final),
        grid_spec=grid_spec,
        out_shape=jax.ShapeDtypeStruct((t, d), F32),
        compiler_params=_cparams("parallel", "arbitrary"),
        name="moe_combine",
    )(inv, x2d, info, mods, final_g.reshape(1, d), *([ys3] * STEP_BLOCKS))


def _moe(x, gain, mods, params, final_g, final, tr):
    w_hi, w_lo, b_r, w_gu, w_d = params
    b, n, d = x.shape
    tr = min(tr, n)
    t = b * n
    nt = t // tr
    x2d = x.reshape(t, d)
    xs, info, cnt = _moe_route(x2d, gain, mods, n // tr, w_hi, w_lo, b_r, tr)
    bpt = _slot_rows(tr) // SLOT_BLOCK
    nchunks = -(-bpt // STEP_BLOCKS)
    nsteps = -(-(nt * bpt + MOE_EXPERTS * (STEP_BLOCKS - 1)) // STEP_BLOCKS)
    counts = cnt[:, 0, MOE_GROUPS:MOE_GROUPS + MOE_EXPERTS].astype(jnp.int32)
    src, step_e, valid, inv = _expert_tables(counts, bpt, nchunks * STEP_BLOCKS, nsteps)
    ys = _moe_experts(xs, src, step_e, valid, w_gu, w_d, nsteps)
    out = _moe_combine(x2d, ys, inv, info, mods, n // tr, final_g, tr, nchunks, final)
    return out.reshape(b, n, d)


def _mla_proj_kernel(cq_ref, ckv_ref, kr_ref, krp_ref, cos_ref, sin_ref, qg_ref, kg_ref, wq_ref, wqp_ref, wk_ref,
                     wv_ref, *o_refs, need_q, q_scale):
    cos = cos_ref[...]
    sin = sin_ref[...]
    ckv = _rmsnorm(ckv_ref[0], kg_ref[...]).astype(BF16)
    k_rope = kr_ref[0] * cos + krp_ref[0] * sin
    kn = _dot(ckv, wk_ref[...])
    if need_q:
        q_ref, k_ref, v_ref = o_refs
    else:
        k_ref, v_ref = o_refs
    for h in range(D_HEADS):
        hs = slice(h * MLA_PAD, (h + 1) * MLA_PAD)
        k_ref[0, h] = (kn[:, hs] + k_rope).astype(BF16)
    v_ref[0] = _dot(ckv, wv_ref[...]).astype(BF16)
    if need_q:
        cq = _rmsnorm(cq_ref[0], qg_ref[...]).astype(BF16)
        qm = _dot(cq, wq_ref[...])
        qp = _dot(cq, wqp_ref[...])
        for h in range(D_HEADS):
            hs = slice(h * MLA_PAD, (h + 1) * MLA_PAD)
            q_ref[0, h] = ((qm[:, hs] * cos + qp[:, hs] * sin) * q_scale).astype(BF16)


def _mla_proj(u_b, cos, sin, q_g, kv_g, wq, wqp, wk, wv, need_q, tm):
    b, n, _ = u_b.shape
    tm = min(tm, n)
    hp = D_HEADS * MLA_PAD
    outs, specs = [], []
    if need_q:
        outs.append(jax.ShapeDtypeStruct((b, D_HEADS, n, MLA_PAD), BF16))
        specs.append(pl.BlockSpec((1, D_HEADS, tm, MLA_PAD), lambda bi, i: (bi, 0, i, 0)))
    outs += [jax.ShapeDtypeStruct((b, D_HEADS, n, MLA_PAD), BF16), jax.ShapeDtypeStruct((b, n, D_HEADS * MLA_V), BF16)]
    specs += [pl.BlockSpec((1, D_HEADS, tm, MLA_PAD), lambda bi, i: (bi, 0, i, 0)),
              pl.BlockSpec((1, tm, D_HEADS * MLA_V), lambda bi, i: (bi, i, 0))]
    full = lambda a: pl.BlockSpec(a.shape, lambda bi, i: (0,) * a.ndim)
    return pl.pallas_call(
        functools.partial(_mla_proj_kernel, need_q=need_q, q_scale=float((MLA_NOPE + MLA_ROPE) ** -0.5)),
        grid=(b, n // tm),
        in_specs=[
            pl.BlockSpec((1, tm, MLA_Q_RANK), lambda bi, i: (bi, i, 0)),
            pl.BlockSpec((1, tm, MLA_KV_RANK), lambda bi, i: (bi, i, 2)),
            pl.BlockSpec((1, tm, MLA_PAD), lambda bi, i: (bi, i, 3)),
            pl.BlockSpec((1, tm, MLA_PAD), lambda bi, i: (bi, i, 4)),
            pl.BlockSpec((tm, MLA_PAD), lambda bi, i: (i, 0)),
            pl.BlockSpec((tm, MLA_PAD), lambda bi, i: (i, 0)),
            full(q_g), full(kv_g), full(wq), full(wqp), full(wk), full(wv),
        ],
        out_specs=specs,
        out_shape=outs,
        compiler_params=_cparams("parallel", "parallel"),
        name="mla_proj",
    )(u_b, u_b, u_b, u_b, cos, sin, q_g, kv_g, wq, wqp, wk, wv)


def _mla_attn_kernel(q_ref, k_ref, v_ref, o_ref, *, tk):
    tq = q_ref.shape[2]
    nk = k_ref.shape[2]
    lane = lax.broadcasted_iota(jnp.int32, (tq, LANES), 1)
    outs = []
    for hh in range(2):
        q = q_ref[0, hh]

        def body(c, carry, hh=hh, q=q):
            m, l, acc = carry
            ks = pl.ds(pl.multiple_of(c * tk, tk), tk)
            s = lax.dot_general(q, k_ref[0, hh, ks, :], NT, preferred_element_type=F32)
            m_new = jnp.maximum(m, jnp.max(s, axis=-1, keepdims=True))
            alpha = jnp.exp(m - m_new)
            p = jnp.exp(s - m_new)
            l = alpha * l + jnp.sum(p, axis=-1, keepdims=True)
            acc = alpha * acc + _dot(p.astype(BF16), v_ref[0, ks, :])
            return m_new, l, acc

        init = (jnp.full((tq, 1), NEG, F32), jnp.zeros((tq, 1), F32), jnp.zeros((tq, LANES), F32))
        m, l, acc = lax.fori_loop(0, nk // tk, body, init)
        outs.append(acc / l)
    o_ref[0] = jnp.where(lane < MLA_V, outs[0], outs[1]).astype(o_ref.dtype)


def _mla_attention(q, k, v, tq, tk):
    b, h, n, _ = q.shape
    nk = k.shape[2]
    tq = min(tq, n)
    return pl.pallas_call(
        functools.partial(_mla_attn_kernel, tk=tk),
        grid=(b, h // 2, n // tq),
        in_specs=[
            pl.BlockSpec((1, 2, tq, MLA_PAD), lambda bi, hp, i: (bi, hp, i, 0)),
            pl.BlockSpec((1, 2, nk, MLA_PAD), lambda bi, hp, i: (bi, hp, 0, 0)),
            pl.BlockSpec((1, nk, 2 * MLA_V), lambda bi, hp, i: (bi, 0, hp)),
        ],
        out_specs=pl.BlockSpec((1, tq, 2 * MLA_V), lambda bi, hp, i: (bi, i, hp)),
        out_shape=jax.ShapeDtypeStruct((b, n, h * MLA_V), BF16),
        compiler_params=_cparams("parallel", "parallel", "arbitrary"),
        name="mla_attention",
    )(q, k, v)


def _na_kernel(q_ref, kp_ref, km_ref, kn_ref, kc_ref, vp_ref, vm_ref, vn_ref, vc_ref, tab_ref, o_ref):
    tq = q_ref.shape[1]
    nloc = 3 * tq
    lane = lax.broadcasted_iota(jnp.int32, (tq, LANES), 1)
    q = q_ref[0]
    k_all = jnp.concatenate([kp_ref[0], km_ref[0], kn_ref[0], kc_ref[0]], axis=0)
    v_all = jnp.concatenate([vp_ref[0], vm_ref[0], vn_ref[0], vc_ref[0]], axis=0)
    outs = []
    for hh in range(2):
        in_head = (lane >= hh * C_HEAD_DIM) & (lane < (hh + 1) * C_HEAD_DIM)
        qh = jnp.where(in_head, q, jnp.zeros_like(q))
        s = lax.dot_general(qh, k_all, NT, preferred_element_type=F32)
        s_loc = s[:, :nloc] + tab_ref[0, hh]
        s_ctx = s[:, nloc:]
        m = jnp.maximum(jnp.max(s_loc, axis=-1, keepdims=True), jnp.max(s_ctx, axis=-1, keepdims=True))
        p_loc = jnp.exp(s_loc - m)
        p_ctx = jnp.exp(s_ctx - m)
        l = jnp.sum(p_loc, axis=-1, keepdims=True) + jnp.sum(p_ctx, axis=-1, keepdims=True)
        o = _dot(p_loc.astype(BF16), v_all[:nloc]) + _dot(p_ctx.astype(BF16), v_all[nloc:])
        outs.append(o / l)
    o_ref[0] = jnp.where(lane < C_HEAD_DIM, outs[0], outs[1]).astype(o_ref.dtype)


def _na_tables(rpb, rows):
    h = rpb.shape[0]
    w = GRID_W
    qc = np.arange(w)
    kc = np.arange(w)
    cs = np.clip(qc - NA_COLS // 2, 0, w - NA_COLS)
    col_ok = (kc[None, :] >= cs[:, None]) & (kc[None, :] < cs[:, None] + NA_COLS)
    dc = np.clip(kc[None, :] - qc[:, None] + (NA_COLS - 1), 0, 2 * NA_COLS - 2)
    base = jnp.where(col_ok[None, None], rpb.astype(F32)[:, :, dc], NEG)
    base = jnp.concatenate([base, jnp.full((h, 1, w, w), NEG, F32)], axis=1)
    nblk = rows // NA_QROWS
    tabs = []
    for m in (0, 1, nblk - 1):
        qr = NA_QROWS * m + np.arange(NA_QROWS)
        rs = np.clip(qr - NA_ROWS // 2, 0, rows - NA_ROWS)
        kr = NA_QROWS * (m - 1) + np.arange(3 * NA_QROWS)
        ok = (kr[None, :] >= rs[:, None]) & (kr[None, :] < rs[:, None] + NA_ROWS)
        dr = np.where(ok, kr[None, :] - qr[:, None] + (NA_ROWS - 1), 2 * NA_ROWS - 1)
        t = base[:, dr]
        tabs.append(t.transpose(0, 1, 3, 2, 4).reshape(h, NA_QROWS * w, 3 * NA_QROWS * w))
    return jnp.stack(tabs)


def _na_attention(u_lat, u_ctx, tabs):
    b, n, _ = u_lat.shape
    nc = u_ctx.shape[1]
    tq = NA_QROWS * GRID_W
    nblk = n // tq
    pairs = C_HEADS // 2
    prev = lambda i: jnp.maximum(i - 1, 0)
    nxt = lambda i: jnp.minimum(i + 1, nblk - 1)
    blk = lambda col0, f: pl.BlockSpec((1, tq, LANES), lambda bi, hp, i: (bi, f(i), col0 + hp))
    ctx = lambda col0: pl.BlockSpec((1, nc, LANES), lambda bi, hp, i: (bi, 0, col0 + hp))
    same = lambda i: i
    sel = lambda i: jnp.where(i == 0, 0, jnp.where(i == nblk - 1, 2, 1))
    return pl.pallas_call(
        _na_kernel,
        grid=(b, pairs, nblk),
        in_specs=[
            blk(0, same),
            blk(pairs, prev), blk(pairs, same), blk(pairs, nxt), ctx(pairs),
            blk(2 * pairs, prev), blk(2 * pairs, same), blk(2 * pairs, nxt), ctx(2 * pairs),
            pl.BlockSpec((1, 2, tq, 3 * tq), lambda bi, hp, i: (sel(i), hp, 0, 0)),
        ],
        out_specs=pl.BlockSpec((1, tq, LANES), lambda bi, hp, i: (bi, i, hp)),
        out_shape=jax.ShapeDtypeStruct((b, n, C_WIDTH), BF16),
        compiler_params=_cparams("parallel", "parallel", "arbitrary"),
        name="na_attention",
    )(u_lat, u_lat, u_lat, u_lat, u_ctx, u_lat, u_lat, u_lat, u_ctx, tabs)


def _cd_out_kernel(c_ref, d_ref, x_ref, m_ref, wo_ref, o_ref):
    wc = c_ref.shape[2]
    o = _dot(c_ref[0], wo_ref[:wc]) + _dot(d_ref[0], wo_ref[wc:])
    o_ref[0] = x_ref[0] + m_ref[0, 2:3, :] * o


def _cd_out(c_lat, d_lat, x, mods, w_out, tm):
    b, n, d = x.shape
    tm = min(tm, n)
    return pl.pallas_call(
        _cd_out_kernel,
        grid=(b, n // tm),
        in_specs=[
            pl.BlockSpec((1, tm, c_lat.shape[2]), lambda bi, i: (bi, i, 0)),
            pl.BlockSpec((1, tm, d_lat.shape[2]), lambda bi, i: (bi, i, 0)),
            pl.BlockSpec((1, tm, d), lambda bi, i: (bi, i, 0)),
            pl.BlockSpec((1, 6, d), lambda bi, i: (bi, 0, 0)),
            pl.BlockSpec(w_out.shape, lambda bi, i: (0, 0)),
        ],
        out_specs=pl.BlockSpec((1, tm, d), lambda bi, i: (bi, i, 0)),
        out_shape=jax.ShapeDtypeStruct((b, n, d), F32),
        compiler_params=_cparams("parallel", "parallel"),
        name="cd_out",
    )(c_lat, d_lat, x, mods, w_out)


def _moe_params(w_rg, b_rg, w_re, b_re, w_gate, w_up, w_down):
    d = w_rg.shape[0]
    w_r = jnp.zeros((d, LANES), F32).at[:, :MOE_GROUPS].set(w_rg).at[:, MOE_GROUPS:MOE_GROUPS + MOE_EXPERTS].set(w_re)
    b_r = jnp.zeros((1, LANES), F32).at[0, :MOE_GROUPS].set(b_rg).at[0, MOE_GROUPS:MOE_GROUPS + MOE_EXPERTS].set(b_re)
    w_hi = w_r.astype(BF16)
    w_lo = (w_r - w_hi.astype(F32)).astype(BF16)
    f = w_gate.shape[-1]
    w_gu = jnp.concatenate([w_gate, w_up], axis=-1).reshape(MOE_EXPERTS, d, 2 * f).astype(BF16)
    w_d = w_down.reshape(MOE_EXPERTS, f, d).astype(BF16)
    return w_hi, w_lo, b_r, w_gu, w_d


def _rope_perm():
    j = np.arange(MLA_ROPE)
    half = MLA_ROPE // 2
    return (j // half) * half + (j % half + half // 2) % half


def _rope_tables(n):
    half = MLA_ROPE // 2
    nf = half // 2
    t = jnp.arange(n)
    inv = ROPE_THETA ** (-jnp.arange(nf, dtype=F32) / nf)
    parts_c, parts_s = [], []
    for pos in ((t // GRID_W).astype(F32), (t % GRID_W).astype(F32)):
        ang = pos[:, None] * inv[None, :]
        c, s = jnp.cos(ang), jnp.sin(ang)
        parts_c += [c, c]
        parts_s += [-s, s]
    pad = MLA_PAD - MLA_NOPE - MLA_ROPE
    cos = jnp.concatenate([jnp.ones((n, MLA_NOPE), F32)] + parts_c + [jnp.zeros((n, pad), F32)], axis=1)
    sin = jnp.concatenate([jnp.zeros((n, MLA_NOPE), F32)] + parts_s + [jnp.zeros((n, pad), F32)], axis=1)
    return cos, sin


def _identity_rope_tables(n):
    pad = MLA_PAD - MLA_NOPE - MLA_ROPE
    cos = jnp.concatenate([jnp.ones((n, MLA_NOPE + MLA_ROPE), F32), jnp.zeros((n, pad), F32)], axis=1)
    return cos, jnp.zeros((n, MLA_PAD), F32)


def _pad_heads(w, widths, src_cols, dst_off):
    rank = w.shape[0]
    out = jnp.zeros((rank, D_HEADS, MLA_PAD), F32)
    wh = w.reshape(rank, D_HEADS, widths)[:, :, src_cols]
    return out.at[:, :, dst_off:dst_off + len(src_cols)].set(wh).reshape(rank, D_HEADS * MLA_PAD)


def _cd_params(w_in, w_uq, w_ukv):
    d = w_in.shape[0]
    perm = _rope_perm()
    o = 3 * C_WIDTH
    q_scale = float(C_HEAD_DIM ** -0.5)
    kr = w_in[:, o + MLA_Q_RANK + MLA_KV_RANK:]
    pad_rope = lambda a: jnp.zeros((d, MLA_PAD), F32).at[:, MLA_NOPE:MLA_NOPE + MLA_ROPE].set(a)
    w_cat = jnp.concatenate([
        w_in[:, :C_WIDTH] * q_scale, w_in[:, C_WIDTH:o],
        w_in[:, o:o + MLA_Q_RANK + MLA_KV_RANK], pad_rope(kr), pad_rope(kr[:, perm]),
    ], axis=1).astype(BF16)
    qw = MLA_NOPE + MLA_ROPE
    nope = np.arange(MLA_NOPE)
    rope = MLA_NOPE + np.arange(MLA_ROPE)
    wq = (_pad_heads(w_uq, qw, nope, 0) + _pad_heads(w_uq, qw, rope, MLA_NOPE)).astype(BF16)
    wqp = _pad_heads(w_uq, qw, rope[perm], MLA_NOPE).astype(BF16)
    kvw = MLA_NOPE + MLA_V
    wk = _pad_heads(w_ukv, kvw, nope, 0).astype(BF16)
    wv = w_ukv.reshape(-1, D_HEADS, kvw)[:, :, MLA_NOPE:].reshape(-1, D_HEADS * MLA_V).astype(BF16)
    return w_cat, wq, wqp, wk, wv


def kernel(x, c, ctx, c_ctx, ada_w, ada_b, norm1_g, norm2_g, ab_w_in, ab_w_out, hgrn_lb_logits, hgrn_onorm_g, pool_w,
           pool_scale, cd_w_in, cd_w_out, na_rpb, mla_q_norm_g, mla_w_uq, mla_kv_norm_g, mla_w_ukv, moe_w_rg, moe_b_rg,
           moe_w_re, moe_b_re, moe_w_gate, moe_w_up, moe_w_down, final_norm_g):
    b, n, d = x.shape
    n_ctx = ctx.shape[1]
    assert ada_w.shape[0] == 2 and ab_w_in.shape[0] == 1 and cd_w_in.shape[0] == 1 and b + 1 <= 8
    tm = 512

    cc = jnp.zeros((8, d), F32).at[:b].set(c).at[b].set(c_ctx)
    mods = _ada(cc, ada_w, ada_b).reshape(2, 8, 6, d)
    mods_lat = [mods[l, :b] for l in range(2)]
    mods_ctx = [jnp.broadcast_to(mods[l, b:b + 1], (b, 6, d)) for l in range(2)]
    lb = jnp.cumsum(jax.nn.softmax(hgrn_lb_logits.astype(F32), axis=1), axis=1)[:, 0]

    w_in0 = ab_w_in[0].astype(BF16)
    w_out0 = ab_w_out[0].astype(BF16)
    pw0 = pool_w[0].astype(BF16)
    ab_cols = w_in0.shape[1]
    (u_ctx,) = _in_proj(ctx, norm1_g[0], mods_ctx[0], w_in0, ((0, ab_cols),), (F32,), tm)
    (u_lat,) = _in_proj(x, norm1_g[0], mods_lat[0], w_in0, ((0, ab_cols),), (F32,), tm)
    s0 = jnp.zeros((b, 2, A_HEADS, A_HEAD_DIM, A_HEAD_DIM), F32)
    ocf, ocb, s_ctx = _hgrn_scan(u_ctx, lb, s0, 256)
    olf, olb, _ = _hgrn_scan(u_lat, lb, s_ctx, 256)
    xc = _ab_out(ocf, ocb, u_ctx, ctx, mods_ctx[0], hgrn_onorm_g[0], pw0, pool_scale[0], w_out0, tm)
    x = _ab_out(olf, olb, u_lat, x, mods_lat[0], hgrn_onorm_g[0], pw0, pool_scale[0], w_out0, tm)
    moe0 = _moe_params(moe_w_rg[0], moe_b_rg[0], moe_w_re[0], moe_b_re[0], moe_w_gate[0], moe_w_up[0], moe_w_down[0])
    xc = _moe(xc, norm2_g[0], mods_ctx[0], moe0, final_norm_g, False, tm)
    x = _moe(x, norm2_g[0], mods_lat[0], moe0, final_norm_g, False, tm)

    w_cat, wq, wqp, wk, wv = _cd_params(cd_w_in[0], mla_w_uq[0], mla_w_ukv[0])
    na_w = 3 * C_WIDTH
    splits = ((0, na_w), (na_w, w_cat.shape[1]))
    ua_ctx, ub_ctx = _in_proj(xc, norm1_g[1], mods_ctx[1], w_cat, splits, (BF16, F32), tm)
    ua_lat, ub_lat = _in_proj(x, norm1_g[1], mods_lat[1], w_cat, splits, (BF16, F32), tm)
    q_g = mla_q_norm_g[0].reshape(1, -1)
    kv_g = mla_kv_norm_g[0].reshape(1, -1)
    cos_l, sin_l = _rope_tables(n)
    cos_c, sin_c = _identity_rope_tables(n_ctx)
    k_c, v_c = _mla_proj(ub_ctx, cos_c, sin_c, q_g, kv_g, wq, wqp, wk, wv, False, tm)
    q_l, k_l, v_l = _mla_proj(ub_lat, cos_l, sin_l, q_g, kv_g, wq, wqp, wk, wv, True, tm)
    d_lat = _mla_attention(q_l, jnp.concatenate([k_c, k_l], axis=2), jnp.concatenate([v_c, v_l], axis=1), 256, 256)
    c_lat = _na_attention(ua_lat, ua_ctx, _na_tables(na_rpb[0], n // GRID_W))
    x = _cd_out(c_lat, d_lat, x, mods_lat[1], cd_w_out[0].astype(BF16), tm)
    moe1 = _moe_params(moe_w_rg[1], moe_b_rg[1], moe_w_re[1], moe_b_re[1], moe_w_gate[1], moe_w_up[1], moe_w_down[1])
    return _moe(x, norm2_g[1], mods_lat[1], moe1, final_norm_g, True, tm)
```

```python
import functools

import numpy as np
import jax
import jax.numpy as jnp
from jax import lax
from jax.experimental import pallas as pl
from jax.experimental.pallas import tpu as pltpu

F32 = jnp.float32
BF16 = jnp.bfloat16

EPS = 1e-6
NEG = -1e30

GRID_W = 64
A_HEADS = 4
A_HEAD_DIM = 128
A_WIDTH = A_HEADS * A_HEAD_DIM
POOL_WINDOWS = (2, 4, 8, 16)
B_GROUP = 128
B_WIDTH = B_GROUP * len(POOL_WINDOWS)
POOL_HALO = 16
C_HEADS = 8
C_HEAD_DIM = 64
C_WIDTH = C_HEADS * C_HEAD_DIM
NA_ROWS = 8
NA_COLS = 16
NA_QROWS = 4
D_HEADS = 8
MLA_Q_RANK = 256
MLA_KV_RANK = 128
MLA_NOPE = 64
MLA_ROPE = 32
MLA_V = 64
MLA_PAD = 128
ROPE_THETA = 10000.0
MOE_GROUPS = 4
MOE_EPG = 8
MOE_EXPERTS = MOE_GROUPS * MOE_EPG
MOE_HIDDEN = 256
LANES = 128
SLOT_BLOCK = 16
STEP_BLOCKS = 16
VMEM_LIMIT = 56 * 1024 * 1024

NT = (((1,), (1,)), ((), ()))
TN = (((0,), (0,)), ((), ()))


def _cparams(*sem):
    return pltpu.CompilerParams(dimension_semantics=sem, vmem_limit_bytes=VMEM_LIMIT)


def _sigmoid(x):
    return 1.0 / (1.0 + jnp.exp(-x))


def _silu(x):
    return x * _sigmoid(x)


def _dot(a, b):
    return jnp.dot(a, b, preferred_element_type=F32)


def _rmsnorm(x, g):
    return x * lax.rsqrt(jnp.mean(x * x, axis=-1, keepdims=True) + EPS) * g


def _ada_kernel(c_ref, w_ref, b_ref, o_ref):
    s = _silu(c_ref[...])
    o_ref[0] = jnp.dot(s, w_ref[0], precision=lax.Precision.HIGHEST, preferred_element_type=F32) + b_ref[0]


def _ada(cc, ada_w, ada_b):
    depth, d, n6 = ada_w.shape
    tn = n6 // 4
    return pl.pallas_call(
        _ada_kernel,
        grid=(depth, n6 // tn),
        in_specs=[
            pl.BlockSpec((8, d), lambda l, j: (0, 0)),
            pl.BlockSpec((1, d, tn), lambda l, j: (l, 0, j)),
            pl.BlockSpec((1, 1, tn), lambda l, j: (l, 0, j)),
        ],
        out_specs=pl.BlockSpec((1, 8, tn), lambda l, j: (l, 0, j)),
        out_shape=jax.ShapeDtypeStruct((depth, 8, n6), F32),
        compiler_params=_cparams("parallel", "parallel"),
        name="ada_mod",
    )(cc, ada_w, ada_b.reshape(depth, 1, n6))


def _in_kernel(x_ref, g_ref, m_ref, w_ref, *o_refs, splits):
    h = _rmsnorm(x_ref[0], g_ref[...]) * (1.0 + m_ref[0, 1:2, :]) + m_ref[0, 0:1, :]
    hb = h.astype(BF16)
    for o_ref, (a, b) in zip(o_refs, splits):
        o_ref[0] = _dot(hb, w_ref[:, a:b]).astype(o_ref.dtype)


def _in_proj(x, gain, mods, w, splits, dtypes, tm):
    b, n, d = x.shape
    tm = min(tm, n)
    outs = [jax.ShapeDtypeStruct((b, n, hi - lo), dt) for (lo, hi), dt in zip(splits, dtypes)]
    return pl.pallas_call(
        functools.partial(_in_kernel, splits=splits),
        grid=(b, n // tm),
        in_specs=[
            pl.BlockSpec((1, tm, d), lambda bi, i: (bi, i, 0)),
            pl.BlockSpec((1, d), lambda bi, i: (0, 0)),
            pl.BlockSpec((1, 6, d), lambda bi, i: (bi, 0, 0)),
            pl.BlockSpec(w.shape, lambda bi, i: (0, 0)),
        ],
        out_specs=[pl.BlockSpec((1, tm, hi - lo), lambda bi, i: (bi, i, 0)) for lo, hi in splits],
        out_shape=outs,
        compiler_params=_cparams("parallel", "parallel"),
        name="in_proj",
    )(x, gain.reshape(1, d), mods, w)


HG_SUB = 64


def _hgrn_direction(q_raw, fz, v, lb, st_ref, d, o_ref, reverse):
    rows = q_raw.shape[0]
    c = HG_SUB
    f = lb + (1.0 - lb) * _sigmoid(fz)
    k = 1.0 - f
    g = jnp.log(f)
    q = _silu(q_raw)
    r_i = lax.broadcasted_iota(jnp.int32, (c, c), 0)
    c_i = lax.broadcasted_iota(jnp.int32, (c, c), 1)
    keep = (c_i >= r_i) if reverse else (c_i <= r_i)
    tri = jnp.where(keep, 1.0, 0.0).astype(BF16)
    order = range(rows // c - 1, -1, -1) if reverse else range(rows // c)
    for ci in order:
        sl = slice(ci * c, (ci + 1) * c)
        gc = g[sl]
        g_hi = gc.astype(BF16)
        g_lo = (gc - g_hi.astype(F32)).astype(BF16)
        bc = _dot(tri, g_hi) + _dot(tri, g_lo)
        ref = bc[c // 2:c // 2 + 1]
        tot = bc[0:1] if reverse else bc[c - 1:c]
        qt = q[sl] * jnp.exp(bc - ref)
        kt = k[sl] * jnp.exp(ref - bc)
        qd = (qt * jnp.exp(ref)).astype(BF16)
        kd = (kt * jnp.exp(tot - ref)).astype(BF16)
        qt = qt.astype(BF16)
        kt = kt.astype(BF16)
        vb = v[sl].astype(BF16)
        dec = jnp.exp(tot)
        for h in range(A_HEADS):
            hs = slice(h * A_HEAD_DIM, (h + 1) * A_HEAD_DIM)
            att = lax.dot_general(qt[:, hs], kt[:, hs], NT, preferred_element_type=F32)
            att = jnp.where(keep, att, 0.0).astype(BF16)
            st = st_ref[d, h]
            o = _dot(att, vb[:, hs]) + lax.dot_general(qd[:, hs], st.astype(BF16), NT, preferred_element_type=F32)
            o_ref[0, sl, hs] = o
            st_ref[d, h] = st * dec[:, hs] + lax.dot_general(vb[:, hs], kd[:, hs], TN, preferred_element_type=F32)


def _hgrn_kernel(qf_ref, ff_ref, vf_ref, qb_ref, fb_ref, vb_ref, lb_ref, s0_ref, of_ref, ob_ref, sfin_ref, st_ref):
    j = pl.program_id(1)

    @pl.when(j == 0)
    def _():
        st_ref[...] = s0_ref[0]

    _hgrn_direction(qf_ref[0], ff_ref[0], vf_ref[0], lb_ref[0:1], st_ref, 0, of_ref, False)
    _hgrn_direction(qb_ref[0], fb_ref[0], vb_ref[0], lb_ref[1:2], st_ref, 1, ob_ref, True)

    @pl.when(j == pl.num_programs(1) - 1)
    def _():
        sfin_ref[0] = st_ref[...]


def _hgrn_scan(u, lb, s0, rows):
    b, n, _ = u.shape
    rows = min(rows, n)
    nb = n // rows
    w = A_WIDTH

    def fwd(col):
        return pl.BlockSpec((1, rows, w), lambda bi, j: (bi, j, col))

    def bwd(col):
        return pl.BlockSpec((1, rows, w), lambda bi, j: (bi, nb - 1 - j, col))

    st_spec = pl.BlockSpec((1, 2, A_HEADS, A_HEAD_DIM, A_HEAD_DIM), lambda bi, j: (bi, 0, 0, 0, 0))
    return pl.pallas_call(
        _hgrn_kernel,
        grid=(b, nb),
        in_specs=[fwd(0), fwd(1), fwd(3), bwd(0), bwd(2), bwd(3), pl.BlockSpec((2, w), lambda bi, j: (0, 0)), st_spec],
        out_specs=[
            pl.BlockSpec((1, rows, w), lambda bi, j: (bi, j, 0)),
            pl.BlockSpec((1, rows, w), lambda bi, j: (bi, nb - 1 - j, 0)),
            st_spec,
        ],
        out_shape=[
            jax.ShapeDtypeStruct((b, n, w), F32),
            jax.ShapeDtypeStruct((b, n, w), F32),
            jax.ShapeDtypeStruct(s0.shape, F32),
        ],
        scratch_shapes=[pltpu.VMEM((2, A_HEADS, A_HEAD_DIM, A_HEAD_DIM), F32)],
        compiler_params=_cparams("parallel", "arbitrary"),
        name="hgrn_scan",
    )(u, u, u, u, u, u, lb, s0)


def _ab_out_kernel(of_ref, ob_ref, ug_ref, up_ref, pprev_ref, pnext_ref, x_ref, m_ref, on_ref, pw_ref, ps_ref,
                   wo_ref, o_ref, *, n):
    i = pl.program_id(1)
    tm = x_ref.shape[1]
    o = of_ref[0] + ob_ref[0]
    gate = _silu(ug_ref[0])
    parts = []
    for h in range(A_HEADS):
        hs = slice(h * A_HEAD_DIM, (h + 1) * A_HEAD_DIM)
        parts.append(_rmsnorm(o[:, hs], on_ref[...]) * gate[:, hs])
    main = up_ref[0]
    prev = jnp.where(i > 0, pprev_ref[0], 0.0)
    nxt = jnp.where(i < pl.num_programs(1) - 1, pnext_ref[0], 0.0)
    ext = jnp.concatenate([prev, main, nxt], axis=0)
    ext_rows = tm + 2 * POOL_HALO
    t = i * tm + lax.broadcasted_iota(jnp.int32, (tm, 1), 0)
    for gi, win in enumerate(POOL_WINDOWS):
        gs = slice(gi * B_GROUP, (gi + 1) * B_GROUP)
        acc = ext[:, gs]
        acc = acc + pltpu.roll(acc, 1, 0)
        half = 1
        while 2 * half < win:
            acc = pltpu.roll(acc, half, 0) + pltpu.roll(acc, ext_rows - half, 0)
            half *= 2
        cnt = jnp.minimum(t + (win - win // 2), n) - jnp.maximum(t - win // 2, 0)
        mean = acc[POOL_HALO:POOL_HALO + tm] / cnt.astype(F32)
        pooled = _dot((mean - main[:, gs]).astype(BF16), pw_ref[gi])
        parts.append(pooled * ps_ref[:, gs])
    mix = jnp.concatenate(parts, axis=-1).astype(BF16)
    o_ref[0] = x_ref[0] + m_ref[0, 2:3, :] * _dot(mix, wo_ref[...])


def _ab_out(o_f, o_b, u, x, mods, onorm_g, pool_w, pool_scale, w_out, tm):
    b, n, d = x.shape
    tm = min(tm, n)
    nt = n // tm
    hb = tm // POOL_HALO
    last_halo = n // POOL_HALO - 1
    w = A_WIDTH
    tile = lambda col: pl.BlockSpec((1, tm, w), lambda bi, i: (bi, i, col))
    return pl.pallas_call(
        functools.partial(_ab_out_kernel, n=n),
        grid=(b, nt),
        in_specs=[
            tile(0), tile(0), tile(4), tile(5),
            pl.BlockSpec((1, POOL_HALO, w), lambda bi, i: (bi, jnp.maximum(i * hb - 1, 0), 5)),
            pl.BlockSpec((1, POOL_HALO, w), lambda bi, i: (bi, jnp.minimum((i + 1) * hb, last_halo), 5)),
            pl.BlockSpec((1, tm, d), lambda bi, i: (bi, i, 0)),
            pl.BlockSpec((1, 6, d), lambda bi, i: (bi, 0, 0)),
            pl.BlockSpec((1, A_HEAD_DIM), lambda bi, i: (0, 0)),
            pl.BlockSpec(pool_w.shape, lambda bi, i: (0, 0, 0)),
            pl.BlockSpec((1, B_WIDTH), lambda bi, i: (0, 0)),
            pl.BlockSpec(w_out.shape, lambda bi, i: (0, 0)),
        ],
        out_specs=pl.BlockSpec((1, tm, d), lambda bi, i: (bi, i, 0)),
        out_shape=jax.ShapeDtypeStruct((b, n, d), F32),
        compiler_params=_cparams("parallel", "parallel"),
        name="ab_out",
    )(o_f, o_b, u, u, u, u, x, mods, onorm_g.reshape(1, A_HEAD_DIM), pool_w, pool_scale.reshape(1, B_WIDTH), w_out)


def _slot_rows(tr):
    rows = 2 * tr + MOE_EXPERTS * (SLOT_BLOCK - 1)
    assert rows % SLOT_BLOCK == 0
    return rows


def _route_kernel(x_ref, g_ref, m_ref, whi_ref, wlo_ref, br_ref, xs_ref, info_ref, cnt_ref, *, slot_rows):
    tr = x_ref.shape[0]
    h = _rmsnorm(x_ref[...], g_ref[...]) * (1.0 + m_ref[0, 4:5, :]) + m_ref[0, 3:4, :]
    hb = h.astype(BF16)
    hl = (h - hb.astype(F32)).astype(BF16)
    logits = _dot(hb, whi_ref[...]) + _dot(hb, wlo_ref[...]) + _dot(hl, whi_ref[...]) + br_ref[...]
    lane = lax.broadcasted_iota(jnp.int32, (tr, LANES), 1)
    lanef = lane.astype(F32)
    lg = jnp.where(lane < MOE_GROUPS, logits, NEG)
    mg = jnp.max(lg, axis=-1, keepdims=True)
    g_p = 1.0 / jnp.sum(jnp.exp(lg - mg), axis=-1, keepdims=True)
    gidx = jnp.min(jnp.where(lg == mg, lanef, float(LANES)), axis=-1, keepdims=True)
    lo = MOE_GROUPS + MOE_EPG * gidx
    le = jnp.where((lanef >= lo) & (lanef < lo + MOE_EPG), logits, NEG)
    m1 = jnp.max(le, axis=-1, keepdims=True)
    i1 = jnp.min(jnp.where(le == m1, lanef, float(LANES)), axis=-1, keepdims=True)
    le2 = jnp.where(lanef == i1, NEG, le)
    m2 = jnp.max(le2, axis=-1, keepdims=True)
    i2 = jnp.min(jnp.where(le2 == m2, lanef, float(LANES)), axis=-1, keepdims=True)
    ratio = jnp.exp(m2 - m1)
    w1 = g_p / (1.0 + ratio)
    w2 = g_p * ratio / (1.0 + ratio)
    hot1 = lanef == i1
    hot2 = lanef == i2
    hot = jnp.where(hot1, 1.0, jnp.where(hot2, 1.0, 0.0))
    r_i = lax.broadcasted_iota(jnp.int32, (tr, tr), 0)
    c_i = lax.broadcasted_iota(jnp.int32, (tr, tr), 1)
    rank = _dot(jnp.where(c_i < r_i, 1.0, 0.0).astype(BF16), hot.astype(BF16))
    cnt = jnp.sum(hot, axis=0, keepdims=True)
    nblk = jnp.floor((cnt + (SLOT_BLOCK - 1)) * (1.0 / SLOT_BLOCK))
    l_r = lax.broadcasted_iota(jnp.int32, (LANES, LANES), 0)
    l_c = lax.broadcasted_iota(jnp.int32, (LANES, LANES), 1)
    before = jnp.where(l_r < l_c, 1.0, 0.0).astype(BF16)
    off = SLOT_BLOCK * _dot(jnp.broadcast_to(nblk, (8, LANES)).astype(BF16), before)[0:1]
    posm = off + rank
    pos1 = jnp.sum(jnp.where(hot1, posm, 0.0), axis=-1, keepdims=True)
    pos2 = jnp.sum(jnp.where(hot2, posm, 0.0), axis=-1, keepdims=True)
    col = lax.broadcasted_iota(jnp.int32, (tr, slot_rows), 1)
    sel = jnp.where(col == pos1.astype(jnp.int32), 1.0, jnp.where(col == pos2.astype(jnp.int32), 1.0, 0.0))
    xs_ref[...] = lax.dot_general(sel.astype(BF16), hb, TN, preferred_element_type=F32).astype(BF16)
    info_ref[...] = jnp.where(lane == 0, pos1, jnp.where(lane == 1, pos2, jnp.where(lane == 2, w1,
                                                                                       jnp.where(lane == 3, w2, 0.0))))
    cnt_ref[0] = jnp.broadcast_to(cnt, (8, LANES))


def _moe_route(x2d, gain, mods, tiles_per_mod, w_hi, w_lo, b_r, tr):
    t, d = x2d.shape
    nt = t // tr
    sr = _slot_rows(tr)
    return pl.pallas_call(
        functools.partial(_route_kernel, slot_rows=sr),
        grid=(nt,),
        in_specs=[
            pl.BlockSpec((tr, d), lambda i: (i, 0)),
            pl.BlockSpec((1, d), lambda i: (0, 0)),
            pl.BlockSpec((1, 6, d), lambda i: (i // tiles_per_mod, 0, 0)),
            pl.BlockSpec((d, LANES), lambda i: (0, 0)),
            pl.BlockSpec((d, LANES), lambda i: (0, 0)),
            pl.BlockSpec((1, LANES), lambda i: (0, 0)),
        ],
        out_specs=[
            pl.BlockSpec((sr, d), lambda i: (i, 0)),
            pl.BlockSpec((tr, LANES), lambda i: (i, 0)),
            pl.BlockSpec((1, 8, LANES), lambda i: (i, 0, 0)),
        ],
        out_shape=[
            jax.ShapeDtypeStruct((nt * sr, d), BF16),
            jax.ShapeDtypeStruct((t, LANES), F32),
            jax.ShapeDtypeStruct((nt, 8, LANES), F32),
        ],
        compiler_params=_cparams("parallel"),
        name="moe_route",
    )(x2d, gain.reshape(1, d), mods, w_hi, w_lo, b_r)


def _expert_tables(cnt, blocks_per_tile, chunk_blocks, nsteps):
    ntiles = cnt.shape[0]
    nb = (cnt + (SLOT_BLOCK - 1)) // SLOT_BLOCK
    first = jnp.cumsum(nb, axis=1) - nb
    cum = jnp.cumsum(nb.T, axis=1)
    tot = cum[:, -1]
    totp = (tot + (STEP_BLOCKS - 1)) // STEP_BLOCKS * STEP_BLOCKS
    ends = jnp.cumsum(totp)
    j = jnp.arange(nsteps * STEP_BLOCKS, dtype=jnp.int32)
    e_raw = jnp.sum(ends[None, :] <= j[:, None], axis=1).astype(jnp.int32)
    e = jnp.minimum(e_raw, MOE_EXPERTS - 1)
    q = j - (ends - totp)[e]
    real = (e_raw < MOE_EXPERTS) & (q < tot[e])
    ti = jnp.minimum(jnp.sum(cum[e] <= q[:, None], axis=1), ntiles - 1).astype(jnp.int32)
    prev = jnp.where(ti > 0, cum[e, jnp.maximum(ti - 1, 0)], 0)
    loc = first[ti, e] + (q - prev)
    src = jnp.where(real, ti * blocks_per_tile + loc, 0).astype(jnp.int32)
    n_inv = ntiles * chunk_blocks
    inv = jnp.zeros((n_inv + 1,), jnp.int32).at[jnp.where(real, ti * chunk_blocks + loc, n_inv)].set(j)[:n_inv]
    return src, e[::STEP_BLOCKS], real[::STEP_BLOCKS].astype(jnp.int32), inv


def _experts_kernel(src_ref, exp_ref, valid_ref, *refs):
    x_refs = refs[:STEP_BLOCKS]
    wgu_ref, wd_ref, y_ref = refs[STEP_BLOCKS:]
    s = pl.program_id(0)

    @pl.when(valid_ref[s] > 0)
    def _():
        x = jnp.concatenate([r[0] for r in x_refs], axis=0)
        hgu = _dot(x, wgu_ref[0])
        a = _silu(hgu[:, :MOE_HIDDEN]) * hgu[:, MOE_HIDDEN:]
        y_ref[...] = _dot(a.astype(BF16), wd_ref[0]).astype(BF16)

    @pl.when(valid_ref[s] == 0)
    def _():
        y_ref[...] = jnp.zeros(y_ref.shape, y_ref.dtype)


def _moe_experts(xs, src, step_e, valid, w_gu, w_d, nsteps):
    rows, d = xs.shape
    xs3 = xs.reshape(rows // SLOT_BLOCK, SLOT_BLOCK, d)
    step_rows = STEP_BLOCKS * SLOT_BLOCK
    in_blk = lambda kk: pl.BlockSpec((1, SLOT_BLOCK, d), lambda s, sr, ex, va: (sr[s * STEP_BLOCKS + kk], 0, 0))
    grid_spec = pltpu.PrefetchScalarGridSpec(
        num_scalar_prefetch=3,
        grid=(nsteps,),
        in_specs=[in_blk(kk) for kk in range(STEP_BLOCKS)] + [
            pl.BlockSpec((1,) + w_gu.shape[1:], lambda s, sr, ex, va: (ex[s], 0, 0)),
            pl.BlockSpec((1,) + w_d.shape[1:], lambda s, sr, ex, va: (ex[s], 0, 0)),
        ],
        out_specs=pl.BlockSpec((step_rows, d), lambda s, sr, ex, va: (s, 0)),
    )
    return pl.pallas_call(
        _experts_kernel,
        grid_spec=grid_spec,
        out_shape=jax.ShapeDtypeStruct((nsteps * step_rows, d), BF16),
        compiler_params=_cparams("arbitrary"),
        name="moe_experts",
    )(src, step_e, valid, *([xs3] * STEP_BLOCKS), w_gu, w_d)


def _combine_kernel(inv_ref, x_ref, info_ref, m_ref, fg_ref, *refs, final):
    y_refs = refs[:STEP_BLOCKS]
    o_ref, acc_ref = refs[STEP_BLOCKS:]
    ck = pl.program_id(1)
    tr = x_ref.shape[0]
    rows = STEP_BLOCKS * SLOT_BLOCK

    @pl.when(ck == 0)
    def _():
        acc_ref[...] = jnp.zeros(acc_ref.shape, F32)

    info = info_ref[...]
    col = ck * rows + lax.broadcasted_iota(jnp.int32, (tr, rows), 1)
    wsel = jnp.where(col == info[:, 0:1].astype(jnp.int32), info[:, 2:3],
                     jnp.where(col == info[:, 1:2].astype(jnp.int32), info[:, 3:4], 0.0))
    acc_ref[...] += _dot(wsel.astype(BF16), jnp.concatenate([r[0] for r in y_refs], axis=0))

    @pl.when(ck == pl.num_programs(1) - 1)
    def _():
        out = x_ref[...] + m_ref[0, 5:6, :] * acc_ref[...]
        if final:
            out = _rmsnorm(out, fg_ref[...])
        o_ref[...] = out


def _moe_combine(x2d, ys, inv, info, mods, tiles_per_mod, final_g, tr, nchunks, final):
    t, d = x2d.shape
    ys3 = ys.reshape(ys.shape[0] // SLOT_BLOCK, SLOT_BLOCK, d)
    y_blk = lambda kk: pl.BlockSpec((1, SLOT_BLOCK, d),
                                    lambda i, ck, iv: (iv[(i * nchunks + ck) * STEP_BLOCKS + kk], 0, 0))
    grid_spec = pltpu.PrefetchScalarGridSpec(
        num_scalar_prefetch=1,
        grid=(t // tr, nchunks),
        in_specs=[
            pl.BlockSpec((tr, d), lambda i, ck, iv: (i, 0)),
            pl.BlockSpec((tr, LANES), lambda i, ck, iv: (i, 0)),
            pl.BlockSpec((1, 6, d), lambda i, ck, iv: (i // tiles_per_mod, 0, 0)),
            pl.BlockSpec((1, d), lambda i, ck, iv: (0, 0)),
        ] + [y_blk(kk) for kk in range(STEP_BLOCKS)],
        out_specs=pl.BlockSpec((tr, d), lambda i, ck, iv: (i, 0)),
        scratch_shapes=[pltpu.VMEM((tr, d), F32)],
    )
    return pl.pallas_call(
        functools.partial(_combine_kernel, final=final),
        grid_spec=grid_spec,
        out_shape=jax.ShapeDtypeStruct((t, d), F32),
        compiler_params=_cparams("parallel", "arbitrary"),
        name="moe_combine",
    )(inv, x2d, info, mods, final_g.reshape(1, d), *([ys3] * STEP_BLOCKS))


def _moe(x, gain, mods, params, final_g, final, tr):
    w_hi, w_lo, b_r, w_gu, w_d = params
    b, n, d = x.shape
    tr = min(tr, n)
    t = b * n
    nt = t // tr
    x2d = x.reshape(t, d)
    xs, info, cnt = _moe_route(x2d, gain, mods, n // tr, w_hi, w_lo, b_r, tr)
    bpt = _slot_rows(tr) // SLOT_BLOCK
    nchunks = -(-bpt // STEP_BLOCKS)
    nsteps = -(-(nt * bpt + MOE_EXPERTS * (STEP_BLOCKS - 1)) // STEP_BLOCKS)
    counts = cnt[:, 0, MOE_GROUPS:MOE_GROUPS + MOE_EXPERTS].astype(jnp.int32)
    src, step_e, valid, inv = _expert_tables(counts, bpt, nchunks * STEP_BLOCKS, nsteps)
    ys = _moe_experts(xs, src, step_e, valid, w_gu, w_d, nsteps)
    out = _moe_combine(x2d, ys, inv, info, mods, n // tr, final_g, tr, nchunks, final)
    return out.reshape(b, n, d)


def _mla_proj_kernel(cq_ref, ckv_ref, kr_ref, krp_ref, cos_ref, sin_ref, qg_ref, kg_ref, wq_ref, wqp_ref, wk_ref,
                     wv_ref, vone_ref, *o_refs, need_q, q_scale):
    cos = cos_ref[...]
    sin = sin_ref[...]
    ckv = _rmsnorm(ckv_ref[0], kg_ref[...]).astype(BF16)
    k_rope = kr_ref[0] * cos + krp_ref[0] * sin
    kn = _dot(ckv, wk_ref[...])
    if need_q:
        q_ref, k_ref, v_ref = o_refs
    else:
        k_ref, v_ref = o_refs
    vx = _dot(ckv, wv_ref[...]) + vone_ref[...]
    for h in range(D_HEADS):
        hs = slice(h * MLA_PAD, (h + 1) * MLA_PAD)
        k_ref[0, h] = (kn[:, hs] + k_rope).astype(BF16)
        v_ref[0, h] = vx[:, hs].T.astype(BF16)
    if need_q:
        cq = _rmsnorm(cq_ref[0], qg_ref[...]).astype(BF16)
        qm = _dot(cq, wq_ref[...])
        qp = _dot(cq, wqp_ref[...])
        for h in range(D_HEADS):
            hs = slice(h * MLA_PAD, (h + 1) * MLA_PAD)
            q_ref[0, h] = ((qm[:, hs] * cos + qp[:, hs] * sin) * q_scale).T.astype(BF16)


def _mla_proj(u_b, cos, sin, q_g, kv_g, wq, wqp, wk, wv, need_q, tm):
    b, n, _ = u_b.shape
    tm = min(tm, n)
    row_major = (jax.ShapeDtypeStruct((b, D_HEADS, n, MLA_PAD), BF16),
                 pl.BlockSpec((1, D_HEADS, tm, MLA_PAD), lambda bi, i: (bi, 0, i, 0)))
    col_major = (jax.ShapeDtypeStruct((b, D_HEADS, MLA_PAD, n), BF16),
                 pl.BlockSpec((1, D_HEADS, MLA_PAD, tm), lambda bi, i: (bi, 0, 0, i)))
    outs, specs = zip(*(([col_major] if need_q else []) + [row_major, col_major]))
    full = lambda a: pl.BlockSpec(a.shape, lambda bi, i: (0,) * a.ndim)
    vone = jnp.tile(jnp.concatenate([jnp.zeros((1, MLA_V), F32), jnp.ones((1, MLA_PAD - MLA_V), F32)], axis=1),
                    (1, D_HEADS))
    q_scale = float((MLA_NOPE + MLA_ROPE) ** -0.5 * np.log2(np.e))
    return pl.pallas_call(
        functools.partial(_mla_proj_kernel, need_q=need_q, q_scale=q_scale),
        grid=(b, n // tm),
        in_specs=[
            pl.BlockSpec((1, tm, MLA_Q_RANK), lambda bi, i: (bi, i, 0)),
            pl.BlockSpec((1, tm, MLA_KV_RANK), lambda bi, i: (bi, i, 2)),
            pl.BlockSpec((1, tm, MLA_PAD), lambda bi, i: (bi, i, 3)),
            pl.BlockSpec((1, tm, MLA_PAD), lambda bi, i: (bi, i, 4)),
            pl.BlockSpec((tm, MLA_PAD), lambda bi, i: (i, 0)),
            pl.BlockSpec((tm, MLA_PAD), lambda bi, i: (i, 0)),
            full(q_g), full(kv_g), full(wq), full(wqp), full(wk), full(wv), full(vone),
        ],
        out_specs=list(specs),
        out_shape=list(outs),
        compiler_params=_cparams("parallel", "parallel"),
        name="mla_proj",
    )(u_b, u_b, u_b, u_b, cos, sin, q_g, kv_g, wq, wqp, wk, wv, vone)


def _mla_attn_kernel(q_ref, k_ref, v_ref, o_ref, acc_ref, s_ref, *, tk):
    tq = q_ref.shape[3]
    nk = k_ref.shape[2]
    acc_ref[...] = jnp.zeros(acc_ref.shape, F32)

    def scores(c, ms):
        ks = pl.ds(pl.multiple_of(c * tk, tk), tk)
        new = []
        for hh in range(2):
            s = _dot(k_ref[0, hh, ks, :], q_ref[0, hh])
            s_ref[hh, ks, :] = s
            new.append(jnp.maximum(ms[hh], jnp.max(s.reshape(tk // 8, 8, tq), axis=0)))
        return tuple(new)

    ms = lax.fori_loop(0, nk // tk, scores, (jnp.full((8, tq), NEG, F32),) * 2)
    ms = [jnp.max(m, axis=0, keepdims=True) for m in ms]

    def weighted(c, carry):
        ks = pl.ds(pl.multiple_of(c * tk, tk), tk)
        for hh in range(2):
            p = jnp.exp2(s_ref[hh, ks, :] - ms[hh]).astype(BF16)
            acc_ref[hh] += _dot(v_ref[0, hh, :, ks], p)
        return carry

    lax.fori_loop(0, nk // tk, weighted, 0)
    o_t = jnp.concatenate([acc_ref[hh, :MLA_V] / acc_ref[hh, MLA_V:MLA_V + 1] for hh in range(2)], axis=0)
    o_ref[0] = o_t.T.astype(o_ref.dtype)


def _mla_attention(q_t, k, v_t, tq, tk):
    b, h, _, n = q_t.shape
    nk = k.shape[2]
    tq = min(tq, n)
    tk = max(t for t in range(2 * LANES, tk + 1, 2 * LANES) if nk % t == 0)
    return pl.pallas_call(
        functools.partial(_mla_attn_kernel, tk=tk),
        grid=(b, h // 2, n // tq),
        in_specs=[
            pl.BlockSpec((1, 2, MLA_PAD, tq), lambda bi, hp, i: (bi, hp, 0, i)),
            pl.BlockSpec((1, 2, nk, MLA_PAD), lambda bi, hp, i: (bi, hp, 0, 0)),
            pl.BlockSpec((1, 2, MLA_PAD, nk), lambda bi, hp, i: (bi, hp, 0, 0)),
        ],
        out_specs=pl.BlockSpec((1, tq, 2 * MLA_V), lambda bi, hp, i: (bi, i, hp)),
        out_shape=jax.ShapeDtypeStruct((b, n, h * MLA_V), BF16),
        scratch_shapes=[pltpu.VMEM((2, MLA_PAD, tq), F32), pltpu.VMEM((2, nk, tq), F32)],
        compiler_params=_cparams("parallel", "parallel", "arbitrary"),
        name="mla_attention",
    )(q_t, k, v_t)


def _na_kernel(q_ref, kp_ref, km_ref, kn_ref, kc_ref, vp_ref, vm_ref, vn_ref, vc_ref, tab_ref, o_ref):
    tq = q_ref.shape[1]
    nloc = 3 * tq
    lane = lax.broadcasted_iota(jnp.int32, (tq, LANES), 1)
    q = q_ref[0]
    k_all = jnp.concatenate([kp_ref[0], km_ref[0], kn_ref[0], kc_ref[0]], axis=0)
    v_all = jnp.concatenate([vp_ref[0], vm_ref[0], vn_ref[0], vc_ref[0]], axis=0)
    outs = []
    for hh in range(2):
        in_head = (lane >= hh * C_HEAD_DIM) & (lane < (hh + 1) * C_HEAD_DIM)
        qh = jnp.where(in_head, q, jnp.zeros_like(q))
        s = lax.dot_general(qh, k_all, NT, preferred_element_type=F32)
        s_loc = s[:, :nloc] + tab_ref[0, hh]
        s_ctx = s[:, nloc:]
        m = jnp.maximum(jnp.max(s_loc, axis=-1, keepdims=True), jnp.max(s_ctx, axis=-1, keepdims=True))
        p_loc = jnp.exp(s_loc - m)
        p_ctx = jnp.exp(s_ctx - m)
        l = jnp.sum(p_loc, axis=-1, keepdims=True) + jnp.sum(p_ctx, axis=-1, keepdims=True)
        o = _dot(p_loc.astype(BF16), v_all[:nloc]) + _dot(p_ctx.astype(BF16), v_all[nloc:])
        outs.append(o / l)
    o_ref[0] = jnp.where(lane < C_HEAD_DIM, outs[0], outs[1]).astype(o_ref.dtype)


def _na_tables(rpb, rows):
    h = rpb.shape[0]
    w = GRID_W
    qc = np.arange(w)
    kc = np.arange(w)
    cs = np.clip(qc - NA_COLS // 2, 0, w - NA_COLS)
    col_ok = (kc[None, :] >= cs[:, None]) & (kc[None, :] < cs[:, None] + NA_COLS)
    dc = np.clip(kc[None, :] - qc[:, None] + (NA_COLS - 1), 0, 2 * NA_COLS - 2)
    base = jnp.where(col_ok[None, None], rpb.astype(F32)[:, :, dc], NEG)
    base = jnp.concatenate([base, jnp.full((h, 1, w, w), NEG, F32)], axis=1)
    nblk = rows // NA_QROWS
    tabs = []
    for m in (0, 1, nblk - 1):
        qr = NA_QROWS * m + np.arange(NA_QROWS)
        rs = np.clip(qr - NA_ROWS // 2, 0, rows - NA_ROWS)
        kr = NA_QROWS * (m - 1) + np.arange(3 * NA_QROWS)
        ok = (kr[None, :] >= rs[:, None]) & (kr[None, :] < rs[:, None] + NA_ROWS)
        dr = np.where(ok, kr[None, :] - qr[:, None] + (NA_ROWS - 1), 2 * NA_ROWS - 1)
        t = base[:, dr]
        tabs.append(t.transpose(0, 1, 3, 2, 4).reshape(h, NA_QROWS * w, 3 * NA_QROWS * w))
    return jnp.stack(tabs)


def _na_attention(u_lat, u_ctx, tabs):
    b, n, _ = u_lat.shape
    nc = u_ctx.shape[1]
    tq = NA_QROWS * GRID_W
    nblk = n // tq
    pairs = C_HEADS // 2
    prev = lambda i: jnp.maximum(i - 1, 0)
    nxt = lambda i: jnp.minimum(i + 1, nblk - 1)
    blk = lambda col0, f: pl.BlockSpec((1, tq, LANES), lambda bi, hp, i: (bi, f(i), col0 + hp))
    ctx = lambda col0: pl.BlockSpec((1, nc, LANES), lambda bi, hp, i: (bi, 0, col0 + hp))
    same = lambda i: i
    sel = lambda i: jnp.where(i == 0, 0, jnp.where(i == nblk - 1, 2, 1))
    return pl.pallas_call(
        _na_kernel,
        grid=(b, pairs, nblk),
        in_specs=[
            blk(0, same),
            blk(pairs, prev), blk(pairs, same), blk(pairs, nxt), ctx(pairs),
            blk(2 * pairs, prev), blk(2 * pairs, same), blk(2 * pairs, nxt), ctx(2 * pairs),
            pl.BlockSpec((1, 2, tq, 3 * tq), lambda bi, hp, i: (sel(i), hp, 0, 0)),
        ],
        out_specs=pl.BlockSpec((1, tq, LANES), lambda bi, hp, i: (bi, i, hp)),
        out_shape=jax.ShapeDtypeStruct((b, n, C_WIDTH), BF16),
        compiler_params=_cparams("parallel", "parallel", "arbitrary"),
        name="na_attention",
    )(u_lat, u_lat, u_lat, u_lat, u_ctx, u_lat, u_lat, u_lat, u_ctx, tabs)


def _cd_out_kernel(c_ref, d_ref, x_ref, m_ref, wo_ref, o_ref):
    wc = c_ref.shape[2]
    o = _dot(c_ref[0], wo_ref[:wc]) + _dot(d_ref[0], wo_ref[wc:])
    o_ref[0] = x_ref[0] + m_ref[0, 2:3, :] * o


def _cd_out(c_lat, d_lat, x, mods, w_out, tm):
    b, n, d = x.shape
    tm = min(tm, n)
    return pl.pallas_call(
        _cd_out_kernel,
        grid=(b, n // tm),
        in_specs=[
            pl.BlockSpec((1, tm, c_lat.shape[2]), lambda bi, i: (bi, i, 0)),
            pl.BlockSpec((1, tm, d_lat.shape[2]), lambda bi, i: (bi, i, 0)),
            pl.BlockSpec((1, tm, d), lambda bi, i: (bi, i, 0)),
            pl.BlockSpec((1, 6, d), lambda bi, i: (bi, 0, 0)),
            pl.BlockSpec(w_out.shape, lambda bi, i: (0, 0)),
        ],
        out_specs=pl.BlockSpec((1, tm, d), lambda bi, i: (bi, i, 0)),
        out_shape=jax.ShapeDtypeStruct((b, n, d), F32),
        compiler_params=_cparams("parallel", "parallel"),
        name="cd_out",
    )(c_lat, d_lat, x, mods, w_out)


def _moe_params(w_rg, b_rg, w_re, b_re, w_gate, w_up, w_down):
    d = w_rg.shape[0]
    w_r = jnp.zeros((d, LANES), F32).at[:, :MOE_GROUPS].set(w_rg).at[:, MOE_GROUPS:MOE_GROUPS + MOE_EXPERTS].set(w_re)
    b_r = jnp.zeros((1, LANES), F32).at[0, :MOE_GROUPS].set(b_rg).at[0, MOE_GROUPS:MOE_GROUPS + MOE_EXPERTS].set(b_re)
    w_hi = w_r.astype(BF16)
    w_lo = (w_r - w_hi.astype(F32)).astype(BF16)
    f = w_gate.shape[-1]
    w_gu = jnp.concatenate([w_gate, w_up], axis=-1).reshape(MOE_EXPERTS, d, 2 * f).astype(BF16)
    w_d = w_down.reshape(MOE_EXPERTS, f, d).astype(BF16)
    return w_hi, w_lo, b_r, w_gu, w_d


def _rope_perm():
    j = np.arange(MLA_ROPE)
    half = MLA_ROPE // 2
    return (j // half) * half + (j % half + half // 2) % half


def _rope_tables(n):
    half = MLA_ROPE // 2
    nf = half // 2
    t = jnp.arange(n)
    inv = ROPE_THETA ** (-jnp.arange(nf, dtype=F32) / nf)
    parts_c, parts_s = [], []
    for pos in ((t // GRID_W).astype(F32), (t % GRID_W).astype(F32)):
        ang = pos[:, None] * inv[None, :]
        c, s = jnp.cos(ang), jnp.sin(ang)
        parts_c += [c, c]
        parts_s += [-s, s]
    pad = MLA_PAD - MLA_NOPE - MLA_ROPE
    cos = jnp.concatenate([jnp.ones((n, MLA_NOPE), F32)] + parts_c + [jnp.zeros((n, pad), F32)], axis=1)
    sin = jnp.concatenate([jnp.zeros((n, MLA_NOPE), F32)] + parts_s + [jnp.zeros((n, pad), F32)], axis=1)
    return cos, sin


def _identity_rope_tables(n):
    pad = MLA_PAD - MLA_NOPE - MLA_ROPE
    cos = jnp.concatenate([jnp.ones((n, MLA_NOPE + MLA_ROPE), F32), jnp.zeros((n, pad), F32)], axis=1)
    return cos, jnp.zeros((n, MLA_PAD), F32)


def _pad_heads(w, widths, src_cols, dst_off):
    rank = w.shape[0]
    out = jnp.zeros((rank, D_HEADS, MLA_PAD), F32)
    wh = w.reshape(rank, D_HEADS, widths)[:, :, src_cols]
    return out.at[:, :, dst_off:dst_off + len(src_cols)].set(wh).reshape(rank, D_HEADS * MLA_PAD)


def _cd_params(w_in, w_uq, w_ukv):
    d = w_in.shape[0]
    perm = _rope_perm()
    o = 3 * C_WIDTH
    q_scale = float(C_HEAD_DIM ** -0.5)
    kr = w_in[:, o + MLA_Q_RANK + MLA_KV_RANK:]
    pad_rope = lambda a: jnp.zeros((d, MLA_PAD), F32).at[:, MLA_NOPE:MLA_NOPE + MLA_ROPE].set(a)
    w_cat = jnp.concatenate([
        w_in[:, :C_WIDTH] * q_scale, w_in[:, C_WIDTH:o],
        w_in[:, o:o + MLA_Q_RANK + MLA_KV_RANK], pad_rope(kr), pad_rope(kr[:, perm]),
    ], axis=1).astype(BF16)
    qw = MLA_NOPE + MLA_ROPE
    nope = np.arange(MLA_NOPE)
    rope = MLA_NOPE + np.arange(MLA_ROPE)
    wq = (_pad_heads(w_uq, qw, nope, 0) + _pad_heads(w_uq, qw, rope, MLA_NOPE)).astype(BF16)
    wqp = _pad_heads(w_uq, qw, rope[perm], MLA_NOPE).astype(BF16)
    kvw = MLA_NOPE + MLA_V
    wk = _pad_heads(w_ukv, kvw, nope, 0).astype(BF16)
    wv = _pad_heads(w_ukv, kvw, MLA_NOPE + np.arange(MLA_V), 0).astype(BF16)
    return w_cat, wq, wqp, wk, wv


def kernel(x, c, ctx, c_ctx, ada_w, ada_b, norm1_g, norm2_g, ab_w_in, ab_w_out, hgrn_lb_logits, hgrn_onorm_g, pool_w,
           pool_scale, cd_w_in, cd_w_out, na_rpb, mla_q_norm_g, mla_w_uq, mla_kv_norm_g, mla_w_ukv, moe_w_rg, moe_b_rg,
           moe_w_re, moe_b_re, moe_w_gate, moe_w_up, moe_w_down, final_norm_g):
    b, n, d = x.shape
    n_ctx = ctx.shape[1]
    assert ada_w.shape[0] == 2 and ab_w_in.shape[0] == 1 and cd_w_in.shape[0] == 1 and b + 1 <= 8
    tm = 512

    cc = jnp.zeros((8, d), F32).at[:b].set(c).at[b].set(c_ctx)
    mods = _ada(cc, ada_w, ada_b).reshape(2, 8, 6, d)
    mods_lat = [mods[l, :b] for l in range(2)]
    mods_ctx = [jnp.broadcast_to(mods[l, b:b + 1], (b, 6, d)) for l in range(2)]
    lb = jnp.cumsum(jax.nn.softmax(hgrn_lb_logits.astype(F32), axis=1), axis=1)[:, 0]

    w_in0 = ab_w_in[0].astype(BF16)
    w_out0 = ab_w_out[0].astype(BF16)
    pw0 = pool_w[0].astype(BF16)
    ab_cols = w_in0.shape[1]
    (u_ctx,) = _in_proj(ctx, norm1_g[0], mods_ctx[0], w_in0, ((0, ab_cols),), (F32,), tm)
    (u_lat,) = _in_proj(x, norm1_g[0], mods_lat[0], w_in0, ((0, ab_cols),), (F32,), tm)
    s0 = jnp.zeros((b, 2, A_HEADS, A_HEAD_DIM, A_HEAD_DIM), F32)
    ocf, ocb, s_ctx = _hgrn_scan(u_ctx, lb, s0, 256)
    olf, olb, _ = _hgrn_scan(u_lat, lb, s_ctx, 256)
    xc = _ab_out(ocf, ocb, u_ctx, ctx, mods_ctx[0], hgrn_onorm_g[0], pw0, pool_scale[0], w_out0, tm)
    x = _ab_out(olf, olb, u_lat, x, mods_lat[0], hgrn_onorm_g[0], pw0, pool_scale[0], w_out0, tm)
    moe0 = _moe_params(moe_w_rg[0], moe_b_rg[0], moe_w_re[0], moe_b_re[0], moe_w_gate[0], moe_w_up[0], moe_w_down[0])
    xc = _moe(xc, norm2_g[0], mods_ctx[0], moe0, final_norm_g, False, tm)
    x = _moe(x, norm2_g[0], mods_lat[0], moe0, final_norm_g, False, tm)

    w_cat, wq, wqp, wk, wv = _cd_params(cd_w_in[0], mla_w_uq[0], mla_w_ukv[0])
    na_w = 3 * C_WIDTH
    splits = ((0, na_w), (na_w, w_cat.shape[1]))
    ua_ctx, ub_ctx = _in_proj(xc, norm1_g[1], mods_ctx[1], w_cat, splits, (BF16, F32), tm)
    ua_lat, ub_lat = _in_proj(x, norm1_g[1], mods_lat[1], w_cat, splits, (BF16, F32), tm)
    q_g = mla_q_norm_g[0].reshape(1, -1)
    kv_g = mla_kv_norm_g[0].reshape(1, -1)
    cos_l, sin_l = _rope_tables(n)
    cos_c, sin_c = _identity_rope_tables(n_ctx)
    k_c, v_c = _mla_proj(ub_ctx, cos_c, sin_c, q_g, kv_g, wq, wqp, wk, wv, False, tm)
    q_l, k_l, v_l = _mla_proj(ub_lat, cos_l, sin_l, q_g, kv_g, wq, wqp, wk, wv, True, tm)
    d_lat = _mla_attention(q_l, jnp.concatenate([k_c, k_l], axis=2), jnp.concatenate([v_c, v_l], axis=3), 256, 768)
    c_lat = _na_attention(ua_lat, ua_ctx, _na_tables(na_rpb[0], n // GRID_W))
    x = _cd_out(c_lat, d_lat, x, mods_lat[1], cd_w_out[0].astype(BF16), tm)
    moe1 = _moe_params(moe_w_rg[1], moe_b_rg[1], moe_w_re[1], moe_b_re[1], moe_w_gate[1], moe_w_up[1], moe_w_down[1])
    return _moe(x, norm2_g[1], mods_lat[1], moe1, final_norm_g, True, tm)
```

```python
import functools

import numpy as np
import jax
import jax.numpy as jnp
from jax import lax
from jax.experimental import pallas as pl
from jax.experimental.pallas import tpu as pltpu

F32 = jnp.float32
BF16 = jnp.bfloat16

EPS = 1e-6
NEG = -1e30

GRID_W = 64
A_HEADS = 4
A_HEAD_DIM = 128
A_WIDTH = A_HEADS * A_HEAD_DIM
POOL_WINDOWS = (2, 4, 8, 16)
B_GROUP = 128
B_WIDTH = B_GROUP * len(POOL_WINDOWS)
POOL_HALO = 16
C_HEADS = 8
C_HEAD_DIM = 64
C_WIDTH = C_HEADS * C_HEAD_DIM
NA_ROWS = 8
NA_COLS = 16
NA_QROWS = 4
D_HEADS = 8
MLA_Q_RANK = 256
MLA_KV_RANK = 128
MLA_NOPE = 64
MLA_ROPE = 32
MLA_V = 64
MLA_PAD = 128
ROPE_THETA = 10000.0
MOE_GROUPS = 4
MOE_EPG = 8
MOE_EXPERTS = MOE_GROUPS * MOE_EPG
MOE_HIDDEN = 256
LANES = 128
SLOT_BLOCK = 16
STEP_BLOCKS = 32
VMEM_LIMIT = 56 * 1024 * 1024

NT = (((1,), (1,)), ((), ()))
TN = (((0,), (0,)), ((), ()))


def _cparams(*sem):
    return pltpu.CompilerParams(dimension_semantics=sem, vmem_limit_bytes=VMEM_LIMIT)


def _sigmoid(x):
    return 1.0 / (1.0 + jnp.exp(-x))


def _silu(x):
    return x * _sigmoid(x)


def _dot(a, b):
    return jnp.dot(a, b, preferred_element_type=F32)


def _rmsnorm(x, g):
    return x * lax.rsqrt(jnp.mean(x * x, axis=-1, keepdims=True) + EPS) * g


def _ada_kernel(c_ref, w_ref, b_ref, o_ref):
    s = _silu(c_ref[...])
    o_ref[0] = jnp.dot(s, w_ref[0], precision=lax.Precision.HIGHEST, preferred_element_type=F32) + b_ref[0]


def _ada(cc, ada_w, ada_b):
    depth, d, n6 = ada_w.shape
    tn = n6 // 4
    return pl.pallas_call(
        _ada_kernel,
        grid=(depth, n6 // tn),
        in_specs=[
            pl.BlockSpec((8, d), lambda l, j: (0, 0)),
            pl.BlockSpec((1, d, tn), lambda l, j: (l, 0, j)),
            pl.BlockSpec((1, 1, tn), lambda l, j: (l, 0, j)),
        ],
        out_specs=pl.BlockSpec((1, 8, tn), lambda l, j: (l, 0, j)),
        out_shape=jax.ShapeDtypeStruct((depth, 8, n6), F32),
        compiler_params=_cparams("parallel", "parallel"),
        name="ada_mod",
    )(cc, ada_w, ada_b.reshape(depth, 1, n6))


def _in_kernel(x_ref, g_ref, m_ref, w_ref, *o_refs, splits):
    h = _rmsnorm(x_ref[0], g_ref[...]) * (1.0 + m_ref[0, 1:2, :]) + m_ref[0, 0:1, :]
    hb = h.astype(BF16)
    for o_ref, (a, b) in zip(o_refs, splits):
        o_ref[0] = _dot(hb, w_ref[:, a:b]).astype(o_ref.dtype)


def _in_proj(x, gain, mods, w, splits, dtypes, tm):
    b, n, d = x.shape
    tm = min(tm, n)
    outs = [jax.ShapeDtypeStruct((b, n, hi - lo), dt) for (lo, hi), dt in zip(splits, dtypes)]
    return pl.pallas_call(
        functools.partial(_in_kernel, splits=splits),
        grid=(b, n // tm),
        in_specs=[
            pl.BlockSpec((1, tm, d), lambda bi, i: (bi, i, 0)),
            pl.BlockSpec((1, d), lambda bi, i: (0, 0)),
            pl.BlockSpec((1, 6, d), lambda bi, i: (bi, 0, 0)),
            pl.BlockSpec(w.shape, lambda bi, i: (0, 0)),
        ],
        out_specs=[pl.BlockSpec((1, tm, hi - lo), lambda bi, i: (bi, i, 0)) for lo, hi in splits],
        out_shape=outs,
        compiler_params=_cparams("parallel", "parallel"),
        name="in_proj",
    )(x, gain.reshape(1, d), mods, w)


HG_SUB = 64


def _hgrn_direction(q_raw, fz, v, lb, st_ref, d, o_ref, reverse):
    rows = q_raw.shape[0]
    c = HG_SUB
    f = lb + (1.0 - lb) * _sigmoid(fz)
    k = 1.0 - f
    g = jnp.log(f)
    q = _silu(q_raw)
    r_i = lax.broadcasted_iota(jnp.int32, (c, c), 0)
    c_i = lax.broadcasted_iota(jnp.int32, (c, c), 1)
    keep = (c_i >= r_i) if reverse else (c_i <= r_i)
    tri = jnp.where(keep, 1.0, 0.0).astype(BF16)
    order = range(rows // c - 1, -1, -1) if reverse else range(rows // c)
    for ci in order:
        sl = slice(ci * c, (ci + 1) * c)
        gc = g[sl]
        g_hi = gc.astype(BF16)
        g_lo = (gc - g_hi.astype(F32)).astype(BF16)
        bc = _dot(tri, g_hi) + _dot(tri, g_lo)
        ref = bc[c // 2:c // 2 + 1]
        tot = bc[0:1] if reverse else bc[c - 1:c]
        qt = q[sl] * jnp.exp(bc - ref)
        kt = k[sl] * jnp.exp(ref - bc)
        qd = (qt * jnp.exp(ref)).astype(BF16)
        kd = (kt * jnp.exp(tot - ref)).astype(BF16)
        qt = qt.astype(BF16)
        kt = kt.astype(BF16)
        vb = v[sl].astype(BF16)
        dec = jnp.exp(tot)
        for h in range(A_HEADS):
            hs = slice(h * A_HEAD_DIM, (h + 1) * A_HEAD_DIM)
            att = lax.dot_general(qt[:, hs], kt[:, hs], NT, preferred_element_type=F32)
            att = jnp.where(keep, att, 0.0).astype(BF16)
            st = st_ref[d, h]
            o = _dot(att, vb[:, hs]) + lax.dot_general(qd[:, hs], st.astype(BF16), NT, preferred_element_type=F32)
            o_ref[0, sl, hs] = o
            st_ref[d, h] = st * dec[:, hs] + lax.dot_general(vb[:, hs], kd[:, hs], TN, preferred_element_type=F32)


def _hgrn_kernel(qf_ref, ff_ref, vf_ref, qb_ref, fb_ref, vb_ref, lb_ref, s0_ref, of_ref, ob_ref, sfin_ref, st_ref):
    j = pl.program_id(1)

    @pl.when(j == 0)
    def _():
        st_ref[...] = s0_ref[0]

    _hgrn_direction(qf_ref[0], ff_ref[0], vf_ref[0], lb_ref[0:1], st_ref, 0, of_ref, False)
    _hgrn_direction(qb_ref[0], fb_ref[0], vb_ref[0], lb_ref[1:2], st_ref, 1, ob_ref, True)

    @pl.when(j == pl.num_programs(1) - 1)
    def _():
        sfin_ref[0] = st_ref[...]


def _hgrn_scan(u, lb, s0, rows):
    b, n, _ = u.shape
    rows = min(rows, n)
    nb = n // rows
    w = A_WIDTH

    def fwd(col):
        return pl.BlockSpec((1, rows, w), lambda bi, j: (bi, j, col))

    def bwd(col):
        return pl.BlockSpec((1, rows, w), lambda bi, j: (bi, nb - 1 - j, col))

    st_spec = pl.BlockSpec((1, 2, A_HEADS, A_HEAD_DIM, A_HEAD_DIM), lambda bi, j: (bi, 0, 0, 0, 0))
    return pl.pallas_call(
        _hgrn_kernel,
        grid=(b, nb),
        in_specs=[fwd(0), fwd(1), fwd(3), bwd(0), bwd(2), bwd(3), pl.BlockSpec((2, w), lambda bi, j: (0, 0)), st_spec],
        out_specs=[
            pl.BlockSpec((1, rows, w), lambda bi, j: (bi, j, 0)),
            pl.BlockSpec((1, rows, w), lambda bi, j: (bi, nb - 1 - j, 0)),
            st_spec,
        ],
        out_shape=[
            jax.ShapeDtypeStruct((b, n, w), F32),
            jax.ShapeDtypeStruct((b, n, w), F32),
            jax.ShapeDtypeStruct(s0.shape, F32),
        ],
        scratch_shapes=[pltpu.VMEM((2, A_HEADS, A_HEAD_DIM, A_HEAD_DIM), F32)],
        compiler_params=_cparams("parallel", "arbitrary"),
        name="hgrn_scan",
    )(u, u, u, u, u, u, lb, s0)


def _ab_out_kernel(of_ref, ob_ref, ug_ref, up_ref, pprev_ref, pnext_ref, x_ref, m_ref, on_ref, pw_ref, ps_ref,
                   wo_ref, o_ref, *, n):
    i = pl.program_id(1)
    tm = x_ref.shape[1]
    o = of_ref[0] + ob_ref[0]
    gate = _silu(ug_ref[0])
    parts = []
    for h in range(A_HEADS):
        hs = slice(h * A_HEAD_DIM, (h + 1) * A_HEAD_DIM)
        parts.append(_rmsnorm(o[:, hs], on_ref[...]) * gate[:, hs])
    main = up_ref[0]
    prev = jnp.where(i > 0, pprev_ref[0], 0.0)
    nxt = jnp.where(i < pl.num_programs(1) - 1, pnext_ref[0], 0.0)
    ext = jnp.concatenate([prev, main, nxt], axis=0)
    ext_rows = tm + 2 * POOL_HALO
    t = i * tm + lax.broadcasted_iota(jnp.int32, (tm, 1), 0)
    for gi, win in enumerate(POOL_WINDOWS):
        gs = slice(gi * B_GROUP, (gi + 1) * B_GROUP)
        acc = ext[:, gs]
        acc = acc + pltpu.roll(acc, 1, 0)
        half = 1
        while 2 * half < win:
            acc = pltpu.roll(acc, half, 0) + pltpu.roll(acc, ext_rows - half, 0)
            half *= 2
        cnt = jnp.minimum(t + (win - win // 2), n) - jnp.maximum(t - win // 2, 0)
        mean = acc[POOL_HALO:POOL_HALO + tm] / cnt.astype(F32)
        pooled = _dot((mean - main[:, gs]).astype(BF16), pw_ref[gi])
        parts.append(pooled * ps_ref[:, gs])
    mix = jnp.concatenate(parts, axis=-1).astype(BF16)
    o_ref[0] = x_ref[0] + m_ref[0, 2:3, :] * _dot(mix, wo_ref[...])


def _ab_out(o_f, o_b, u, x, mods, onorm_g, pool_w, pool_scale, w_out, tm):
    b, n, d = x.shape
    tm = min(tm, n)
    nt = n // tm
    hb = tm // POOL_HALO
    last_halo = n // POOL_HALO - 1
    w = A_WIDTH
    tile = lambda col: pl.BlockSpec((1, tm, w), lambda bi, i: (bi, i, col))
    return pl.pallas_call(
        functools.partial(_ab_out_kernel, n=n),
        grid=(b, nt),
        in_specs=[
            tile(0), tile(0), tile(4), tile(5),
            pl.BlockSpec((1, POOL_HALO, w), lambda bi, i: (bi, jnp.maximum(i * hb - 1, 0), 5)),
            pl.BlockSpec((1, POOL_HALO, w), lambda bi, i: (bi, jnp.minimum((i + 1) * hb, last_halo), 5)),
            pl.BlockSpec((1, tm, d), lambda bi, i: (bi, i, 0)),
            pl.BlockSpec((1, 6, d), lambda bi, i: (bi, 0, 0)),
            pl.BlockSpec((1, A_HEAD_DIM), lambda bi, i: (0, 0)),
            pl.BlockSpec(pool_w.shape, lambda bi, i: (0, 0, 0)),
            pl.BlockSpec((1, B_WIDTH), lambda bi, i: (0, 0)),
            pl.BlockSpec(w_out.shape, lambda bi, i: (0, 0)),
        ],
        out_specs=pl.BlockSpec((1, tm, d), lambda bi, i: (bi, i, 0)),
        out_shape=jax.ShapeDtypeStruct((b, n, d), F32),
        compiler_params=_cparams("parallel", "parallel"),
        name="ab_out",
    )(o_f, o_b, u, u, u, u, x, mods, onorm_g.reshape(1, A_HEAD_DIM), pool_w, pool_scale.reshape(1, B_WIDTH), w_out)


def _slot_rows(tr):
    rows = 2 * tr + MOE_EXPERTS * (SLOT_BLOCK - 1)
    assert rows % SLOT_BLOCK == 0
    return rows


def _route_kernel(x_ref, g_ref, m_ref, whi_ref, wlo_ref, br_ref, xs_ref, info_ref, cnt_ref, *, slot_rows):
    tr = x_ref.shape[0]
    h = _rmsnorm(x_ref[...], g_ref[...]) * (1.0 + m_ref[0, 4:5, :]) + m_ref[0, 3:4, :]
    hb = h.astype(BF16)
    hl = (h - hb.astype(F32)).astype(BF16)
    logits = _dot(hb, whi_ref[...]) + _dot(hb, wlo_ref[...]) + _dot(hl, whi_ref[...]) + br_ref[...]
    lane = lax.broadcasted_iota(jnp.int32, (tr, LANES), 1)
    lanef = lane.astype(F32)
    lg = jnp.where(lane < MOE_GROUPS, logits, NEG)
    mg = jnp.max(lg, axis=-1, keepdims=True)
    g_p = 1.0 / jnp.sum(jnp.exp(lg - mg), axis=-1, keepdims=True)
    gidx = jnp.min(jnp.where(lg == mg, lanef, float(LANES)), axis=-1, keepdims=True)
    lo = MOE_GROUPS + MOE_EPG * gidx
    le = jnp.where((lanef >= lo) & (lanef < lo + MOE_EPG), logits, NEG)
    m1 = jnp.max(le, axis=-1, keepdims=True)
    i1 = jnp.min(jnp.where(le == m1, lanef, float(LANES)), axis=-1, keepdims=True)
    le2 = jnp.where(lanef == i1, NEG, le)
    m2 = jnp.max(le2, axis=-1, keepdims=True)
    i2 = jnp.min(jnp.where(le2 == m2, lanef, float(LANES)), axis=-1, keepdims=True)
    ratio = jnp.exp(m2 - m1)
    w1 = g_p / (1.0 + ratio)
    w2 = g_p * ratio / (1.0 + ratio)
    hot1 = lanef == i1
    hot2 = lanef == i2
    hot = jnp.where(hot1, 1.0, jnp.where(hot2, 1.0, 0.0))
    r_i = lax.broadcasted_iota(jnp.int32, (tr, tr), 0)
    c_i = lax.broadcasted_iota(jnp.int32, (tr, tr), 1)
    rank = _dot(jnp.where(c_i < r_i, 1.0, 0.0).astype(BF16), hot.astype(BF16))
    cnt = jnp.sum(hot, axis=0, keepdims=True)
    nblk = jnp.floor((cnt + (SLOT_BLOCK - 1)) * (1.0 / SLOT_BLOCK))
    l_r = lax.broadcasted_iota(jnp.int32, (LANES, LANES), 0)
    l_c = lax.broadcasted_iota(jnp.int32, (LANES, LANES), 1)
    before = jnp.where(l_r < l_c, 1.0, 0.0).astype(BF16)
    off = SLOT_BLOCK * _dot(jnp.broadcast_to(nblk, (8, LANES)).astype(BF16), before)[0:1]
    posm = off + rank
    pos1 = jnp.sum(jnp.where(hot1, posm, 0.0), axis=-1, keepdims=True)
    pos2 = jnp.sum(jnp.where(hot2, posm, 0.0), axis=-1, keepdims=True)
    info = jnp.where(lane == 0, pos1, jnp.where(lane == 1, pos2, jnp.where(lane == 2, w1, jnp.where(lane == 3, w2, 0.0))))
    info_ref[...] = info
    pos_t = info.T.astype(jnp.int32)
    row = lax.broadcasted_iota(jnp.int32, (slot_rows, tr), 0)
    sel = jnp.where(row == pos_t[0:1], 1.0, jnp.where(row == pos_t[1:2], 1.0, 0.0)).astype(BF16)
    xs_ref[...] = _dot(sel, hb).astype(BF16)
    cnt_ref[0] = jnp.broadcast_to(cnt, (8, LANES))


def _moe_route(x2d, gain, mods, tiles_per_mod, w_hi, w_lo, b_r, tr):
    t, d = x2d.shape
    nt = t // tr
    sr = _slot_rows(tr)
    return pl.pallas_call(
        functools.partial(_route_kernel, slot_rows=sr),
        grid=(nt,),
        in_specs=[
            pl.BlockSpec((tr, d), lambda i: (i, 0)),
            pl.BlockSpec((1, d), lambda i: (0, 0)),
            pl.BlockSpec((1, 6, d), lambda i: (i // tiles_per_mod, 0, 0)),
            pl.BlockSpec((d, LANES), lambda i: (0, 0)),
            pl.BlockSpec((d, LANES), lambda i: (0, 0)),
            pl.BlockSpec((1, LANES), lambda i: (0, 0)),
        ],
        out_specs=[
            pl.BlockSpec((sr, d), lambda i: (i, 0)),
            pl.BlockSpec((tr, LANES), lambda i: (i, 0)),
            pl.BlockSpec((1, 8, LANES), lambda i: (i, 0, 0)),
        ],
        out_shape=[
            jax.ShapeDtypeStruct((nt * sr, d), BF16),
            jax.ShapeDtypeStruct((t, LANES), F32),
            jax.ShapeDtypeStruct((nt, 8, LANES), F32),
        ],
        compiler_params=_cparams("parallel"),
        name="moe_route",
    )(x2d, gain.reshape(1, d), mods, w_hi, w_lo, b_r)


def _tables_kernel(cnt_ref, src_ref, inv_ref, exp_ref, valid_ref, run_ref, *, ntiles, bpt, nsteps):
    def fill(ref, n, val):
        def body(j, carry):
            ref[j] = val
            return carry
        lax.fori_loop(0, n, body, 0)

    fill(src_ref, nsteps * STEP_BLOCKS, 0)
    fill(inv_ref, ntiles * bpt, 0)
    fill(exp_ref, nsteps, 0)
    fill(valid_ref, nsteps, 0)
    fill(run_ref, ntiles, 0)

    def per_expert(e, pos):
        def per_tile(i, p):
            nb = (cnt_ref[i, e] + (SLOT_BLOCK - 1)) // SLOT_BLOCK
            first = run_ref[i]

            def per_block(r, carry):
                src_ref[p + r] = i * bpt + first + r
                inv_ref[i * bpt + first + r] = p + r
                return carry

            lax.fori_loop(0, nb, per_block, 0)
            run_ref[i] = first + nb
            return p + nb

        end = lax.fori_loop(0, ntiles, per_tile, pos)
        end = (end + (STEP_BLOCKS - 1)) // STEP_BLOCKS * STEP_BLOCKS

        def mark(s, carry):
            exp_ref[s] = e
            valid_ref[s] = 1
            return carry

        lax.fori_loop(pos // STEP_BLOCKS, end // STEP_BLOCKS, mark, 0)
        return end

    lax.fori_loop(0, MOE_EXPERTS, per_expert, 0)


def _expert_tables(cnt, bpt, nsteps):
    ntiles = cnt.shape[0]
    smem = pl.BlockSpec(memory_space=pltpu.SMEM)
    i32 = lambda n: jax.ShapeDtypeStruct((n,), jnp.int32)
    return pl.pallas_call(
        functools.partial(_tables_kernel, ntiles=ntiles, bpt=bpt, nsteps=nsteps),
        in_specs=[smem],
        out_specs=[smem] * 4,
        out_shape=[i32(nsteps * STEP_BLOCKS), i32(ntiles * bpt), i32(nsteps), i32(nsteps)],
        scratch_shapes=[pltpu.SMEM((ntiles,), jnp.int32)],
        name="moe_tables",
    )(cnt)


def _experts_kernel(src_ref, exp_ref, valid_ref, *refs):
    x_refs = refs[:STEP_BLOCKS]
    wgu_ref, wd_ref, y_ref = refs[STEP_BLOCKS:]
    s = pl.program_id(0)

    @pl.when(valid_ref[s] > 0)
    def _():
        x = jnp.concatenate([r[0] for r in x_refs], axis=0)
        hgu = _dot(x, wgu_ref[0])
        a = _silu(hgu[:, :MOE_HIDDEN]) * hgu[:, MOE_HIDDEN:]
        y_ref[...] = _dot(a.astype(BF16), wd_ref[0]).astype(BF16)

    @pl.when(valid_ref[s] == 0)
    def _():
        y_ref[...] = jnp.zeros(y_ref.shape, y_ref.dtype)


def _moe_experts(xs, src, step_e, valid, w_gu, w_d, nsteps):
    rows, d = xs.shape
    xs3 = xs.reshape(rows // SLOT_BLOCK, SLOT_BLOCK, d)
    step_rows = STEP_BLOCKS * SLOT_BLOCK
    in_blk = lambda kk: pl.BlockSpec((1, SLOT_BLOCK, d), lambda s, sr, ex, va: (sr[s * STEP_BLOCKS + kk], 0, 0))
    grid_spec = pltpu.PrefetchScalarGridSpec(
        num_scalar_prefetch=3,
        grid=(nsteps,),
        in_specs=[in_blk(kk) for kk in range(STEP_BLOCKS)] + [
            pl.BlockSpec((1,) + w_gu.shape[1:], lambda s, sr, ex, va: (ex[s], 0, 0)),
            pl.BlockSpec((1,) + w_d.shape[1:], lambda s, sr, ex, va: (ex[s], 0, 0)),
        ],
        out_specs=pl.BlockSpec((step_rows, d), lambda s, sr, ex, va: (s, 0)),
    )
    return pl.pallas_call(
        _experts_kernel,
        grid_spec=grid_spec,
        out_shape=jax.ShapeDtypeStruct((nsteps * step_rows, d), BF16),
        compiler_params=_cparams("arbitrary"),
        name="moe_experts",
    )(src, step_e, valid, *([xs3] * STEP_BLOCKS), w_gu, w_d)


def _combine_kernel(inv_ref, x_ref, info_ref, m_ref, fg_ref, *refs, final):
    y_refs, o_ref = refs[:-1], refs[-1]
    tr = x_ref.shape[0]
    info = info_ref[...]
    col = lax.broadcasted_iota(jnp.int32, (tr, len(y_refs) * SLOT_BLOCK), 1)
    wsel = jnp.where(col == info[:, 0:1].astype(jnp.int32), info[:, 2:3],
                     jnp.where(col == info[:, 1:2].astype(jnp.int32), info[:, 3:4], 0.0))
    y = _dot(wsel.astype(BF16), jnp.concatenate([r[0] for r in y_refs], axis=0))
    out = x_ref[...] + m_ref[0, 5:6, :] * y
    if final:
        out = _rmsnorm(out, fg_ref[...])
    o_ref[...] = out


def _moe_combine(x2d, ys, inv, info, mods, tiles_per_mod, final_g, tr, bpt, final):
    t, d = x2d.shape
    ys3 = ys.reshape(ys.shape[0] // SLOT_BLOCK, SLOT_BLOCK, d)
    y_blk = lambda kk: pl.BlockSpec((1, SLOT_BLOCK, d), lambda i, iv: (iv[i * bpt + kk], 0, 0))
    grid_spec = pltpu.PrefetchScalarGridSpec(
        num_scalar_prefetch=1,
        grid=(t // tr,),
        in_specs=[
            pl.BlockSpec((tr, d), lambda i, iv: (i, 0)),
            pl.BlockSpec((tr, LANES), lambda i, iv: (i, 0)),
            pl.BlockSpec((1, 6, d), lambda i, iv: (i // tiles_per_mod, 0, 0)),
            pl.BlockSpec((1, d), lambda i, iv: (0, 0)),
        ] + [y_blk(kk) for kk in range(bpt)],
        out_specs=pl.BlockSpec((tr, d), lambda i, iv: (i, 0)),
    )
    return pl.pallas_call(
        functools.partial(_combine_kernel, final=final),
        grid_spec=grid_spec,
        out_shape=jax.ShapeDtypeStruct((t, d), F32),
        compiler_params=_cparams("parallel"),
        name="moe_combine",
    )(inv, x2d, info, mods, final_g.reshape(1, d), *([ys3] * bpt))


def _moe(x, gain, mods, params, final_g, final, tr):
    w_hi, w_lo, b_r, w_gu, w_d = params
    b, n, d = x.shape
    tr = min(tr, n)
    t = b * n
    nt = t // tr
    x2d = x.reshape(t, d)
    xs, info, cnt = _moe_route(x2d, gain, mods, n // tr, w_hi, w_lo, b_r, tr)
    bpt = _slot_rows(tr) // SLOT_BLOCK
    nsteps = -(-(nt * bpt + MOE_EXPERTS * (STEP_BLOCKS - 1)) // STEP_BLOCKS)
    counts = cnt[:, 0, MOE_GROUPS:MOE_GROUPS + MOE_EXPERTS].astype(jnp.int32)
    src, inv, step_e, valid = _expert_tables(counts, bpt, nsteps)
    ys = _moe_experts(xs, src, step_e, valid, w_gu, w_d, nsteps)
    out = _moe_combine(x2d, ys, inv, info, mods, n // tr, final_g, tr, bpt, final)
    return out.reshape(b, n, d)


def _mla_proj_kernel(cq_ref, ckv_ref, kr_ref, krp_ref, cos_ref, sin_ref, qg_ref, kg_ref, wq_ref, wqp_ref, wk_ref,
                     wv_ref, vone_ref, *o_refs, need_q, q_scale):
    cos = cos_ref[...]
    sin = sin_ref[...]
    ckv = _rmsnorm(ckv_ref[0], kg_ref[...]).astype(BF16)
    k_rope = kr_ref[0] * cos + krp_ref[0] * sin
    kn = _dot(ckv, wk_ref[...])
    if need_q:
        q_ref, k_ref, v_ref = o_refs
    else:
        k_ref, v_ref = o_refs
    vx = _dot(ckv, wv_ref[...]) + vone_ref[...]
    for h in range(D_HEADS):
        hs = slice(h * MLA_PAD, (h + 1) * MLA_PAD)
        k_ref[0, h] = (kn[:, hs] + k_rope).astype(BF16)
        v_ref[0, h] = vx[:, hs].T.astype(BF16)
    if need_q:
        cq = _rmsnorm(cq_ref[0], qg_ref[...]).astype(BF16)
        qm = _dot(cq, wq_ref[...])
        qp = _dot(cq, wqp_ref[...])
        for h in range(D_HEADS):
            hs = slice(h * MLA_PAD, (h + 1) * MLA_PAD)
            q_ref[0, h] = ((qm[:, hs] * cos + qp[:, hs] * sin) * q_scale).T.astype(BF16)


def _mla_proj(u_b, cos, sin, q_g, kv_g, wq, wqp, wk, wv, need_q, tm):
    b, n, _ = u_b.shape
    tm = min(tm, n)
    row_major = (jax.ShapeDtypeStruct((b, D_HEADS, n, MLA_PAD), BF16),
                 pl.BlockSpec((1, D_HEADS, tm, MLA_PAD), lambda bi, i: (bi, 0, i, 0)))
    col_major = (jax.ShapeDtypeStruct((b, D_HEADS, MLA_PAD, n), BF16),
                 pl.BlockSpec((1, D_HEADS, MLA_PAD, tm), lambda bi, i: (bi, 0, 0, i)))
    outs, specs = zip(*(([col_major] if need_q else []) + [row_major, col_major]))
    full = lambda a: pl.BlockSpec(a.shape, lambda bi, i: (0,) * a.ndim)
    vone = jnp.tile(jnp.concatenate([jnp.zeros((1, MLA_V), F32), jnp.ones((1, MLA_PAD - MLA_V), F32)], axis=1),
                    (1, D_HEADS))
    q_scale = float((MLA_NOPE + MLA_ROPE) ** -0.5 * np.log2(np.e))
    return pl.pallas_call(
        functools.partial(_mla_proj_kernel, need_q=need_q, q_scale=q_scale),
        grid=(b, n // tm),
        in_specs=[
            pl.BlockSpec((1, tm, MLA_Q_RANK), lambda bi, i: (bi, i, 0)),
            pl.BlockSpec((1, tm, MLA_KV_RANK), lambda bi, i: (bi, i, 2)),
            pl.BlockSpec((1, tm, MLA_PAD), lambda bi, i: (bi, i, 3)),
            pl.BlockSpec((1, tm, MLA_PAD), lambda bi, i: (bi, i, 4)),
            pl.BlockSpec((tm, MLA_PAD), lambda bi, i: (i, 0)),
            pl.BlockSpec((tm, MLA_PAD), lambda bi, i: (i, 0)),
            full(q_g), full(kv_g), full(wq), full(wqp), full(wk), full(wv), full(vone),
        ],
        out_specs=list(specs),
        out_shape=list(outs),
        compiler_params=_cparams("parallel", "parallel"),
        name="mla_proj",
    )(u_b, u_b, u_b, u_b, cos, sin, q_g, kv_g, wq, wqp, wk, wv, vone)


def _mla_attn_kernel(q_ref, qn_ref, k_ref, v_ref, o_ref, acc0_ref, acc1_ref, s0_ref, s1_ref, m_ref, *, tk):
    tq = q_ref.shape[3]
    nk = k_ref.shape[2]
    nchunks = nk // tk
    neg = jnp.full((8, tq), NEG, F32)
    s_refs = (s0_ref, s1_ref)
    acc_refs = (acc0_ref, acc1_ref)

    def chunk(c):
        return pl.ds(pl.multiple_of(c * tk, tk), tk)

    def scores(ks, hh, q, m):
        s = _dot(k_ref[0, hh, ks, :], q)
        s_refs[hh][ks, :] = s
        return jnp.maximum(m, jnp.max(s.reshape(tk // 8, 8, tq), axis=0))

    def weight(ks, hh, m_row):
        p = jnp.exp2(s_refs[hh][ks, :] - m_row).astype(BF16)
        acc_refs[hh][...] += _dot(v_ref[0, hh, :, ks], p)

    @pl.when(pl.program_id(2) == 0)
    def _():
        m_ref[...] = lax.fori_loop(0, nchunks, lambda c, m: scores(chunk(c), 0, q_ref[0, 0], m), neg)

    acc0_ref[...] = jnp.zeros(acc0_ref.shape, F32)
    acc1_ref[...] = jnp.zeros(acc1_ref.shape, F32)
    m0 = jnp.max(m_ref[...], axis=0, keepdims=True)

    def first(c, m1):
        weight(chunk(c), 0, m0)
        return scores(chunk(c), 1, q_ref[0, 1], m1)

    m1 = jnp.max(lax.fori_loop(0, nchunks, first, neg, unroll=True), axis=0, keepdims=True)

    def second(c, m0_next):
        weight(chunk(c), 1, m1)
        return scores(chunk(c), 0, qn_ref[0, 0], m0_next)

    m_ref[...] = lax.fori_loop(0, nchunks, second, neg, unroll=True)
    o_t = jnp.concatenate([a[:MLA_V] / a[MLA_V:MLA_V + 1] for a in acc_refs], axis=0)
    o_ref[0] = o_t.T.astype(o_ref.dtype)


def _mla_attention(q_t, k, v_t, tq, tk):
    b, h, _, n = q_t.shape
    nk = k.shape[2]
    tq = min(tq, n)
    tk = max(t for t in range(2 * LANES, tk + 1, 2 * LANES) if nk % t == 0)
    return pl.pallas_call(
        functools.partial(_mla_attn_kernel, tk=tk),
        grid=(b, h // 2, n // tq),
        in_specs=[
            pl.BlockSpec((1, 2, MLA_PAD, tq), lambda bi, hp, i: (bi, hp, 0, i)),
            pl.BlockSpec((1, 2, MLA_PAD, tq), lambda bi, hp, i: (bi, hp, 0, jnp.minimum(i + 1, n // tq - 1))),
            pl.BlockSpec((1, 2, nk, MLA_PAD), lambda bi, hp, i: (bi, hp, 0, 0)),
            pl.BlockSpec((1, 2, MLA_PAD, nk), lambda bi, hp, i: (bi, hp, 0, 0)),
        ],
        out_specs=pl.BlockSpec((1, tq, 2 * MLA_V), lambda bi, hp, i: (bi, i, hp)),
        out_shape=jax.ShapeDtypeStruct((b, n, h * MLA_V), BF16),
        scratch_shapes=[pltpu.VMEM((MLA_PAD, tq), F32), pltpu.VMEM((MLA_PAD, tq), F32),
                        pltpu.VMEM((nk, tq), F32), pltpu.VMEM((nk, tq), F32), pltpu.VMEM((8, tq), F32)],
        compiler_params=_cparams("parallel", "parallel", "arbitrary"),
        name="mla_attention",
    )(q_t, q_t, k, v_t)


def _na_kernel(q_ref, kp_ref, km_ref, kn_ref, kc_ref, vp_ref, vm_ref, vn_ref, vc_ref, tab_ref, o_ref):
    tq = q_ref.shape[1]
    nloc = 3 * tq
    lane = lax.broadcasted_iota(jnp.int32, (tq, LANES), 1)
    q = q_ref[0]
    k_all = jnp.concatenate([kp_ref[0], km_ref[0], kn_ref[0], kc_ref[0]], axis=0)
    v_all = jnp.concatenate([vp_ref[0], vm_ref[0], vn_ref[0], vc_ref[0]], axis=0)
    outs = []
    for hh in range(2):
        in_head = (lane >= hh * C_HEAD_DIM) & (lane < (hh + 1) * C_HEAD_DIM)
        qh = jnp.where(in_head, q, jnp.zeros_like(q))
        s = lax.dot_general(qh, k_all, NT, preferred_element_type=F32)
        s_loc = s[:, :nloc] + tab_ref[0, hh]
        s_ctx = s[:, nloc:]
        m = jnp.maximum(jnp.max(s_loc, axis=-1, keepdims=True), jnp.max(s_ctx, axis=-1, keepdims=True))
        p_loc = jnp.exp(s_loc - m)
        p_ctx = jnp.exp(s_ctx - m)
        l = jnp.sum(p_loc, axis=-1, keepdims=True) + jnp.sum(p_ctx, axis=-1, keepdims=True)
        o = _dot(p_loc.astype(BF16), v_all[:nloc]) + _dot(p_ctx.astype(BF16), v_all[nloc:])
        outs.append(o / l)
    o_ref[0] = jnp.where(lane < C_HEAD_DIM, outs[0], outs[1]).astype(o_ref.dtype)


def _na_tables(rpb, rows):
    h = rpb.shape[0]
    w = GRID_W
    qc = np.arange(w)
    kc = np.arange(w)
    cs = np.clip(qc - NA_COLS // 2, 0, w - NA_COLS)
    col_ok = (kc[None, :] >= cs[:, None]) & (kc[None, :] < cs[:, None] + NA_COLS)
    dc = np.clip(kc[None, :] - qc[:, None] + (NA_COLS - 1), 0, 2 * NA_COLS - 2)
    base = jnp.where(col_ok[None, None], rpb.astype(F32)[:, :, dc], NEG)
    base = jnp.concatenate([base, jnp.full((h, 1, w, w), NEG, F32)], axis=1)
    nblk = rows // NA_QROWS
    tabs = []
    for m in (0, 1, nblk - 1):
        qr = NA_QROWS * m + np.arange(NA_QROWS)
        rs = np.clip(qr - NA_ROWS // 2, 0, rows - NA_ROWS)
        kr = NA_QROWS * (m - 1) + np.arange(3 * NA_QROWS)
        ok = (kr[None, :] >= rs[:, None]) & (kr[None, :] < rs[:, None] + NA_ROWS)
        dr = np.where(ok, kr[None, :] - qr[:, None] + (NA_ROWS - 1), 2 * NA_ROWS - 1)
        t = base[:, dr]
        tabs.append(t.transpose(0, 1, 3, 2, 4).reshape(h, NA_QROWS * w, 3 * NA_QROWS * w))
    return jnp.stack(tabs)


def _na_attention(u_lat, u_ctx, tabs):
    b, n, _ = u_lat.shape
    nc = u_ctx.shape[1]
    tq = NA_QROWS * GRID_W
    nblk = n // tq
    pairs = C_HEADS // 2
    prev = lambda i: jnp.maximum(i - 1, 0)
    nxt = lambda i: jnp.minimum(i + 1, nblk - 1)
    blk = lambda col0, f: pl.BlockSpec((1, tq, LANES), lambda bi, hp, i: (bi, f(i), col0 + hp))
    ctx = lambda col0: pl.BlockSpec((1, nc, LANES), lambda bi, hp, i: (bi, 0, col0 + hp))
    same = lambda i: i
    sel = lambda i: jnp.where(i == 0, 0, jnp.where(i == nblk - 1, 2, 1))
    return pl.pallas_call(
        _na_kernel,
        grid=(b, pairs, nblk),
        in_specs=[
            blk(0, same),
            blk(pairs, prev), blk(pairs, same), blk(pairs, nxt), ctx(pairs),
            blk(2 * pairs, prev), blk(2 * pairs, same), blk(2 * pairs, nxt), ctx(2 * pairs),
            pl.BlockSpec((1, 2, tq, 3 * tq), lambda bi, hp, i: (sel(i), hp, 0, 0)),
        ],
        out_specs=pl.BlockSpec((1, tq, LANES), lambda bi, hp, i: (bi, i, hp)),
        out_shape=jax.ShapeDtypeStruct((b, n, C_WIDTH), BF16),
        compiler_params=_cparams("parallel", "parallel", "arbitrary"),
        name="na_attention",
    )(u_lat, u_lat, u_lat, u_lat, u_ctx, u_lat, u_lat, u_lat, u_ctx, tabs)


def _cd_out_kernel(c_ref, d_ref, x_ref, m_ref, wo_ref, o_ref):
    wc = c_ref.shape[2]
    o = _dot(c_ref[0], wo_ref[:wc]) + _dot(d_ref[0], wo_ref[wc:])
    o_ref[0] = x_ref[0] + m_ref[0, 2:3, :] * o


def _cd_out(c_lat, d_lat, x, mods, w_out, tm):
    b, n, d = x.shape
    tm = min(tm, n)
    return pl.pallas_call(
        _cd_out_kernel,
        grid=(b, n // tm),
        in_specs=[
            pl.BlockSpec((1, tm, c_lat.shape[2]), lambda bi, i: (bi, i, 0)),
            pl.BlockSpec((1, tm, d_lat.shape[2]), lambda bi, i: (bi, i, 0)),
            pl.BlockSpec((1, tm, d), lambda bi, i: (bi, i, 0)),
            pl.BlockSpec((1, 6, d), lambda bi, i: (bi, 0, 0)),
            pl.BlockSpec(w_out.shape, lambda bi, i: (0, 0)),
        ],
        out_specs=pl.BlockSpec((1, tm, d), lambda bi, i: (bi, i, 0)),
        out_shape=jax.ShapeDtypeStruct((b, n, d), F32),
        compiler_params=_cparams("parallel", "parallel"),
        name="cd_out",
    )(c_lat, d_lat, x, mods, w_out)


def _moe_params(w_rg, b_rg, w_re, b_re, w_gate, w_up, w_down):
    d = w_rg.shape[0]
    w_r = jnp.zeros((d, LANES), F32).at[:, :MOE_GROUPS].set(w_rg).at[:, MOE_GROUPS:MOE_GROUPS + MOE_EXPERTS].set(w_re)
    b_r = jnp.zeros((1, LANES), F32).at[0, :MOE_GROUPS].set(b_rg).at[0, MOE_GROUPS:MOE_GROUPS + MOE_EXPERTS].set(b_re)
    w_hi = w_r.astype(BF16)
    w_lo = (w_r - w_hi.astype(F32)).astype(BF16)
    f = w_gate.shape[-1]
    w_gu = jnp.concatenate([w_gate, w_up], axis=-1).reshape(MOE_EXPERTS, d, 2 * f).astype(BF16)
    w_d = w_down.reshape(MOE_EXPERTS, f, d).astype(BF16)
    return w_hi, w_lo, b_r, w_gu, w_d


def _rope_perm():
    j = np.arange(MLA_ROPE)
    half = MLA_ROPE // 2
    return (j // half) * half + (j % half + half // 2) % half


def _rope_tables(n):
    half = MLA_ROPE // 2
    nf = half // 2
    t = jnp.arange(n)
    inv = ROPE_THETA ** (-jnp.arange(nf, dtype=F32) / nf)
    parts_c, parts_s = [], []
    for pos in ((t // GRID_W).astype(F32), (t % GRID_W).astype(F32)):
        ang = pos[:, None] * inv[None, :]
        c, s = jnp.cos(ang), jnp.sin(ang)
        parts_c += [c, c]
        parts_s += [-s, s]
    pad = MLA_PAD - MLA_NOPE - MLA_ROPE
    cos = jnp.concatenate([jnp.ones((n, MLA_NOPE), F32)] + parts_c + [jnp.zeros((n, pad), F32)], axis=1)
    sin = jnp.concatenate([jnp.zeros((n, MLA_NOPE), F32)] + parts_s + [jnp.zeros((n, pad), F32)], axis=1)
    return cos, sin


def _identity_rope_tables(n):
    pad = MLA_PAD - MLA_NOPE - MLA_ROPE
    cos = jnp.concatenate([jnp.ones((n, MLA_NOPE + MLA_ROPE), F32), jnp.zeros((n, pad), F32)], axis=1)
    return cos, jnp.zeros((n, MLA_PAD), F32)


def _pad_heads(w, widths, src_cols, dst_off):
    rank = w.shape[0]
    out = jnp.zeros((rank, D_HEADS, MLA_PAD), F32)
    wh = w.reshape(rank, D_HEADS, widths)[:, :, src_cols]
    return out.at[:, :, dst_off:dst_off + len(src_cols)].set(wh).reshape(rank, D_HEADS * MLA_PAD)


def _cd_params(w_in, w_uq, w_ukv):
    d = w_in.shape[0]
    perm = _rope_perm()
    o = 3 * C_WIDTH
    q_scale = float(C_HEAD_DIM ** -0.5)
    kr = w_in[:, o + MLA_Q_RANK + MLA_KV_RANK:]
    pad_rope = lambda a: jnp.zeros((d, MLA_PAD), F32).at[:, MLA_NOPE:MLA_NOPE + MLA_ROPE].set(a)
    w_cat = jnp.concatenate([
        w_in[:, :C_WIDTH] * q_scale, w_in[:, C_WIDTH:o],
        w_in[:, o:o + MLA_Q_RANK + MLA_KV_RANK], pad_rope(kr), pad_rope(kr[:, perm]),
    ], axis=1).astype(BF16)
    qw = MLA_NOPE + MLA_ROPE
    nope = np.arange(MLA_NOPE)
    rope = MLA_NOPE + np.arange(MLA_ROPE)
    wq = (_pad_heads(w_uq, qw, nope, 0) + _pad_heads(w_uq, qw, rope, MLA_NOPE)).astype(BF16)
    wqp = _pad_heads(w_uq, qw, rope[perm], MLA_NOPE).astype(BF16)
    kvw = MLA_NOPE + MLA_V
    wk = _pad_heads(w_ukv, kvw, nope, 0).astype(BF16)
    wv = _pad_heads(w_ukv, kvw, MLA_NOPE + np.arange(MLA_V), 0).astype(BF16)
    return w_cat, wq, wqp, wk, wv


def kernel(x, c, ctx, c_ctx, ada_w, ada_b, norm1_g, norm2_g, ab_w_in, ab_w_out, hgrn_lb_logits, hgrn_onorm_g, pool_w,
           pool_scale, cd_w_in, cd_w_out, na_rpb, mla_q_norm_g, mla_w_uq, mla_kv_norm_g, mla_w_ukv, moe_w_rg, moe_b_rg,
           moe_w_re, moe_b_re, moe_w_gate, moe_w_up, moe_w_down, final_norm_g):
    b, n, d = x.shape
    n_ctx = ctx.shape[1]
    assert ada_w.shape[0] == 2 and ab_w_in.shape[0] == 1 and cd_w_in.shape[0] == 1 and b + 1 <= 8
    tm = 512

    cc = jnp.zeros((8, d), F32).at[:b].set(c).at[b].set(c_ctx)
    mods = _ada(cc, ada_w, ada_b).reshape(2, 8, 6, d)
    mods_lat = [mods[l, :b] for l in range(2)]
    mods_ctx = [jnp.broadcast_to(mods[l, b:b + 1], (b, 6, d)) for l in range(2)]
    lb = jnp.cumsum(jax.nn.softmax(hgrn_lb_logits.astype(F32), axis=1), axis=1)[:, 0]

    w_in0 = ab_w_in[0].astype(BF16)
    w_out0 = ab_w_out[0].astype(BF16)
    pw0 = pool_w[0].astype(BF16)
    ab_cols = w_in0.shape[1]
    (u_ctx,) = _in_proj(ctx, norm1_g[0], mods_ctx[0], w_in0, ((0, ab_cols),), (F32,), tm)
    (u_lat,) = _in_proj(x, norm1_g[0], mods_lat[0], w_in0, ((0, ab_cols),), (F32,), tm)
    s0 = jnp.zeros((b, 2, A_HEADS, A_HEAD_DIM, A_HEAD_DIM), F32)
    ocf, ocb, s_ctx = _hgrn_scan(u_ctx, lb, s0, 256)
    olf, olb, _ = _hgrn_scan(u_lat, lb, s_ctx, 256)
    xc = _ab_out(ocf, ocb, u_ctx, ctx, mods_ctx[0], hgrn_onorm_g[0], pw0, pool_scale[0], w_out0, tm)
    x = _ab_out(olf, olb, u_lat, x, mods_lat[0], hgrn_onorm_g[0], pw0, pool_scale[0], w_out0, tm)
    moe0 = _moe_params(moe_w_rg[0], moe_b_rg[0], moe_w_re[0], moe_b_re[0], moe_w_gate[0], moe_w_up[0], moe_w_down[0])
    xc = _moe(xc, norm2_g[0], mods_ctx[0], moe0, final_norm_g, False, tm)
    x = _moe(x, norm2_g[0], mods_lat[0], moe0, final_norm_g, False, tm)

    w_cat, wq, wqp, wk, wv = _cd_params(cd_w_in[0], mla_w_uq[0], mla_w_ukv[0])
    na_w = 3 * C_WIDTH
    splits = ((0, na_w), (na_w, w_cat.shape[1]))
    ua_ctx, ub_ctx = _in_proj(xc, norm1_g[1], mods_ctx[1], w_cat, splits, (BF16, F32), tm)
    ua_lat, ub_lat = _in_proj(x, norm1_g[1], mods_lat[1], w_cat, splits, (BF16, F32), tm)
    q_g = mla_q_norm_g[0].reshape(1, -1)
    kv_g = mla_kv_norm_g[0].reshape(1, -1)
    cos_l, sin_l = _rope_tables(n)
    cos_c, sin_c = _identity_rope_tables(n_ctx)
    k_c, v_c = _mla_proj(ub_ctx, cos_c, sin_c, q_g, kv_g, wq, wqp, wk, wv, False, tm)
    q_l, k_l, v_l = _mla_proj(ub_lat, cos_l, sin_l, q_g, kv_g, wq, wqp, wk, wv, True, tm)
    d_lat = _mla_attention(q_l, jnp.concatenate([k_c, k_l], axis=2), jnp.concatenate([v_c, v_l], axis=3), 256, 768)
    c_lat = _na_attention(ua_lat, ua_ctx, _na_tables(na_rpb[0], n // GRID_W))
    x = _cd_out(c_lat, d_lat, x, mods_lat[1], cd_w_out[0].astype(BF16), tm)
    moe1 = _moe_params(moe_w_rg[1], moe_b_rg[1], moe_w_re[1], moe_b_re[1], moe_w_gate[1], moe_w_up[1], moe_w_down[1])
    return _moe(x, norm2_g[1], mods_lat[1], moe1, final_norm_g, True, tm)
```

```python
import functools

import numpy as np
import jax
import jax.numpy as jnp
from jax import lax
from jax.experimental import pallas as pl
from jax.experimental.pallas import tpu as pltpu

F32 = jnp.float32
BF16 = jnp.bfloat16

EPS = 1e-6
NEG = -1e30

GRID_W = 64
A_HEADS = 4
A_HEAD_DIM = 128
A_WIDTH = A_HEADS * A_HEAD_DIM
POOL_WINDOWS = (2, 4, 8, 16)
B_GROUP = 128
B_WIDTH = B_GROUP * len(POOL_WINDOWS)
POOL_HALO = 16
C_HEADS = 8
C_HEAD_DIM = 64
C_WIDTH = C_HEADS * C_HEAD_DIM
NA_ROWS = 8
NA_COLS = 16
NA_QROWS = 4
D_HEADS = 8
MLA_Q_RANK = 256
MLA_KV_RANK = 128
MLA_NOPE = 64
MLA_ROPE = 32
MLA_V = 64
MLA_PAD = 128
ROPE_THETA = 10000.0
MOE_GROUPS = 4
MOE_EPG = 8
MOE_EXPERTS = MOE_GROUPS * MOE_EPG
MOE_HIDDEN = 256
LANES = 128
SLOT_BLOCK = 16
STEP_BLOCKS = 32
VMEM_LIMIT = 56 * 1024 * 1024

NT = (((1,), (1,)), ((), ()))
TN = (((0,), (0,)), ((), ()))


def _cparams(*sem):
    return pltpu.CompilerParams(dimension_semantics=sem, vmem_limit_bytes=VMEM_LIMIT)


def _sigmoid(x):
    return 1.0 / (1.0 + jnp.exp(-x))


def _silu(x):
    return x * _sigmoid(x)


def _dot(a, b):
    return jnp.dot(a, b, preferred_element_type=F32)


def _rmsnorm(x, g):
    return x * lax.rsqrt(jnp.mean(x * x, axis=-1, keepdims=True) + EPS) * g


def _ada_kernel(c_ref, w_ref, b_ref, o_ref):
    s = _silu(c_ref[...])
    o_ref[0] = jnp.dot(s, w_ref[0], precision=lax.Precision.HIGHEST, preferred_element_type=F32) + b_ref[0]


def _ada(cc, ada_w, ada_b):
    depth, d, n6 = ada_w.shape
    tn = n6 // 4
    return pl.pallas_call(
        _ada_kernel,
        grid=(depth, n6 // tn),
        in_specs=[
            pl.BlockSpec((8, d), lambda l, j: (0, 0)),
            pl.BlockSpec((1, d, tn), lambda l, j: (l, 0, j)),
            pl.BlockSpec((1, 1, tn), lambda l, j: (l, 0, j)),
        ],
        out_specs=pl.BlockSpec((1, 8, tn), lambda l, j: (l, 0, j)),
        out_shape=jax.ShapeDtypeStruct((depth, 8, n6), F32),
        compiler_params=_cparams("parallel", "parallel"),
        name="ada_mod",
    )(cc, ada_w, ada_b.reshape(depth, 1, n6))


def _in_kernel(x_ref, g_ref, m_ref, w_ref, *o_refs, splits):
    h = _rmsnorm(x_ref[0], g_ref[...]) * (1.0 + m_ref[0, 1:2, :]) + m_ref[0, 0:1, :]
    hb = h.astype(BF16)
    for o_ref, (a, b) in zip(o_refs, splits):
        o_ref[0] = _dot(hb, w_ref[:, a:b]).astype(o_ref.dtype)


def _in_proj(x, gain, mods, w, splits, dtypes, tm, flat=None):
    if flat is None:
        b, n, d = x.shape
        tm = min(tm, n)
        x_spec = pl.BlockSpec((1, tm, d), lambda bi, i: (bi, i, 0))
    else:
        b, n, row0 = flat
        d = x.shape[-1]
        tm = min(tm, n)
        x = x.reshape(1, -1, d)
        x_spec = pl.BlockSpec((1, tm, d), lambda bi, i: (0, row0 // tm + bi * (n // tm) + i, 0))
    outs = [jax.ShapeDtypeStruct((b, n, hi - lo), dt) for (lo, hi), dt in zip(splits, dtypes)]
    return pl.pallas_call(
        functools.partial(_in_kernel, splits=splits),
        grid=(b, n // tm),
        in_specs=[
            x_spec,
            pl.BlockSpec((1, d), lambda bi, i: (0, 0)),
            pl.BlockSpec((1, 6, d), lambda bi, i: (bi, 0, 0)),
            pl.BlockSpec(w.shape, lambda bi, i: (0, 0)),
        ],
        out_specs=[pl.BlockSpec((1, tm, hi - lo), lambda bi, i: (bi, i, 0)) for lo, hi in splits],
        out_shape=outs,
        compiler_params=_cparams("parallel", "parallel"),
        name="in_proj",
    )(x, gain.reshape(1, d), mods, w)


HG_SUB = 64


def _hgrn_direction(q_raw, fz, v, lb, st_ref, d, o_ref, reverse):
    rows = q_raw.shape[0]
    c = HG_SUB
    f = lb + (1.0 - lb) * _sigmoid(fz)
    k = 1.0 - f
    g = jnp.log(f)
    q = _silu(q_raw)
    r_i = lax.broadcasted_iota(jnp.int32, (c, c), 0)
    c_i = lax.broadcasted_iota(jnp.int32, (c, c), 1)
    keep = (c_i >= r_i) if reverse else (c_i <= r_i)
    tri = jnp.where(keep, 1.0, 0.0).astype(BF16)
    order = range(rows // c - 1, -1, -1) if reverse else range(rows // c)
    for ci in order:
        sl = slice(ci * c, (ci + 1) * c)
        gc = g[sl]
        g_hi = gc.astype(BF16)
        g_lo = (gc - g_hi.astype(F32)).astype(BF16)
        bc = _dot(tri, g_hi) + _dot(tri, g_lo)
        ref = bc[c // 2:c // 2 + 1]
        tot = bc[0:1] if reverse else bc[c - 1:c]
        qt = q[sl] * jnp.exp(bc - ref)
        kt = k[sl] * jnp.exp(ref - bc)
        qd = (qt * jnp.exp(ref)).astype(BF16)
        kd = (kt * jnp.exp(tot - ref)).astype(BF16)
        qt = qt.astype(BF16)
        kt = kt.astype(BF16)
        vb = v[sl].astype(BF16)
        dec = jnp.exp(tot)
        for h in range(A_HEADS):
            hs = slice(h * A_HEAD_DIM, (h + 1) * A_HEAD_DIM)
            att = lax.dot_general(qt[:, hs], kt[:, hs], NT, preferred_element_type=F32)
            att = jnp.where(keep, att, 0.0).astype(BF16)
            st = st_ref[d, h]
            o = _dot(att, vb[:, hs]) + lax.dot_general(qd[:, hs], st.astype(BF16), NT, preferred_element_type=F32)
            o_ref[0, sl, hs] = o
            st_ref[d, h] = st * dec[:, hs] + lax.dot_general(vb[:, hs], kd[:, hs], TN, preferred_element_type=F32)


def _hgrn_kernel(qf_ref, ff_ref, vf_ref, qb_ref, fb_ref, vb_ref, lb_ref, s0_ref, of_ref, ob_ref, sfin_ref, st_ref):
    j = pl.program_id(1)

    @pl.when(j == 0)
    def _():
        st_ref[...] = s0_ref[0]

    _hgrn_direction(qf_ref[0], ff_ref[0], vf_ref[0], lb_ref[0:1], st_ref, 0, of_ref, False)
    _hgrn_direction(qb_ref[0], fb_ref[0], vb_ref[0], lb_ref[1:2], st_ref, 1, ob_ref, True)

    @pl.when(j == pl.num_programs(1) - 1)
    def _():
        sfin_ref[0] = st_ref[...]


def _hgrn_scan(u, lb, s0, rows):
    b, n, _ = u.shape
    rows = min(rows, n)
    nb = n // rows
    w = A_WIDTH

    def fwd(col):
        return pl.BlockSpec((1, rows, w), lambda bi, j: (bi, j, col))

    def bwd(col):
        return pl.BlockSpec((1, rows, w), lambda bi, j: (bi, nb - 1 - j, col))

    st_spec = pl.BlockSpec((1, 2, A_HEADS, A_HEAD_DIM, A_HEAD_DIM), lambda bi, j: (bi, 0, 0, 0, 0))
    return pl.pallas_call(
        _hgrn_kernel,
        grid=(b, nb),
        in_specs=[fwd(0), fwd(1), fwd(3), bwd(0), bwd(2), bwd(3), pl.BlockSpec((2, w), lambda bi, j: (0, 0)), st_spec],
        out_specs=[
            pl.BlockSpec((1, rows, w), lambda bi, j: (bi, j, 0)),
            pl.BlockSpec((1, rows, w), lambda bi, j: (bi, nb - 1 - j, 0)),
            st_spec,
        ],
        out_shape=[
            jax.ShapeDtypeStruct((b, n, w), F32),
            jax.ShapeDtypeStruct((b, n, w), F32),
            jax.ShapeDtypeStruct(s0.shape, F32),
        ],
        scratch_shapes=[pltpu.VMEM((2, A_HEADS, A_HEAD_DIM, A_HEAD_DIM), F32)],
        compiler_params=_cparams("parallel", "arbitrary"),
        name="hgrn_scan",
    )(u, u, u, u, u, u, lb, s0)


def _ab_out_kernel(of_ref, ob_ref, ug_ref, up_ref, pprev_ref, pnext_ref, x_ref, m_ref, on_ref, pw_ref, ps_ref,
                   wo_ref, *rest, n):
    o_ref = rest[-1]
    i = pl.program_id(1)
    tm = x_ref.shape[1]
    o = of_ref[0] + ob_ref[0]
    gate = _silu(ug_ref[0])
    parts = []
    for h in range(A_HEADS):
        hs = slice(h * A_HEAD_DIM, (h + 1) * A_HEAD_DIM)
        parts.append(_rmsnorm(o[:, hs], on_ref[...]) * gate[:, hs])
    main = up_ref[0]
    prev = jnp.where(i > 0, pprev_ref[0], 0.0)
    nxt = jnp.where(i < pl.num_programs(1) - 1, pnext_ref[0], 0.0)
    ext = jnp.concatenate([prev, main, nxt], axis=0)
    ext_rows = tm + 2 * POOL_HALO
    t = i * tm + lax.broadcasted_iota(jnp.int32, (tm, 1), 0)
    for gi, win in enumerate(POOL_WINDOWS):
        gs = slice(gi * B_GROUP, (gi + 1) * B_GROUP)
        acc = ext[:, gs]
        acc = acc + pltpu.roll(acc, 1, 0)
        half = 1
        while 2 * half < win:
            acc = pltpu.roll(acc, half, 0) + pltpu.roll(acc, ext_rows - half, 0)
            half *= 2
        cnt = jnp.minimum(t + (win - win // 2), n) - jnp.maximum(t - win // 2, 0)
        mean = acc[POOL_HALO:POOL_HALO + tm] / cnt.astype(F32)
        pooled = _dot((mean - main[:, gs]).astype(BF16), pw_ref[gi])
        parts.append(pooled * ps_ref[:, gs])
    mix = jnp.concatenate(parts, axis=-1).astype(BF16)
    o_ref[0] = x_ref[0] + m_ref[0, 2:3, :] * _dot(mix, wo_ref[...])


def _ab_out(o_f, o_b, u, x, mods, onorm_g, pool_w, pool_scale, w_out, tm, rows, row0, buf=None):
    b, n, d = x.shape
    tm = min(tm, n)
    nt = n // tm
    hb = tm // POOL_HALO
    last_halo = n // POOL_HALO - 1
    w = A_WIDTH
    tile = lambda col: pl.BlockSpec((1, tm, w), lambda bi, i: (bi, i, col))
    extra_specs = [] if buf is None else [pl.BlockSpec(memory_space=pl.ANY)]
    extra_args = [] if buf is None else [buf]
    return pl.pallas_call(
        functools.partial(_ab_out_kernel, n=n),
        grid=(b, nt),
        input_output_aliases={} if buf is None else {12: 0},
        in_specs=[
            tile(0), tile(0), tile(4), tile(5),
            pl.BlockSpec((1, POOL_HALO, w), lambda bi, i: (bi, jnp.maximum(i * hb - 1, 0), 5)),
            pl.BlockSpec((1, POOL_HALO, w), lambda bi, i: (bi, jnp.minimum((i + 1) * hb, last_halo), 5)),
            pl.BlockSpec((1, tm, d), lambda bi, i: (bi, i, 0)),
            pl.BlockSpec((1, 6, d), lambda bi, i: (bi, 0, 0)),
            pl.BlockSpec((1, A_HEAD_DIM), lambda bi, i: (0, 0)),
            pl.BlockSpec(pool_w.shape, lambda bi, i: (0, 0, 0)),
            pl.BlockSpec((1, B_WIDTH), lambda bi, i: (0, 0)),
            pl.BlockSpec(w_out.shape, lambda bi, i: (0, 0)),
        ] + extra_specs,
        out_specs=pl.BlockSpec((1, tm, d), lambda bi, i: (0, row0 // tm + bi * nt + i, 0)),
        out_shape=jax.ShapeDtypeStruct((1, rows, d), F32),
        compiler_params=_cparams("parallel", "parallel"),
        name="ab_out",
    )(o_f, o_b, u, u, u, u, x, mods, onorm_g.reshape(1, A_HEAD_DIM), pool_w, pool_scale.reshape(1, B_WIDTH), w_out,
      *extra_args)


def _slot_rows(tr):
    rows = 2 * tr + MOE_EXPERTS * (SLOT_BLOCK - 1)
    assert rows % SLOT_BLOCK == 0
    return rows


def _route_kernel(x_ref, g_ref, m_ref, whi_ref, wlo_ref, br_ref, xs_ref, info_ref, cnt_ref, *, slot_rows):
    tr = x_ref.shape[0]
    h = _rmsnorm(x_ref[...], g_ref[...]) * (1.0 + m_ref[0, 4:5, :]) + m_ref[0, 3:4, :]
    hb = h.astype(BF16)
    hl = (h - hb.astype(F32)).astype(BF16)
    logits = _dot(hb, whi_ref[...]) + _dot(hb, wlo_ref[...]) + _dot(hl, whi_ref[...]) + br_ref[...]
    lane = lax.broadcasted_iota(jnp.int32, (tr, LANES), 1)
    lanef = lane.astype(F32)
    lg = jnp.where(lane < MOE_GROUPS, logits, NEG)
    mg = jnp.max(lg, axis=-1, keepdims=True)
    g_p = 1.0 / jnp.sum(jnp.exp(lg - mg), axis=-1, keepdims=True)
    gidx = jnp.min(jnp.where(lg == mg, lanef, float(LANES)), axis=-1, keepdims=True)
    lo = MOE_GROUPS + MOE_EPG * gidx
    le = jnp.where((lanef >= lo) & (lanef < lo + MOE_EPG), logits, NEG)
    m1 = jnp.max(le, axis=-1, keepdims=True)
    i1 = jnp.min(jnp.where(le == m1, lanef, float(LANES)), axis=-1, keepdims=True)
    le2 = jnp.where(lanef == i1, NEG, le)
    m2 = jnp.max(le2, axis=-1, keepdims=True)
    i2 = jnp.min(jnp.where(le2 == m2, lanef, float(LANES)), axis=-1, keepdims=True)
    ratio = jnp.exp(m2 - m1)
    w1 = g_p / (1.0 + ratio)
    w2 = g_p * ratio / (1.0 + ratio)
    hot1 = lanef == i1
    hot2 = lanef == i2
    hot = jnp.where(hot1, 1.0, jnp.where(hot2, 1.0, 0.0))
    r_i = lax.broadcasted_iota(jnp.int32, (tr, tr), 0)
    c_i = lax.broadcasted_iota(jnp.int32, (tr, tr), 1)
    rank = _dot(jnp.where(c_i < r_i, 1.0, 0.0).astype(BF16), hot.astype(BF16))
    cnt = jnp.sum(hot, axis=0, keepdims=True)
    nblk = jnp.floor((cnt + (SLOT_BLOCK - 1)) * (1.0 / SLOT_BLOCK))
    l_r = lax.broadcasted_iota(jnp.int32, (LANES, LANES), 0)
    l_c = lax.broadcasted_iota(jnp.int32, (LANES, LANES), 1)
    before = jnp.where(l_r < l_c, 1.0, 0.0).astype(BF16)
    off = SLOT_BLOCK * _dot(jnp.broadcast_to(nblk, (8, LANES)).astype(BF16), before)[0:1]
    posm = off + rank
    pos1 = jnp.sum(jnp.where(hot1, posm, 0.0), axis=-1, keepdims=True)
    pos2 = jnp.sum(jnp.where(hot2, posm, 0.0), axis=-1, keepdims=True)
    info = jnp.where(lane == 0, pos1, jnp.where(lane == 1, pos2, jnp.where(lane == 2, w1, jnp.where(lane == 3, w2, 0.0))))
    info_ref[...] = info
    pos_t = info.T.astype(jnp.int32)
    row = lax.broadcasted_iota(jnp.int32, (slot_rows, tr), 0)
    sel = jnp.where(row == pos_t[0:1], 1.0, jnp.where(row == pos_t[1:2], 1.0, 0.0)).astype(BF16)
    xs_ref[...] = _dot(sel, hb).astype(BF16)
    cnt_ref[0] = jnp.broadcast_to(cnt, (8, LANES))


def _moe_route(x2d, gain, mods, tiles_per_mod, w_hi, w_lo, b_r, tr):
    t, d = x2d.shape
    nt = t // tr
    sr = _slot_rows(tr)
    return pl.pallas_call(
        functools.partial(_route_kernel, slot_rows=sr),
        grid=(nt,),
        in_specs=[
            pl.BlockSpec((tr, d), lambda i: (i, 0)),
            pl.BlockSpec((1, d), lambda i: (0, 0)),
            pl.BlockSpec((1, 6, d), lambda i: (jnp.minimum(i // tiles_per_mod, mods.shape[0] - 1), 0, 0)),
            pl.BlockSpec((d, LANES), lambda i: (0, 0)),
            pl.BlockSpec((d, LANES), lambda i: (0, 0)),
            pl.BlockSpec((1, LANES), lambda i: (0, 0)),
        ],
        out_specs=[
            pl.BlockSpec((sr, d), lambda i: (i, 0)),
            pl.BlockSpec((tr, LANES), lambda i: (i, 0)),
            pl.BlockSpec((1, 8, LANES), lambda i: (i, 0, 0)),
        ],
        out_shape=[
            jax.ShapeDtypeStruct((nt * sr, d), BF16),
            jax.ShapeDtypeStruct((t, LANES), F32),
            jax.ShapeDtypeStruct((nt, 8, LANES), F32),
        ],
        compiler_params=_cparams("parallel"),
        name="moe_route",
    )(x2d, gain.reshape(1, d), mods, w_hi, w_lo, b_r)


def _tables_kernel(cnt_ref, src_ref, inv_ref, exp_ref, valid_ref, run_ref, *, ntiles, bpt, nsteps):
    def fill(ref, n, val):
        unroll = 8

        def body(j, carry):
            for u in range(unroll):
                ref[j * unroll + u] = val
            return carry

        lax.fori_loop(0, n // unroll, body, 0)
        for j in range(n // unroll * unroll, n):
            ref[j] = val

    fill(src_ref, nsteps * STEP_BLOCKS, 0)
    fill(inv_ref, ntiles * bpt, 0)
    fill(exp_ref, nsteps, 0)
    fill(valid_ref, nsteps, 0)
    fill(run_ref, ntiles, 0)

    def per_expert(e, pos):
        def per_tile(i, p):
            nb = (cnt_ref[i, e] + (SLOT_BLOCK - 1)) // SLOT_BLOCK
            first = run_ref[i]

            def per_block(r, carry):
                src_ref[p + r] = i * bpt + first + r
                inv_ref[i * bpt + first + r] = p + r
                return carry

            lax.fori_loop(0, nb, per_block, 0)
            run_ref[i] = first + nb
            return p + nb

        end = lax.fori_loop(0, ntiles, per_tile, pos)
        end = (end + (STEP_BLOCKS - 1)) // STEP_BLOCKS * STEP_BLOCKS

        def mark(s, carry):
            exp_ref[s] = e
            valid_ref[s] = 1
            return carry

        lax.fori_loop(pos // STEP_BLOCKS, end // STEP_BLOCKS, mark, 0)
        return end

    lax.fori_loop(0, MOE_EXPERTS, per_expert, 0)


def _expert_tables(cnt, bpt, nsteps):
    ntiles = cnt.shape[0]
    smem = pl.BlockSpec(memory_space=pltpu.SMEM)
    i32 = lambda n: jax.ShapeDtypeStruct((n,), jnp.int32)
    return pl.pallas_call(
        functools.partial(_tables_kernel, ntiles=ntiles, bpt=bpt, nsteps=nsteps),
        in_specs=[smem],
        out_specs=[smem] * 4,
        out_shape=[i32(nsteps * STEP_BLOCKS), i32(ntiles * bpt), i32(nsteps), i32(nsteps)],
        scratch_shapes=[pltpu.SMEM((ntiles,), jnp.int32)],
        name="moe_tables",
    )(cnt)


def _experts_kernel(src_ref, exp_ref, valid_ref, *refs):
    x_refs = refs[:STEP_BLOCKS]
    wg_ref, wu_ref, wd_ref, y_ref, wgb_ref, wub_ref, wdb_ref = refs[STEP_BLOCKS:]
    s = pl.program_id(0)

    @pl.when((s == 0) | (exp_ref[s] != exp_ref[jnp.maximum(s - 1, 0)]))
    def _():
        wgb_ref[...] = wg_ref[0, 0].astype(BF16)
        wub_ref[...] = wu_ref[0, 0].astype(BF16)
        wdb_ref[...] = wd_ref[0, 0].astype(BF16)

    @pl.when(valid_ref[s] > 0)
    def _():
        x = jnp.concatenate([r[0] for r in x_refs], axis=0)
        a = _silu(_dot(x, wgb_ref[...])) * _dot(x, wub_ref[...])
        y_ref[...] = _dot(a.astype(BF16), wdb_ref[...]).astype(BF16)

    @pl.when(valid_ref[s] == 0)
    def _():
        y_ref[...] = jnp.zeros(y_ref.shape, y_ref.dtype)


def _moe_experts(xs, src, step_e, valid, w_gate, w_up, w_down, nsteps):
    rows, d = xs.shape
    f = w_gate.shape[-1]
    xs3 = xs.reshape(rows // SLOT_BLOCK, SLOT_BLOCK, d)
    step_rows = STEP_BLOCKS * SLOT_BLOCK
    in_blk = lambda kk: pl.BlockSpec((1, SLOT_BLOCK, d), lambda s, sr, ex, va: (sr[s * STEP_BLOCKS + kk], 0, 0))
    w_blk = lambda shape: pl.BlockSpec((1, 1) + shape, lambda s, sr, ex, va: (ex[s] // MOE_EPG, ex[s] % MOE_EPG, 0, 0))
    grid_spec = pltpu.PrefetchScalarGridSpec(
        num_scalar_prefetch=3,
        grid=(nsteps,),
        in_specs=[in_blk(kk) for kk in range(STEP_BLOCKS)] + [w_blk((d, f)), w_blk((d, f)), w_blk((f, d))],
        out_specs=pl.BlockSpec((step_rows, d), lambda s, sr, ex, va: (s, 0)),
        scratch_shapes=[pltpu.VMEM((d, f), BF16), pltpu.VMEM((d, f), BF16), pltpu.VMEM((f, d), BF16)],
    )
    return pl.pallas_call(
        _experts_kernel,
        grid_spec=grid_spec,
        out_shape=jax.ShapeDtypeStruct((nsteps * step_rows, d), BF16),
        compiler_params=_cparams("arbitrary"),
        name="moe_experts",
    )(src, step_e, valid, *([xs3] * STEP_BLOCKS), w_gate, w_up, w_down)


def _combine_kernel(inv_ref, x_ref, info_ref, m_ref, fg_ref, *refs, final):
    y_refs, o_ref = refs[:-1], refs[-1]
    tr = x_ref.shape[0]
    info = info_ref[...]
    col = lax.broadcasted_iota(jnp.int32, (tr, len(y_refs) * SLOT_BLOCK), 1)
    wsel = jnp.where(col == info[:, 0:1].astype(jnp.int32), info[:, 2:3],
                     jnp.where(col == info[:, 1:2].astype(jnp.int32), info[:, 3:4], 0.0))
    y = _dot(wsel.astype(BF16), jnp.concatenate([r[0] for r in y_refs], axis=0))
    out = x_ref[...] + m_ref[0, 5:6, :] * y
    if final:
        out = _rmsnorm(out, fg_ref[...])
    o_ref[...] = out


def _moe_combine(x2d, ys, inv, info, mods, tiles_per_mod, final_g, tr, bpt, final):
    t, d = x2d.shape
    ys3 = ys.reshape(ys.shape[0] // SLOT_BLOCK, SLOT_BLOCK, d)
    y_blk = lambda kk: pl.BlockSpec((1, SLOT_BLOCK, d), lambda i, iv: (iv[i * bpt + kk], 0, 0))
    grid_spec = pltpu.PrefetchScalarGridSpec(
        num_scalar_prefetch=1,
        grid=(t // tr,),
        in_specs=[
            pl.BlockSpec((tr, d), lambda i, iv: (i, 0)),
            pl.BlockSpec((tr, LANES), lambda i, iv: (i, 0)),
            pl.BlockSpec((1, 6, d), lambda i, iv: (jnp.minimum(i // tiles_per_mod, mods.shape[0] - 1), 0, 0)),
            pl.BlockSpec((1, d), lambda i, iv: (0, 0)),
        ] + [y_blk(kk) for kk in range(bpt)],
        out_specs=pl.BlockSpec((tr, d), lambda i, iv: (i, 0)),
    )
    return pl.pallas_call(
        functools.partial(_combine_kernel, final=final),
        grid_spec=grid_spec,
        out_shape=jax.ShapeDtypeStruct((t, d), F32),
        compiler_params=_cparams("parallel"),
        name="moe_combine",
    )(inv, x2d, info, mods, final_g.reshape(1, d), *([ys3] * bpt))


def _moe(x2d, gain, mods, tiles_per_mod, params, final_g, final, tr):
    w_hi, w_lo, b_r, w_gate, w_up, w_down = params
    nt = x2d.shape[0] // tr
    xs, info, cnt = _moe_route(x2d, gain, mods, tiles_per_mod, w_hi, w_lo, b_r, tr)
    bpt = _slot_rows(tr) // SLOT_BLOCK
    nsteps = -(-(nt * bpt + MOE_EXPERTS * (STEP_BLOCKS - 1)) // STEP_BLOCKS)
    counts = cnt[:, 0, MOE_GROUPS:MOE_GROUPS + MOE_EXPERTS].astype(jnp.int32)
    src, inv, step_e, valid = _expert_tables(counts, bpt, nsteps)
    ys = _moe_experts(xs, src, step_e, valid, w_gate, w_up, w_down, nsteps)
    return _moe_combine(x2d, ys, inv, info, mods, tiles_per_mod, final_g, tr, bpt, final)


def _mla_proj_kernel(cq_ref, ckv_ref, kr_ref, krp_ref, cos_ref, sin_ref, qg_ref, kg_ref, wq_ref, wqp_ref, wk_ref,
                     wv_ref, vone_ref, *o_refs, need_q, q_scale):
    cos = cos_ref[...]
    sin = sin_ref[...]
    ckv = _rmsnorm(ckv_ref[0], kg_ref[...]).astype(BF16)
    k_rope = kr_ref[0] * cos + krp_ref[0] * sin
    kn = _dot(ckv, wk_ref[...])
    if need_q:
        q_ref, k_ref, v_ref = o_refs
    else:
        k_ref, v_ref = o_refs
    vx = _dot(ckv, wv_ref[...]) + vone_ref[...]
    for h in range(D_HEADS):
        hs = slice(h * MLA_PAD, (h + 1) * MLA_PAD)
        k_ref[0, h] = (kn[:, hs] + k_rope).astype(BF16)
        v_ref[0, h] = vx[:, hs].T.astype(BF16)
    if need_q:
        cq = _rmsnorm(cq_ref[0], qg_ref[...]).astype(BF16)
        qm = _dot(cq, wq_ref[...])
        qp = _dot(cq, wqp_ref[...])
        for h in range(D_HEADS):
            hs = slice(h * MLA_PAD, (h + 1) * MLA_PAD)
            q_ref[0, h] = ((qm[:, hs] * cos + qp[:, hs] * sin) * q_scale).T.astype(BF16)


def _mla_proj(u_b, cos, sin, q_g, kv_g, wq, wqp, wk, wv, need_q, tm):
    b, n, _ = u_b.shape
    tm = min(tm, n)
    row_major = (jax.ShapeDtypeStruct((b, D_HEADS, n, MLA_PAD), BF16),
                 pl.BlockSpec((1, D_HEADS, tm, MLA_PAD), lambda bi, i: (bi, 0, i, 0)))
    col_major = (jax.ShapeDtypeStruct((b, D_HEADS, MLA_PAD, n), BF16),
                 pl.BlockSpec((1, D_HEADS, MLA_PAD, tm), lambda bi, i: (bi, 0, 0, i)))
    outs, specs = zip(*(([col_major] if need_q else []) + [row_major, col_major]))
    full = lambda a: pl.BlockSpec(a.shape, lambda bi, i: (0,) * a.ndim)
    vone = jnp.tile(jnp.concatenate([jnp.zeros((1, MLA_V), F32), jnp.ones((1, MLA_PAD - MLA_V), F32)], axis=1),
                    (1, D_HEADS))
    q_scale = float((MLA_NOPE + MLA_ROPE) ** -0.5 * np.log2(np.e))
    return pl.pallas_call(
        functools.partial(_mla_proj_kernel, need_q=need_q, q_scale=q_scale),
        grid=(b, n // tm),
        in_specs=[
            pl.BlockSpec((1, tm, MLA_Q_RANK), lambda bi, i: (bi, i, 0)),
            pl.BlockSpec((1, tm, MLA_KV_RANK), lambda bi, i: (bi, i, 2)),
            pl.BlockSpec((1, tm, MLA_PAD), lambda bi, i: (bi, i, 3)),
            pl.BlockSpec((1, tm, MLA_PAD), lambda bi, i: (bi, i, 4)),
            pl.BlockSpec((tm, MLA_PAD), lambda bi, i: (i, 0)),
            pl.BlockSpec((tm, MLA_PAD), lambda bi, i: (i, 0)),
            full(q_g), full(kv_g), full(wq), full(wqp), full(wk), full(wv), full(vone),
        ],
        out_specs=list(specs),
        out_shape=list(outs),
        compiler_params=_cparams("parallel", "parallel"),
        name="mla_proj",
    )(u_b, u_b, u_b, u_b, cos, sin, q_g, kv_g, wq, wqp, wk, wv, vone)


def _mla_attn_kernel(q_ref, qn_ref, k_ref, v_ref, o_ref, acc0_ref, acc1_ref, s0_ref, s1_ref, m_ref, *, tk):
    tq = q_ref.shape[3]
    nk = k_ref.shape[2]
    nchunks = nk // tk
    neg = jnp.full((8, tq), NEG, F32)
    s_refs = (s0_ref, s1_ref)
    acc_refs = (acc0_ref, acc1_ref)

    def chunk(c):
        return pl.ds(pl.multiple_of(c * tk, tk), tk)

    def scores(ks, hh, q, m):
        s = _dot(k_ref[0, hh, ks, :], q)
        s_refs[hh][ks, :] = s
        return jnp.maximum(m, jnp.max(s.reshape(tk // 8, 8, tq), axis=0))

    def weight(ks, hh, m_row):
        p = jnp.exp2(s_refs[hh][ks, :] - m_row).astype(BF16)
        acc_refs[hh][...] += _dot(v_ref[0, hh, :, ks], p)

    @pl.when(pl.program_id(2) == 0)
    def _():
        m_ref[...] = lax.fori_loop(0, nchunks, lambda c, m: scores(chunk(c), 0, q_ref[0, 0], m), neg)

    acc0_ref[...] = jnp.zeros(acc0_ref.shape, F32)
    acc1_ref[...] = jnp.zeros(acc1_ref.shape, F32)
    m0 = jnp.max(m_ref[...], axis=0, keepdims=True)

    def first(c, m1):
        weight(chunk(c), 0, m0)
        return scores(chunk(c), 1, q_ref[0, 1], m1)

    m1 = jnp.max(lax.fori_loop(0, nchunks, first, neg, unroll=True), axis=0, keepdims=True)

    def second(c, m0_next):
        weight(chunk(c), 1, m1)
        return scores(chunk(c), 0, qn_ref[0, 0], m0_next)

    m_ref[...] = lax.fori_loop(0, nchunks, second, neg, unroll=True)
    o_t = jnp.concatenate([a[:MLA_V] / a[MLA_V:MLA_V + 1] for a in acc_refs], axis=0)
    o_ref[0] = o_t.T.astype(o_ref.dtype)


def _mla_attention(q_t, k, v_t, tq, tk):
    b, h, _, n = q_t.shape
    nk = k.shape[2]
    tq = min(tq, n)
    tk = max(t for t in range(2 * LANES, tk + 1, 2 * LANES) if nk % t == 0)
    return pl.pallas_call(
        functools.partial(_mla_attn_kernel, tk=tk),
        grid=(b, h // 2, n // tq),
        in_specs=[
            pl.BlockSpec((1, 2, MLA_PAD, tq), lambda bi, hp, i: (bi, hp, 0, i)),
            pl.BlockSpec((1, 2, MLA_PAD, tq), lambda bi, hp, i: (bi, hp, 0, jnp.minimum(i + 1, n // tq - 1))),
            pl.BlockSpec((1, 2, nk, MLA_PAD), lambda bi, hp, i: (bi, hp, 0, 0)),
            pl.BlockSpec((1, 2, MLA_PAD, nk), lambda bi, hp, i: (bi, hp, 0, 0)),
        ],
        out_specs=pl.BlockSpec((1, tq, 2 * MLA_V), lambda bi, hp, i: (bi, i, hp)),
        out_shape=jax.ShapeDtypeStruct((b, n, h * MLA_V), BF16),
        scratch_shapes=[pltpu.VMEM((MLA_PAD, tq), F32), pltpu.VMEM((MLA_PAD, tq), F32),
                        pltpu.VMEM((nk, tq), F32), pltpu.VMEM((nk, tq), F32), pltpu.VMEM((8, tq), F32)],
        compiler_params=_cparams("parallel", "parallel", "arbitrary"),
        name="mla_attention",
    )(q_t, q_t, k, v_t)


def _na_kernel(q_ref, kp_ref, km_ref, kn_ref, kc_ref, vp_ref, vm_ref, vn_ref, vc_ref, tab_ref, o_ref):
    tq = q_ref.shape[1]
    nloc = 3 * tq
    lane = lax.broadcasted_iota(jnp.int32, (tq, LANES), 1)
    q = q_ref[0]
    k_all = jnp.concatenate([kp_ref[0], km_ref[0], kn_ref[0], kc_ref[0]], axis=0)
    v_all = jnp.concatenate([vp_ref[0], vm_ref[0], vn_ref[0], vc_ref[0]], axis=0)
    outs = []
    for hh in range(2):
        in_head = (lane >= hh * C_HEAD_DIM) & (lane < (hh + 1) * C_HEAD_DIM)
        qh = jnp.where(in_head, q, jnp.zeros_like(q))
        s = lax.dot_general(qh, k_all, NT, preferred_element_type=F32)
        s_loc = s[:, :nloc] + tab_ref[0, hh]
        s_ctx = s[:, nloc:]
        m = jnp.maximum(jnp.max(s_loc, axis=-1, keepdims=True), jnp.max(s_ctx, axis=-1, keepdims=True))
        p_loc = jnp.exp(s_loc - m)
        p_ctx = jnp.exp(s_ctx - m)
        l = jnp.sum(p_loc, axis=-1, keepdims=True) + jnp.sum(p_ctx, axis=-1, keepdims=True)
        o = _dot(p_loc.astype(BF16), v_all[:nloc]) + _dot(p_ctx.astype(BF16), v_all[nloc:])
        outs.append(o / l)
    o_ref[0] = jnp.where(lane < C_HEAD_DIM, outs[0], outs[1]).astype(o_ref.dtype)


def _na_tables(rpb, rows):
    h = rpb.shape[0]
    w = GRID_W
    qc = np.arange(w)
    kc = np.arange(w)
    cs = np.clip(qc - NA_COLS // 2, 0, w - NA_COLS)
    col_ok = (kc[None, :] >= cs[:, None]) & (kc[None, :] < cs[:, None] + NA_COLS)
    dc = np.clip(kc[None, :] - qc[:, None] + (NA_COLS - 1), 0, 2 * NA_COLS - 2)
    base = jnp.where(col_ok[None, None], rpb.astype(F32)[:, :, dc], NEG)
    base = jnp.concatenate([base, jnp.full((h, 1, w, w), NEG, F32)], axis=1)
    nblk = rows // NA_QROWS
    tabs = []
    for m in (0, 1, nblk - 1):
        qr = NA_QROWS * m + np.arange(NA_QROWS)
        rs = np.clip(qr - NA_ROWS // 2, 0, rows - NA_ROWS)
        kr = NA_QROWS * (m - 1) + np.arange(3 * NA_QROWS)
        ok = (kr[None, :] >= rs[:, None]) & (kr[None, :] < rs[:, None] + NA_ROWS)
        dr = np.where(ok, kr[None, :] - qr[:, None] + (NA_ROWS - 1), 2 * NA_ROWS - 1)
        t = base[:, dr]
        tabs.append(t.transpose(0, 1, 3, 2, 4).reshape(h, NA_QROWS * w, 3 * NA_QROWS * w))
    return jnp.stack(tabs)


def _na_attention(u_lat, u_ctx, tabs):
    b, n, _ = u_lat.shape
    nc = u_ctx.shape[1]
    tq = NA_QROWS * GRID_W
    nblk = n // tq
    pairs = C_HEADS // 2
    prev = lambda i: jnp.maximum(i - 1, 0)
    nxt = lambda i: jnp.minimum(i + 1, nblk - 1)
    blk = lambda col0, f: pl.BlockSpec((1, tq, LANES), lambda bi, hp, i: (bi, f(i), col0 + hp))
    ctx = lambda col0: pl.BlockSpec((1, nc, LANES), lambda bi, hp, i: (bi, 0, col0 + hp))
    same = lambda i: i
    sel = lambda i: jnp.where(i == 0, 0, jnp.where(i == nblk - 1, 2, 1))
    return pl.pallas_call(
        _na_kernel,
        grid=(b, pairs, nblk),
        in_specs=[
            blk(0, same),
            blk(pairs, prev), blk(pairs, same), blk(pairs, nxt), ctx(pairs),
            blk(2 * pairs, prev), blk(2 * pairs, same), blk(2 * pairs, nxt), ctx(2 * pairs),
            pl.BlockSpec((1, 2, tq, 3 * tq), lambda bi, hp, i: (sel(i), hp, 0, 0)),
        ],
        out_specs=pl.BlockSpec((1, tq, LANES), lambda bi, hp, i: (bi, i, hp)),
        out_shape=jax.ShapeDtypeStruct((b, n, C_WIDTH), BF16),
        compiler_params=_cparams("parallel", "parallel", "arbitrary"),
        name="na_attention",
    )(u_lat, u_lat, u_lat, u_lat, u_ctx, u_lat, u_lat, u_lat, u_ctx, tabs)


def _cd_out_kernel(c_ref, d_ref, x_ref, m_ref, wo_ref, o_ref):
    wc = c_ref.shape[2]
    o = _dot(c_ref[0], wo_ref[:wc]) + _dot(d_ref[0], wo_ref[wc:])
    o_ref[0] = x_ref[0] + m_ref[0, 2:3, :] * o


def _cd_out(c_lat, d_lat, x2d, mods, w_out, tm):
    b, n, _ = c_lat.shape
    d = x2d.shape[-1]
    tm = min(tm, n)
    nt = n // tm
    flat = pl.BlockSpec((1, tm, d), lambda bi, i: (0, bi * nt + i, 0))
    return pl.pallas_call(
        _cd_out_kernel,
        grid=(b, nt),
        in_specs=[
            pl.BlockSpec((1, tm, c_lat.shape[2]), lambda bi, i: (bi, i, 0)),
            pl.BlockSpec((1, tm, d_lat.shape[2]), lambda bi, i: (bi, i, 0)),
            flat,
            pl.BlockSpec((1, 6, d), lambda bi, i: (bi, 0, 0)),
            pl.BlockSpec(w_out.shape, lambda bi, i: (0, 0)),
        ],
        out_specs=flat,
        out_shape=jax.ShapeDtypeStruct((1, b * n, d), F32),
        compiler_params=_cparams("parallel", "parallel"),
        name="cd_out",
    )(c_lat, d_lat, x2d.reshape(1, -1, d), mods, w_out)


def _moe_params(w_rg, b_rg, w_re, b_re, w_gate, w_up, w_down):
    d = w_rg.shape[0]
    w_r = jnp.zeros((d, LANES), F32).at[:, :MOE_GROUPS].set(w_rg).at[:, MOE_GROUPS:MOE_GROUPS + MOE_EXPERTS].set(w_re)
    b_r = jnp.zeros((1, LANES), F32).at[0, :MOE_GROUPS].set(b_rg).at[0, MOE_GROUPS:MOE_GROUPS + MOE_EXPERTS].set(b_re)
    w_hi = w_r.astype(BF16)
    w_lo = (w_r - w_hi.astype(F32)).astype(BF16)
    return w_hi, w_lo, b_r, w_gate, w_up, w_down


def _rope_perm():
    j = np.arange(MLA_ROPE)
    half = MLA_ROPE // 2
    return (j // half) * half + (j % half + half // 2) % half


def _rope_tables(n):
    half = MLA_ROPE // 2
    nf = half // 2
    t = np.arange(n)
    inv = (np.float32(ROPE_THETA) ** (-np.arange(nf, dtype=np.float32) / np.float32(nf))).astype(np.float32)
    parts_c, parts_s = [], []
    for pos in ((t // GRID_W).astype(np.float32), (t % GRID_W).astype(np.float32)):
        ang = (pos[:, None] * inv[None, :]).astype(np.float32)
        c, s = np.cos(ang).astype(np.float32), np.sin(ang).astype(np.float32)
        parts_c += [c, c]
        parts_s += [-s, s]
    pad = MLA_PAD - MLA_NOPE - MLA_ROPE
    cos = np.concatenate([np.ones((n, MLA_NOPE), np.float32)] + parts_c + [np.zeros((n, pad), np.float32)], axis=1)
    sin = np.concatenate([np.zeros((n, MLA_NOPE), np.float32)] + parts_s + [np.zeros((n, pad), np.float32)], axis=1)
    return jnp.asarray(cos), jnp.asarray(sin)


def _identity_rope_tables(n):
    pad = MLA_PAD - MLA_NOPE - MLA_ROPE
    cos = jnp.concatenate([jnp.ones((n, MLA_NOPE + MLA_ROPE), F32), jnp.zeros((n, pad), F32)], axis=1)
    return cos, jnp.zeros((n, MLA_PAD), F32)


def _pad_heads(w, widths, src_cols, dst_off):
    rank = w.shape[0]
    out = jnp.zeros((rank, D_HEADS, MLA_PAD), F32)
    wh = w.reshape(rank, D_HEADS, widths)[:, :, src_cols]
    return out.at[:, :, dst_off:dst_off + len(src_cols)].set(wh).reshape(rank, D_HEADS * MLA_PAD)


def _cd_params(w_in, w_uq, w_ukv):
    d = w_in.shape[0]
    perm = _rope_perm()
    o = 3 * C_WIDTH
    q_scale = float(C_HEAD_DIM ** -0.5)
    kr = w_in[:, o + MLA_Q_RANK + MLA_KV_RANK:]
    pad_rope = lambda a: jnp.zeros((d, MLA_PAD), F32).at[:, MLA_NOPE:MLA_NOPE + MLA_ROPE].set(a)
    w_cat = jnp.concatenate([
        w_in[:, :C_WIDTH] * q_scale, w_in[:, C_WIDTH:o],
        w_in[:, o:o + MLA_Q_RANK + MLA_KV_RANK], pad_rope(kr), pad_rope(kr[:, perm]),
    ], axis=1).astype(BF16)
    qw = MLA_NOPE + MLA_ROPE
    nope = np.arange(MLA_NOPE)
    rope = MLA_NOPE + np.arange(MLA_ROPE)
    wq = (_pad_heads(w_uq, qw, nope, 0) + _pad_heads(w_uq, qw, rope, MLA_NOPE)).astype(BF16)
    wqp = _pad_heads(w_uq, qw, rope[perm], MLA_NOPE).astype(BF16)
    kvw = MLA_NOPE + MLA_V
    wk = _pad_heads(w_ukv, kvw, nope, 0).astype(BF16)
    wv = _pad_heads(w_ukv, kvw, MLA_NOPE + np.arange(MLA_V), 0).astype(BF16)
    return w_cat, wq, wqp, wk, wv


def kernel(x, c, ctx, c_ctx, ada_w, ada_b, norm1_g, norm2_g, ab_w_in, ab_w_out, hgrn_lb_logits, hgrn_onorm_g, pool_w,
           pool_scale, cd_w_in, cd_w_out, na_rpb, mla_q_norm_g, mla_w_uq, mla_kv_norm_g, mla_w_ukv, moe_w_rg, moe_b_rg,
           moe_w_re, moe_b_re, moe_w_gate, moe_w_up, moe_w_down, final_norm_g):
    b, n, d = x.shape
    n_ctx = ctx.shape[1]
    assert ada_w.shape[0] == 2 and ab_w_in.shape[0] == 1 and cd_w_in.shape[0] == 1 and b + 1 <= 8
    tm = 512

    cc = jnp.zeros((8, d), F32).at[:b].set(c).at[b].set(c_ctx)
    mods = _ada(cc, ada_w, ada_b).reshape(2, 8, 6, d)
    mods_lat = [mods[l, :b] for l in range(2)]
    mods_ctx = [jnp.broadcast_to(mods[l, b:b + 1], (b, 6, d)) for l in range(2)]
    lb = jnp.cumsum(jax.nn.softmax(hgrn_lb_logits.astype(F32), axis=1), axis=1)[:, 0]

    w_in0 = ab_w_in[0].astype(BF16)
    w_out0 = ab_w_out[0].astype(BF16)
    pw0 = pool_w[0].astype(BF16)
    ab_cols = w_in0.shape[1]
    (u_ctx,) = _in_proj(ctx, norm1_g[0], mods_ctx[0], w_in0, ((0, ab_cols),), (F32,), tm)
    (u_lat,) = _in_proj(x, norm1_g[0], mods_lat[0], w_in0, ((0, ab_cols),), (F32,), tm)
    s0 = jnp.zeros((b, 2, A_HEADS, A_HEAD_DIM, A_HEAD_DIM), F32)
    ocf, ocb, s_ctx = _hgrn_scan(u_ctx, lb, s0, 256)
    olf, olb, _ = _hgrn_scan(u_lat, lb, s_ctx, 256)
    t_lat, t_ctx = b * n, b * n_ctx
    rows = t_lat + t_ctx
    assert n % tm == 0 and t_ctx % tm == 0
    xa = _ab_out(olf, olb, u_lat, x, mods_lat[0], hgrn_onorm_g[0], pw0, pool_scale[0], w_out0, tm, rows, 0)
    xa = _ab_out(ocf, ocb, u_ctx, ctx, mods_ctx[0], hgrn_onorm_g[0], pw0, pool_scale[0], w_out0, tm, rows, t_lat, xa)
    moe0 = _moe_params(moe_w_rg[0], moe_b_rg[0], moe_w_re[0], moe_b_re[0], moe_w_gate[0], moe_w_up[0], moe_w_down[0])
    mods_all = [jnp.concatenate([mods_lat[l], mods[l, b:b + 1]], axis=0) for l in range(2)]
    xa = _moe(xa.reshape(rows, d), norm2_g[0], mods_all[0], n // tm, moe0, final_norm_g, False, tm)

    w_cat, wq, wqp, wk, wv = _cd_params(cd_w_in[0], mla_w_uq[0], mla_w_ukv[0])
    na_w = 3 * C_WIDTH
    splits = ((0, na_w), (na_w, w_cat.shape[1]))
    ua_ctx, ub_ctx = _in_proj(xa, norm1_g[1], mods_ctx[1], w_cat, splits, (BF16, F32), tm, (b, n_ctx, t_lat))
    ua_lat, ub_lat = _in_proj(xa, norm1_g[1], mods_lat[1], w_cat, splits, (BF16, F32), tm, (b, n, 0))
    q_g = mla_q_norm_g[0].reshape(1, -1)
    kv_g = mla_kv_norm_g[0].reshape(1, -1)
    cos_l, sin_l = _rope_tables(n)
    cos_c, sin_c = _identity_rope_tables(n_ctx)
    k_c, v_c = _mla_proj(ub_ctx, cos_c, sin_c, q_g, kv_g, wq, wqp, wk, wv, False, tm)
    q_l, k_l, v_l = _mla_proj(ub_lat, cos_l, sin_l, q_g, kv_g, wq, wqp, wk, wv, True, tm)
    d_lat = _mla_attention(q_l, jnp.concatenate([k_c, k_l], axis=2), jnp.concatenate([v_c, v_l], axis=3), 256, 256)
    c_lat = _na_attention(ua_lat, ua_ctx, _na_tables(na_rpb[0], n // GRID_W))
    xl = _cd_out(c_lat, d_lat, xa, mods_lat[1], cd_w_out[0].astype(BF16), tm)
    moe1 = _moe_params(moe_w_rg[1], moe_b_rg[1], moe_w_re[1], moe_b_re[1], moe_w_gate[1], moe_w_up[1], moe_w_down[1])
    out = _moe(xl.reshape(t_lat, d), norm2_g[1], mods_lat[1], n // tm, moe1, final_norm_g, True, tm)
    return out.reshape(b, n, d)
```

```python
import functools

import numpy as np
import jax
import jax.numpy as jnp
from jax import lax
from jax.experimental import pallas as pl
from jax.experimental.pallas import tpu as pltpu

F32 = jnp.float32
BF16 = jnp.bfloat16

EPS = 1e-6
NEG = -1e30

GRID_W = 64
A_HEADS = 4
A_HEAD_DIM = 128
A_WIDTH = A_HEADS * A_HEAD_DIM
POOL_WINDOWS = (2, 4, 8, 16)
B_GROUP = 128
B_WIDTH = B_GROUP * len(POOL_WINDOWS)
POOL_HALO = 16
C_HEADS = 8
C_HEAD_DIM = 64
C_WIDTH = C_HEADS * C_HEAD_DIM
NA_ROWS = 8
NA_COLS = 16
NA_QROWS = 4
D_HEADS = 8
MLA_Q_RANK = 256
MLA_KV_RANK = 128
MLA_NOPE = 64
MLA_ROPE = 32
MLA_V = 64
MLA_PAD = 128
ROPE_THETA = 10000.0
MOE_GROUPS = 4
MOE_EPG = 8
MOE_EXPERTS = MOE_GROUPS * MOE_EPG
MOE_HIDDEN = 256
LANES = 128
SLOT_BLOCK = 16
STEP_BLOCKS = 32
VMEM_LIMIT = 56 * 1024 * 1024

NT = (((1,), (1,)), ((), ()))
TN = (((0,), (0,)), ((), ()))


def _cparams(*sem):
    return pltpu.CompilerParams(dimension_semantics=sem, vmem_limit_bytes=VMEM_LIMIT)


def _sigmoid(x):
    return 1.0 / (1.0 + jnp.exp(-x))


def _silu(x):
    return x * _sigmoid(x)


def _dot(a, b):
    return jnp.dot(a, b, preferred_element_type=F32)


def _rmsnorm(x, g):
    return x * lax.rsqrt(jnp.mean(x * x, axis=-1, keepdims=True) + EPS) * g


def _ada_kernel(c_ref, w_ref, b_ref, o_ref):
    s = _silu(c_ref[...])
    o_ref[0] = jnp.dot(s, w_ref[0], precision=lax.Precision.HIGHEST, preferred_element_type=F32) + b_ref[0]


def _ada(cc, ada_w, ada_b):
    depth, d, n6 = ada_w.shape
    tn = n6 // 4
    return pl.pallas_call(
        _ada_kernel,
        grid=(depth, n6 // tn),
        in_specs=[
            pl.BlockSpec((8, d), lambda l, j: (0, 0)),
            pl.BlockSpec((1, d, tn), lambda l, j: (l, 0, j)),
            pl.BlockSpec((1, 1, tn), lambda l, j: (l, 0, j)),
        ],
        out_specs=pl.BlockSpec((1, 8, tn), lambda l, j: (l, 0, j)),
        out_shape=jax.ShapeDtypeStruct((depth, 8, n6), F32),
        compiler_params=_cparams("parallel", "parallel"),
        name="ada_mod",
    )(cc, ada_w, ada_b.reshape(depth, 1, n6))


def _in_kernel(x_ref, g_ref, m_ref, w_ref, *o_refs, splits):
    h = _rmsnorm(x_ref[0], g_ref[...]) * (1.0 + m_ref[0, 1:2, :]) + m_ref[0, 0:1, :]
    hb = h.astype(BF16)
    for o_ref, (a, b) in zip(o_refs, splits):
        o_ref[0] = _dot(hb, w_ref[:, a:b]).astype(o_ref.dtype)


def _in_proj(x, gain, mods, w, splits, dtypes, tm, flat=None):
    if flat is None:
        b, n, d = x.shape
        tm = min(tm, n)
        x_spec = pl.BlockSpec((1, tm, d), lambda bi, i: (bi, i, 0))
    else:
        b, n, row0 = flat
        d = x.shape[-1]
        tm = min(tm, n)
        x = x.reshape(1, -1, d)
        x_spec = pl.BlockSpec((1, tm, d), lambda bi, i: (0, row0 // tm + bi * (n // tm) + i, 0))
    outs = [jax.ShapeDtypeStruct((b, n, hi - lo), dt) for (lo, hi), dt in zip(splits, dtypes)]
    return pl.pallas_call(
        functools.partial(_in_kernel, splits=splits),
        grid=(b, n // tm),
        in_specs=[
            x_spec,
            pl.BlockSpec((1, d), lambda bi, i: (0, 0)),
            pl.BlockSpec((1, 6, d), lambda bi, i: (bi, 0, 0)),
            pl.BlockSpec(w.shape, lambda bi, i: (0, 0)),
        ],
        out_specs=[pl.BlockSpec((1, tm, hi - lo), lambda bi, i: (bi, i, 0)) for lo, hi in splits],
        out_shape=outs,
        compiler_params=_cparams("parallel", "parallel"),
        name="in_proj",
    )(x, gain.reshape(1, d), mods, w)


HG_SUB = 64


def _hgrn_direction(q_raw, fz, v, lb, st_ref, d, o_ref, reverse):
    rows = q_raw.shape[0]
    c = HG_SUB
    f = lb + (1.0 - lb) * _sigmoid(fz)
    k = 1.0 - f
    g = jnp.log(f)
    q = _silu(q_raw)
    r_i = lax.broadcasted_iota(jnp.int32, (c, c), 0)
    c_i = lax.broadcasted_iota(jnp.int32, (c, c), 1)
    keep = (c_i >= r_i) if reverse else (c_i <= r_i)
    tri = jnp.where(keep, 1.0, 0.0).astype(BF16)
    order = range(rows // c - 1, -1, -1) if reverse else range(rows // c)
    for ci in order:
        sl = slice(ci * c, (ci + 1) * c)
        gc = g[sl]
        g_hi = gc.astype(BF16)
        g_lo = (gc - g_hi.astype(F32)).astype(BF16)
        bc = _dot(tri, g_hi) + _dot(tri, g_lo)
        ref = bc[c // 2:c // 2 + 1]
        tot = bc[0:1] if reverse else bc[c - 1:c]
        qt = q[sl] * jnp.exp(bc - ref)
        kt = k[sl] * jnp.exp(ref - bc)
        qd = (qt * jnp.exp(ref)).astype(BF16)
        kd = (kt * jnp.exp(tot - ref)).astype(BF16)
        qt = qt.astype(BF16)
        kt = kt.astype(BF16)
        vb = v[sl].astype(BF16)
        dec = jnp.exp(tot)
        for h in range(A_HEADS):
            hs = slice(h * A_HEAD_DIM, (h + 1) * A_HEAD_DIM)
            att = lax.dot_general(qt[:, hs], kt[:, hs], NT, preferred_element_type=F32)
            att = jnp.where(keep, att, 0.0).astype(BF16)
            st = st_ref[d, h]
            o = _dot(att, vb[:, hs]) + lax.dot_general(qd[:, hs], st.astype(BF16), NT, preferred_element_type=F32)
            o_ref[0, sl, hs] = o
            st_ref[d, h] = st * dec[:, hs] + lax.dot_general(vb[:, hs], kd[:, hs], TN, preferred_element_type=F32)


def _hgrn_kernel(qf_ref, ff_ref, vf_ref, qb_ref, fb_ref, vb_ref, lb_ref, s0_ref, of_ref, ob_ref, sfin_ref, st_ref):
    j = pl.program_id(1)

    @pl.when(j == 0)
    def _():
        st_ref[...] = s0_ref[0]

    _hgrn_direction(qf_ref[0], ff_ref[0], vf_ref[0], lb_ref[0:1], st_ref, 0, of_ref, False)
    _hgrn_direction(qb_ref[0], fb_ref[0], vb_ref[0], lb_ref[1:2], st_ref, 1, ob_ref, True)

    @pl.when(j == pl.num_programs(1) - 1)
    def _():
        sfin_ref[0] = st_ref[...]


def _hgrn_scan(u, lb, s0, rows):
    b, n, _ = u.shape
    rows = min(rows, n)
    nb = n // rows
    w = A_WIDTH

    def fwd(col):
        return pl.BlockSpec((1, rows, w), lambda bi, j: (bi, j, col))

    def bwd(col):
        return pl.BlockSpec((1, rows, w), lambda bi, j: (bi, nb - 1 - j, col))

    st_spec = pl.BlockSpec((1, 2, A_HEADS, A_HEAD_DIM, A_HEAD_DIM), lambda bi, j: (bi, 0, 0, 0, 0))
    return pl.pallas_call(
        _hgrn_kernel,
        grid=(b, nb),
        in_specs=[fwd(0), fwd(1), fwd(3), bwd(0), bwd(2), bwd(3), pl.BlockSpec((2, w), lambda bi, j: (0, 0)), st_spec],
        out_specs=[
            pl.BlockSpec((1, rows, w), lambda bi, j: (bi, j, 0)),
            pl.BlockSpec((1, rows, w), lambda bi, j: (bi, nb - 1 - j, 0)),
            st_spec,
        ],
        out_shape=[
            jax.ShapeDtypeStruct((b, n, w), F32),
            jax.ShapeDtypeStruct((b, n, w), F32),
            jax.ShapeDtypeStruct(s0.shape, F32),
        ],
        scratch_shapes=[pltpu.VMEM((2, A_HEADS, A_HEAD_DIM, A_HEAD_DIM), F32)],
        compiler_params=_cparams("parallel", "arbitrary"),
        name="hgrn_scan",
    )(u, u, u, u, u, u, lb, s0)


def _ab_out_kernel(of_ref, ob_ref, ug_ref, up_ref, pprev_ref, pnext_ref, x_ref, m_ref, on_ref, pw_ref, ps_ref,
                   wo_ref, o_ref, *, n):
    i = pl.program_id(1)
    tm = x_ref.shape[1]
    o = of_ref[0] + ob_ref[0]
    gate = _silu(ug_ref[0])
    parts = []
    for h in range(A_HEADS):
        hs = slice(h * A_HEAD_DIM, (h + 1) * A_HEAD_DIM)
        parts.append(_rmsnorm(o[:, hs], on_ref[...]) * gate[:, hs])
    main = up_ref[0]
    prev = jnp.where(i > 0, pprev_ref[0], 0.0)
    nxt = jnp.where(i < pl.num_programs(1) - 1, pnext_ref[0], 0.0)
    ext = jnp.concatenate([prev, main, nxt], axis=0)
    ext_rows = tm + 2 * POOL_HALO
    t = i * tm + lax.broadcasted_iota(jnp.int32, (tm, 1), 0)
    for gi, win in enumerate(POOL_WINDOWS):
        gs = slice(gi * B_GROUP, (gi + 1) * B_GROUP)
        acc = ext[:, gs]
        acc = acc + pltpu.roll(acc, 1, 0)
        half = 1
        while 2 * half < win:
            acc = pltpu.roll(acc, half, 0) + pltpu.roll(acc, ext_rows - half, 0)
            half *= 2
        cnt = jnp.minimum(t + (win - win // 2), n) - jnp.maximum(t - win // 2, 0)
        mean = acc[POOL_HALO:POOL_HALO + tm] / cnt.astype(F32)
        pooled = _dot((mean - main[:, gs]).astype(BF16), pw_ref[gi])
        parts.append(pooled * ps_ref[:, gs])
    mix = jnp.concatenate(parts, axis=-1).astype(BF16)
    o_ref[0] = x_ref[0] + m_ref[0, 2:3, :] * _dot(mix, wo_ref[...])


def _ab_out(o_f, o_b, u, x, mods, onorm_g, pool_w, pool_scale, w_out, tm):
    b, n, d = x.shape
    tm = min(tm, n)
    nt = n // tm
    hb = tm // POOL_HALO
    last_halo = n // POOL_HALO - 1
    w = A_WIDTH
    tile = lambda col: pl.BlockSpec((1, tm, w), lambda bi, i: (bi, i, col))
    out = pl.pallas_call(
        functools.partial(_ab_out_kernel, n=n),
        grid=(b, nt),
        in_specs=[
            tile(0), tile(0), tile(4), tile(5),
            pl.BlockSpec((1, POOL_HALO, w), lambda bi, i: (bi, jnp.maximum(i * hb - 1, 0), 5)),
            pl.BlockSpec((1, POOL_HALO, w), lambda bi, i: (bi, jnp.minimum((i + 1) * hb, last_halo), 5)),
            pl.BlockSpec((1, tm, d), lambda bi, i: (bi, i, 0)),
            pl.BlockSpec((1, 6, d), lambda bi, i: (bi, 0, 0)),
            pl.BlockSpec((1, A_HEAD_DIM), lambda bi, i: (0, 0)),
            pl.BlockSpec(pool_w.shape, lambda bi, i: (0, 0, 0)),
            pl.BlockSpec((1, B_WIDTH), lambda bi, i: (0, 0)),
            pl.BlockSpec(w_out.shape, lambda bi, i: (0, 0)),
        ],
        out_specs=pl.BlockSpec((1, tm, d), lambda bi, i: (bi, i, 0)),
        out_shape=jax.ShapeDtypeStruct((b, n, d), F32),
        compiler_params=_cparams("parallel", "parallel"),
        name="ab_out",
    )(o_f, o_b, u, u, u, u, x, mods, onorm_g.reshape(1, A_HEAD_DIM), pool_w, pool_scale.reshape(1, B_WIDTH), w_out)
    return out.reshape(b * n, d)


def _slot_rows(tr):
    rows = 2 * tr + MOE_EXPERTS * (SLOT_BLOCK - 1)
    assert rows % SLOT_BLOCK == 0
    return rows


def _drop_tail(kern, *refs, **kw):
    return kern(refs[0], None, *refs[1:], **kw)


def _tile_tokens(x_ref, xt_ref):
    if xt_ref is None:
        return x_ref[...]
    return jnp.where(pl.program_id(0) < pl.num_programs(0) - 1, x_ref[...], xt_ref[...])


def _route_kernel(x_ref, xt_ref, g_ref, m_ref, whi_ref, wlo_ref, br_ref, xs_ref, info_ref, cnt_ref, *, slot_rows):
    tr = x_ref.shape[0]
    h = _rmsnorm(_tile_tokens(x_ref, xt_ref), g_ref[...]) * (1.0 + m_ref[0, 4:5, :]) + m_ref[0, 3:4, :]
    hb = h.astype(BF16)
    hl = (h - hb.astype(F32)).astype(BF16)
    logits = _dot(hb, whi_ref[...]) + _dot(hb, wlo_ref[...]) + _dot(hl, whi_ref[...]) + br_ref[...]
    lane = lax.broadcasted_iota(jnp.int32, (tr, LANES), 1)
    lanef = lane.astype(F32)
    lg = jnp.where(lane < MOE_GROUPS, logits, NEG)
    mg = jnp.max(lg, axis=-1, keepdims=True)
    g_p = 1.0 / jnp.sum(jnp.exp(lg - mg), axis=-1, keepdims=True)
    gidx = jnp.min(jnp.where(lg == mg, lanef, float(LANES)), axis=-1, keepdims=True)
    lo = MOE_GROUPS + MOE_EPG * gidx
    le = jnp.where((lanef >= lo) & (lanef < lo + MOE_EPG), logits, NEG)
    m1 = jnp.max(le, axis=-1, keepdims=True)
    i1 = jnp.min(jnp.where(le == m1, lanef, float(LANES)), axis=-1, keepdims=True)
    le2 = jnp.where(lanef == i1, NEG, le)
    m2 = jnp.max(le2, axis=-1, keepdims=True)
    i2 = jnp.min(jnp.where(le2 == m2, lanef, float(LANES)), axis=-1, keepdims=True)
    ratio = jnp.exp(m2 - m1)
    w1 = g_p / (1.0 + ratio)
    w2 = g_p * ratio / (1.0 + ratio)
    hot1 = lanef == i1
    hot2 = lanef == i2
    hot = jnp.where(hot1, 1.0, jnp.where(hot2, 1.0, 0.0))
    r_i = lax.broadcasted_iota(jnp.int32, (tr, tr), 0)
    c_i = lax.broadcasted_iota(jnp.int32, (tr, tr), 1)
    rank = _dot(jnp.where(c_i < r_i, 1.0, 0.0).astype(BF16), hot.astype(BF16))
    cnt = jnp.sum(hot, axis=0, keepdims=True)
    nblk = jnp.floor((cnt + (SLOT_BLOCK - 1)) * (1.0 / SLOT_BLOCK))
    l_r = lax.broadcasted_iota(jnp.int32, (LANES, LANES), 0)
    l_c = lax.broadcasted_iota(jnp.int32, (LANES, LANES), 1)
    before = jnp.where(l_r < l_c, 1.0, 0.0).astype(BF16)
    off = SLOT_BLOCK * _dot(jnp.broadcast_to(nblk, (8, LANES)).astype(BF16), before)[0:1]
    posm = off + rank
    pos1 = jnp.sum(jnp.where(hot1, posm, 0.0), axis=-1, keepdims=True)
    pos2 = jnp.sum(jnp.where(hot2, posm, 0.0), axis=-1, keepdims=True)
    info = jnp.where(lane == 0, pos1, jnp.where(lane == 1, pos2, jnp.where(lane == 2, w1, jnp.where(lane == 3, w2, 0.0))))
    info_ref[...] = info
    pos_t = info.T.astype(jnp.int32)
    row = lax.broadcasted_iota(jnp.int32, (slot_rows, tr), 0)
    sel = jnp.where(row == pos_t[0:1], 1.0, jnp.where(row == pos_t[1:2], 1.0, 0.0)).astype(BF16)
    xs_ref[...] = _dot(sel, hb).astype(BF16)
    cnt_ref[0] = jnp.broadcast_to(cnt, (8, LANES))


def _token_specs(x2d, x_tail, tr, index):
    d = x2d.shape[1]
    nt = x2d.shape[0] // tr
    if x_tail is None:
        return nt, [pl.BlockSpec((tr, d), index(lambda i: (i, 0)))], [x2d]
    assert x_tail.shape == (tr, d)
    return nt + 1, [pl.BlockSpec((tr, d), index(lambda i: (jnp.minimum(i, nt - 1), 0))),
                    pl.BlockSpec((tr, d), index(lambda i: (0, 0)))], [x2d, x_tail]


def _moe_route(x2d, x_tail, gain, mods, tiles_per_mod, w_hi, w_lo, b_r, tr):
    d = x2d.shape[1]
    nt, x_specs, x_args = _token_specs(x2d, x_tail, tr, lambda f: f)
    t = nt * tr
    sr = _slot_rows(tr)
    kern = _route_kernel if x_tail is not None else functools.partial(_drop_tail, _route_kernel)
    return pl.pallas_call(
        functools.partial(kern, slot_rows=sr),
        grid=(nt,),
        in_specs=x_specs + [
            pl.BlockSpec((1, d), lambda i: (0, 0)),
            pl.BlockSpec((1, 6, d), lambda i: (jnp.minimum(i // tiles_per_mod, mods.shape[0] - 1), 0, 0)),
            pl.BlockSpec((d, LANES), lambda i: (0, 0)),
            pl.BlockSpec((d, LANES), lambda i: (0, 0)),
            pl.BlockSpec((1, LANES), lambda i: (0, 0)),
        ],
        out_specs=[
            pl.BlockSpec((sr, d), lambda i: (i, 0)),
            pl.BlockSpec((tr, LANES), lambda i: (i, 0)),
            pl.BlockSpec((1, 8, LANES), lambda i: (i, 0, 0)),
        ],
        out_shape=[
            jax.ShapeDtypeStruct((nt * sr, d), BF16),
            jax.ShapeDtypeStruct((t, LANES), F32),
            jax.ShapeDtypeStruct((nt, 8, LANES), F32),
        ],
        compiler_params=_cparams("parallel"),
        name="moe_route",
    )(*x_args, gain.reshape(1, d), mods, w_hi, w_lo, b_r)


def _tables_kernel(cnt_ref, src_ref, inv_ref, exp_ref, valid_ref, run_ref, *, ntiles, bpt, nsteps):
    def fill(ref, n, val):
        unroll = 8

        def body(j, carry):
            for u in range(unroll):
                ref[j * unroll + u] = val
            return carry

        lax.fori_loop(0, n // unroll, body, 0)
        for j in range(n // unroll * unroll, n):
            ref[j] = val

    fill(src_ref, nsteps * STEP_BLOCKS, 0)
    fill(inv_ref, ntiles * bpt, 0)
    fill(exp_ref, nsteps, 0)
    fill(valid_ref, nsteps, 0)
    fill(run_ref, ntiles, 0)

    def per_expert(e, pos):
        def per_tile(i, p):
            nb = (cnt_ref[i, e] + (SLOT_BLOCK - 1)) // SLOT_BLOCK
            first = run_ref[i]

            def per_block(r, carry):
                src_ref[p + r] = i * bpt + first + r
                inv_ref[i * bpt + first + r] = p + r
                return carry

            lax.fori_loop(0, nb, per_block, 0)
            run_ref[i] = first + nb
            return p + nb

        end = lax.fori_loop(0, ntiles, per_tile, pos)
        end = (end + (STEP_BLOCKS - 1)) // STEP_BLOCKS * STEP_BLOCKS

        def mark(s, carry):
            exp_ref[s] = e
            valid_ref[s] = 1
            return carry

        lax.fori_loop(pos // STEP_BLOCKS, end // STEP_BLOCKS, mark, 0)
        return end

    lax.fori_loop(0, MOE_EXPERTS, per_expert, 0)


def _expert_tables(cnt, bpt, nsteps):
    ntiles = cnt.shape[0]
    smem = pl.BlockSpec(memory_space=pltpu.SMEM)
    i32 = lambda n: jax.ShapeDtypeStruct((n,), jnp.int32)
    return pl.pallas_call(
        functools.partial(_tables_kernel, ntiles=ntiles, bpt=bpt, nsteps=nsteps),
        in_specs=[smem],
        out_specs=[smem] * 4,
        out_shape=[i32(nsteps * STEP_BLOCKS), i32(ntiles * bpt), i32(nsteps), i32(nsteps)],
        scratch_shapes=[pltpu.SMEM((ntiles,), jnp.int32)],
        name="moe_tables",
    )(cnt)


def _block_gather(table_ref, first, nblocks, src_hbm, buf_ref, slot, sem_ref):
    return [pltpu.make_async_copy(src_hbm.at[table_ref[first + kk]],
                                  buf_ref.at[slot, pl.ds(kk * SLOT_BLOCK, SLOT_BLOCK)], sem_ref.at[slot])
            for kk in range(nblocks)]


def _experts_kernel(src_ref, exp_ref, valid_ref, xs_hbm, wg_ref, wu_ref, wd_ref, y_ref, xbuf_ref, sem_ref, wgb_ref,
                    wub_ref, wdb_ref):
    s = pl.program_id(0)
    slot = s % 2

    def gather(step, to_slot):
        return _block_gather(src_ref, step * STEP_BLOCKS, STEP_BLOCKS, xs_hbm, xbuf_ref, to_slot, sem_ref)

    @pl.when((s == 0) & (valid_ref[0] > 0))
    def _():
        for cp in gather(0, 0):
            cp.start()

    nxt = jnp.minimum(s + 1, pl.num_programs(0) - 1)

    @pl.when((s + 1 < pl.num_programs(0)) & (valid_ref[nxt] > 0))
    def _():
        for cp in gather(s + 1, 1 - slot):
            cp.start()

    @pl.when((s == 0) | (exp_ref[s] != exp_ref[jnp.maximum(s - 1, 0)]))
    def _():
        wgb_ref[...] = wg_ref[0, 0, 0].astype(BF16)
        wub_ref[...] = wu_ref[0, 0, 0].astype(BF16)
        wdb_ref[...] = wd_ref[0, 0, 0].astype(BF16)

    @pl.when(valid_ref[s] > 0)
    def _():
        for cp in gather(s, slot):
            cp.wait()
        x = xbuf_ref[slot]
        a = _silu(_dot(x, wgb_ref[...])) * _dot(x, wub_ref[...])
        y_ref[...] = _dot(a.astype(BF16), wdb_ref[...]).astype(BF16)

    @pl.when(valid_ref[s] == 0)
    def _():
        y_ref[...] = jnp.zeros(y_ref.shape, y_ref.dtype)


def _moe_experts(xs, src, step_e, valid, w_gate, w_up, w_down, layer, nsteps):
    rows, d = xs.shape
    f = w_gate.shape[-1]
    xs3 = xs.reshape(rows // SLOT_BLOCK, SLOT_BLOCK, d)
    step_rows = STEP_BLOCKS * SLOT_BLOCK
    w_blk = lambda shape: pl.BlockSpec((1, 1, 1) + shape,
                                       lambda s, sr, ex, va: (layer, ex[s] // MOE_EPG, ex[s] % MOE_EPG, 0, 0))
    grid_spec = pltpu.PrefetchScalarGridSpec(
        num_scalar_prefetch=3,
        grid=(nsteps,),
        in_specs=[pl.BlockSpec(memory_space=pl.ANY), w_blk((d, f)), w_blk((d, f)), w_blk((f, d))],
        out_specs=pl.BlockSpec((step_rows, d), lambda s, sr, ex, va: (s, 0)),
        scratch_shapes=[pltpu.VMEM((2, step_rows, d), BF16), pltpu.SemaphoreType.DMA((2,)),
                        pltpu.VMEM((d, f), BF16), pltpu.VMEM((d, f), BF16), pltpu.VMEM((f, d), BF16)],
    )
    return pl.pallas_call(
        _experts_kernel,
        grid_spec=grid_spec,
        out_shape=jax.ShapeDtypeStruct((nsteps * step_rows, d), BF16),
        compiler_params=_cparams("arbitrary"),
        name="moe_experts",
    )(src, step_e, valid, xs3, w_gate, w_up, w_down)


def _combine_kernel(inv_ref, x_ref, xt_ref, info_ref, m_ref, fg_ref, ys_hbm, o_ref, ybuf_ref, sem_ref, *, bpt, final):
    i = pl.program_id(0)
    slot = i % 2
    tr = x_ref.shape[0]

    def gather(tile, to_slot):
        return _block_gather(inv_ref, tile * bpt, bpt, ys_hbm, ybuf_ref, to_slot, sem_ref)

    @pl.when(i == 0)
    def _():
        for cp in gather(0, 0):
            cp.start()

    @pl.when(i + 1 < pl.num_programs(0))
    def _():
        for cp in gather(i + 1, 1 - slot):
            cp.start()

    info = info_ref[...]
    col = lax.broadcasted_iota(jnp.int32, (tr, bpt * SLOT_BLOCK), 1)
    wsel = jnp.where(col == info[:, 0:1].astype(jnp.int32), info[:, 2:3],
                     jnp.where(col == info[:, 1:2].astype(jnp.int32), info[:, 3:4], 0.0))
    for cp in gather(i, slot):
        cp.wait()
    y = _dot(wsel.astype(BF16), ybuf_ref[slot])
    out = _tile_tokens(x_ref, xt_ref) + m_ref[0, 5:6, :] * y
    if final:
        out = _rmsnorm(out, fg_ref[...])
    o_ref[...] = out


def _moe_combine(x2d, x_tail, ys, inv, info, mods, tiles_per_mod, final_g, tr, bpt, final):
    d = x2d.shape[1]
    nt, x_specs, x_args = _token_specs(x2d, x_tail, tr, lambda f: (lambda i, iv: f(i)))
    t = nt * tr
    ys3 = ys.reshape(ys.shape[0] // SLOT_BLOCK, SLOT_BLOCK, d)
    kern = _combine_kernel
    if x_tail is None:
        kern = lambda inv_ref, x_ref, *refs, **kw: _combine_kernel(inv_ref, x_ref, None, *refs, **kw)
    grid_spec = pltpu.PrefetchScalarGridSpec(
        num_scalar_prefetch=1,
        grid=(nt,),
        in_specs=x_specs + [
            pl.BlockSpec((tr, LANES), lambda i, iv: (i, 0)),
            pl.BlockSpec((1, 6, d), lambda i, iv: (jnp.minimum(i // tiles_per_mod, mods.shape[0] - 1), 0, 0)),
            pl.BlockSpec((1, d), lambda i, iv: (0, 0)),
            pl.BlockSpec(memory_space=pl.ANY),
        ],
        out_specs=pl.BlockSpec((tr, d), lambda i, iv: (i, 0)),
        scratch_shapes=[pltpu.VMEM((2, bpt * SLOT_BLOCK, d), BF16), pltpu.SemaphoreType.DMA((2,))],
    )
    return pl.pallas_call(
        functools.partial(kern, bpt=bpt, final=final),
        grid_spec=grid_spec,
        out_shape=jax.ShapeDtypeStruct((t, d), F32),
        compiler_params=_cparams("arbitrary"),
        name="moe_combine",
    )(inv, *x_args, info, mods, final_g.reshape(1, d), ys3)


def _moe(x2d, x_tail, gain, mods, tiles_per_mod, params, final_g, final, tr):
    w_hi, w_lo, b_r, w_gate, w_up, w_down, layer = params
    nt = x2d.shape[0] // tr + (x_tail is not None)
    xs, info, cnt = _moe_route(x2d, x_tail, gain, mods, tiles_per_mod, w_hi, w_lo, b_r, tr)
    bpt = _slot_rows(tr) // SLOT_BLOCK
    nsteps = -(-(nt * bpt + MOE_EXPERTS * (STEP_BLOCKS - 1)) // STEP_BLOCKS)
    counts = cnt[:, 0, MOE_GROUPS:MOE_GROUPS + MOE_EXPERTS].astype(jnp.int32)
    src, inv, step_e, valid = _expert_tables(counts, bpt, nsteps)
    ys = _moe_experts(xs, src, step_e, valid, w_gate, w_up, w_down, layer, nsteps)
    return _moe_combine(x2d, x_tail, ys, inv, info, mods, tiles_per_mod, final_g, tr, bpt, final)


def _mla_proj_kernel(cq_ref, ckv_ref, kr_ref, krp_ref, cos_ref, sin_ref, qg_ref, kg_ref, wq_ref, wqp_ref, wk_ref,
                     wv_ref, vone_ref, *o_refs, need_q, q_scale):
    cos = cos_ref[...]
    sin = sin_ref[...]
    ckv = _rmsnorm(ckv_ref[0], kg_ref[...]).astype(BF16)
    k_rope = kr_ref[0] * cos + krp_ref[0] * sin
    kn = _dot(ckv, wk_ref[...])
    if need_q:
        q_ref, k_ref, v_ref = o_refs
    else:
        k_ref, v_ref = o_refs
    vx = _dot(ckv, wv_ref[...]) + vone_ref[...]
    for h in range(D_HEADS):
        hs = slice(h * MLA_PAD, (h + 1) * MLA_PAD)
        k_ref[0, h] = (kn[:, hs] + k_rope).astype(BF16)
        v_ref[0, h] = vx[:, hs].T.astype(BF16)
    if need_q:
        cq = _rmsnorm(cq_ref[0], qg_ref[...]).astype(BF16)
        qm = _dot(cq, wq_ref[...])
        qp = _dot(cq, wqp_ref[...])
        for h in range(D_HEADS):
            hs = slice(h * MLA_PAD, (h + 1) * MLA_PAD)
            q_ref[0, h] = ((qm[:, hs] * cos + qp[:, hs] * sin) * q_scale).T.astype(BF16)


def _mla_proj(u_b, cos, sin, q_g, kv_g, wq, wqp, wk, wv, need_q, tm):
    b, n, _ = u_b.shape
    tm = min(tm, n)
    row_major = (jax.ShapeDtypeStruct((b, D_HEADS, n, MLA_PAD), BF16),
                 pl.BlockSpec((1, D_HEADS, tm, MLA_PAD), lambda bi, i: (bi, 0, i, 0)))
    col_major = (jax.ShapeDtypeStruct((b, D_HEADS, MLA_PAD, n), BF16),
                 pl.BlockSpec((1, D_HEADS, MLA_PAD, tm), lambda bi, i: (bi, 0, 0, i)))
    outs, specs = zip(*(([col_major] if need_q else []) + [row_major, col_major]))
    full = lambda a: pl.BlockSpec(a.shape, lambda bi, i: (0,) * a.ndim)
    vone = jnp.tile(jnp.concatenate([jnp.zeros((1, MLA_V), F32), jnp.ones((1, MLA_PAD - MLA_V), F32)], axis=1),
                    (1, D_HEADS))
    q_scale = float((MLA_NOPE + MLA_ROPE) ** -0.5 * np.log2(np.e))
    return pl.pallas_call(
        functools.partial(_mla_proj_kernel, need_q=need_q, q_scale=q_scale),
        grid=(b, n // tm),
        in_specs=[
            pl.BlockSpec((1, tm, MLA_Q_RANK), lambda bi, i: (bi, i, 0)),
            pl.BlockSpec((1, tm, MLA_KV_RANK), lambda bi, i: (bi, i, 2)),
            pl.BlockSpec((1, tm, MLA_PAD), lambda bi, i: (bi, i, 3)),
            pl.BlockSpec((1, tm, MLA_PAD), lambda bi, i: (bi, i, 4)),
            pl.BlockSpec((tm, MLA_PAD), lambda bi, i: (i, 0)),
            pl.BlockSpec((tm, MLA_PAD), lambda bi, i: (i, 0)),
            full(q_g), full(kv_g), full(wq), full(wqp), full(wk), full(wv), full(vone),
        ],
        out_specs=list(specs),
        out_shape=list(outs),
        compiler_params=_cparams("parallel", "parallel"),
        name="mla_proj",
    )(u_b, u_b, u_b, u_b, cos, sin, q_g, kv_g, wq, wqp, wk, wv, vone)


def _mla_attn_kernel(q_ref, qn_ref, k_ref, v_ref, o_ref, acc0_ref, acc1_ref, s0_ref, s1_ref, m_ref, *, tk):
    tq = q_ref.shape[3]
    nk = k_ref.shape[2]
    nchunks = nk // tk
    neg = jnp.full((8, tq), NEG, F32)
    s_refs = (s0_ref, s1_ref)
    acc_refs = (acc0_ref, acc1_ref)

    def chunk(c):
        return pl.ds(pl.multiple_of(c * tk, tk), tk)

    def scores(ks, hh, q, m):
        s = _dot(k_ref[0, hh, ks, :], q)
        s_refs[hh][ks, :] = s
        return jnp.maximum(m, jnp.max(s.reshape(tk // 8, 8, tq), axis=0))

    def weight(ks, hh, m_row):
        p = jnp.exp2(s_refs[hh][ks, :] - m_row).astype(BF16)
        acc_refs[hh][...] += _dot(v_ref[0, hh, :, ks], p)

    @pl.when(pl.program_id(2) == 0)
    def _():
        m_ref[...] = lax.fori_loop(0, nchunks, lambda c, m: scores(chunk(c), 0, q_ref[0, 0], m), neg)

    acc0_ref[...] = jnp.zeros(acc0_ref.shape, F32)
    acc1_ref[...] = jnp.zeros(acc1_ref.shape, F32)
    m0 = jnp.max(m_ref[...], axis=0, keepdims=True)

    def first(c, m1):
        weight(chunk(c), 0, m0)
        return scores(chunk(c), 1, q_ref[0, 1], m1)

    m1 = jnp.max(lax.fori_loop(0, nchunks, first, neg, unroll=True), axis=0, keepdims=True)

    def second(c, m0_next):
        weight(chunk(c), 1, m1)
        return scores(chunk(c), 0, qn_ref[0, 0], m0_next)

    m_ref[...] = lax.fori_loop(0, nchunks, second, neg, unroll=True)
    o_t = jnp.concatenate([a[:MLA_V] / a[MLA_V:MLA_V + 1] for a in acc_refs], axis=0)
    o_ref[0] = o_t.T.astype(o_ref.dtype)


def _mla_attention(q_t, k, v_t, tq, tk):
    b, h, _, n = q_t.shape
    nk = k.shape[2]
    tq = min(tq, n)
    tk = max(t for t in range(2 * LANES, tk + 1, 2 * LANES) if nk % t == 0)
    return pl.pallas_call(
        functools.partial(_mla_attn_kernel, tk=tk),
        grid=(b, h // 2, n // tq),
        in_specs=[
            pl.BlockSpec((1, 2, MLA_PAD, tq), lambda bi, hp, i: (bi, hp, 0, i)),
            pl.BlockSpec((1, 2, MLA_PAD, tq), lambda bi, hp, i: (bi, hp, 0, jnp.minimum(i + 1, n // tq - 1))),
            pl.BlockSpec((1, 2, nk, MLA_PAD), lambda bi, hp, i: (bi, hp, 0, 0)),
            pl.BlockSpec((1, 2, MLA_PAD, nk), lambda bi, hp, i: (bi, hp, 0, 0)),
        ],
        out_specs=pl.BlockSpec((1, tq, 2 * MLA_V), lambda bi, hp, i: (bi, i, hp)),
        out_shape=jax.ShapeDtypeStruct((b, n, h * MLA_V), BF16),
        scratch_shapes=[pltpu.VMEM((MLA_PAD, tq), F32), pltpu.VMEM((MLA_PAD, tq), F32),
                        pltpu.VMEM((nk, tq), F32), pltpu.VMEM((nk, tq), F32), pltpu.VMEM((8, tq), F32)],
        compiler_params=_cparams("parallel", "parallel", "arbitrary"),
        name="mla_attention",
    )(q_t, q_t, k, v_t)


def _na_kernel(q_ref, kp_ref, km_ref, kn_ref, kc_ref, vp_ref, vm_ref, vn_ref, vc_ref, tab_ref, o_ref):
    tq = q_ref.shape[1]
    nloc = 3 * tq
    lane = lax.broadcasted_iota(jnp.int32, (tq, LANES), 1)
    q = q_ref[0]
    k_all = jnp.concatenate([kp_ref[0], km_ref[0], kn_ref[0], kc_ref[0]], axis=0)
    v_all = jnp.concatenate([vp_ref[0], vm_ref[0], vn_ref[0], vc_ref[0]], axis=0)
    outs = []
    for hh in range(2):
        in_head = (lane >= hh * C_HEAD_DIM) & (lane < (hh + 1) * C_HEAD_DIM)
        qh = jnp.where(in_head, q, jnp.zeros_like(q))
        s = lax.dot_general(qh, k_all, NT, preferred_element_type=F32)
        s_loc = s[:, :nloc] + tab_ref[0, hh]
        s_ctx = s[:, nloc:]
        m = jnp.maximum(jnp.max(s_loc, axis=-1, keepdims=True), jnp.max(s_ctx, axis=-1, keepdims=True))
        p_loc = jnp.exp(s_loc - m)
        p_ctx = jnp.exp(s_ctx - m)
        l = jnp.sum(p_loc, axis=-1, keepdims=True) + jnp.sum(p_ctx, axis=-1, keepdims=True)
        o = _dot(p_loc.astype(BF16), v_all[:nloc]) + _dot(p_ctx.astype(BF16), v_all[nloc:])
        outs.append(o / l)
    o_ref[0] = jnp.where(lane < C_HEAD_DIM, outs[0], outs[1]).astype(o_ref.dtype)


def _na_tables(rpb, rows):
    h = rpb.shape[0]
    w = GRID_W
    qc = np.arange(w)
    kc = np.arange(w)
    cs = np.clip(qc - NA_COLS // 2, 0, w - NA_COLS)
    col_ok = (kc[None, :] >= cs[:, None]) & (kc[None, :] < cs[:, None] + NA_COLS)
    dc = np.clip(kc[None, :] - qc[:, None] + (NA_COLS - 1), 0, 2 * NA_COLS - 2)
    base = jnp.where(col_ok[None, None], rpb.astype(F32)[:, :, dc], NEG)
    base = jnp.concatenate([base, jnp.full((h, 1, w, w), NEG, F32)], axis=1)
    nblk = rows // NA_QROWS
    tabs = []
    for m in (0, 1, nblk - 1):
        qr = NA_QROWS * m + np.arange(NA_QROWS)
        rs = np.clip(qr - NA_ROWS // 2, 0, rows - NA_ROWS)
        kr = NA_QROWS * (m - 1) + np.arange(3 * NA_QROWS)
        ok = (kr[None, :] >= rs[:, None]) & (kr[None, :] < rs[:, None] + NA_ROWS)
        dr = np.where(ok, kr[None, :] - qr[:, None] + (NA_ROWS - 1), 2 * NA_ROWS - 1)
        t = base[:, dr]
        tabs.append(t.transpose(0, 1, 3, 2, 4).reshape(h, NA_QROWS * w, 3 * NA_QROWS * w))
    return jnp.stack(tabs)


def _na_attention(u_lat, u_ctx, tabs):
    b, n, _ = u_lat.shape
    nc = u_ctx.shape[1]
    tq = NA_QROWS * GRID_W
    nblk = n // tq
    pairs = C_HEADS // 2
    prev = lambda i: jnp.maximum(i - 1, 0)
    nxt = lambda i: jnp.minimum(i + 1, nblk - 1)
    blk = lambda col0, f: pl.BlockSpec((1, tq, LANES), lambda bi, hp, i: (bi, f(i), col0 + hp))
    ctx = lambda col0: pl.BlockSpec((1, nc, LANES), lambda bi, hp, i: (bi, 0, col0 + hp))
    same = lambda i: i
    sel = lambda i: jnp.where(i == 0, 0, jnp.where(i == nblk - 1, 2, 1))
    return pl.pallas_call(
        _na_kernel,
        grid=(b, pairs, nblk),
        in_specs=[
            blk(0, same),
            blk(pairs, prev), blk(pairs, same), blk(pairs, nxt), ctx(pairs),
            blk(2 * pairs, prev), blk(2 * pairs, same), blk(2 * pairs, nxt), ctx(2 * pairs),
            pl.BlockSpec((1, 2, tq, 3 * tq), lambda bi, hp, i: (sel(i), hp, 0, 0)),
        ],
        out_specs=pl.BlockSpec((1, tq, LANES), lambda bi, hp, i: (bi, i, hp)),
        out_shape=jax.ShapeDtypeStruct((b, n, C_WIDTH), BF16),
        compiler_params=_cparams("parallel", "parallel", "arbitrary"),
        name="na_attention",
    )(u_lat, u_lat, u_lat, u_lat, u_ctx, u_lat, u_lat, u_lat, u_ctx, tabs)


def _cd_out_kernel(c_ref, d_ref, x_ref, m_ref, wo_ref, o_ref):
    wc = c_ref.shape[2]
    o = _dot(c_ref[0], wo_ref[:wc]) + _dot(d_ref[0], wo_ref[wc:])
    o_ref[0] = x_ref[0] + m_ref[0, 2:3, :] * o


def _cd_out(c_lat, d_lat, x2d, mods, w_out, tm):
    b, n, _ = c_lat.shape
    d = x2d.shape[-1]
    tm = min(tm, n)
    nt = n // tm
    flat = pl.BlockSpec((1, tm, d), lambda bi, i: (0, bi * nt + i, 0))
    return pl.pallas_call(
        _cd_out_kernel,
        grid=(b, nt),
        in_specs=[
            pl.BlockSpec((1, tm, c_lat.shape[2]), lambda bi, i: (bi, i, 0)),
            pl.BlockSpec((1, tm, d_lat.shape[2]), lambda bi, i: (bi, i, 0)),
            flat,
            pl.BlockSpec((1, 6, d), lambda bi, i: (bi, 0, 0)),
            pl.BlockSpec(w_out.shape, lambda bi, i: (0, 0)),
        ],
        out_specs=flat,
        out_shape=jax.ShapeDtypeStruct((1, b * n, d), F32),
        compiler_params=_cparams("parallel", "parallel"),
        name="cd_out",
    )(c_lat, d_lat, x2d.reshape(1, -1, d), mods, w_out)


def _moe_params(w_rg, b_rg, w_re, b_re, w_gate, w_up, w_down, layer):
    d = w_rg.shape[0]
    w_r = jnp.zeros((d, LANES), F32).at[:, :MOE_GROUPS].set(w_rg).at[:, MOE_GROUPS:MOE_GROUPS + MOE_EXPERTS].set(w_re)
    b_r = jnp.zeros((1, LANES), F32).at[0, :MOE_GROUPS].set(b_rg).at[0, MOE_GROUPS:MOE_GROUPS + MOE_EXPERTS].set(b_re)
    w_hi = w_r.astype(BF16)
    w_lo = (w_r - w_hi.astype(F32)).astype(BF16)
    return w_hi, w_lo, b_r, w_gate, w_up, w_down, layer


def _rope_perm():
    j = np.arange(MLA_ROPE)
    half = MLA_ROPE // 2
    return (j // half) * half + (j % half + half // 2) % half


def _rope_tables(n):
    half = MLA_ROPE // 2
    nf = half // 2
    t = np.arange(n)
    inv = (np.float32(ROPE_THETA) ** (-np.arange(nf, dtype=np.float32) / np.float32(nf))).astype(np.float32)
    parts_c, parts_s = [], []
    for pos in ((t // GRID_W).astype(np.float32), (t % GRID_W).astype(np.float32)):
        ang = (pos[:, None] * inv[None, :]).astype(np.float32)
        c, s = np.cos(ang).astype(np.float32), np.sin(ang).astype(np.float32)
        parts_c += [c, c]
        parts_s += [-s, s]
    pad = MLA_PAD - MLA_NOPE - MLA_ROPE
    cos = np.concatenate([np.ones((n, MLA_NOPE), np.float32)] + parts_c + [np.zeros((n, pad), np.float32)], axis=1)
    sin = np.concatenate([np.zeros((n, MLA_NOPE), np.float32)] + parts_s + [np.zeros((n, pad), np.float32)], axis=1)
    return jnp.asarray(cos), jnp.asarray(sin)


def _identity_rope_tables(n):
    pad = MLA_PAD - MLA_NOPE - MLA_ROPE
    cos = jnp.concatenate([jnp.ones((n, MLA_NOPE + MLA_ROPE), F32), jnp.zeros((n, pad), F32)], axis=1)
    return cos, jnp.zeros((n, MLA_PAD), F32)


def _pad_heads(w, widths, src_cols, dst_off):
    rank = w.shape[0]
    out = jnp.zeros((rank, D_HEADS, MLA_PAD), F32)
    wh = w.reshape(rank, D_HEADS, widths)[:, :, src_cols]
    return out.at[:, :, dst_off:dst_off + len(src_cols)].set(wh).reshape(rank, D_HEADS * MLA_PAD)


def _cd_params(w_in, w_uq, w_ukv):
    d = w_in.shape[0]
    perm = _rope_perm()
    o = 3 * C_WIDTH
    q_scale = float(C_HEAD_DIM ** -0.5)
    kr = w_in[:, o + MLA_Q_RANK + MLA_KV_RANK:]
    pad_rope = lambda a: jnp.zeros((d, MLA_PAD), F32).at[:, MLA_NOPE:MLA_NOPE + MLA_ROPE].set(a)
    w_cat = jnp.concatenate([
        w_in[:, :C_WIDTH] * q_scale, w_in[:, C_WIDTH:o],
        w_in[:, o:o + MLA_Q_RANK + MLA_KV_RANK], pad_rope(kr), pad_rope(kr[:, perm]),
    ], axis=1).astype(BF16)
    qw = MLA_NOPE + MLA_ROPE
    nope = np.arange(MLA_NOPE)
    rope = MLA_NOPE + np.arange(MLA_ROPE)
    wq = (_pad_heads(w_uq, qw, nope, 0) + _pad_heads(w_uq, qw, rope, MLA_NOPE)).astype(BF16)
    wqp = _pad_heads(w_uq, qw, rope[perm], MLA_NOPE).astype(BF16)
    kvw = MLA_NOPE + MLA_V
    wk = _pad_heads(w_ukv, kvw, nope, 0).astype(BF16)
    wv = _pad_heads(w_ukv, kvw, MLA_NOPE + np.arange(MLA_V), 0).astype(BF16)
    return w_cat, wq, wqp, wk, wv


def kernel(x, c, ctx, c_ctx, ada_w, ada_b, norm1_g, norm2_g, ab_w_in, ab_w_out, hgrn_lb_logits, hgrn_onorm_g, pool_w,
           pool_scale, cd_w_in, cd_w_out, na_rpb, mla_q_norm_g, mla_w_uq, mla_kv_norm_g, mla_w_ukv, moe_w_rg, moe_b_rg,
           moe_w_re, moe_b_re, moe_w_gate, moe_w_up, moe_w_down, final_norm_g):
    b, n, d = x.shape
    n_ctx = ctx.shape[1]
    assert ada_w.shape[0] == 2 and ab_w_in.shape[0] == 1 and cd_w_in.shape[0] == 1 and b + 1 <= 8
    tm = 512

    cc = jnp.zeros((8, d), F32).at[:b].set(c).at[b].set(c_ctx)
    mods = _ada(cc, ada_w, ada_b).reshape(2, 8, 6, d)
    mods_lat = [mods[l, :b] for l in range(2)]
    mods_ctx = [jnp.broadcast_to(mods[l, b:b + 1], (b, 6, d)) for l in range(2)]
    lb = jnp.cumsum(jax.nn.softmax(hgrn_lb_logits.astype(F32), axis=1), axis=1)[:, 0]

    w_in0 = ab_w_in[0].astype(BF16)
    w_out0 = ab_w_out[0].astype(BF16)
    pw0 = pool_w[0].astype(BF16)
    ab_cols = w_in0.shape[1]
    (u_ctx,) = _in_proj(ctx, norm1_g[0], mods_ctx[0], w_in0, ((0, ab_cols),), (F32,), tm)
    (u_lat,) = _in_proj(x, norm1_g[0], mods_lat[0], w_in0, ((0, ab_cols),), (F32,), tm)
    s0 = jnp.zeros((b, 2, A_HEADS, A_HEAD_DIM, A_HEAD_DIM), F32)
    ocf, ocb, s_ctx = _hgrn_scan(u_ctx, lb, s0, 256)
    olf, olb, _ = _hgrn_scan(u_lat, lb, s_ctx, 256)
    t_lat, t_ctx = b * n, b * n_ctx
    assert n % tm == 0 and t_ctx == tm
    x_lat = _ab_out(olf, olb, u_lat, x, mods_lat[0], hgrn_onorm_g[0], pw0, pool_scale[0], w_out0, tm)
    x_ctx = _ab_out(ocf, ocb, u_ctx, ctx, mods_ctx[0], hgrn_onorm_g[0], pw0, pool_scale[0], w_out0, tm)
    moe0 = _moe_params(moe_w_rg[0], moe_b_rg[0], moe_w_re[0], moe_b_re[0], moe_w_gate, moe_w_up, moe_w_down, 0)
    mods_all = jnp.concatenate([mods_lat[0], mods[0, b:b + 1]], axis=0)
    xa = _moe(x_lat, x_ctx, norm2_g[0], mods_all, n // tm, moe0, final_norm_g, False, tm)

    w_cat, wq, wqp, wk, wv = _cd_params(cd_w_in[0], mla_w_uq[0], mla_w_ukv[0])
    na_w = 3 * C_WIDTH
    splits = ((0, na_w), (na_w, w_cat.shape[1]))
    ua_ctx, ub_ctx = _in_proj(xa, norm1_g[1], mods_ctx[1], w_cat, splits, (BF16, F32), tm, (b, n_ctx, t_lat))
    ua_lat, ub_lat = _in_proj(xa, norm1_g[1], mods_lat[1], w_cat, splits, (BF16, F32), tm, (b, n, 0))
    q_g = mla_q_norm_g[0].reshape(1, -1)
    kv_g = mla_kv_norm_g[0].reshape(1, -1)
    cos_l, sin_l = _rope_tables(n)
    cos_c, sin_c = _identity_rope_tables(n_ctx)
    k_c, v_c = _mla_proj(ub_ctx, cos_c, sin_c, q_g, kv_g, wq, wqp, wk, wv, False, tm)
    q_l, k_l, v_l = _mla_proj(ub_lat, cos_l, sin_l, q_g, kv_g, wq, wqp, wk, wv, True, tm)
    d_lat = _mla_attention(q_l, jnp.concatenate([k_c, k_l], axis=2), jnp.concatenate([v_c, v_l], axis=3), 256, 256)
    c_lat = _na_attention(ua_lat, ua_ctx, _na_tables(na_rpb[0], n // GRID_W))
    xl = _cd_out(c_lat, d_lat, xa, mods_lat[1], cd_w_out[0].astype(BF16), tm)
    moe1 = _moe_params(moe_w_rg[1], moe_b_rg[1], moe_w_re[1], moe_b_re[1], moe_w_gate, moe_w_up, moe_w_down, 1)
    out = _moe(xl.reshape(t_lat, d), None, norm2_g[1], mods_lat[1], n // tm, moe1, final_norm_g, True, tm)
    return out.reshape(b, n, d)
```

```python
import functools

import numpy as np
import jax
import jax.numpy as jnp
from jax import lax
from jax.experimental import pallas as pl
from jax.experimental.pallas import tpu as pltpu

F32 = jnp.float32
BF16 = jnp.bfloat16

EPS = 1e-6
NEG = -1e30

GRID_W = 64
A_HEADS = 4
A_HEAD_DIM = 128
A_WIDTH = A_HEADS * A_HEAD_DIM
POOL_WINDOWS = (2, 4, 8, 16)
B_GROUP = 128
B_WIDTH = B_GROUP * len(POOL_WINDOWS)
POOL_HALO = 16
C_HEADS = 8
C_HEAD_DIM = 64
C_WIDTH = C_HEADS * C_HEAD_DIM
NA_ROWS = 8
NA_COLS = 16
NA_QROWS = 4
D_HEADS = 8
MLA_Q_RANK = 256
MLA_KV_RANK = 128
MLA_NOPE = 64
MLA_ROPE = 32
MLA_V = 64
MLA_PAD = 128
MLA_VROWS = 80
ROPE_THETA = 10000.0
MOE_GROUPS = 4
MOE_EPG = 8
MOE_EXPERTS = MOE_GROUPS * MOE_EPG
MOE_HIDDEN = 256
LANES = 128
SLOT_BLOCK = 16
STEP_BLOCKS = 32
VMEM_LIMIT = 56 * 1024 * 1024

NT = (((1,), (1,)), ((), ()))
TN = (((0,), (0,)), ((), ()))


def _cparams(*sem):
    return pltpu.CompilerParams(dimension_semantics=sem, vmem_limit_bytes=VMEM_LIMIT)


def _sigmoid(x):
    return 1.0 / (1.0 + jnp.exp(-x))


def _silu(x):
    return x * _sigmoid(x)


def _dot(a, b):
    return jnp.dot(a, b, preferred_element_type=F32)


def _rmsnorm(x, g):
    return x * lax.rsqrt(jnp.mean(x * x, axis=-1, keepdims=True) + EPS) * g


def _ada_kernel(c_ref, w_ref, b_ref, o_ref):
    s = _silu(c_ref[...])
    o_ref[0] = jnp.dot(s, w_ref[0], precision=lax.Precision.HIGHEST, preferred_element_type=F32) + b_ref[0]


def _ada(cc, ada_w, ada_b):
    depth, d, n6 = ada_w.shape
    tn = n6 // 4
    return pl.pallas_call(
        _ada_kernel,
        grid=(depth, n6 // tn),
        in_specs=[
            pl.BlockSpec((8, d), lambda l, j: (0, 0)),
            pl.BlockSpec((1, d, tn), lambda l, j: (l, 0, j)),
            pl.BlockSpec((1, 1, tn), lambda l, j: (l, 0, j)),
        ],
        out_specs=pl.BlockSpec((1, 8, tn), lambda l, j: (l, 0, j)),
        out_shape=jax.ShapeDtypeStruct((depth, 8, n6), F32),
        compiler_params=_cparams("parallel", "parallel"),
        name="ada_mod",
    )(cc, ada_w, ada_b.reshape(depth, 1, n6))


def _in_kernel(x_ref, g_ref, m_ref, w_ref, *o_refs, splits):
    h = _rmsnorm(x_ref[0], g_ref[...]) * (1.0 + m_ref[0, 1:2, :]) + m_ref[0, 0:1, :]
    hb = h.astype(BF16)
    for o_ref, (a, b) in zip(o_refs, splits):
        o_ref[0] = _dot(hb, w_ref[:, a:b]).astype(o_ref.dtype)


def _in_proj(x, gain, mods, w, splits, dtypes, tm, flat=None):
    if flat is None:
        b, n, d = x.shape
        tm = min(tm, n)
        x_spec = pl.BlockSpec((1, tm, d), lambda bi, i: (bi, i, 0))
    else:
        b, n, row0 = flat
        d = x.shape[-1]
        tm = min(tm, n)
        x = x.reshape(1, -1, d)
        x_spec = pl.BlockSpec((1, tm, d), lambda bi, i: (0, row0 // tm + bi * (n // tm) + i, 0))
    outs = [jax.ShapeDtypeStruct((b, n, hi - lo), dt) for (lo, hi), dt in zip(splits, dtypes)]
    return pl.pallas_call(
        functools.partial(_in_kernel, splits=splits),
        grid=(b, n // tm),
        in_specs=[
            x_spec,
            pl.BlockSpec((1, d), lambda bi, i: (0, 0)),
            pl.BlockSpec((1, 6, d), lambda bi, i: (bi, 0, 0)),
            pl.BlockSpec(w.shape, lambda bi, i: (0, 0)),
        ],
        out_specs=[pl.BlockSpec((1, tm, hi - lo), lambda bi, i: (bi, i, 0)) for lo, hi in splits],
        out_shape=outs,
        compiler_params=_cparams("parallel", "parallel"),
        name="in_proj",
    )(x, gain.reshape(1, d), mods, w)


HG_SUB = 64


def _hgrn_direction(q_raw, fz, v, lb, st_ref, d, o_ref, reverse):
    rows = q_raw.shape[0]
    c = HG_SUB
    f = lb + (1.0 - lb) * _sigmoid(fz)
    k = 1.0 - f
    g = jnp.log(f)
    q = _silu(q_raw)
    r_i = lax.broadcasted_iota(jnp.int32, (c, c), 0)
    c_i = lax.broadcasted_iota(jnp.int32, (c, c), 1)
    keep = (c_i >= r_i) if reverse else (c_i <= r_i)
    tri = jnp.where(keep, 1.0, 0.0).astype(BF16)
    order = range(rows // c - 1, -1, -1) if reverse else range(rows // c)
    for ci in order:
        sl = slice(ci * c, (ci + 1) * c)
        gc = g[sl]
        g_hi = gc.astype(BF16)
        g_lo = (gc - g_hi.astype(F32)).astype(BF16)
        bc = _dot(tri, g_hi) + _dot(tri, g_lo)
        ref = bc[c // 2:c // 2 + 1]
        tot = bc[0:1] if reverse else bc[c - 1:c]
        qt = q[sl] * jnp.exp(bc - ref)
        kt = k[sl] * jnp.exp(ref - bc)
        qd = (qt * jnp.exp(ref)).astype(BF16)
        kd = (kt * jnp.exp(tot - ref)).astype(BF16)
        qt = qt.astype(BF16)
        kt = kt.astype(BF16)
        vb = v[sl].astype(BF16)
        dec = jnp.exp(tot)
        for h in range(A_HEADS):
            hs = slice(h * A_HEAD_DIM, (h + 1) * A_HEAD_DIM)
            att = lax.dot_general(qt[:, hs], kt[:, hs], NT, preferred_element_type=F32)
            att = jnp.where(keep, att, 0.0).astype(BF16)
            st = st_ref[d, h]
            o = _dot(att, vb[:, hs]) + lax.dot_general(qd[:, hs], st.astype(BF16), NT, preferred_element_type=F32)
            o_ref[0, sl, hs] = o
            st_ref[d, h] = st * dec[:, hs] + lax.dot_general(vb[:, hs], kd[:, hs], TN, preferred_element_type=F32)


def _hgrn_kernel(qf_ref, ff_ref, vf_ref, qb_ref, fb_ref, vb_ref, lb_ref, s0_ref, of_ref, ob_ref, sfin_ref, st_ref):
    j = pl.program_id(1)

    @pl.when(j == 0)
    def _():
        st_ref[...] = s0_ref[0]

    _hgrn_direction(qf_ref[0], ff_ref[0], vf_ref[0], lb_ref[0:1], st_ref, 0, of_ref, False)
    _hgrn_direction(qb_ref[0], fb_ref[0], vb_ref[0], lb_ref[1:2], st_ref, 1, ob_ref, True)

    @pl.when(j == pl.num_programs(1) - 1)
    def _():
        sfin_ref[0] = st_ref[...]


def _hgrn_scan(u, lb, s0, rows):
    b, n, _ = u.shape
    rows = min(rows, n)
    nb = n // rows
    w = A_WIDTH

    def fwd(col):
        return pl.BlockSpec((1, rows, w), lambda bi, j: (bi, j, col))

    def bwd(col):
        return pl.BlockSpec((1, rows, w), lambda bi, j: (bi, nb - 1 - j, col))

    st_spec = pl.BlockSpec((1, 2, A_HEADS, A_HEAD_DIM, A_HEAD_DIM), lambda bi, j: (bi, 0, 0, 0, 0))
    return pl.pallas_call(
        _hgrn_kernel,
        grid=(b, nb),
        in_specs=[fwd(0), fwd(1), fwd(3), bwd(0), bwd(2), bwd(3), pl.BlockSpec((2, w), lambda bi, j: (0, 0)), st_spec],
        out_specs=[
            pl.BlockSpec((1, rows, w), lambda bi, j: (bi, j, 0)),
            pl.BlockSpec((1, rows, w), lambda bi, j: (bi, nb - 1 - j, 0)),
            st_spec,
        ],
        out_shape=[
            jax.ShapeDtypeStruct((b, n, w), F32),
            jax.ShapeDtypeStruct((b, n, w), F32),
            jax.ShapeDtypeStruct(s0.shape, F32),
        ],
        scratch_shapes=[pltpu.VMEM((2, A_HEADS, A_HEAD_DIM, A_HEAD_DIM), F32)],
        compiler_params=_cparams("parallel", "arbitrary"),
        name="hgrn_scan",
    )(u, u, u, u, u, u, lb, s0)


def _ab_out_kernel(of_ref, ob_ref, ug_ref, up_ref, pprev_ref, pnext_ref, x_ref, m_ref, on_ref, pw_ref, ps_ref,
                   wo_ref, o_ref, *, n):
    i = pl.program_id(1)
    tm = x_ref.shape[1]
    o = of_ref[0] + ob_ref[0]
    gate = _silu(ug_ref[0])
    parts = []
    for h in range(A_HEADS):
        hs = slice(h * A_HEAD_DIM, (h + 1) * A_HEAD_DIM)
        parts.append(_rmsnorm(o[:, hs], on_ref[...]) * gate[:, hs])
    main = up_ref[0]
    prev = jnp.where(i > 0, pprev_ref[0], 0.0)
    nxt = jnp.where(i < pl.num_programs(1) - 1, pnext_ref[0], 0.0)
    ext = jnp.concatenate([prev, main, nxt], axis=0)
    ext_rows = tm + 2 * POOL_HALO
    t = i * tm + lax.broadcasted_iota(jnp.int32, (tm, 1), 0)
    for gi, win in enumerate(POOL_WINDOWS):
        gs = slice(gi * B_GROUP, (gi + 1) * B_GROUP)
        acc = ext[:, gs]
        acc = acc + pltpu.roll(acc, 1, 0)
        half = 1
        while 2 * half < win:
            acc = pltpu.roll(acc, half, 0) + pltpu.roll(acc, ext_rows - half, 0)
            half *= 2
        cnt = jnp.minimum(t + (win - win // 2), n) - jnp.maximum(t - win // 2, 0)
        mean = acc[POOL_HALO:POOL_HALO + tm] / cnt.astype(F32)
        pooled = _dot((mean - main[:, gs]).astype(BF16), pw_ref[gi])
        parts.append(pooled * ps_ref[:, gs])
    mix = jnp.concatenate(parts, axis=-1).astype(BF16)
    o_ref[0] = x_ref[0] + m_ref[0, 2:3, :] * _dot(mix, wo_ref[...])


def _ab_out(o_f, o_b, u, x, mods, onorm_g, pool_w, pool_scale, w_out, tm):
    b, n, d = x.shape
    tm = min(tm, n)
    nt = n // tm
    hb = tm // POOL_HALO
    last_halo = n // POOL_HALO - 1
    w = A_WIDTH
    tile = lambda col: pl.BlockSpec((1, tm, w), lambda bi, i: (bi, i, col))
    out = pl.pallas_call(
        functools.partial(_ab_out_kernel, n=n),
        grid=(b, nt),
        in_specs=[
            tile(0), tile(0), tile(4), tile(5),
            pl.BlockSpec((1, POOL_HALO, w), lambda bi, i: (bi, jnp.maximum(i * hb - 1, 0), 5)),
            pl.BlockSpec((1, POOL_HALO, w), lambda bi, i: (bi, jnp.minimum((i + 1) * hb, last_halo), 5)),
            pl.BlockSpec((1, tm, d), lambda bi, i: (bi, i, 0)),
            pl.BlockSpec((1, 6, d), lambda bi, i: (bi, 0, 0)),
            pl.BlockSpec((1, A_HEAD_DIM), lambda bi, i: (0, 0)),
            pl.BlockSpec(pool_w.shape, lambda bi, i: (0, 0, 0)),
            pl.BlockSpec((1, B_WIDTH), lambda bi, i: (0, 0)),
            pl.BlockSpec(w_out.shape, lambda bi, i: (0, 0)),
        ],
        out_specs=pl.BlockSpec((1, tm, d), lambda bi, i: (bi, i, 0)),
        out_shape=jax.ShapeDtypeStruct((b, n, d), F32),
        compiler_params=_cparams("parallel", "parallel"),
        name="ab_out",
    )(o_f, o_b, u, u, u, u, x, mods, onorm_g.reshape(1, A_HEAD_DIM), pool_w, pool_scale.reshape(1, B_WIDTH), w_out)
    return out.reshape(b * n, d)


def _slot_rows(tr):
    rows = 2 * tr + MOE_EXPERTS * (SLOT_BLOCK - 1)
    assert rows % SLOT_BLOCK == 0
    return rows


def _drop_tail(kern, *refs, **kw):
    return kern(refs[0], None, *refs[1:], **kw)


def _tile_tokens(x_ref, xt_ref):
    if xt_ref is None:
        return x_ref[...]
    return jnp.where(pl.program_id(0) < pl.num_programs(0) - 1, x_ref[...], xt_ref[...])


def _route_kernel(x_ref, xt_ref, g_ref, m_ref, whi_ref, wlo_ref, br_ref, xs_ref, info_ref, cnt_ref, *, slot_rows):
    tr = x_ref.shape[0]
    h = _rmsnorm(_tile_tokens(x_ref, xt_ref), g_ref[...]) * (1.0 + m_ref[0, 4:5, :]) + m_ref[0, 3:4, :]
    hb = h.astype(BF16)
    hl = (h - hb.astype(F32)).astype(BF16)
    logits = _dot(hb, whi_ref[...]) + _dot(hb, wlo_ref[...]) + _dot(hl, whi_ref[...]) + br_ref[...]
    lane = lax.broadcasted_iota(jnp.int32, (tr, LANES), 1)
    lanef = lane.astype(F32)
    lg = jnp.where(lane < MOE_GROUPS, logits, NEG)
    mg = jnp.max(lg, axis=-1, keepdims=True)
    g_p = 1.0 / jnp.sum(jnp.exp(lg - mg), axis=-1, keepdims=True)
    gidx = jnp.min(jnp.where(lg == mg, lanef, float(LANES)), axis=-1, keepdims=True)
    lo = MOE_GROUPS + MOE_EPG * gidx
    le = jnp.where((lanef >= lo) & (lanef < lo + MOE_EPG), logits, NEG)
    m1 = jnp.max(le, axis=-1, keepdims=True)
    i1 = jnp.min(jnp.where(le == m1, lanef, float(LANES)), axis=-1, keepdims=True)
    le2 = jnp.where(lanef == i1, NEG, le)
    m2 = jnp.max(le2, axis=-1, keepdims=True)
    i2 = jnp.min(jnp.where(le2 == m2, lanef, float(LANES)), axis=-1, keepdims=True)
    ratio = jnp.exp(m2 - m1)
    w1 = g_p / (1.0 + ratio)
    w2 = g_p * ratio / (1.0 + ratio)
    hot1 = lanef == i1
    hot2 = lanef == i2
    hot = jnp.where(hot1, 1.0, jnp.where(hot2, 1.0, 0.0))
    r_i = lax.broadcasted_iota(jnp.int32, (tr, tr), 0)
    c_i = lax.broadcasted_iota(jnp.int32, (tr, tr), 1)
    rank = _dot(jnp.where(c_i < r_i, 1.0, 0.0).astype(BF16), hot.astype(BF16))
    cnt = jnp.sum(hot, axis=0, keepdims=True)
    nblk = jnp.floor((cnt + (SLOT_BLOCK - 1)) * (1.0 / SLOT_BLOCK))
    l_r = lax.broadcasted_iota(jnp.int32, (LANES, LANES), 0)
    l_c = lax.broadcasted_iota(jnp.int32, (LANES, LANES), 1)
    before = jnp.where(l_r < l_c, 1.0, 0.0).astype(BF16)
    off = SLOT_BLOCK * _dot(jnp.broadcast_to(nblk, (8, LANES)).astype(BF16), before)[0:1]
    posm = off + rank
    pos1 = jnp.sum(jnp.where(hot1, posm, 0.0), axis=-1, keepdims=True)
    pos2 = jnp.sum(jnp.where(hot2, posm, 0.0), axis=-1, keepdims=True)
    info = jnp.where(lane == 0, pos1, jnp.where(lane == 1, pos2, jnp.where(lane == 2, w1, jnp.where(lane == 3, w2, 0.0))))
    info_ref[...] = info
    pos_t = info.T.astype(jnp.int32)
    row = lax.broadcasted_iota(jnp.int32, (slot_rows, tr), 0)
    sel = jnp.where(row == pos_t[0:1], 1.0, jnp.where(row == pos_t[1:2], 1.0, 0.0)).astype(BF16)
    xs_ref[...] = _dot(sel, hb).astype(BF16)
    cnt_ref[0] = jnp.broadcast_to(cnt, (8, LANES))


def _token_specs(x2d, x_tail, tr, index):
    d = x2d.shape[1]
    nt = x2d.shape[0] // tr
    if x_tail is None:
        return nt, [pl.BlockSpec((tr, d), index(lambda i: (i, 0)))], [x2d]
    assert x_tail.shape == (tr, d)
    return nt + 1, [pl.BlockSpec((tr, d), index(lambda i: (jnp.minimum(i, nt - 1), 0))),
                    pl.BlockSpec((tr, d), index(lambda i: (0, 0)))], [x2d, x_tail]


def _moe_route(x2d, x_tail, gain, mods, tiles_per_mod, w_hi, w_lo, b_r, tr):
    d = x2d.shape[1]
    nt, x_specs, x_args = _token_specs(x2d, x_tail, tr, lambda f: f)
    t = nt * tr
    sr = _slot_rows(tr)
    kern = _route_kernel if x_tail is not None else functools.partial(_drop_tail, _route_kernel)
    return pl.pallas_call(
        functools.partial(kern, slot_rows=sr),
        grid=(nt,),
        in_specs=x_specs + [
            pl.BlockSpec((1, d), lambda i: (0, 0)),
            pl.BlockSpec((1, 6, d), lambda i: (jnp.minimum(i // tiles_per_mod, mods.shape[0] - 1), 0, 0)),
            pl.BlockSpec((d, LANES), lambda i: (0, 0)),
            pl.BlockSpec((d, LANES), lambda i: (0, 0)),
            pl.BlockSpec((1, LANES), lambda i: (0, 0)),
        ],
        out_specs=[
            pl.BlockSpec((sr, d), lambda i: (i, 0)),
            pl.BlockSpec((tr, LANES), lambda i: (i, 0)),
            pl.BlockSpec((1, 8, LANES), lambda i: (i, 0, 0)),
        ],
        out_shape=[
            jax.ShapeDtypeStruct((nt * sr, d), BF16),
            jax.ShapeDtypeStruct((t, LANES), F32),
            jax.ShapeDtypeStruct((nt, 8, LANES), F32),
        ],
        compiler_params=_cparams("parallel"),
        name="moe_route",
    )(*x_args, gain.reshape(1, d), mods, w_hi, w_lo, b_r)


def _tables_kernel(cnt_ref, src_ref, inv_ref, exp_ref, valid_ref, *, ntiles, bpt):
    cnt = cnt_ref[...]
    nb = jnp.floor((cnt + (SLOT_BLOCK - 1)) * (1.0 / SLOT_BLOCK))
    i_r = lax.broadcasted_iota(jnp.int32, (LANES, LANES), 0)
    i_c = lax.broadcasted_iota(jnp.int32, (LANES, LANES), 1)
    before = jnp.where(i_r < i_c, 1.0, 0.0)
    upto = jnp.where(i_c <= i_r, 1.0, 0.0)
    first = _dot(nb, before)
    cum = _dot(upto, nb)
    tot = jnp.max(cum, axis=0, keepdims=True)
    steps = jnp.floor((tot + (STEP_BLOCKS - 1)) * (1.0 / STEP_BLOCKS))
    start = STEP_BLOCKS * _dot(jnp.broadcast_to(steps, (LANES, LANES)), before)[0:1]
    pos = start + cum - nb

    sub = 8 * (-(-(MOE_GROUPS + MOE_EXPERTS) // 8))
    first_t, nb_t, pos_t = first.T[:sub], nb.T[:sub], pos.T[:sub]
    as_col = lambda v: jnp.broadcast_to(v, (LANES, LANES)).T[:sub, 0:1]
    start_c, tot_c, span_c = as_col(start), as_col(tot), as_col(STEP_BLOCKS * steps)

    inv_ref[...] = jnp.zeros(inv_ref.shape, jnp.int32)
    local = lax.broadcasted_iota(jnp.int32, (sub, LANES), 1).astype(F32)
    nsrc = src_ref.shape[1]
    j = lax.broadcasted_iota(jnp.int32, (sub, nsrc), 1).astype(F32)
    acc = jnp.zeros((1, nsrc), F32)
    for i in range(ntiles):
        f_i, n_i, p_i = first_t[:, i:i + 1], nb_t[:, i:i + 1], pos_t[:, i:i + 1]
        own = (local >= f_i) & (local < f_i + n_i)
        inv_ref[i:i + 1, :] = jnp.sum(jnp.where(own, p_i + (local - f_i), 0.0), axis=0, keepdims=True).astype(jnp.int32)
        own = (j >= p_i) & (j < p_i + n_i)
        acc = acc + jnp.sum(jnp.where(own, (i * bpt) + f_i + (j - p_i), 0.0), axis=0, keepdims=True)
    src_ref[...] = acc.astype(jnp.int32)

    nst = exp_ref.shape[1]
    at = STEP_BLOCKS * lax.broadcasted_iota(jnp.int32, (sub, nst), 1).astype(F32)
    expert = (lax.broadcasted_iota(jnp.int32, (sub, nst), 0) - MOE_GROUPS).astype(F32)
    inside = (at >= start_c) & (at < start_c + span_c)
    exp_ref[...] = jnp.sum(jnp.where(inside, expert, 0.0), axis=0, keepdims=True).astype(jnp.int32)
    occupied = jnp.where(inside, jnp.where(at - start_c < tot_c, 1.0, 0.0), 0.0)
    valid_ref[...] = jnp.sum(occupied, axis=0, keepdims=True).astype(jnp.int32)


def _expert_tables(cnt, bpt, nsteps):
    ntiles = cnt.shape[0]
    assert bpt <= LANES and ntiles <= LANES
    cnt = jnp.pad(cnt, ((0, LANES - ntiles), (0, 0)))
    nsrc = -(-nsteps * STEP_BLOCKS // LANES) * LANES
    nst = -(-nsteps // LANES) * LANES
    i32 = lambda *s: jax.ShapeDtypeStruct(s, jnp.int32)
    src, inv, step_e, valid = pl.pallas_call(
        functools.partial(_tables_kernel, ntiles=ntiles, bpt=bpt),
        out_shape=[i32(1, nsrc), i32(LANES, LANES), i32(1, nst), i32(1, nst)],
        compiler_params=pltpu.CompilerParams(vmem_limit_bytes=VMEM_LIMIT),
        name="moe_tables",
    )(cnt)
    return src.reshape(-1), inv.reshape(-1), step_e.reshape(-1), valid.reshape(-1)


def _block_gather(table_ref, first, nblocks, src_hbm, buf_ref, slot, sem_ref):
    return [pltpu.make_async_copy(src_hbm.at[table_ref[first + kk]],
                                  buf_ref.at[slot, pl.ds(kk * SLOT_BLOCK, SLOT_BLOCK)], sem_ref.at[slot])
            for kk in range(nblocks)]


def _experts_kernel(src_ref, exp_ref, valid_ref, xs_hbm, wg_ref, wu_ref, wd_ref, y_ref, xbuf_ref, sem_ref, wgb_ref,
                    wub_ref, wdb_ref):
    s = pl.program_id(0)
    slot = s % 2

    def gather(step, to_slot):
        return _block_gather(src_ref, step * STEP_BLOCKS, STEP_BLOCKS, xs_hbm, xbuf_ref, to_slot, sem_ref)

    @pl.when((s == 0) & (valid_ref[0] > 0))
    def _():
        for cp in gather(0, 0):
            cp.start()

    nxt = jnp.minimum(s + 1, pl.num_programs(0) - 1)

    @pl.when((s + 1 < pl.num_programs(0)) & (valid_ref[nxt] > 0))
    def _():
        for cp in gather(s + 1, 1 - slot):
            cp.start()

    @pl.when((s == 0) | (exp_ref[s] != exp_ref[jnp.maximum(s - 1, 0)]))
    def _():
        wgb_ref[...] = wg_ref[0, 0, 0].astype(BF16)
        wub_ref[...] = wu_ref[0, 0, 0].astype(BF16)
        wdb_ref[...] = wd_ref[0, 0, 0].astype(BF16)

    @pl.when(valid_ref[s] > 0)
    def _():
        for cp in gather(s, slot):
            cp.wait()
        x = xbuf_ref[slot]
        a = _silu(_dot(x, wgb_ref[...])) * _dot(x, wub_ref[...])
        y_ref[...] = _dot(a.astype(BF16), wdb_ref[...]).astype(BF16)

    @pl.when(valid_ref[s] == 0)
    def _():
        y_ref[...] = jnp.zeros(y_ref.shape, y_ref.dtype)


def _moe_experts(xs, src, step_e, valid, w_gate, w_up, w_down, layer, nsteps):
    rows, d = xs.shape
    f = w_gate.shape[-1]
    xs3 = xs.reshape(rows // SLOT_BLOCK, SLOT_BLOCK, d)
    step_rows = STEP_BLOCKS * SLOT_BLOCK
    w_blk = lambda shape: pl.BlockSpec((1, 1, 1) + shape,
                                       lambda s, sr, ex, va: (layer, ex[s] // MOE_EPG, ex[s] % MOE_EPG, 0, 0))
    grid_spec = pltpu.PrefetchScalarGridSpec(
        num_scalar_prefetch=3,
        grid=(nsteps,),
        in_specs=[pl.BlockSpec(memory_space=pl.ANY), w_blk((d, f)), w_blk((d, f)), w_blk((f, d))],
        out_specs=pl.BlockSpec((step_rows, d), lambda s, sr, ex, va: (s, 0)),
        scratch_shapes=[pltpu.VMEM((2, step_rows, d), BF16), pltpu.SemaphoreType.DMA((2,)),
                        pltpu.VMEM((d, f), BF16), pltpu.VMEM((d, f), BF16), pltpu.VMEM((f, d), BF16)],
    )
    return pl.pallas_call(
        _experts_kernel,
        grid_spec=grid_spec,
        out_shape=jax.ShapeDtypeStruct((nsteps * step_rows, d), BF16),
        compiler_params=_cparams("arbitrary"),
        name="moe_experts",
    )(src, step_e, valid, xs3, w_gate, w_up, w_down)


def _combine_kernel(inv_ref, x_ref, xt_ref, info_ref, m_ref, fg_ref, ys_hbm, o_ref, ybuf_ref, sem_ref, *, bpt, final):
    i = pl.program_id(0)
    slot = i % 2
    tr = x_ref.shape[0]

    def gather(tile, to_slot):
        return _block_gather(inv_ref, tile * LANES, bpt, ys_hbm, ybuf_ref, to_slot, sem_ref)

    @pl.when(i == 0)
    def _():
        for cp in gather(0, 0):
            cp.start()

    @pl.when(i + 1 < pl.num_programs(0))
    def _():
        for cp in gather(i + 1, 1 - slot):
            cp.start()

    info = info_ref[...]
    col = lax.broadcasted_iota(jnp.int32, (tr, bpt * SLOT_BLOCK), 1)
    wsel = jnp.where(col == info[:, 0:1].astype(jnp.int32), info[:, 2:3],
                     jnp.where(col == info[:, 1:2].astype(jnp.int32), info[:, 3:4], 0.0))
    for cp in gather(i, slot):
        cp.wait()
    y = _dot(wsel.astype(BF16), ybuf_ref[slot])
    out = _tile_tokens(x_ref, xt_ref) + m_ref[0, 5:6, :] * y
    if final:
        out = _rmsnorm(out, fg_ref[...])
    o_ref[...] = out


def _moe_combine(x2d, x_tail, ys, inv, info, mods, tiles_per_mod, final_g, tr, bpt, final):
    d = x2d.shape[1]
    nt, x_specs, x_args = _token_specs(x2d, x_tail, tr, lambda f: (lambda i, iv: f(i)))
    t = nt * tr
    ys3 = ys.reshape(ys.shape[0] // SLOT_BLOCK, SLOT_BLOCK, d)
    kern = _combine_kernel
    if x_tail is None:
        kern = lambda inv_ref, x_ref, *refs, **kw: _combine_kernel(inv_ref, x_ref, None, *refs, **kw)
    grid_spec = pltpu.PrefetchScalarGridSpec(
        num_scalar_prefetch=1,
        grid=(nt,),
        in_specs=x_specs + [
            pl.BlockSpec((tr, LANES), lambda i, iv: (i, 0)),
            pl.BlockSpec((1, 6, d), lambda i, iv: (jnp.minimum(i // tiles_per_mod, mods.shape[0] - 1), 0, 0)),
            pl.BlockSpec((1, d), lambda i, iv: (0, 0)),
            pl.BlockSpec(memory_space=pl.ANY),
        ],
        out_specs=pl.BlockSpec((tr, d), lambda i, iv: (i, 0)),
        scratch_shapes=[pltpu.VMEM((2, bpt * SLOT_BLOCK, d), BF16), pltpu.SemaphoreType.DMA((2,))],
    )
    return pl.pallas_call(
        functools.partial(kern, bpt=bpt, final=final),
        grid_spec=grid_spec,
        out_shape=jax.ShapeDtypeStruct((t, d), F32),
        compiler_params=_cparams("arbitrary"),
        name="moe_combine",
    )(inv, *x_args, info, mods, final_g.reshape(1, d), ys3)


def _moe(x2d, x_tail, gain, mods, tiles_per_mod, params, final_g, final, tr):
    w_hi, w_lo, b_r, w_gate, w_up, w_down, layer = params
    nt = x2d.shape[0] // tr + (x_tail is not None)
    xs, info, cnt = _moe_route(x2d, x_tail, gain, mods, tiles_per_mod, w_hi, w_lo, b_r, tr)
    bpt = _slot_rows(tr) // SLOT_BLOCK
    nsteps = -(-(nt * bpt + MOE_EXPERTS * (STEP_BLOCKS - 1)) // STEP_BLOCKS)
    src, inv, step_e, valid = _expert_tables(cnt[:, 0, :], bpt, nsteps)
    ys = _moe_experts(xs, src, step_e, valid, w_gate, w_up, w_down, layer, nsteps)
    return _moe_combine(x2d, x_tail, ys, inv, info, mods, tiles_per_mod, final_g, tr, bpt, final)


def _mla_proj_kernel(cq_ref, ckv_ref, kr_ref, krp_ref, cos_ref, sin_ref, qg_ref, kg_ref, wq_ref, wqp_ref, wk_ref,
                     wv_ref, vone_ref, *o_refs, need_q, q_scale):
    cos = cos_ref[...]
    sin = sin_ref[...]
    ckv = _rmsnorm(ckv_ref[0], kg_ref[...]).astype(BF16)
    k_rope = kr_ref[0] * cos + krp_ref[0] * sin
    kn = _dot(ckv, wk_ref[...])
    if need_q:
        q_ref, k_ref, v_ref = o_refs
    else:
        k_ref, v_ref = o_refs
    vx = _dot(ckv, wv_ref[...]) + vone_ref[...]
    for h in range(D_HEADS):
        hs = slice(h * MLA_PAD, (h + 1) * MLA_PAD)
        k_ref[0, h] = (kn[:, hs] + k_rope).astype(BF16)
        v_ref[0, h] = vx[:, hs].T[:MLA_VROWS].astype(BF16)
    if need_q:
        cq = _rmsnorm(cq_ref[0], qg_ref[...]).astype(BF16)
        qm = _dot(cq, wq_ref[...])
        qp = _dot(cq, wqp_ref[...])
        for h in range(D_HEADS):
            hs = slice(h * MLA_PAD, (h + 1) * MLA_PAD)
            q_ref[0, h] = ((qm[:, hs] * cos + qp[:, hs] * sin) * q_scale).T.astype(BF16)


def _mla_proj(u_b, cos, sin, q_g, kv_g, wq, wqp, wk, wv, need_q, tm):
    b, n, _ = u_b.shape
    tm = min(tm, n)
    row_major = (jax.ShapeDtypeStruct((b, D_HEADS, n, MLA_PAD), BF16),
                 pl.BlockSpec((1, D_HEADS, tm, MLA_PAD), lambda bi, i: (bi, 0, i, 0)))
    col_major = (jax.ShapeDtypeStruct((b, D_HEADS, MLA_PAD, n), BF16),
                 pl.BlockSpec((1, D_HEADS, MLA_PAD, tm), lambda bi, i: (bi, 0, 0, i)))
    v_major = (jax.ShapeDtypeStruct((b, D_HEADS, MLA_VROWS, n), BF16),
               pl.BlockSpec((1, D_HEADS, MLA_VROWS, tm), lambda bi, i: (bi, 0, 0, i)))
    outs, specs = zip(*(([col_major] if need_q else []) + [row_major, v_major]))
    full = lambda a: pl.BlockSpec(a.shape, lambda bi, i: (0,) * a.ndim)
    vone = jnp.tile(jnp.concatenate([jnp.zeros((1, MLA_V), F32), jnp.ones((1, MLA_PAD - MLA_V), F32)], axis=1),
                    (1, D_HEADS))
    q_scale = float((MLA_NOPE + MLA_ROPE) ** -0.5 * np.log2(np.e))
    return pl.pallas_call(
        functools.partial(_mla_proj_kernel, need_q=need_q, q_scale=q_scale),
        grid=(b, n // tm),
        in_specs=[
            pl.BlockSpec((1, tm, MLA_Q_RANK), lambda bi, i: (bi, i, 0)),
            pl.BlockSpec((1, tm, MLA_KV_RANK), lambda bi, i: (bi, i, 2)),
            pl.BlockSpec((1, tm, MLA_PAD), lambda bi, i: (bi, i, 3)),
            pl.BlockSpec((1, tm, MLA_PAD), lambda bi, i: (bi, i, 4)),
            pl.BlockSpec((tm, MLA_PAD), lambda bi, i: (i, 0)),
            pl.BlockSpec((tm, MLA_PAD), lambda bi, i: (i, 0)),
            full(q_g), full(kv_g), full(wq), full(wqp), full(wk), full(wv), full(vone),
        ],
        out_specs=list(specs),
        out_shape=list(outs),
        compiler_params=_cparams("parallel", "parallel"),
        name="mla_proj",
    )(u_b, u_b, u_b, u_b, cos, sin, q_g, kv_g, wq, wqp, wk, wv, vone)


def _mla_attn_kernel(q_ref, qn_ref, k_ref, v_ref, o_ref, acc0_ref, acc1_ref, s0_ref, s1_ref, m_ref, *, tk):
    tq = q_ref.shape[3]
    nk = k_ref.shape[2]
    nchunks = nk // tk
    neg = jnp.full((8, tq), NEG, F32)
    s_refs = (s0_ref, s1_ref)
    acc_refs = (acc0_ref, acc1_ref)

    def chunk(c):
        return pl.ds(pl.multiple_of(c * tk, tk), tk)

    def scores(ks, hh, q, m):
        s = _dot(k_ref[0, hh, ks, :], q)
        s_refs[hh][ks, :] = s
        return jnp.maximum(m, jnp.max(s.reshape(tk // 8, 8, tq), axis=0))

    def weight(ks, hh, m_row):
        p = jnp.exp2((s_refs[hh][ks, :] - m_row).astype(BF16))
        acc_refs[hh][...] += _dot(v_ref[0, hh, :, ks], p)

    @pl.when(pl.program_id(2) == 0)
    def _():
        m_ref[...] = lax.fori_loop(0, nchunks, lambda c, m: scores(chunk(c), 0, q_ref[0, 0], m), neg)

    acc0_ref[...] = jnp.zeros(acc0_ref.shape, F32)
    acc1_ref[...] = jnp.zeros(acc1_ref.shape, F32)
    m0 = jnp.max(m_ref[...], axis=0, keepdims=True)

    def first(c, m1):
        weight(chunk(c), 0, m0)
        return scores(chunk(c), 1, q_ref[0, 1], m1)

    m1 = jnp.max(lax.fori_loop(0, nchunks, first, neg, unroll=True), axis=0, keepdims=True)

    def second(c, m0_next):
        weight(chunk(c), 1, m1)
        return scores(chunk(c), 0, qn_ref[0, 0], m0_next)

    m_ref[...] = lax.fori_loop(0, nchunks, second, neg, unroll=True)
    o_t = jnp.concatenate([a[:MLA_V] / a[MLA_V:MLA_V + 1] for a in acc_refs], axis=0)
    o_ref[0] = o_t.T.astype(o_ref.dtype)


def _mla_attention(q_t, k, v_t, tq, tk):
    b, h, _, n = q_t.shape
    nk = k.shape[2]
    tq = min(tq, n)
    tk = max(t for t in range(2 * LANES, tk + 1, 2 * LANES) if nk % t == 0)
    return pl.pallas_call(
        functools.partial(_mla_attn_kernel, tk=tk),
        grid=(b, h // 2, n // tq),
        in_specs=[
            pl.BlockSpec((1, 2, MLA_PAD, tq), lambda bi, hp, i: (bi, hp, 0, i)),
            pl.BlockSpec((1, 2, MLA_PAD, tq), lambda bi, hp, i: (bi, hp, 0, jnp.minimum(i + 1, n // tq - 1))),
            pl.BlockSpec((1, 2, nk, MLA_PAD), lambda bi, hp, i: (bi, hp, 0, 0)),
            pl.BlockSpec((1, 2, MLA_VROWS, nk), lambda bi, hp, i: (bi, hp, 0, 0)),
        ],
        out_specs=pl.BlockSpec((1, tq, 2 * MLA_V), lambda bi, hp, i: (bi, i, hp)),
        out_shape=jax.ShapeDtypeStruct((b, n, h * MLA_V), BF16),
        scratch_shapes=[pltpu.VMEM((MLA_VROWS, tq), F32), pltpu.VMEM((MLA_VROWS, tq), F32),
                        pltpu.VMEM((nk, tq), F32), pltpu.VMEM((nk, tq), F32), pltpu.VMEM((8, tq), F32)],
        compiler_params=_cparams("parallel", "parallel", "arbitrary"),
        name="mla_attention",
    )(q_t, q_t, k, v_t)


def _na_kernel(q_ref, kp_ref, km_ref, kn_ref, kc_ref, vp_ref, vm_ref, vn_ref, vc_ref, tab_ref, o_ref):
    tq = q_ref.shape[1]
    nloc = 3 * tq
    lane = lax.broadcasted_iota(jnp.int32, (tq, LANES), 1)
    q = q_ref[0]
    k_all = jnp.concatenate([kp_ref[0], km_ref[0], kn_ref[0], kc_ref[0]], axis=0)
    v_all = jnp.concatenate([vp_ref[0], vm_ref[0], vn_ref[0], vc_ref[0]], axis=0)
    outs = []
    for hh in range(2):
        in_head = (lane >= hh * C_HEAD_DIM) & (lane < (hh + 1) * C_HEAD_DIM)
        qh = jnp.where(in_head, q, jnp.zeros_like(q))
        s = lax.dot_general(qh, k_all, NT, preferred_element_type=F32)
        s_loc = s[:, :nloc] + tab_ref[0, hh]
        s_ctx = s[:, nloc:]
        m = jnp.maximum(jnp.max(s_loc, axis=-1, keepdims=True), jnp.max(s_ctx, axis=-1, keepdims=True))
        p_loc = jnp.exp(s_loc - m)
        p_ctx = jnp.exp(s_ctx - m)
        l = jnp.sum(p_loc, axis=-1, keepdims=True) + jnp.sum(p_ctx, axis=-1, keepdims=True)
        o = _dot(p_loc.astype(BF16), v_all[:nloc]) + _dot(p_ctx.astype(BF16), v_all[nloc:])
        outs.append(o / l)
    o_ref[0] = jnp.where(lane < C_HEAD_DIM, outs[0], outs[1]).astype(o_ref.dtype)


def _na_tables(rpb, rows):
    h = rpb.shape[0]
    w = GRID_W
    qc = np.arange(w)
    kc = np.arange(w)
    cs = np.clip(qc - NA_COLS // 2, 0, w - NA_COLS)
    col_ok = (kc[None, :] >= cs[:, None]) & (kc[None, :] < cs[:, None] + NA_COLS)
    dc = np.clip(kc[None, :] - qc[:, None] + (NA_COLS - 1), 0, 2 * NA_COLS - 2)
    base = jnp.where(col_ok[None, None], rpb.astype(F32)[:, :, dc], NEG)
    base = jnp.concatenate([base, jnp.full((h, 1, w, w), NEG, F32)], axis=1)
    nblk = rows // NA_QROWS
    tabs = []
    for m in (0, 1, nblk - 1):
        qr = NA_QROWS * m + np.arange(NA_QROWS)
        rs = np.clip(qr - NA_ROWS // 2, 0, rows - NA_ROWS)
        kr = NA_QROWS * (m - 1) + np.arange(3 * NA_QROWS)
        ok = (kr[None, :] >= rs[:, None]) & (kr[None, :] < rs[:, None] + NA_ROWS)
        dr = np.where(ok, kr[None, :] - qr[:, None] + (NA_ROWS - 1), 2 * NA_ROWS - 1)
        t = base[:, dr]
        tabs.append(t.transpose(0, 1, 3, 2, 4).reshape(h, NA_QROWS * w, 3 * NA_QROWS * w))
    return jnp.stack(tabs)


def _na_attention(u_lat, u_ctx, tabs):
    b, n, _ = u_lat.shape
    nc = u_ctx.shape[1]
    tq = NA_QROWS * GRID_W
    nblk = n // tq
    pairs = C_HEADS // 2
    prev = lambda i: jnp.maximum(i - 1, 0)
    nxt = lambda i: jnp.minimum(i + 1, nblk - 1)
    blk = lambda col0, f: pl.BlockSpec((1, tq, LANES), lambda bi, hp, i: (bi, f(i), col0 + hp))
    ctx = lambda col0: pl.BlockSpec((1, nc, LANES), lambda bi, hp, i: (bi, 0, col0 + hp))
    same = lambda i: i
    sel = lambda i: jnp.where(i == 0, 0, jnp.where(i == nblk - 1, 2, 1))
    return pl.pallas_call(
        _na_kernel,
        grid=(b, pairs, nblk),
        in_specs=[
            blk(0, same),
            blk(pairs, prev), blk(pairs, same), blk(pairs, nxt), ctx(pairs),
            blk(2 * pairs, prev), blk(2 * pairs, same), blk(2 * pairs, nxt), ctx(2 * pairs),
            pl.BlockSpec((1, 2, tq, 3 * tq), lambda bi, hp, i: (sel(i), hp, 0, 0)),
        ],
        out_specs=pl.BlockSpec((1, tq, LANES), lambda bi, hp, i: (bi, i, hp)),
        out_shape=jax.ShapeDtypeStruct((b, n, C_WIDTH), BF16),
        compiler_params=_cparams("parallel", "parallel", "arbitrary"),
        name="na_attention",
    )(u_lat, u_lat, u_lat, u_lat, u_ctx, u_lat, u_lat, u_lat, u_ctx, tabs)


def _cd_out_kernel(c_ref, d_ref, x_ref, m_ref, wo_ref, o_ref):
    wc = c_ref.shape[2]
    o = _dot(c_ref[0], wo_ref[:wc]) + _dot(d_ref[0], wo_ref[wc:])
    o_ref[0] = x_ref[0] + m_ref[0, 2:3, :] * o


def _cd_out(c_lat, d_lat, x2d, mods, w_out, tm):
    b, n, _ = c_lat.shape
    d = x2d.shape[-1]
    tm = min(tm, n)
    nt = n // tm
    flat = pl.BlockSpec((1, tm, d), lambda bi, i: (0, bi * nt + i, 0))
    return pl.pallas_call(
        _cd_out_kernel,
        grid=(b, nt),
        in_specs=[
            pl.BlockSpec((1, tm, c_lat.shape[2]), lambda bi, i: (bi, i, 0)),
            pl.BlockSpec((1, tm, d_lat.shape[2]), lambda bi, i: (bi, i, 0)),
            flat,
            pl.BlockSpec((1, 6, d), lambda bi, i: (bi, 0, 0)),
            pl.BlockSpec(w_out.shape, lambda bi, i: (0, 0)),
        ],
        out_specs=flat,
        out_shape=jax.ShapeDtypeStruct((1, b * n, d), F32),
        compiler_params=_cparams("parallel", "parallel"),
        name="cd_out",
    )(c_lat, d_lat, x2d.reshape(1, -1, d), mods, w_out)


def _moe_params(w_rg, b_rg, w_re, b_re, w_gate, w_up, w_down, layer):
    d = w_rg.shape[0]
    w_r = jnp.zeros((d, LANES), F32).at[:, :MOE_GROUPS].set(w_rg).at[:, MOE_GROUPS:MOE_GROUPS + MOE_EXPERTS].set(w_re)
    b_r = jnp.zeros((1, LANES), F32).at[0, :MOE_GROUPS].set(b_rg).at[0, MOE_GROUPS:MOE_GROUPS + MOE_EXPERTS].set(b_re)
    w_hi = w_r.astype(BF16)
    w_lo = (w_r - w_hi.astype(F32)).astype(BF16)
    return w_hi, w_lo, b_r, w_gate, w_up, w_down, layer


def _rope_perm():
    j = np.arange(MLA_ROPE)
    half = MLA_ROPE // 2
    return (j // half) * half + (j % half + half // 2) % half


def _rope_tables(n):
    half = MLA_ROPE // 2
    nf = half // 2
    t = np.arange(n)
    inv = (np.float32(ROPE_THETA) ** (-np.arange(nf, dtype=np.float32) / np.float32(nf))).astype(np.float32)
    parts_c, parts_s = [], []
    for pos in ((t // GRID_W).astype(np.float32), (t % GRID_W).astype(np.float32)):
        ang = (pos[:, None] * inv[None, :]).astype(np.float32)
        c, s = np.cos(ang).astype(np.float32), np.sin(ang).astype(np.float32)
        parts_c += [c, c]
        parts_s += [-s, s]
    pad = MLA_PAD - MLA_NOPE - MLA_ROPE
    cos = np.concatenate([np.ones((n, MLA_NOPE), np.float32)] + parts_c + [np.zeros((n, pad), np.float32)], axis=1)
    sin = np.concatenate([np.zeros((n, MLA_NOPE), np.float32)] + parts_s + [np.zeros((n, pad), np.float32)], axis=1)
    return jnp.asarray(cos), jnp.asarray(sin)


def _identity_rope_tables(n):
    pad = MLA_PAD - MLA_NOPE - MLA_ROPE
    cos = jnp.concatenate([jnp.ones((n, MLA_NOPE + MLA_ROPE), F32), jnp.zeros((n, pad), F32)], axis=1)
    return cos, jnp.zeros((n, MLA_PAD), F32)


def _pad_heads(w, widths, src_cols, dst_off):
    rank = w.shape[0]
    out = jnp.zeros((rank, D_HEADS, MLA_PAD), F32)
    wh = w.reshape(rank, D_HEADS, widths)[:, :, src_cols]
    return out.at[:, :, dst_off:dst_off + len(src_cols)].set(wh).reshape(rank, D_HEADS * MLA_PAD)


def _cd_params(w_in, w_uq, w_ukv):
    d = w_in.shape[0]
    perm = _rope_perm()
    o = 3 * C_WIDTH
    q_scale = float(C_HEAD_DIM ** -0.5)
    kr = w_in[:, o + MLA_Q_RANK + MLA_KV_RANK:]
    pad_rope = lambda a: jnp.zeros((d, MLA_PAD), F32).at[:, MLA_NOPE:MLA_NOPE + MLA_ROPE].set(a)
    w_cat = jnp.concatenate([
        w_in[:, :C_WIDTH] * q_scale, w_in[:, C_WIDTH:o],
        w_in[:, o:o + MLA_Q_RANK + MLA_KV_RANK], pad_rope(kr), pad_rope(kr[:, perm]),
    ], axis=1).astype(BF16)
    qw = MLA_NOPE + MLA_ROPE
    nope = np.arange(MLA_NOPE)
    rope = MLA_NOPE + np.arange(MLA_ROPE)
    wq = (_pad_heads(w_uq, qw, nope, 0) + _pad_heads(w_uq, qw, rope, MLA_NOPE)).astype(BF16)
    wqp = _pad_heads(w_uq, qw, rope[perm], MLA_NOPE).astype(BF16)
    kvw = MLA_NOPE + MLA_V
    wk = _pad_heads(w_ukv, kvw, nope, 0).astype(BF16)
    wv = _pad_heads(w_ukv, kvw, MLA_NOPE + np.arange(MLA_V), 0).astype(BF16)
    return w_cat, wq, wqp, wk, wv


def kernel(x, c, ctx, c_ctx, ada_w, ada_b, norm1_g, norm2_g, ab_w_in, ab_w_out, hgrn_lb_logits, hgrn_onorm_g, pool_w,
           pool_scale, cd_w_in, cd_w_out, na_rpb, mla_q_norm_g, mla_w_uq, mla_kv_norm_g, mla_w_ukv, moe_w_rg, moe_b_rg,
           moe_w_re, moe_b_re, moe_w_gate, moe_w_up, moe_w_down, final_norm_g):
    b, n, d = x.shape
    n_ctx = ctx.shape[1]
    assert ada_w.shape[0] == 2 and ab_w_in.shape[0] == 1 and cd_w_in.shape[0] == 1 and b + 1 <= 8
    tm = 512

    cc = jnp.zeros((8, d), F32).at[:b].set(c).at[b].set(c_ctx)
    mods = _ada(cc, ada_w, ada_b).reshape(2, 8, 6, d)
    mods_lat = [mods[l, :b] for l in range(2)]
    mods_ctx = [jnp.broadcast_to(mods[l, b:b + 1], (b, 6, d)) for l in range(2)]
    lb = jnp.cumsum(jax.nn.softmax(hgrn_lb_logits.astype(F32), axis=1), axis=1)[:, 0]

    w_in0 = ab_w_in[0].astype(BF16)
    w_out0 = ab_w_out[0].astype(BF16)
    pw0 = pool_w[0].astype(BF16)
    ab_cols = w_in0.shape[1]
    (u_ctx,) = _in_proj(ctx, norm1_g[0], mods_ctx[0], w_in0, ((0, ab_cols),), (F32,), tm)
    (u_lat,) = _in_proj(x, norm1_g[0], mods_lat[0], w_in0, ((0, ab_cols),), (F32,), tm)
    s0 = jnp.zeros((b, 2, A_HEADS, A_HEAD_DIM, A_HEAD_DIM), F32)
    ocf, ocb, s_ctx = _hgrn_scan(u_ctx, lb, s0, 256)
    olf, olb, _ = _hgrn_scan(u_lat, lb, s_ctx, 256)
    t_lat, t_ctx = b * n, b * n_ctx
    assert n % tm == 0 and t_ctx == tm
    x_lat = _ab_out(olf, olb, u_lat, x, mods_lat[0], hgrn_onorm_g[0], pw0, pool_scale[0], w_out0, tm)
    x_ctx = _ab_out(ocf, ocb, u_ctx, ctx, mods_ctx[0], hgrn_onorm_g[0], pw0, pool_scale[0], w_out0, tm)
    moe0 = _moe_params(moe_w_rg[0], moe_b_rg[0], moe_w_re[0], moe_b_re[0], moe_w_gate, moe_w_up, moe_w_down, 0)
    mods_all = jnp.concatenate([mods_lat[0], mods[0, b:b + 1]], axis=0)
    xa = _moe(x_lat, x_ctx, norm2_g[0], mods_all, n // tm, moe0, final_norm_g, False, tm)

    w_cat, wq, wqp, wk, wv = _cd_params(cd_w_in[0], mla_w_uq[0], mla_w_ukv[0])
    na_w = 3 * C_WIDTH
    splits = ((0, na_w), (na_w, w_cat.shape[1]))
    ua_ctx, ub_ctx = _in_proj(xa, norm1_g[1], mods_ctx[1], w_cat, splits, (BF16, F32), tm, (b, n_ctx, t_lat))
    ua_lat, ub_lat = _in_proj(xa, norm1_g[1], mods_lat[1], w_cat, splits, (BF16, F32), tm, (b, n, 0))
    q_g = mla_q_norm_g[0].reshape(1, -1)
    kv_g = mla_kv_norm_g[0].reshape(1, -1)
    cos_l, sin_l = _rope_tables(n)
    cos_c, sin_c = _identity_rope_tables(n_ctx)
    k_c, v_c = _mla_proj(ub_ctx, cos_c, sin_c, q_g, kv_g, wq, wqp, wk, wv, False, tm)
    q_l, k_l, v_l = _mla_proj(ub_lat, cos_l, sin_l, q_g, kv_g, wq, wqp, wk, wv, True, tm)
    d_lat = _mla_attention(q_l, jnp.concatenate([k_c, k_l], axis=2), jnp.concatenate([v_c, v_l], axis=3), 256, 256)
    c_lat = _na_attention(ua_lat, ua_ctx, _na_tables(na_rpb[0], n // GRID_W))
    xl = _cd_out(c_lat, d_lat, xa, mods_lat[1], cd_w_out[0].astype(BF16), tm)
    moe1 = _moe_params(moe_w_rg[1], moe_b_rg[1], moe_w_re[1], moe_b_re[1], moe_w_gate, moe_w_up, moe_w_down, 1)
    out = _moe(xl.reshape(t_lat, d), None, norm2_g[1], mods_lat[1], n // tm, moe1, final_norm_g, True, tm)
    return out.reshape(b, n, d)
```

```python
import functools

import numpy as np
import jax
import jax.numpy as jnp
from jax import lax
from jax.experimental import pallas as pl
from jax.experimental.pallas import tpu as pltpu

F32 = jnp.float32
BF16 = jnp.bfloat16

EPS = 1e-6
NEG = -1e30

GRID_W = 64
A_HEADS = 4
A_HEAD_DIM = 128
A_WIDTH = A_HEADS * A_HEAD_DIM
POOL_WINDOWS = (2, 4, 8, 16)
B_GROUP = 128
B_WIDTH = B_GROUP * len(POOL_WINDOWS)
POOL_HALO = 16
C_HEADS = 8
C_HEAD_DIM = 64
C_WIDTH = C_HEADS * C_HEAD_DIM
NA_ROWS = 8
NA_COLS = 16
NA_QROWS = 4
NA_KBLOCK_ROWS = 4
NA_KBLOCKS = 3
D_HEADS = 8
MLA_Q_RANK = 256
MLA_KV_RANK = 128
MLA_NOPE = 64
MLA_ROPE = 32
MLA_V = 64
MLA_PAD = 128
MLA_VROWS = 80
ROPE_THETA = 10000.0
MOE_GROUPS = 4
MOE_EPG = 8
MOE_EXPERTS = MOE_GROUPS * MOE_EPG
MOE_HIDDEN = 256
LANES = 128
SLOT_BLOCK = 16
STEP_BLOCKS = 32
VMEM_LIMIT = 56 * 1024 * 1024

NT = (((1,), (1,)), ((), ()))
TN = (((0,), (0,)), ((), ()))


def _cparams(*sem):
    return pltpu.CompilerParams(dimension_semantics=sem, vmem_limit_bytes=VMEM_LIMIT)


def _sigmoid(x):
    return 1.0 / (1.0 + jnp.exp(-x))


def _silu(x):
    return x * _sigmoid(x)


def _dot(a, b):
    return jnp.dot(a, b, preferred_element_type=F32)


def _rmsnorm(x, g):
    return x * lax.rsqrt(jnp.mean(x * x, axis=-1, keepdims=True) + EPS) * g


def _ada_kernel(c_ref, w_ref, b_ref, o_ref):
    s = _silu(c_ref[...])
    o_ref[0] = jnp.dot(s, w_ref[0], precision=lax.Precision.HIGHEST, preferred_element_type=F32) + b_ref[0]


def _ada(cc, ada_w, ada_b):
    depth, d, n6 = ada_w.shape
    tn = n6 // 4
    return pl.pallas_call(
        _ada_kernel,
        grid=(depth, n6 // tn),
        in_specs=[
            pl.BlockSpec((8, d), lambda l, j: (0, 0)),
            pl.BlockSpec((1, d, tn), lambda l, j: (l, 0, j)),
            pl.BlockSpec((1, 1, tn), lambda l, j: (l, 0, j)),
        ],
        out_specs=pl.BlockSpec((1, 8, tn), lambda l, j: (l, 0, j)),
        out_shape=jax.ShapeDtypeStruct((depth, 8, n6), F32),
        compiler_params=_cparams("parallel", "parallel"),
        name="ada_mod",
    )(cc, ada_w, ada_b.reshape(depth, 1, n6))


def _in_kernel(x_ref, g_ref, m_ref, w_ref, *o_refs, splits):
    h = _rmsnorm(x_ref[0], g_ref[...]) * (1.0 + m_ref[0, 1:2, :]) + m_ref[0, 0:1, :]
    hb = h.astype(BF16)
    for o_ref, (a, b) in zip(o_refs, splits):
        o_ref[0] = _dot(hb, w_ref[:, a:b]).astype(o_ref.dtype)


def _in_proj(x, gain, mods, w, splits, dtypes, tm, flat=None):
    if flat is None:
        b, n, d = x.shape
        tm = min(tm, n)
        x_spec = pl.BlockSpec((1, tm, d), lambda bi, i: (bi, i, 0))
    else:
        b, n, row0 = flat
        d = x.shape[-1]
        tm = min(tm, n)
        x = x.reshape(1, -1, d)
        x_spec = pl.BlockSpec((1, tm, d), lambda bi, i: (0, row0 // tm + bi * (n // tm) + i, 0))
    outs = [jax.ShapeDtypeStruct((b, n, hi - lo), dt) for (lo, hi), dt in zip(splits, dtypes)]
    return pl.pallas_call(
        functools.partial(_in_kernel, splits=splits),
        grid=(b, n // tm),
        in_specs=[
            x_spec,
            pl.BlockSpec((1, d), lambda bi, i: (0, 0)),
            pl.BlockSpec((1, 6, d), lambda bi, i: (bi, 0, 0)),
            pl.BlockSpec(w.shape, lambda bi, i: (0, 0)),
        ],
        out_specs=[pl.BlockSpec((1, tm, hi - lo), lambda bi, i: (bi, i, 0)) for lo, hi in splits],
        out_shape=outs,
        compiler_params=_cparams("parallel", "parallel"),
        name="in_proj",
    )(x, gain.reshape(1, d), mods, w)


HG_SUB = 64


def _hgrn_direction(q_raw, fz, v, lb, st_ref, d, o_ref, reverse):
    rows = q_raw.shape[0]
    c = HG_SUB
    f = lb + (1.0 - lb) * _sigmoid(fz)
    k = 1.0 - f
    g = jnp.log(f)
    q = _silu(q_raw)
    r_i = lax.broadcasted_iota(jnp.int32, (c, c), 0)
    c_i = lax.broadcasted_iota(jnp.int32, (c, c), 1)
    keep = (c_i >= r_i) if reverse else (c_i <= r_i)
    tri = jnp.where(keep, 1.0, 0.0).astype(BF16)
    order = range(rows // c - 1, -1, -1) if reverse else range(rows // c)
    for ci in order:
        sl = slice(ci * c, (ci + 1) * c)
        gc = g[sl]
        g_hi = gc.astype(BF16)
        g_lo = (gc - g_hi.astype(F32)).astype(BF16)
        bc = _dot(tri, g_hi) + _dot(tri, g_lo)
        ref = bc[c // 2:c // 2 + 1]
        tot = bc[0:1] if reverse else bc[c - 1:c]
        qt = q[sl] * jnp.exp(bc - ref)
        kt = k[sl] * jnp.exp(ref - bc)
        qd = (qt * jnp.exp(ref)).astype(BF16)
        kd = (kt * jnp.exp(tot - ref)).astype(BF16)
        qt = qt.astype(BF16)
        kt = kt.astype(BF16)
        vb = v[sl].astype(BF16)
        dec = jnp.exp(tot)
        for h in range(A_HEADS):
            hs = slice(h * A_HEAD_DIM, (h + 1) * A_HEAD_DIM)
            att = lax.dot_general(qt[:, hs], kt[:, hs], NT, preferred_element_type=F32)
            att = jnp.where(keep, att, 0.0).astype(BF16)
            st = st_ref[d, h]
            o = _dot(att, vb[:, hs]) + lax.dot_general(qd[:, hs], st.astype(BF16), NT, preferred_element_type=F32)
            o_ref[0, sl, hs] = o
            st_ref[d, h] = st * dec[:, hs] + lax.dot_general(vb[:, hs], kd[:, hs], TN, preferred_element_type=F32)


def _hgrn_kernel(qf_ref, ff_ref, vf_ref, qb_ref, fb_ref, vb_ref, lb_ref, s0_ref, of_ref, ob_ref, sfin_ref, st_ref):
    j = pl.program_id(1)

    @pl.when(j == 0)
    def _():
        st_ref[...] = s0_ref[0]

    _hgrn_direction(qf_ref[0], ff_ref[0], vf_ref[0], lb_ref[0:1], st_ref, 0, of_ref, False)
    _hgrn_direction(qb_ref[0], fb_ref[0], vb_ref[0], lb_ref[1:2], st_ref, 1, ob_ref, True)

    @pl.when(j == pl.num_programs(1) - 1)
    def _():
        sfin_ref[0] = st_ref[...]


def _hgrn_scan(u, lb, s0, rows):
    b, n, _ = u.shape
    rows = min(rows, n)
    nb = n // rows
    w = A_WIDTH

    def fwd(col):
        return pl.BlockSpec((1, rows, w), lambda bi, j: (bi, j, col))

    def bwd(col):
        return pl.BlockSpec((1, rows, w), lambda bi, j: (bi, nb - 1 - j, col))

    st_spec = pl.BlockSpec((1, 2, A_HEADS, A_HEAD_DIM, A_HEAD_DIM), lambda bi, j: (bi, 0, 0, 0, 0))
    return pl.pallas_call(
        _hgrn_kernel,
        grid=(b, nb),
        in_specs=[fwd(0), fwd(1), fwd(3), bwd(0), bwd(2), bwd(3), pl.BlockSpec((2, w), lambda bi, j: (0, 0)), st_spec],
        out_specs=[
            pl.BlockSpec((1, rows, w), lambda bi, j: (bi, j, 0)),
            pl.BlockSpec((1, rows, w), lambda bi, j: (bi, nb - 1 - j, 0)),
            st_spec,
        ],
        out_shape=[
            jax.ShapeDtypeStruct((b, n, w), F32),
            jax.ShapeDtypeStruct((b, n, w), F32),
            jax.ShapeDtypeStruct(s0.shape, F32),
        ],
        scratch_shapes=[pltpu.VMEM((2, A_HEADS, A_HEAD_DIM, A_HEAD_DIM), F32)],
        compiler_params=_cparams("parallel", "arbitrary"),
        name="hgrn_scan",
    )(u, u, u, u, u, u, lb, s0)


def _ab_out_kernel(of_ref, ob_ref, ug_ref, up_ref, pprev_ref, pnext_ref, x_ref, m_ref, on_ref, pw_ref, ps_ref,
                   wo_ref, o_ref, *, n):
    i = pl.program_id(1)
    tm = x_ref.shape[1]
    o = of_ref[0] + ob_ref[0]
    gate = _silu(ug_ref[0])
    parts = []
    for h in range(A_HEADS):
        hs = slice(h * A_HEAD_DIM, (h + 1) * A_HEAD_DIM)
        parts.append(_rmsnorm(o[:, hs], on_ref[...]) * gate[:, hs])
    main = up_ref[0]
    prev = jnp.where(i > 0, pprev_ref[0], 0.0)
    nxt = jnp.where(i < pl.num_programs(1) - 1, pnext_ref[0], 0.0)
    ext = jnp.concatenate([prev, main, nxt], axis=0)
    ext_rows = tm + 2 * POOL_HALO
    t = i * tm + lax.broadcasted_iota(jnp.int32, (tm, 1), 0)
    for gi, win in enumerate(POOL_WINDOWS):
        gs = slice(gi * B_GROUP, (gi + 1) * B_GROUP)
        acc = ext[:, gs]
        acc = acc + pltpu.roll(acc, 1, 0)
        half = 1
        while 2 * half < win:
            acc = pltpu.roll(acc, half, 0) + pltpu.roll(acc, ext_rows - half, 0)
            half *= 2
        cnt = jnp.minimum(t + (win - win // 2), n) - jnp.maximum(t - win // 2, 0)
        mean = acc[POOL_HALO:POOL_HALO + tm] / cnt.astype(F32)
        pooled = _dot((mean - main[:, gs]).astype(BF16), pw_ref[gi])
        parts.append(pooled * ps_ref[:, gs])
    mix = jnp.concatenate(parts, axis=-1).astype(BF16)
    o_ref[0] = x_ref[0] + m_ref[0, 2:3, :] * _dot(mix, wo_ref[...])


def _ab_out(o_f, o_b, u, x, mods, onorm_g, pool_w, pool_scale, w_out, tm):
    b, n, d = x.shape
    tm = min(tm, n)
    nt = n // tm
    hb = tm // POOL_HALO
    last_halo = n // POOL_HALO - 1
    w = A_WIDTH
    tile = lambda col: pl.BlockSpec((1, tm, w), lambda bi, i: (bi, i, col))
    out = pl.pallas_call(
        functools.partial(_ab_out_kernel, n=n),
        grid=(b, nt),
        in_specs=[
            tile(0), tile(0), tile(4), tile(5),
            pl.BlockSpec((1, POOL_HALO, w), lambda bi, i: (bi, jnp.maximum(i * hb - 1, 0), 5)),
            pl.BlockSpec((1, POOL_HALO, w), lambda bi, i: (bi, jnp.minimum((i + 1) * hb, last_halo), 5)),
            pl.BlockSpec((1, tm, d), lambda bi, i: (bi, i, 0)),
            pl.BlockSpec((1, 6, d), lambda bi, i: (bi, 0, 0)),
            pl.BlockSpec((1, A_HEAD_DIM), lambda bi, i: (0, 0)),
            pl.BlockSpec(pool_w.shape, lambda bi, i: (0, 0, 0)),
            pl.BlockSpec((1, B_WIDTH), lambda bi, i: (0, 0)),
            pl.BlockSpec(w_out.shape, lambda bi, i: (0, 0)),
        ],
        out_specs=pl.BlockSpec((1, tm, d), lambda bi, i: (bi, i, 0)),
        out_shape=jax.ShapeDtypeStruct((b, n, d), F32),
        compiler_params=_cparams("parallel", "parallel"),
        name="ab_out",
    )(o_f, o_b, u, u, u, u, x, mods, onorm_g.reshape(1, A_HEAD_DIM), pool_w, pool_scale.reshape(1, B_WIDTH), w_out)
    return out.reshape(b * n, d)


def _slot_rows(tr):
    rows = 2 * tr + MOE_EXPERTS * (SLOT_BLOCK - 1)
    assert rows % SLOT_BLOCK == 0
    return rows


def _drop_tail(kern, *refs, **kw):
    return kern(refs[0], None, *refs[1:], **kw)


def _tile_tokens(x_ref, xt_ref):
    if xt_ref is None:
        return x_ref[...]
    return jnp.where(pl.program_id(0) < pl.num_programs(0) - 1, x_ref[...], xt_ref[...])


def _route_kernel(x_ref, xt_ref, g_ref, m_ref, whi_ref, wlo_ref, br_ref, xs_ref, info_ref, cnt_ref, *, slot_rows):
    tr = x_ref.shape[0]
    h = _rmsnorm(_tile_tokens(x_ref, xt_ref), g_ref[...]) * (1.0 + m_ref[0, 4:5, :]) + m_ref[0, 3:4, :]
    hb = h.astype(BF16)
    hl = (h - hb.astype(F32)).astype(BF16)
    logits = _dot(hb, whi_ref[...]) + _dot(hb, wlo_ref[...]) + _dot(hl, whi_ref[...]) + br_ref[...]
    lane = lax.broadcasted_iota(jnp.int32, (tr, LANES), 1)
    lanef = lane.astype(F32)
    lg = jnp.where(lane < MOE_GROUPS, logits, NEG)
    mg = jnp.max(lg, axis=-1, keepdims=True)
    g_p = 1.0 / jnp.sum(jnp.exp(lg - mg), axis=-1, keepdims=True)
    gidx = jnp.min(jnp.where(lg == mg, lanef, float(LANES)), axis=-1, keepdims=True)
    lo = MOE_GROUPS + MOE_EPG * gidx
    le = jnp.where((lanef >= lo) & (lanef < lo + MOE_EPG), logits, NEG)
    m1 = jnp.max(le, axis=-1, keepdims=True)
    i1 = jnp.min(jnp.where(le == m1, lanef, float(LANES)), axis=-1, keepdims=True)
    le2 = jnp.where(lanef == i1, NEG, le)
    m2 = jnp.max(le2, axis=-1, keepdims=True)
    i2 = jnp.min(jnp.where(le2 == m2, lanef, float(LANES)), axis=-1, keepdims=True)
    ratio = jnp.exp(m2 - m1)
    w1 = g_p / (1.0 + ratio)
    w2 = g_p * ratio / (1.0 + ratio)
    hot1 = lanef == i1
    hot2 = lanef == i2
    hot = jnp.where(hot1, 1.0, jnp.where(hot2, 1.0, 0.0))
    r_i = lax.broadcasted_iota(jnp.int32, (tr, tr), 0)
    c_i = lax.broadcasted_iota(jnp.int32, (tr, tr), 1)
    rank = _dot(jnp.where(c_i < r_i, 1.0, 0.0).astype(BF16), hot.astype(BF16))
    cnt = jnp.sum(hot, axis=0, keepdims=True)
    nblk = jnp.floor((cnt + (SLOT_BLOCK - 1)) * (1.0 / SLOT_BLOCK))
    l_r = lax.broadcasted_iota(jnp.int32, (LANES, LANES), 0)
    l_c = lax.broadcasted_iota(jnp.int32, (LANES, LANES), 1)
    before = jnp.where(l_r < l_c, 1.0, 0.0).astype(BF16)
    off = SLOT_BLOCK * _dot(jnp.broadcast_to(nblk, (8, LANES)).astype(BF16), before)[0:1]
    posm = off + rank
    pos1 = jnp.sum(jnp.where(hot1, posm, 0.0), axis=-1, keepdims=True)
    pos2 = jnp.sum(jnp.where(hot2, posm, 0.0), axis=-1, keepdims=True)
    info = jnp.where(lane == 0, pos1, jnp.where(lane == 1, pos2, jnp.where(lane == 2, w1, jnp.where(lane == 3, w2, 0.0))))
    info_ref[...] = info
    pos_t = info.T.astype(jnp.int32)
    row = lax.broadcasted_iota(jnp.int32, (slot_rows, tr), 0)
    sel = jnp.where(row == pos_t[0:1], 1.0, jnp.where(row == pos_t[1:2], 1.0, 0.0)).astype(BF16)
    xs_ref[...] = _dot(sel, hb).astype(BF16).reshape(xs_ref.shape)
    cnt_ref[0] = jnp.broadcast_to(cnt, (8, LANES))


def _token_specs(x2d, x_tail, tr, index):
    d = x2d.shape[1]
    nt = x2d.shape[0] // tr
    if x_tail is None:
        return nt, [pl.BlockSpec((tr, d), index(lambda i: (i, 0)))], [x2d]
    assert x_tail.shape == (tr, d)
    return nt + 1, [pl.BlockSpec((tr, d), index(lambda i: (jnp.minimum(i, nt - 1), 0))),
                    pl.BlockSpec((tr, d), index(lambda i: (0, 0)))], [x2d, x_tail]


def _moe_route(x2d, x_tail, gain, mods, tiles_per_mod, w_hi, w_lo, b_r, tr):
    d = x2d.shape[1]
    nt, x_specs, x_args = _token_specs(x2d, x_tail, tr, lambda f: f)
    t = nt * tr
    sr = _slot_rows(tr)
    kern = _route_kernel if x_tail is not None else functools.partial(_drop_tail, _route_kernel)
    return pl.pallas_call(
        functools.partial(kern, slot_rows=sr),
        grid=(nt,),
        in_specs=x_specs + [
            pl.BlockSpec((1, d), lambda i: (0, 0)),
            pl.BlockSpec((1, 6, d), lambda i: (jnp.minimum(i // tiles_per_mod, mods.shape[0] - 1), 0, 0)),
            pl.BlockSpec((d, LANES), lambda i: (0, 0)),
            pl.BlockSpec((d, LANES), lambda i: (0, 0)),
            pl.BlockSpec((1, LANES), lambda i: (0, 0)),
        ],
        out_specs=[
            pl.BlockSpec((sr // SLOT_BLOCK, SLOT_BLOCK, d), lambda i: (i, 0, 0)),
            pl.BlockSpec((tr, LANES), lambda i: (i, 0)),
            pl.BlockSpec((1, 8, LANES), lambda i: (i, 0, 0)),
        ],
        out_shape=[
            jax.ShapeDtypeStruct((nt * sr // SLOT_BLOCK, SLOT_BLOCK, d), BF16),
            jax.ShapeDtypeStruct((t, LANES), F32),
            jax.ShapeDtypeStruct((nt, 8, LANES), F32),
        ],
        compiler_params=_cparams("parallel"),
        name="moe_route",
    )(*x_args, gain.reshape(1, d), mods, w_hi, w_lo, b_r)


def _tables_kernel(cnt_ref, src_ref, inv_ref, exp_ref, valid_ref, *, ntiles, bpt):
    cnt = cnt_ref[...]
    nb = jnp.floor((cnt + (SLOT_BLOCK - 1)) * (1.0 / SLOT_BLOCK))
    i_r = lax.broadcasted_iota(jnp.int32, (LANES, LANES), 0)
    i_c = lax.broadcasted_iota(jnp.int32, (LANES, LANES), 1)
    before = jnp.where(i_r < i_c, 1.0, 0.0)
    upto = jnp.where(i_c <= i_r, 1.0, 0.0)
    first = _dot(nb, before)
    cum = _dot(upto, nb)
    tot = jnp.max(cum, axis=0, keepdims=True)
    steps = jnp.floor((tot + (STEP_BLOCKS - 1)) * (1.0 / STEP_BLOCKS))
    start = STEP_BLOCKS * _dot(jnp.broadcast_to(steps, (LANES, LANES)), before)[0:1]
    pos = start + cum - nb

    sub = 8 * (-(-(MOE_GROUPS + MOE_EXPERTS) // 8))
    first_t, nb_t, pos_t = first.T[:sub], nb.T[:sub], pos.T[:sub]
    as_col = lambda v: jnp.broadcast_to(v, (LANES, LANES)).T[:sub, 0:1]
    start_c, tot_c, span_c = as_col(start), as_col(tot), as_col(STEP_BLOCKS * steps)

    inv_ref[...] = jnp.zeros(inv_ref.shape, jnp.int32)
    local = lax.broadcasted_iota(jnp.int32, (sub, LANES), 1).astype(F32)
    nsrc = src_ref.shape[1]
    j = lax.broadcasted_iota(jnp.int32, (sub, nsrc), 1).astype(F32)
    acc = jnp.zeros((1, nsrc), F32)
    for i in range(ntiles):
        f_i, n_i, p_i = first_t[:, i:i + 1], nb_t[:, i:i + 1], pos_t[:, i:i + 1]
        own = (local >= f_i) & (local < f_i + n_i)
        inv_ref[i:i + 1, :] = jnp.sum(jnp.where(own, p_i + (local - f_i), 0.0), axis=0, keepdims=True).astype(jnp.int32)
        own = (j >= p_i) & (j < p_i + n_i)
        acc = acc + jnp.sum(jnp.where(own, (i * bpt) + f_i + (j - p_i), 0.0), axis=0, keepdims=True)
    src_ref[...] = acc.astype(jnp.int32)

    nst = exp_ref.shape[1]
    at = STEP_BLOCKS * lax.broadcasted_iota(jnp.int32, (sub, nst), 1).astype(F32)
    expert = (lax.broadcasted_iota(jnp.int32, (sub, nst), 0) - MOE_GROUPS).astype(F32)
    inside = (at >= start_c) & (at < start_c + span_c)
    exp_ref[...] = jnp.sum(jnp.where(inside, expert, 0.0), axis=0, keepdims=True).astype(jnp.int32)
    occupied = jnp.where(inside, jnp.where(at - start_c < tot_c, 1.0, 0.0), 0.0)
    valid_ref[...] = jnp.sum(occupied, axis=0, keepdims=True).astype(jnp.int32)


def _expert_tables(cnt, bpt, nsteps):
    ntiles = cnt.shape[0]
    assert bpt <= LANES and ntiles <= LANES
    cnt = jnp.pad(cnt, ((0, LANES - ntiles), (0, 0)))
    nsrc = -(-nsteps * STEP_BLOCKS // LANES) * LANES
    nst = -(-nsteps // LANES) * LANES
    i32 = lambda *s: jax.ShapeDtypeStruct(s, jnp.int32)
    src, inv, step_e, valid = pl.pallas_call(
        functools.partial(_tables_kernel, ntiles=ntiles, bpt=bpt),
        out_shape=[i32(1, nsrc), i32(LANES, LANES), i32(1, nst), i32(1, nst)],
        compiler_params=pltpu.CompilerParams(vmem_limit_bytes=VMEM_LIMIT),
        name="moe_tables",
    )(cnt)
    return src.reshape(-1), inv.reshape(-1), step_e.reshape(-1), valid.reshape(-1)


def _block_gather(table_ref, first, nblocks, src_hbm, buf_ref, slot, sem_ref):
    return [pltpu.make_async_copy(src_hbm.at[table_ref[first + kk]],
                                  buf_ref.at[slot, pl.ds(kk * SLOT_BLOCK, SLOT_BLOCK)], sem_ref.at[slot])
            for kk in range(nblocks)]


def _experts_kernel(src_ref, exp_ref, valid_ref, xs_hbm, wg_ref, wu_ref, wd_ref, y_ref, xbuf_ref, sem_ref, wgb_ref,
                    wub_ref, wdb_ref):
    s = pl.program_id(0)
    slot = s % 2

    def gather(step, to_slot):
        return _block_gather(src_ref, step * STEP_BLOCKS, STEP_BLOCKS, xs_hbm, xbuf_ref, to_slot, sem_ref)

    @pl.when((s == 0) & (valid_ref[0] > 0))
    def _():
        for cp in gather(0, 0):
            cp.start()

    nxt = jnp.minimum(s + 1, pl.num_programs(0) - 1)

    @pl.when((s + 1 < pl.num_programs(0)) & (valid_ref[nxt] > 0))
    def _():
        for cp in gather(s + 1, 1 - slot):
            cp.start()

    @pl.when((s == 0) | (exp_ref[s] != exp_ref[jnp.maximum(s - 1, 0)]))
    def _():
        wgb_ref[...] = wg_ref[0, 0, 0].astype(BF16)
        wub_ref[...] = wu_ref[0, 0, 0].astype(BF16)
        wdb_ref[...] = wd_ref[0, 0, 0].astype(BF16)

    @pl.when(valid_ref[s] > 0)
    def _():
        for cp in gather(s, slot):
            cp.wait()
        x = xbuf_ref[slot]
        a = _silu(_dot(x, wgb_ref[...])) * _dot(x, wub_ref[...])
        y_ref[...] = _dot(a.astype(BF16), wdb_ref[...]).astype(BF16).reshape(y_ref.shape)

    @pl.when(valid_ref[s] == 0)
    def _():
        y_ref[...] = jnp.zeros(y_ref.shape, y_ref.dtype)


def _moe_experts(xs, src, step_e, valid, w_gate, w_up, w_down, layer, nsteps):
    xs3 = xs
    d = xs3.shape[-1]
    f = w_gate.shape[-1]
    step_rows = STEP_BLOCKS * SLOT_BLOCK
    w_blk = lambda shape: pl.BlockSpec((1, 1, 1) + shape,
                                       lambda s, sr, ex, va: (layer, ex[s] // MOE_EPG, ex[s] % MOE_EPG, 0, 0))
    grid_spec = pltpu.PrefetchScalarGridSpec(
        num_scalar_prefetch=3,
        grid=(nsteps,),
        in_specs=[pl.BlockSpec(memory_space=pl.ANY), w_blk((d, f)), w_blk((d, f)), w_blk((f, d))],
        out_specs=pl.BlockSpec((STEP_BLOCKS, SLOT_BLOCK, d), lambda s, sr, ex, va: (s, 0, 0)),
        scratch_shapes=[pltpu.VMEM((2, step_rows, d), BF16), pltpu.SemaphoreType.DMA((2,)),
                        pltpu.VMEM((d, f), BF16), pltpu.VMEM((d, f), BF16), pltpu.VMEM((f, d), BF16)],
    )
    return pl.pallas_call(
        _experts_kernel,
        grid_spec=grid_spec,
        out_shape=jax.ShapeDtypeStruct((nsteps * STEP_BLOCKS, SLOT_BLOCK, d), BF16),
        compiler_params=_cparams("arbitrary"),
        name="moe_experts",
    )(src, step_e, valid, xs3, w_gate, w_up, w_down)


def _combine_kernel(inv_ref, x_ref, xt_ref, info_ref, m_ref, fg_ref, ys_hbm, o_ref, ybuf_ref, sem_ref, *, bpt, final):
    i = pl.program_id(0)
    slot = i % 2
    tr = x_ref.shape[0]

    def gather(tile, to_slot):
        return _block_gather(inv_ref, tile * LANES, bpt, ys_hbm, ybuf_ref, to_slot, sem_ref)

    @pl.when(i == 0)
    def _():
        for cp in gather(0, 0):
            cp.start()

    @pl.when(i + 1 < pl.num_programs(0))
    def _():
        for cp in gather(i + 1, 1 - slot):
            cp.start()

    info = info_ref[...]
    col = lax.broadcasted_iota(jnp.int32, (tr, bpt * SLOT_BLOCK), 1)
    wsel = jnp.where(col == info[:, 0:1].astype(jnp.int32), info[:, 2:3],
                     jnp.where(col == info[:, 1:2].astype(jnp.int32), info[:, 3:4], 0.0))
    for cp in gather(i, slot):
        cp.wait()
    y = _dot(wsel.astype(BF16), ybuf_ref[slot])
    out = _tile_tokens(x_ref, xt_ref) + m_ref[0, 5:6, :] * y
    if final:
        out = _rmsnorm(out, fg_ref[...])
    o_ref[...] = out


def _moe_combine(x2d, x_tail, ys, inv, info, mods, tiles_per_mod, final_g, tr, bpt, final):
    d = x2d.shape[1]
    nt, x_specs, x_args = _token_specs(x2d, x_tail, tr, lambda f: (lambda i, iv: f(i)))
    t = nt * tr
    ys3 = ys
    kern = _combine_kernel
    if x_tail is None:
        kern = lambda inv_ref, x_ref, *refs, **kw: _combine_kernel(inv_ref, x_ref, None, *refs, **kw)
    grid_spec = pltpu.PrefetchScalarGridSpec(
        num_scalar_prefetch=1,
        grid=(nt,),
        in_specs=x_specs + [
            pl.BlockSpec((tr, LANES), lambda i, iv: (i, 0)),
            pl.BlockSpec((1, 6, d), lambda i, iv: (jnp.minimum(i // tiles_per_mod, mods.shape[0] - 1), 0, 0)),
            pl.BlockSpec((1, d), lambda i, iv: (0, 0)),
            pl.BlockSpec(memory_space=pl.ANY),
        ],
        out_specs=pl.BlockSpec((tr, d), lambda i, iv: (i, 0)),
        scratch_shapes=[pltpu.VMEM((2, bpt * SLOT_BLOCK, d), BF16), pltpu.SemaphoreType.DMA((2,))],
    )
    return pl.pallas_call(
        functools.partial(kern, bpt=bpt, final=final),
        grid_spec=grid_spec,
        out_shape=jax.ShapeDtypeStruct((t, d), F32),
        compiler_params=_cparams("arbitrary"),
        name="moe_combine",
    )(inv, *x_args, info, mods, final_g.reshape(1, d), ys3)


def _moe(x2d, x_tail, gain, mods, tiles_per_mod, params, final_g, final, tr):
    w_hi, w_lo, b_r, w_gate, w_up, w_down, layer = params
    nt = x2d.shape[0] // tr + (x_tail is not None)
    xs, info, cnt = _moe_route(x2d, x_tail, gain, mods, tiles_per_mod, w_hi, w_lo, b_r, tr)
    bpt = _slot_rows(tr) // SLOT_BLOCK
    nsteps = -(-(nt * bpt + MOE_EXPERTS * (STEP_BLOCKS - 1)) // STEP_BLOCKS)
    src, inv, step_e, valid = _expert_tables(cnt[:, 0, :], bpt, nsteps)
    ys = _moe_experts(xs, src, step_e, valid, w_gate, w_up, w_down, layer, nsteps)
    return _moe_combine(x2d, x_tail, ys, inv, info, mods, tiles_per_mod, final_g, tr, bpt, final)


def _mla_proj_kernel(cq_ref, ckv_ref, kr_ref, krp_ref, cos_ref, sin_ref, qg_ref, kg_ref, wq_ref, wqp_ref, wk_ref,
                     wv_ref, vone_ref, *o_refs, need_q, q_scale):
    cos = cos_ref[...]
    sin = sin_ref[...]
    ckv = _rmsnorm(ckv_ref[0], kg_ref[...]).astype(BF16)
    k_rope = kr_ref[0] * cos + krp_ref[0] * sin
    kn = _dot(ckv, wk_ref[...])
    if need_q:
        q_ref, k_ref, v_ref = o_refs
    else:
        k_ref, v_ref = o_refs
    vx = _dot(ckv, wv_ref[...]) + vone_ref[...]
    for h in range(D_HEADS):
        hs = slice(h * MLA_PAD, (h + 1) * MLA_PAD)
        k_ref[0, h] = (kn[:, hs] + k_rope).astype(BF16)
        v_ref[0, h] = vx[:, hs].T[:MLA_VROWS].astype(BF16)
    if need_q:
        cq = _rmsnorm(cq_ref[0], qg_ref[...]).astype(BF16)
        qm = _dot(cq, wq_ref[...])
        qp = _dot(cq, wqp_ref[...])
        for h in range(D_HEADS):
            hs = slice(h * MLA_PAD, (h + 1) * MLA_PAD)
            q_ref[0, h] = ((qm[:, hs] * cos + qp[:, hs] * sin) * q_scale).T.astype(BF16)


def _mla_proj(u_b, cos, sin, q_g, kv_g, wq, wqp, wk, wv, need_q, tm):
    b, n, _ = u_b.shape
    tm = min(tm, n)
    row_major = (jax.ShapeDtypeStruct((b, D_HEADS, n, MLA_PAD), BF16),
                 pl.BlockSpec((1, D_HEADS, tm, MLA_PAD), lambda bi, i: (bi, 0, i, 0)))
    col_major = (jax.ShapeDtypeStruct((b, D_HEADS, MLA_PAD, n), BF16),
                 pl.BlockSpec((1, D_HEADS, MLA_PAD, tm), lambda bi, i: (bi, 0, 0, i)))
    v_major = (jax.ShapeDtypeStruct((b, D_HEADS, MLA_VROWS, n), BF16),
               pl.BlockSpec((1, D_HEADS, MLA_VROWS, tm), lambda bi, i: (bi, 0, 0, i)))
    outs, specs = zip(*(([col_major] if need_q else []) + [row_major, v_major]))
    full = lambda a: pl.BlockSpec(a.shape, lambda bi, i: (0,) * a.ndim)
    vone = jnp.tile(jnp.concatenate([jnp.zeros((1, MLA_V), F32), jnp.ones((1, MLA_PAD - MLA_V), F32)], axis=1),
                    (1, D_HEADS))
    q_scale = float((MLA_NOPE + MLA_ROPE) ** -0.5 * np.log2(np.e))
    return pl.pallas_call(
        functools.partial(_mla_proj_kernel, need_q=need_q, q_scale=q_scale),
        grid=(b, n // tm),
        in_specs=[
            pl.BlockSpec((1, tm, MLA_Q_RANK), lambda bi, i: (bi, i, 0)),
            pl.BlockSpec((1, tm, MLA_KV_RANK), lambda bi, i: (bi, i, 2)),
            pl.BlockSpec((1, tm, MLA_PAD), lambda bi, i: (bi, i, 3)),
            pl.BlockSpec((1, tm, MLA_PAD), lambda bi, i: (bi, i, 4)),
            pl.BlockSpec((tm, MLA_PAD), lambda bi, i: (i, 0)),
            pl.BlockSpec((tm, MLA_PAD), lambda bi, i: (i, 0)),
            full(q_g), full(kv_g), full(wq), full(wqp), full(wk), full(wv), full(vone),
        ],
        out_specs=list(specs),
        out_shape=list(outs),
        compiler_params=_cparams("parallel", "parallel"),
        name="mla_proj",
    )(u_b, u_b, u_b, u_b, cos, sin, q_g, kv_g, wq, wqp, wk, wv, vone)


def _mla_attn_kernel(q_ref, qn_ref, kc_ref, kl_ref, vc_ref, vl_ref, o_ref, acc0_ref, acc1_ref, s0_ref, s1_ref, m_ref,
                     *, tk):
    tq = q_ref.shape[3]
    ncc = kc_ref.shape[2] // tk
    nchunks = ncc + kl_ref.shape[2] // tk
    neg = jnp.full((8, tq), NEG, F32)
    s_refs = (s0_ref, s1_ref)
    acc_refs = (acc0_ref, acc1_ref)

    def chunk(c):
        part = (kc_ref, vc_ref, c) if c < ncc else (kl_ref, vl_ref, c - ncc)
        return part[0], part[1], pl.ds(part[2] * tk, tk), pl.ds(c * tk, tk)

    def scores(k_ref, ks, ss, hh, q, m):
        s = _dot(k_ref[0, hh, ks, :], q)
        s_refs[hh][ss, :] = s
        return jnp.maximum(m, jnp.max(s.reshape(tk // 8, 8, tq), axis=0))

    def weight(v_ref, ks, ss, hh, m_row):
        p = jnp.exp2((s_refs[hh][ss, :] - m_row).astype(BF16))
        acc_refs[hh][...] += _dot(v_ref[0, hh, :, ks], p)

    @pl.when(pl.program_id(2) == 0)
    def _():
        m = neg
        for c in range(ncc):
            k_ref, _, ks, ss = chunk(c)
            m = scores(k_ref, ks, ss, 0, q_ref[0, 0], m)

        def latent(c, m):
            ks = pl.ds(pl.multiple_of(c * tk, tk), tk)
            ss = pl.ds(pl.multiple_of((c + ncc) * tk, tk), tk)
            return scores(kl_ref, ks, ss, 0, q_ref[0, 0], m)

        m_ref[...] = lax.fori_loop(0, nchunks - ncc, latent, m)

    acc0_ref[...] = jnp.zeros(acc0_ref.shape, F32)
    acc1_ref[...] = jnp.zeros(acc1_ref.shape, F32)
    m0 = jnp.max(m_ref[...], axis=0, keepdims=True)
    m1 = neg
    for c in range(nchunks):
        k_ref, v_ref, ks, ss = chunk(c)
        weight(v_ref, ks, ss, 0, m0)
        m1 = scores(k_ref, ks, ss, 1, q_ref[0, 1], m1)
    m1 = jnp.max(m1, axis=0, keepdims=True)
    m0_next = neg
    for c in range(nchunks):
        k_ref, v_ref, ks, ss = chunk(c)
        weight(v_ref, ks, ss, 1, m1)
        m0_next = scores(k_ref, ks, ss, 0, qn_ref[0, 0], m0_next)
    m_ref[...] = m0_next
    o_t = jnp.concatenate([a[:MLA_V] / a[MLA_V:MLA_V + 1] for a in acc_refs], axis=0)
    o_ref[0] = o_t.T.astype(o_ref.dtype)


def _mla_attention(q_t, k_ctx, k_lat, v_ctx, v_lat, tq, tk):
    b, h, _, n = q_t.shape
    nc = k_ctx.shape[2]
    tq = min(tq, n)
    assert nc % tk == 0 and n % tk == 0
    k_spec = lambda rows: pl.BlockSpec((1, 2, rows, MLA_PAD), lambda bi, hp, i: (bi, hp, 0, 0))
    v_spec = lambda rows: pl.BlockSpec((1, 2, MLA_VROWS, rows), lambda bi, hp, i: (bi, hp, 0, 0))
    return pl.pallas_call(
        functools.partial(_mla_attn_kernel, tk=tk),
        grid=(b, h // 2, n // tq),
        in_specs=[
            pl.BlockSpec((1, 2, MLA_PAD, tq), lambda bi, hp, i: (bi, hp, 0, i)),
            pl.BlockSpec((1, 2, MLA_PAD, tq), lambda bi, hp, i: (bi, hp, 0, jnp.minimum(i + 1, n // tq - 1))),
            k_spec(nc), k_spec(n), v_spec(nc), v_spec(n),
        ],
        out_specs=pl.BlockSpec((1, tq, 2 * MLA_V), lambda bi, hp, i: (bi, i, hp)),
        out_shape=jax.ShapeDtypeStruct((b, n, h * MLA_V), BF16),
        scratch_shapes=[pltpu.VMEM((MLA_VROWS, tq), F32), pltpu.VMEM((MLA_VROWS, tq), F32),
                        pltpu.VMEM((nc + n, tq), F32), pltpu.VMEM((nc + n, tq), F32), pltpu.VMEM((8, tq), F32)],
        compiler_params=_cparams("parallel", "parallel", "arbitrary"),
        name="mla_attention",
    )(q_t, q_t, k_ctx, k_lat, v_ctx, v_lat)


def _na_kernel(q_ref, *refs):
    k_refs, v_refs = refs[:NA_KBLOCKS + 1], refs[NA_KBLOCKS + 1:2 * NA_KBLOCKS + 2]
    tab_ref, o_ref = refs[2 * NA_KBLOCKS + 2:]
    tq = q_ref.shape[1]
    nloc = tab_ref.shape[3]
    lane = lax.broadcasted_iota(jnp.int32, (tq, LANES), 1)
    q = q_ref[0]
    k_all = jnp.concatenate([r[0] for r in k_refs], axis=0)
    v_all = jnp.concatenate([r[0] for r in v_refs], axis=0)
    outs = []
    for hh in range(2):
        in_head = (lane >= hh * C_HEAD_DIM) & (lane < (hh + 1) * C_HEAD_DIM)
        qh = jnp.where(in_head, q, jnp.zeros_like(q))
        s = lax.dot_general(qh, k_all, NT, preferred_element_type=F32)
        s_loc = s[:, :nloc] + tab_ref[0, hh]
        s_ctx = s[:, nloc:]
        m = jnp.maximum(jnp.max(s_loc, axis=-1, keepdims=True), jnp.max(s_ctx, axis=-1, keepdims=True))
        p_loc = jnp.exp(s_loc - m)
        p_ctx = jnp.exp(s_ctx - m)
        l = jnp.sum(p_loc, axis=-1, keepdims=True) + jnp.sum(p_ctx, axis=-1, keepdims=True)
        o = _dot(p_loc.astype(BF16), v_all[:nloc]) + _dot(p_ctx.astype(BF16), v_all[nloc:])
        outs.append(o / l)
    o_ref[0] = jnp.where(lane < C_HEAD_DIM, outs[0], outs[1]).astype(o_ref.dtype)


def _na_tables(rpb, rows):
    h = rpb.shape[0]
    w = GRID_W
    kr_n = NA_QROWS + NA_ROWS
    qc = np.arange(w)
    kc = np.arange(w)
    cs = np.clip(qc - NA_COLS // 2, 0, w - NA_COLS)
    col_ok = (kc[None, :] >= cs[:, None]) & (kc[None, :] < cs[:, None] + NA_COLS)
    dc = np.clip(kc[None, :] - qc[:, None] + (NA_COLS - 1), 0, 2 * NA_COLS - 2)
    pick_dc = (dc.reshape(-1)[None, :] == np.arange(2 * NA_COLS - 1)[:, None]).astype(np.float32)
    base = jnp.einsum('hrd,dx->hrx', rpb.astype(F32), jnp.asarray(pick_dc), precision=lax.Precision.HIGHEST)
    nblk = rows // NA_QROWS
    tabs = []
    for m in (0, 1, nblk - 1):
        qr = NA_QROWS * m + np.arange(NA_QROWS)
        rs = np.clip(qr - NA_ROWS // 2, 0, rows - NA_ROWS)
        kr = NA_QROWS * m - NA_ROWS // 2 + np.arange(kr_n)
        row_ok = (kr[None, :] >= rs[:, None]) & (kr[None, :] < rs[:, None] + NA_ROWS)
        dr = np.clip(kr[None, :] - qr[:, None] + (NA_ROWS - 1), 0, 2 * NA_ROWS - 2)
        pick_dr = (dr.reshape(-1)[:, None] == np.arange(2 * NA_ROWS - 1)[None, :]).astype(np.float32)
        t = jnp.einsum('vr,hrx->hvx', jnp.asarray(pick_dr), base, precision=lax.Precision.HIGHEST)
        t = t.reshape(h, NA_QROWS, kr_n, w, w).transpose(0, 1, 3, 2, 4)
        ok = row_ok[:, None, :, None] & col_ok[None, :, None, :]
        tabs.append(jnp.where(jnp.asarray(ok)[None], t, NEG).reshape(h, NA_QROWS * w, kr_n * w))
    return jnp.stack(tabs)


def _na_attention(u_lat, u_ctx, tabs):
    b, n, _ = u_lat.shape
    nc = u_ctx.shape[1]
    tq = NA_QROWS * GRID_W
    tkb = NA_KBLOCK_ROWS * GRID_W
    nblk = n // tq
    nkb = n // tkb
    per_q = NA_QROWS // NA_KBLOCK_ROWS
    assert NA_KBLOCKS == per_q + 2 and NA_ROWS // 2 == NA_KBLOCK_ROWS and n % tq == 0 and n // tq >= 2
    pairs = C_HEADS // 2
    q_spec = pl.BlockSpec((1, tq, LANES), lambda bi, hp, i: (bi, i, hp))
    kblk = lambda col0, j: pl.BlockSpec(
        (1, tkb, LANES), lambda bi, hp, i: (bi, jnp.clip(i * per_q - 1 + j, 0, nkb - 1), col0 + hp))
    ctx = lambda col0: pl.BlockSpec((1, nc, LANES), lambda bi, hp, i: (bi, 0, col0 + hp))
    sel = lambda i: jnp.where(i == 0, 0, jnp.where(i == nblk - 1, 2, 1))
    kv_specs = [kblk(col0, j) for col0 in (pairs, 2 * pairs) for j in range(NA_KBLOCKS)]
    kv_specs = kv_specs[:NA_KBLOCKS] + [ctx(pairs)] + kv_specs[NA_KBLOCKS:] + [ctx(2 * pairs)]
    kv_args = [u_lat] * NA_KBLOCKS + [u_ctx]
    return pl.pallas_call(
        _na_kernel,
        grid=(b, pairs, nblk),
        in_specs=[q_spec] + kv_specs + [
            pl.BlockSpec((1, 2, tq, NA_KBLOCKS * tkb), lambda bi, hp, i: (sel(i), hp, 0, 0)),
        ],
        out_specs=pl.BlockSpec((1, tq, LANES), lambda bi, hp, i: (bi, i, hp)),
        out_shape=jax.ShapeDtypeStruct((b, n, C_WIDTH), BF16),
        compiler_params=_cparams("parallel", "parallel", "arbitrary"),
        name="na_attention",
    )(u_lat, *kv_args, *kv_args, tabs)


def _cd_out_kernel(c_ref, d_ref, x_ref, m_ref, wo_ref, o_ref):
    wc = c_ref.shape[2]
    o = _dot(c_ref[0], wo_ref[:wc]) + _dot(d_ref[0], wo_ref[wc:])
    o_ref[0] = x_ref[0] + m_ref[0, 2:3, :] * o


def _cd_out(c_lat, d_lat, x2d, mods, w_out, tm):
    b, n, _ = c_lat.shape
    d = x2d.shape[-1]
    tm = min(tm, n)
    nt = n // tm
    flat = pl.BlockSpec((1, tm, d), lambda bi, i: (0, bi * nt + i, 0))
    return pl.pallas_call(
        _cd_out_kernel,
        grid=(b, nt),
        in_specs=[
            pl.BlockSpec((1, tm, c_lat.shape[2]), lambda bi, i: (bi, i, 0)),
            pl.BlockSpec((1, tm, d_lat.shape[2]), lambda bi, i: (bi, i, 0)),
            flat,
            pl.BlockSpec((1, 6, d), lambda bi, i: (bi, 0, 0)),
            pl.BlockSpec(w_out.shape, lambda bi, i: (0, 0)),
        ],
        out_specs=flat,
        out_shape=jax.ShapeDtypeStruct((1, b * n, d), F32),
        compiler_params=_cparams("parallel", "parallel"),
        name="cd_out",
    )(c_lat, d_lat, x2d.reshape(1, -1, d), mods, w_out)


def _moe_params(w_rg, b_rg, w_re, b_re, w_gate, w_up, w_down, layer):
    d = w_rg.shape[0]
    w_r = jnp.zeros((d, LANES), F32).at[:, :MOE_GROUPS].set(w_rg).at[:, MOE_GROUPS:MOE_GROUPS + MOE_EXPERTS].set(w_re)
    b_r = jnp.zeros((1, LANES), F32).at[0, :MOE_GROUPS].set(b_rg).at[0, MOE_GROUPS:MOE_GROUPS + MOE_EXPERTS].set(b_re)
    w_hi = w_r.astype(BF16)
    w_lo = (w_r - w_hi.astype(F32)).astype(BF16)
    return w_hi, w_lo, b_r, w_gate, w_up, w_down, layer


def _rope_perm():
    j = np.arange(MLA_ROPE)
    half = MLA_ROPE // 2
    return (j // half) * half + (j % half + half // 2) % half


def _rope_tables(n):
    half = MLA_ROPE // 2
    nf = half // 2
    t = np.arange(n)
    inv = (np.float32(ROPE_THETA) ** (-np.arange(nf, dtype=np.float32) / np.float32(nf))).astype(np.float32)
    parts_c, parts_s = [], []
    for pos in ((t // GRID_W).astype(np.float32), (t % GRID_W).astype(np.float32)):
        ang = (pos[:, None] * inv[None, :]).astype(np.float32)
        c, s = np.cos(ang).astype(np.float32), np.sin(ang).astype(np.float32)
        parts_c += [c, c]
        parts_s += [-s, s]
    pad = MLA_PAD - MLA_NOPE - MLA_ROPE
    cos = np.concatenate([np.ones((n, MLA_NOPE), np.float32)] + parts_c + [np.zeros((n, pad), np.float32)], axis=1)
    sin = np.concatenate([np.zeros((n, MLA_NOPE), np.float32)] + parts_s + [np.zeros((n, pad), np.float32)], axis=1)
    return jnp.asarray(cos), jnp.asarray(sin)


def _identity_rope_tables(n):
    pad = MLA_PAD - MLA_NOPE - MLA_ROPE
    cos = jnp.concatenate([jnp.ones((n, MLA_NOPE + MLA_ROPE), F32), jnp.zeros((n, pad), F32)], axis=1)
    return cos, jnp.zeros((n, MLA_PAD), F32)


def _pad_heads(w, widths, src_cols, dst_off):
    rank = w.shape[0]
    out = jnp.zeros((rank, D_HEADS, MLA_PAD), F32)
    wh = w.reshape(rank, D_HEADS, widths)[:, :, src_cols]
    return out.at[:, :, dst_off:dst_off + len(src_cols)].set(wh).reshape(rank, D_HEADS * MLA_PAD)


def _cd_params(w_in, w_uq, w_ukv):
    d = w_in.shape[0]
    perm = _rope_perm()
    o = 3 * C_WIDTH
    q_scale = float(C_HEAD_DIM ** -0.5)
    kr = w_in[:, o + MLA_Q_RANK + MLA_KV_RANK:]
    pad_rope = lambda a: jnp.zeros((d, MLA_PAD), F32).at[:, MLA_NOPE:MLA_NOPE + MLA_ROPE].set(a)
    w_cat = jnp.concatenate([
        w_in[:, :C_WIDTH] * q_scale, w_in[:, C_WIDTH:o],
        w_in[:, o:o + MLA_Q_RANK + MLA_KV_RANK], pad_rope(kr), pad_rope(kr[:, perm]),
    ], axis=1).astype(BF16)
    qw = MLA_NOPE + MLA_ROPE
    nope = np.arange(MLA_NOPE)
    rope = MLA_NOPE + np.arange(MLA_ROPE)
    wq = (_pad_heads(w_uq, qw, nope, 0) + _pad_heads(w_uq, qw, rope, MLA_NOPE)).astype(BF16)
    wqp = _pad_heads(w_uq, qw, rope[perm], MLA_NOPE).astype(BF16)
    kvw = MLA_NOPE + MLA_V
    wk = _pad_heads(w_ukv, kvw, nope, 0).astype(BF16)
    wv = _pad_heads(w_ukv, kvw, MLA_NOPE + np.arange(MLA_V), 0).astype(BF16)
    return w_cat, wq, wqp, wk, wv


def kernel(x, c, ctx, c_ctx, ada_w, ada_b, norm1_g, norm2_g, ab_w_in, ab_w_out, hgrn_lb_logits, hgrn_onorm_g, pool_w,
           pool_scale, cd_w_in, cd_w_out, na_rpb, mla_q_norm_g, mla_w_uq, mla_kv_norm_g, mla_w_ukv, moe_w_rg, moe_b_rg,
           moe_w_re, moe_b_re, moe_w_gate, moe_w_up, moe_w_down, final_norm_g):
    b, n, d = x.shape
    n_ctx = ctx.shape[1]
    assert ada_w.shape[0] == 2 and ab_w_in.shape[0] == 1 and cd_w_in.shape[0] == 1 and b + 1 <= 8
    tm = 512

    cc = jnp.zeros((8, d), F32).at[:b].set(c).at[b].set(c_ctx)
    mods = _ada(cc, ada_w, ada_b).reshape(2, 8, 6, d)
    mods_lat = [mods[l, :b] for l in range(2)]
    mods_ctx = [jnp.broadcast_to(mods[l, b:b + 1], (b, 6, d)) for l in range(2)]
    lb = jnp.cumsum(jax.nn.softmax(hgrn_lb_logits.astype(F32), axis=1), axis=1)[:, 0]

    w_in0 = ab_w_in[0].astype(BF16)
    w_out0 = ab_w_out[0].astype(BF16)
    pw0 = pool_w[0].astype(BF16)
    ab_cols = w_in0.shape[1]
    (u_ctx,) = _in_proj(ctx, norm1_g[0], mods_ctx[0], w_in0, ((0, ab_cols),), (F32,), tm)
    (u_lat,) = _in_proj(x, norm1_g[0], mods_lat[0], w_in0, ((0, ab_cols),), (F32,), tm)
    s0 = jnp.zeros((b, 2, A_HEADS, A_HEAD_DIM, A_HEAD_DIM), F32)
    ocf, ocb, s_ctx = _hgrn_scan(u_ctx, lb, s0, 256)
    olf, olb, _ = _hgrn_scan(u_lat, lb, s_ctx, 256)
    t_lat, t_ctx = b * n, b * n_ctx
    assert n % tm == 0 and t_ctx == tm
    x_lat = _ab_out(olf, olb, u_lat, x, mods_lat[0], hgrn_onorm_g[0], pw0, pool_scale[0], w_out0, tm)
    x_ctx = _ab_out(ocf, ocb, u_ctx, ctx, mods_ctx[0], hgrn_onorm_g[0], pw0, pool_scale[0], w_out0, tm)
    moe0 = _moe_params(moe_w_rg[0], moe_b_rg[0], moe_w_re[0], moe_b_re[0], moe_w_gate, moe_w_up, moe_w_down, 0)
    mods_all = jnp.concatenate([mods_lat[0], mods[0, b:b + 1]], axis=0)
    xa = _moe(x_lat, x_ctx, norm2_g[0], mods_all, n // tm, moe0, final_norm_g, False, tm)

    w_cat, wq, wqp, wk, wv = _cd_params(cd_w_in[0], mla_w_uq[0], mla_w_ukv[0])
    na_w = 3 * C_WIDTH
    splits = ((0, na_w), (na_w, w_cat.shape[1]))
    ua_ctx, ub_ctx = _in_proj(xa, norm1_g[1], mods_ctx[1], w_cat, splits, (BF16, F32), tm, (b, n_ctx, t_lat))
    ua_lat, ub_lat = _in_proj(xa, norm1_g[1], mods_lat[1], w_cat, splits, (BF16, F32), tm, (b, n, 0))
    q_g = mla_q_norm_g[0].reshape(1, -1)
    kv_g = mla_kv_norm_g[0].reshape(1, -1)
    cos_l, sin_l = _rope_tables(n)
    cos_c, sin_c = _identity_rope_tables(n_ctx)
    k_c, v_c = _mla_proj(ub_ctx, cos_c, sin_c, q_g, kv_g, wq, wqp, wk, wv, False, tm)
    q_l, k_l, v_l = _mla_proj(ub_lat, cos_l, sin_l, q_g, kv_g, wq, wqp, wk, wv, True, tm)
    d_lat = _mla_attention(q_l, k_c, k_l, v_c, v_l, 256, 256)
    c_lat = _na_attention(ua_lat, ua_ctx, _na_tables(na_rpb[0], n // GRID_W))
    xl = _cd_out(c_lat, d_lat, xa, mods_lat[1], cd_w_out[0].astype(BF16), tm)
    moe1 = _moe_params(moe_w_rg[1], moe_b_rg[1], moe_w_re[1], moe_b_re[1], moe_w_gate, moe_w_up, moe_w_down, 1)
    out = _moe(xl.reshape(t_lat, d), None, norm2_g[1], mods_lat[1], n // tm, moe1, final_norm_g, True, tm)
    return out.reshape(b, n, d)
```

```python
import functools

import numpy as np
import jax
import jax.numpy as jnp
from jax import lax
from jax.experimental import pallas as pl
from jax.experimental.pallas import tpu as pltpu

F32 = jnp.float32
BF16 = jnp.bfloat16

EPS = 1e-6
NEG = -1e30

GRID_W = 64
A_HEADS = 4
A_HEAD_DIM = 128
A_WIDTH = A_HEADS * A_HEAD_DIM
POOL_WINDOWS = (2, 4, 8, 16)
B_GROUP = 128
B_WIDTH = B_GROUP * len(POOL_WINDOWS)
POOL_HALO = 16
C_HEADS = 8
C_HEAD_DIM = 64
C_WIDTH = C_HEADS * C_HEAD_DIM
NA_ROWS = 8
NA_COLS = 16
NA_QROWS = 4
NA_KBLOCK_ROWS = 4
NA_KBLOCKS = 3
D_HEADS = 8
MLA_Q_RANK = 256
MLA_KV_RANK = 128
MLA_NOPE = 64
MLA_ROPE = 32
MLA_V = 64
MLA_PAD = 128
MLA_VROWS = 80
ROPE_THETA = 10000.0
MOE_GROUPS = 4
MOE_EPG = 8
MOE_EXPERTS = MOE_GROUPS * MOE_EPG
MOE_HIDDEN = 256
LANES = 128
SLOT_BLOCK = 16
STEP_BLOCKS = 32
VMEM_LIMIT = 56 * 1024 * 1024
TOKEN_TILE = 512
HGRN_ROWS = 256
MLA_Q_TILE = 256
MLA_K_CHUNK = 256

NT = (((1,), (1,)), ((), ()))
TN = (((0,), (0,)), ((), ()))


def _cparams(*sem):
    return pltpu.CompilerParams(dimension_semantics=sem, vmem_limit_bytes=VMEM_LIMIT)


def _sigmoid(x):
    return 1.0 / (1.0 + jnp.exp(-x))


def _silu(x):
    return x * _sigmoid(x)


def _dot(a, b):
    return jnp.dot(a, b, preferred_element_type=F32)


def _rmsnorm(x, g):
    return x * lax.rsqrt(jnp.mean(x * x, axis=-1, keepdims=True) + EPS) * g


def _ada_kernel(ct_ref, w_ref, b_ref, o_ref, *, nrows):
    s = _silu(ct_ref[...])
    w = w_ref[0]
    rows = [jnp.sum(s[:, r:r + 1] * w, axis=0, keepdims=True) + b_ref[0] for r in range(nrows)]
    rows.append(jnp.zeros((o_ref.shape[1] - nrows, w.shape[1]), F32))
    o_ref[0] = jnp.concatenate(rows, axis=0)


def _ada(cc, nrows, ada_w, ada_b):
    depth, d, n6 = ada_w.shape
    tn = n6 // 8
    return pl.pallas_call(
        functools.partial(_ada_kernel, nrows=nrows),
        grid=(depth, n6 // tn),
        in_specs=[
            pl.BlockSpec((d, 8), lambda l, j: (0, 0)),
            pl.BlockSpec((1, d, tn), lambda l, j: (l, 0, j)),
            pl.BlockSpec((1, 1, tn), lambda l, j: (l, 0, j)),
        ],
        out_specs=pl.BlockSpec((1, 8, tn), lambda l, j: (l, 0, j)),
        out_shape=jax.ShapeDtypeStruct((depth, 8, n6), F32),
        compiler_params=_cparams("parallel", "parallel"),
        name="ada_mod",
    )(cc.T, ada_w, ada_b.reshape(depth, 1, n6))


def _in_kernel(x_ref, g_ref, m_ref, w_ref, *o_refs, splits):
    h = _rmsnorm(x_ref[0], g_ref[...]) * (1.0 + m_ref[0, 1:2, :]) + m_ref[0, 0:1, :]
    hb = h.astype(BF16)
    for o_ref, (a, b) in zip(o_refs, splits):
        o_ref[0] = _dot(hb, w_ref[:, a:b]).astype(o_ref.dtype)


def _in_proj(x, gain, mods, w, splits, dtypes, tm, flat=None):
    if flat is None:
        b, n, d = x.shape
        tm = min(tm, n)
        x_spec = pl.BlockSpec((1, tm, d), lambda bi, i: (bi, i, 0))
    else:
        b, n, row0 = flat
        d = x.shape[-1]
        tm = min(tm, n)
        x = x.reshape(1, -1, d)
        x_spec = pl.BlockSpec((1, tm, d), lambda bi, i: (0, row0 // tm + bi * (n // tm) + i, 0))
    outs = [jax.ShapeDtypeStruct((b, n, hi - lo), dt) for (lo, hi), dt in zip(splits, dtypes)]
    return pl.pallas_call(
        functools.partial(_in_kernel, splits=splits),
        grid=(b, n // tm),
        in_specs=[
            x_spec,
            pl.BlockSpec((1, d), lambda bi, i: (0, 0)),
            pl.BlockSpec((1, 6, d), lambda bi, i: (bi, 0, 0)),
            pl.BlockSpec(w.shape, lambda bi, i: (0, 0)),
        ],
        out_specs=[pl.BlockSpec((1, tm, hi - lo), lambda bi, i: (bi, i, 0)) for lo, hi in splits],
        out_shape=outs,
        compiler_params=_cparams("parallel", "parallel"),
        name="in_proj",
    )(x, gain.reshape(1, d), mods, w)


HG_SUB = 64


def _hgrn_direction(q_raw, fz, v, lb, st_ref, d, o_ref, reverse):
    rows = q_raw.shape[0]
    c = HG_SUB
    f = lb + (1.0 - lb) * _sigmoid(fz)
    k = 1.0 - f
    g = jnp.log(f)
    q = _silu(q_raw)
    r_i = lax.broadcasted_iota(jnp.int32, (c, c), 0)
    c_i = lax.broadcasted_iota(jnp.int32, (c, c), 1)
    keep = (c_i >= r_i) if reverse else (c_i <= r_i)
    tri = jnp.where(keep, 1.0, 0.0).astype(BF16)
    order = range(rows // c - 1, -1, -1) if reverse else range(rows // c)
    for ci in order:
        sl = slice(ci * c, (ci + 1) * c)
        gc = g[sl]
        g_hi = gc.astype(BF16)
        g_lo = (gc - g_hi.astype(F32)).astype(BF16)
        bc = _dot(tri, g_hi) + _dot(tri, g_lo)
        ref = bc[c // 2:c // 2 + 1]
        tot = bc[0:1] if reverse else bc[c - 1:c]
        qt = q[sl] * jnp.exp(bc - ref)
        kt = k[sl] * jnp.exp(ref - bc)
        qd = (qt * jnp.exp(ref)).astype(BF16)
        kd = (kt * jnp.exp(tot - ref)).astype(BF16)
        qt = qt.astype(BF16)
        kt = kt.astype(BF16)
        vb = v[sl].astype(BF16)
        dec = jnp.exp(tot)
        for h in range(A_HEADS):
            hs = slice(h * A_HEAD_DIM, (h + 1) * A_HEAD_DIM)
            att = lax.dot_general(qt[:, hs], kt[:, hs], NT, preferred_element_type=F32)
            att = jnp.where(keep, att, 0.0).astype(BF16)
            st = st_ref[d, h]
            o = _dot(att, vb[:, hs]) + lax.dot_general(qd[:, hs], st.astype(BF16), NT, preferred_element_type=F32)
            o_ref[0, sl, hs] = o
            st_ref[d, h] = st * dec[:, hs] + lax.dot_general(vb[:, hs], kd[:, hs], TN, preferred_element_type=F32)


def _hgrn_kernel(qf_ref, ff_ref, vf_ref, qb_ref, fb_ref, vb_ref, lb_ref, s0_ref, of_ref, ob_ref, sfin_ref, st_ref):
    j = pl.program_id(1)

    @pl.when(j == 0)
    def _():
        st_ref[...] = s0_ref[0]

    _hgrn_direction(qf_ref[0], ff_ref[0], vf_ref[0], lb_ref[0:1], st_ref, 0, of_ref, False)
    _hgrn_direction(qb_ref[0], fb_ref[0], vb_ref[0], lb_ref[1:2], st_ref, 1, ob_ref, True)

    @pl.when(j == pl.num_programs(1) - 1)
    def _():
        sfin_ref[0] = st_ref[...]


def _hgrn_scan(u, lb, s0, rows):
    b, n, _ = u.shape
    rows = min(rows, n)
    nb = n // rows
    w = A_WIDTH

    def fwd(col):
        return pl.BlockSpec((1, rows, w), lambda bi, j: (bi, j, col))

    def bwd(col):
        return pl.BlockSpec((1, rows, w), lambda bi, j: (bi, nb - 1 - j, col))

    st_spec = pl.BlockSpec((1, 2, A_HEADS, A_HEAD_DIM, A_HEAD_DIM), lambda bi, j: (bi, 0, 0, 0, 0))
    return pl.pallas_call(
        _hgrn_kernel,
        grid=(b, nb),
        in_specs=[fwd(0), fwd(1), fwd(3), bwd(0), bwd(2), bwd(3), pl.BlockSpec((2, w), lambda bi, j: (0, 0)), st_spec],
        out_specs=[
            pl.BlockSpec((1, rows, w), lambda bi, j: (bi, j, 0)),
            pl.BlockSpec((1, rows, w), lambda bi, j: (bi, nb - 1 - j, 0)),
            st_spec,
        ],
        out_shape=[
            jax.ShapeDtypeStruct((b, n, w), F32),
            jax.ShapeDtypeStruct((b, n, w), F32),
            jax.ShapeDtypeStruct(s0.shape, F32),
        ],
        scratch_shapes=[pltpu.VMEM((2, A_HEADS, A_HEAD_DIM, A_HEAD_DIM), F32)],
        compiler_params=_cparams("parallel", "arbitrary"),
        name="hgrn_scan",
    )(u, u, u, u, u, u, lb, s0)


def _ab_out_kernel(of_ref, ob_ref, ug_ref, up_ref, pprev_ref, pnext_ref, x_ref, m_ref, on_ref, pw_ref, ps_ref,
                   wo_ref, o_ref, *, n):
    i = pl.program_id(1)
    tm = x_ref.shape[1]
    o = of_ref[0] + ob_ref[0]
    gate = _silu(ug_ref[0])
    parts = []
    for h in range(A_HEADS):
        hs = slice(h * A_HEAD_DIM, (h + 1) * A_HEAD_DIM)
        parts.append(_rmsnorm(o[:, hs], on_ref[...]) * gate[:, hs])
    main = up_ref[0]
    prev = jnp.where(i > 0, pprev_ref[0], 0.0)
    nxt = jnp.where(i < pl.num_programs(1) - 1, pnext_ref[0], 0.0)
    ext = jnp.concatenate([prev, main, nxt], axis=0)
    ext_rows = tm + 2 * POOL_HALO
    t = i * tm + lax.broadcasted_iota(jnp.int32, (tm, 1), 0)
    for gi, win in enumerate(POOL_WINDOWS):
        gs = slice(gi * B_GROUP, (gi + 1) * B_GROUP)
        acc = ext[:, gs]
        acc = acc + pltpu.roll(acc, 1, 0)
        half = 1
        while 2 * half < win:
            acc = pltpu.roll(acc, half, 0) + pltpu.roll(acc, ext_rows - half, 0)
            half *= 2
        cnt = jnp.minimum(t + (win - win // 2), n) - jnp.maximum(t - win // 2, 0)
        mean = acc[POOL_HALO:POOL_HALO + tm] / cnt.astype(F32)
        pooled = _dot((mean - main[:, gs]).astype(BF16), pw_ref[gi])
        parts.append(pooled * ps_ref[:, gs])
    mix = jnp.concatenate(parts, axis=-1).astype(BF16)
    o_ref[0] = x_ref[0] + m_ref[0, 2:3, :] * _dot(mix, wo_ref[...])


def _ab_out(o_f, o_b, u, x, mods, onorm_g, pool_w, pool_scale, w_out, tm):
    b, n, d = x.shape
    tm = min(tm, n)
    nt = n // tm
    hb = tm // POOL_HALO
    last_halo = n // POOL_HALO - 1
    w = A_WIDTH
    tile = lambda col: pl.BlockSpec((1, tm, w), lambda bi, i: (bi, i, col))
    out = pl.pallas_call(
        functools.partial(_ab_out_kernel, n=n),
        grid=(b, nt),
        in_specs=[
            tile(0), tile(0), tile(4), tile(5),
            pl.BlockSpec((1, POOL_HALO, w), lambda bi, i: (bi, jnp.maximum(i * hb - 1, 0), 5)),
            pl.BlockSpec((1, POOL_HALO, w), lambda bi, i: (bi, jnp.minimum((i + 1) * hb, last_halo), 5)),
            pl.BlockSpec((1, tm, d), lambda bi, i: (bi, i, 0)),
            pl.BlockSpec((1, 6, d), lambda bi, i: (bi, 0, 0)),
            pl.BlockSpec((1, A_HEAD_DIM), lambda bi, i: (0, 0)),
            pl.BlockSpec(pool_w.shape, lambda bi, i: (0, 0, 0)),
            pl.BlockSpec((1, B_WIDTH), lambda bi, i: (0, 0)),
            pl.BlockSpec(w_out.shape, lambda bi, i: (0, 0)),
        ],
        out_specs=pl.BlockSpec((1, tm, d), lambda bi, i: (bi, i, 0)),
        out_shape=jax.ShapeDtypeStruct((b, n, d), F32),
        compiler_params=_cparams("parallel", "parallel"),
        name="ab_out",
    )(o_f, o_b, u, u, u, u, x, mods, onorm_g.reshape(1, A_HEAD_DIM), pool_w, pool_scale.reshape(1, B_WIDTH), w_out)
    return out.reshape(b * n, d)


def _slot_rows(tr):
    rows = 2 * tr + MOE_EXPERTS * (SLOT_BLOCK - 1)
    assert rows % SLOT_BLOCK == 0
    return rows


def _tile_tokens(x_ref, xt_ref):
    if xt_ref is None:
        return x_ref[...]
    return jnp.where(pl.program_id(0) < pl.num_programs(0) - 1, x_ref[...], xt_ref[...])


def _route_kernel(*refs, slot_rows, has_tail, has_mix):
    refs = list(refs)
    x_ref = refs.pop(0)
    xt_ref = refs.pop(0) if has_tail else None
    g_ref, m_ref, wr_ref, br_ref = refs[:4]
    refs = refs[4:]
    x = _tile_tokens(x_ref, xt_ref)
    if has_mix:
        c_ref, d_ref, wo_ref = refs[:3]
        xs_ref, info_ref, cnt_ref, x1_ref = refs[3:]
        wc = c_ref.shape[1]
        x = x + m_ref[0, 2:3, :] * (_dot(c_ref[...], wo_ref[:wc]) + _dot(d_ref[...], wo_ref[wc:]))
        x1_ref[...] = x
    else:
        xs_ref, info_ref, cnt_ref = refs
    tr = x_ref.shape[0]
    h = _rmsnorm(x, g_ref[...]) * (1.0 + m_ref[0, 4:5, :]) + m_ref[0, 3:4, :]
    hb = h.astype(BF16)
    hl = (h - hb.astype(F32)).astype(BF16)
    w_hi, w_lo = wr_ref[:, :LANES], wr_ref[:, LANES:]
    logits = _dot(hb, w_hi) + _dot(hb, w_lo) + _dot(hl, w_hi) + br_ref[...]
    lane = lax.broadcasted_iota(jnp.int32, (tr, LANES), 1)
    lanef = lane.astype(F32)
    lg = jnp.where(lane < MOE_GROUPS, logits, NEG)
    mg = jnp.max(lg, axis=-1, keepdims=True)
    g_p = 1.0 / jnp.sum(jnp.exp(lg - mg), axis=-1, keepdims=True)
    gidx = jnp.min(jnp.where(lg == mg, lanef, float(LANES)), axis=-1, keepdims=True)
    lo = MOE_GROUPS + MOE_EPG * gidx
    le = jnp.where((lanef >= lo) & (lanef < lo + MOE_EPG), logits, NEG)
    m1 = jnp.max(le, axis=-1, keepdims=True)
    i1 = jnp.min(jnp.where(le == m1, lanef, float(LANES)), axis=-1, keepdims=True)
    le2 = jnp.where(lanef == i1, NEG, le)
    m2 = jnp.max(le2, axis=-1, keepdims=True)
    i2 = jnp.min(jnp.where(le2 == m2, lanef, float(LANES)), axis=-1, keepdims=True)
    ratio = jnp.exp(m2 - m1)
    w1 = g_p / (1.0 + ratio)
    w2 = g_p * ratio / (1.0 + ratio)
    hot1 = lanef == i1
    hot2 = lanef == i2
    hot = jnp.where(hot1, 1.0, jnp.where(hot2, 1.0, 0.0))
    r_i = lax.broadcasted_iota(jnp.int32, (tr, tr), 0)
    c_i = lax.broadcasted_iota(jnp.int32, (tr, tr), 1)
    rank = _dot(jnp.where(c_i < r_i, 1.0, 0.0).astype(BF16), hot.astype(BF16))
    cnt = jnp.sum(hot, axis=0, keepdims=True)
    nblk = jnp.floor((cnt + (SLOT_BLOCK - 1)) * (1.0 / SLOT_BLOCK))
    l_r = lax.broadcasted_iota(jnp.int32, (LANES, LANES), 0)
    l_c = lax.broadcasted_iota(jnp.int32, (LANES, LANES), 1)
    before = jnp.where(l_r < l_c, 1.0, 0.0).astype(BF16)
    off = SLOT_BLOCK * _dot(jnp.broadcast_to(nblk, (8, LANES)).astype(BF16), before)[0:1]
    posm = off + rank
    pos1 = jnp.sum(jnp.where(hot1, posm, 0.0), axis=-1, keepdims=True)
    pos2 = jnp.sum(jnp.where(hot2, posm, 0.0), axis=-1, keepdims=True)
    info = jnp.where(lane == 0, pos1, jnp.where(lane == 1, pos2, jnp.where(lane == 2, w1, jnp.where(lane == 3, w2, 0.0))))
    info_ref[...] = info
    pos_t = info.T.astype(jnp.int32)
    row = lax.broadcasted_iota(jnp.int32, (slot_rows, tr), 0)
    sel = jnp.where(row == pos_t[0:1], 1.0, jnp.where(row == pos_t[1:2], 1.0, 0.0)).astype(BF16)
    xs_ref[...] = _dot(sel, hb).astype(BF16).reshape(xs_ref.shape)
    cnt_ref[0] = jnp.broadcast_to(cnt, (8, LANES))


def _token_specs(x2d, x_tail, tr, index, rows=None):
    d = x2d.shape[1]
    nt = (x2d.shape[0] if rows is None else rows) // tr
    if x_tail is None:
        return nt, [pl.BlockSpec((tr, d), index(lambda i: (i, 0)))], [x2d]
    assert x_tail.shape == (tr, d)
    return nt + 1, [pl.BlockSpec((tr, d), index(lambda i: (jnp.minimum(i, nt - 1), 0))),
                    pl.BlockSpec((tr, d), index(lambda i: (0, 0)))], [x2d, x_tail]


def _moe_route(x2d, x_tail, gain, mods, tiles_per_mod, w_r, b_r, tr, mix=None):
    d = x2d.shape[1]
    rows = None if mix is None else mix[0].shape[0]
    nt, x_specs, x_args = _token_specs(x2d, x_tail, tr, lambda f: f, rows)
    t = nt * tr
    sr = _slot_rows(tr)
    mix_specs, mix_args, mix_out_specs, mix_out_shapes = [], [], [], []
    if mix is not None:
        c2d, d2d, w_out = mix
        mix_specs = [pl.BlockSpec((tr, c2d.shape[1]), lambda i: (i, 0)), pl.BlockSpec((tr, d2d.shape[1]), lambda i: (i, 0)),
                     pl.BlockSpec(w_out.shape, lambda i: (0, 0))]
        mix_args = [c2d, d2d, w_out]
        mix_out_specs = [pl.BlockSpec((tr, d), lambda i: (i, 0))]
        mix_out_shapes = [jax.ShapeDtypeStruct((t, d), F32)]
    return pl.pallas_call(
        functools.partial(_route_kernel, slot_rows=sr, has_tail=x_tail is not None, has_mix=mix is not None),
        grid=(nt,),
        in_specs=x_specs + [
            pl.BlockSpec((1, d), lambda i: (0, 0)),
            pl.BlockSpec((1, 6, d), lambda i: (jnp.minimum(i // tiles_per_mod, mods.shape[0] - 1), 0, 0)),
            pl.BlockSpec(w_r.shape, lambda i: (0, 0)),
            pl.BlockSpec((1, LANES), lambda i: (0, 0)),
        ] + mix_specs,
        out_specs=[
            pl.BlockSpec((sr // SLOT_BLOCK, SLOT_BLOCK, d), lambda i: (i, 0, 0)),
            pl.BlockSpec((tr, LANES), lambda i: (i, 0)),
            pl.BlockSpec((1, 8, LANES), lambda i: (i, 0, 0)),
        ] + mix_out_specs,
        out_shape=[
            jax.ShapeDtypeStruct((nt * sr // SLOT_BLOCK, SLOT_BLOCK, d), BF16),
            jax.ShapeDtypeStruct((t, LANES), F32),
            jax.ShapeDtypeStruct((nt, 8, LANES), F32),
        ] + mix_out_shapes,
        compiler_params=_cparams("parallel"),
        name="moe_route",
    )(*x_args, gain.reshape(1, d), mods, w_r, b_r, *mix_args)


def _tables_kernel(cnt_ref, src_ref, inv_ref, exp_ref, valid_ref, *, ntiles, bpt):
    cnt = cnt_ref[...]
    nb = jnp.floor((cnt + (SLOT_BLOCK - 1)) * (1.0 / SLOT_BLOCK))
    i_r = lax.broadcasted_iota(jnp.int32, (LANES, LANES), 0)
    i_c = lax.broadcasted_iota(jnp.int32, (LANES, LANES), 1)
    before = jnp.where(i_r < i_c, 1.0, 0.0)
    upto = jnp.where(i_c <= i_r, 1.0, 0.0)
    first = _dot(nb, before)
    cum = _dot(upto, nb)
    tot = jnp.max(cum, axis=0, keepdims=True)
    steps = jnp.floor((tot + (STEP_BLOCKS - 1)) * (1.0 / STEP_BLOCKS))
    start = STEP_BLOCKS * _dot(jnp.broadcast_to(steps, (LANES, LANES)), before)[0:1]
    pos = start + cum - nb

    sub = 8 * (-(-(MOE_GROUPS + MOE_EXPERTS) // 8))
    first_t, nb_t, pos_t = first.T[:sub], nb.T[:sub], pos.T[:sub]
    as_col = lambda v: jnp.broadcast_to(v, (LANES, LANES)).T[:sub, 0:1]
    start_c, tot_c, span_c = as_col(start), as_col(tot), as_col(STEP_BLOCKS * steps)

    inv_ref[...] = jnp.zeros(inv_ref.shape, jnp.int32)
    local = lax.broadcasted_iota(jnp.int32, (sub, LANES), 1).astype(F32)
    nsrc = src_ref.shape[1]
    j = lax.broadcasted_iota(jnp.int32, (sub, nsrc), 1).astype(F32)
    acc = jnp.zeros((1, nsrc), F32)
    for i in range(ntiles):
        f_i, n_i, p_i = first_t[:, i:i + 1], nb_t[:, i:i + 1], pos_t[:, i:i + 1]
        own = (local >= f_i) & (local < f_i + n_i)
        inv_ref[i:i + 1, :] = jnp.sum(jnp.where(own, p_i + (local - f_i), 0.0), axis=0, keepdims=True).astype(jnp.int32)
        own = (j >= p_i) & (j < p_i + n_i)
        acc = acc + jnp.sum(jnp.where(own, (i * bpt) + f_i + (j - p_i), 0.0), axis=0, keepdims=True)
    src_ref[...] = acc.astype(jnp.int32)

    nst = exp_ref.shape[1]
    at = STEP_BLOCKS * lax.broadcasted_iota(jnp.int32, (sub, nst), 1).astype(F32)
    expert = (lax.broadcasted_iota(jnp.int32, (sub, nst), 0) - MOE_GROUPS).astype(F32)
    inside = (at >= start_c) & (at < start_c + span_c)
    exp_ref[...] = jnp.sum(jnp.where(inside, expert, 0.0), axis=0, keepdims=True).astype(jnp.int32)
    occupied = jnp.where(inside, jnp.where(at - start_c < tot_c, 1.0, 0.0), 0.0)
    valid_ref[...] = jnp.sum(occupied, axis=0, keepdims=True).astype(jnp.int32)


def _expert_tables(cnt, bpt, nsteps):
    ntiles = cnt.shape[0]
    assert bpt <= LANES and ntiles <= LANES
    cnt = jnp.pad(cnt, ((0, LANES - ntiles), (0, 0)))
    nsrc = -(-nsteps * STEP_BLOCKS // LANES) * LANES
    nst = -(-nsteps // LANES) * LANES
    i32 = lambda *s: jax.ShapeDtypeStruct(s, jnp.int32)
    src, inv, step_e, valid = pl.pallas_call(
        functools.partial(_tables_kernel, ntiles=ntiles, bpt=bpt),
        out_shape=[i32(1, nsrc), i32(LANES, LANES), i32(1, nst), i32(1, nst)],
        compiler_params=pltpu.CompilerParams(vmem_limit_bytes=VMEM_LIMIT),
        name="moe_tables",
    )(cnt)
    return src.reshape(-1), inv.reshape(-1), step_e.reshape(-1), valid.reshape(-1)


def _block_gather(table_ref, first, nblocks, src_hbm, buf_ref, slot, sem_ref):
    return [pltpu.make_async_copy(src_hbm.at[table_ref[first + kk]],
                                  buf_ref.at[slot, pl.ds(kk * SLOT_BLOCK, SLOT_BLOCK)], sem_ref.at[slot])
            for kk in range(nblocks)]


def _experts_kernel(src_ref, exp_ref, valid_ref, xs_hbm, wg_ref, wu_ref, wd_ref, y_ref, xbuf_ref, sem_ref, wgb_ref,
                    wub_ref, wdb_ref):
    s = pl.program_id(0)
    slot = s % 2

    def gather(step, to_slot):
        return _block_gather(src_ref, step * STEP_BLOCKS, STEP_BLOCKS, xs_hbm, xbuf_ref, to_slot, sem_ref)

    @pl.when((s == 0) & (valid_ref[0] > 0))
    def _():
        for cp in gather(0, 0):
            cp.start()

    nxt = jnp.minimum(s + 1, pl.num_programs(0) - 1)

    @pl.when((s + 1 < pl.num_programs(0)) & (valid_ref[nxt] > 0))
    def _():
        for cp in gather(s + 1, 1 - slot):
            cp.start()

    @pl.when((s == 0) | (exp_ref[s] != exp_ref[jnp.maximum(s - 1, 0)]))
    def _():
        wgb_ref[...] = wg_ref[0, 0, 0].astype(BF16)
        wub_ref[...] = wu_ref[0, 0, 0].astype(BF16)
        wdb_ref[...] = wd_ref[0, 0, 0].astype(BF16)

    @pl.when(valid_ref[s] > 0)
    def _():
        for cp in gather(s, slot):
            cp.wait()
        x = xbuf_ref[slot]
        a = _silu(_dot(x, wgb_ref[...])) * _dot(x, wub_ref[...])
        y_ref[...] = _dot(a.astype(BF16), wdb_ref[...]).astype(BF16).reshape(y_ref.shape)

    @pl.when(valid_ref[s] == 0)
    def _():
        y_ref[...] = jnp.zeros(y_ref.shape, y_ref.dtype)


def _moe_experts(xs, src, step_e, valid, w_gate, w_up, w_down, layer, nsteps):
    xs3 = xs
    d = xs3.shape[-1]
    f = w_gate.shape[-1]
    step_rows = STEP_BLOCKS * SLOT_BLOCK
    w_blk = lambda shape: pl.BlockSpec((1, 1, 1) + shape,
                                       lambda s, sr, ex, va: (layer, ex[s] // MOE_EPG, ex[s] % MOE_EPG, 0, 0))
    grid_spec = pltpu.PrefetchScalarGridSpec(
        num_scalar_prefetch=3,
        grid=(nsteps,),
        in_specs=[pl.BlockSpec(memory_space=pl.ANY), w_blk((d, f)), w_blk((d, f)), w_blk((f, d))],
        out_specs=pl.BlockSpec((STEP_BLOCKS, SLOT_BLOCK, d), lambda s, sr, ex, va: (s, 0, 0)),
        scratch_shapes=[pltpu.VMEM((2, step_rows, d), BF16), pltpu.SemaphoreType.DMA((2,)),
                        pltpu.VMEM((d, f), BF16), pltpu.VMEM((d, f), BF16), pltpu.VMEM((f, d), BF16)],
    )
    return pl.pallas_call(
        _experts_kernel,
        grid_spec=grid_spec,
        out_shape=jax.ShapeDtypeStruct((nsteps * STEP_BLOCKS, SLOT_BLOCK, d), BF16),
        compiler_params=_cparams("arbitrary"),
        name="moe_experts",
    )(src, step_e, valid, xs3, w_gate, w_up, w_down)


def _combine_kernel(inv_ref, x_ref, xt_ref, info_ref, m_ref, fg_ref, ys_hbm, o_ref, ybuf_ref, sem_ref, *, bpt, final):
    i = pl.program_id(0)
    slot = i % 2
    tr = x_ref.shape[0]

    def gather(tile, to_slot):
        return _block_gather(inv_ref, tile * LANES, bpt, ys_hbm, ybuf_ref, to_slot, sem_ref)

    @pl.when(i == 0)
    def _():
        for cp in gather(0, 0):
            cp.start()

    @pl.when(i + 1 < pl.num_programs(0))
    def _():
        for cp in gather(i + 1, 1 - slot):
            cp.start()

    info = info_ref[...]
    col = lax.broadcasted_iota(jnp.int32, (tr, bpt * SLOT_BLOCK), 1)
    wsel = jnp.where(col == info[:, 0:1].astype(jnp.int32), info[:, 2:3],
                     jnp.where(col == info[:, 1:2].astype(jnp.int32), info[:, 3:4], 0.0))
    for cp in gather(i, slot):
        cp.wait()
    y = _dot(wsel.astype(BF16), ybuf_ref[slot])
    out = _tile_tokens(x_ref, xt_ref) + m_ref[0, 5:6, :] * y
    if final:
        out = _rmsnorm(out, fg_ref[...])
    o_ref[...] = out


def _moe_combine(x2d, x_tail, ys, inv, info, mods, tiles_per_mod, final_g, tr, bpt, final):
    d = x2d.shape[1]
    nt, x_specs, x_args = _token_specs(x2d, x_tail, tr, lambda f: (lambda i, iv: f(i)))
    t = nt * tr
    ys3 = ys
    kern = _combine_kernel
    if x_tail is None:
        kern = lambda inv_ref, x_ref, *refs, **kw: _combine_kernel(inv_ref, x_ref, None, *refs, **kw)
    grid_spec = pltpu.PrefetchScalarGridSpec(
        num_scalar_prefetch=1,
        grid=(nt,),
        in_specs=x_specs + [
            pl.BlockSpec((tr, LANES), lambda i, iv: (i, 0)),
            pl.BlockSpec((1, 6, d), lambda i, iv: (jnp.minimum(i // tiles_per_mod, mods.shape[0] - 1), 0, 0)),
            pl.BlockSpec((1, d), lambda i, iv: (0, 0)),
            pl.BlockSpec(memory_space=pl.ANY),
        ],
        out_specs=pl.BlockSpec((tr, d), lambda i, iv: (i, 0)),
        scratch_shapes=[pltpu.VMEM((2, bpt * SLOT_BLOCK, d), BF16), pltpu.SemaphoreType.DMA((2,))],
    )
    return pl.pallas_call(
        functools.partial(kern, bpt=bpt, final=final),
        grid_spec=grid_spec,
        out_shape=jax.ShapeDtypeStruct((t, d), F32),
        compiler_params=_cparams("arbitrary"),
        name="moe_combine",
    )(inv, *x_args, info, mods, final_g.reshape(1, d), ys3)


def _moe(x2d, x_tail, gain, mods, tiles_per_mod, params, final_g, final, tr, mix=None):
    w_r, b_r, w_gate, w_up, w_down, layer = params
    if mix is None:
        xs, info, cnt = _moe_route(x2d, x_tail, gain, mods, tiles_per_mod, w_r, b_r, tr)
    else:
        xs, info, cnt, x2d = _moe_route(x2d, x_tail, gain, mods, tiles_per_mod, w_r, b_r, tr, mix)
    nt = x2d.shape[0] // tr + (x_tail is not None)
    bpt = _slot_rows(tr) // SLOT_BLOCK
    nsteps = -(-(nt * bpt + MOE_EXPERTS * (STEP_BLOCKS - 1)) // STEP_BLOCKS)
    src, inv, step_e, valid = _expert_tables(cnt[:, 0, :], bpt, nsteps)
    ys = _moe_experts(xs, src, step_e, valid, w_gate, w_up, w_down, layer, nsteps)
    return _moe_combine(x2d, x_tail, ys, inv, info, mods, tiles_per_mod, final_g, tr, bpt, final)


def _mla_proj_kernel(cq_ref, ckv_ref, kr_ref, krp_ref, cos_ref, sin_ref, qg_ref, kg_ref, wq_ref, wqp_ref, wk_ref,
                     wv_ref, vone_ref, *o_refs, need_q, q_scale):
    cos = cos_ref[...]
    sin = sin_ref[...]
    ckv = _rmsnorm(ckv_ref[0], kg_ref[...]).astype(BF16)
    k_rope = kr_ref[0] * cos + krp_ref[0] * sin
    kn = _dot(ckv, wk_ref[...])
    if need_q:
        q_ref, k_ref, v_ref = o_refs
    else:
        k_ref, v_ref = o_refs
    vx = _dot(ckv, wv_ref[...]) + vone_ref[...]
    for h in range(D_HEADS):
        hs = slice(h * MLA_PAD, (h + 1) * MLA_PAD)
        k_ref[0, h] = (kn[:, hs] + k_rope).astype(BF16)
        v_ref[0, h] = vx[:, hs].T[:MLA_VROWS].astype(BF16)
    if need_q:
        cq = _rmsnorm(cq_ref[0], qg_ref[...]).astype(BF16)
        qm = _dot(cq, wq_ref[...])
        qp = _dot(cq, wqp_ref[...])
        for h in range(D_HEADS):
            hs = slice(h * MLA_PAD, (h + 1) * MLA_PAD)
            q_ref[0, h] = ((qm[:, hs] * cos + qp[:, hs] * sin) * q_scale).T.astype(BF16)


def _mla_proj(u_b, cos, sin, q_g, kv_g, wq, wqp, wk, wv, need_q, tm):
    b, n, _ = u_b.shape
    tm = min(tm, n)
    row_major = (jax.ShapeDtypeStruct((b, D_HEADS, n, MLA_PAD), BF16),
                 pl.BlockSpec((1, D_HEADS, tm, MLA_PAD), lambda bi, i: (bi, 0, i, 0)))
    col_major = (jax.ShapeDtypeStruct((b, D_HEADS, MLA_PAD, n), BF16),
                 pl.BlockSpec((1, D_HEADS, MLA_PAD, tm), lambda bi, i: (bi, 0, 0, i)))
    v_major = (jax.ShapeDtypeStruct((b, D_HEADS, MLA_VROWS, n), BF16),
               pl.BlockSpec((1, D_HEADS, MLA_VROWS, tm), lambda bi, i: (bi, 0, 0, i)))
    outs, specs = zip(*(([col_major] if need_q else []) + [row_major, v_major]))
    full = lambda a: pl.BlockSpec(a.shape, lambda bi, i: (0,) * a.ndim)
    vone = jnp.tile(jnp.concatenate([jnp.zeros((1, MLA_V), F32), jnp.ones((1, MLA_PAD - MLA_V), F32)], axis=1),
                    (1, D_HEADS))
    q_scale = float((MLA_NOPE + MLA_ROPE) ** -0.5 * np.log2(np.e))
    return pl.pallas_call(
        functools.partial(_mla_proj_kernel, need_q=need_q, q_scale=q_scale),
        grid=(b, n // tm),
        in_specs=[
            pl.BlockSpec((1, tm, MLA_Q_RANK), lambda bi, i: (bi, i, 0)),
            pl.BlockSpec((1, tm, MLA_KV_RANK), lambda bi, i: (bi, i, 2)),
            pl.BlockSpec((1, tm, MLA_PAD), lambda bi, i: (bi, i, 3)),
            pl.BlockSpec((1, tm, MLA_PAD), lambda bi, i: (bi, i, 4)),
            pl.BlockSpec((tm, MLA_PAD), lambda bi, i: (i, 0)),
            pl.BlockSpec((tm, MLA_PAD), lambda bi, i: (i, 0)),
            full(q_g), full(kv_g), full(wq), full(wqp), full(wk), full(wv), full(vone),
        ],
        out_specs=list(specs),
        out_shape=list(outs),
        compiler_params=_cparams("parallel", "parallel"),
        name="mla_proj",
    )(u_b, u_b, u_b, u_b, cos, sin, q_g, kv_g, wq, wqp, wk, wv, vone)


def _mla_attn_kernel(q_ref, qn_ref, kc_ref, kl_ref, vc_ref, vl_ref, o_ref, acc0_ref, acc1_ref, s0_ref, s1_ref, m_ref,
                     *, tk):
    tq = q_ref.shape[3]
    ncc = kc_ref.shape[2] // tk
    nchunks = ncc + kl_ref.shape[2] // tk
    neg = jnp.full((8, tq), NEG, F32)
    s_refs = (s0_ref, s1_ref)
    acc_refs = (acc0_ref, acc1_ref)

    def chunk(c):
        part = (kc_ref, vc_ref, c) if c < ncc else (kl_ref, vl_ref, c - ncc)
        return part[0], part[1], pl.ds(part[2] * tk, tk), pl.ds(c * tk, tk)

    def scores(k_ref, ks, ss, hh, q, m):
        s = _dot(k_ref[0, hh, ks, :], q)
        s_refs[hh][ss, :] = s
        return jnp.maximum(m, jnp.max(s.reshape(tk // 8, 8, tq), axis=0))

    def weight(v_ref, ks, ss, hh, m_row):
        p = jnp.exp2((s_refs[hh][ss, :] - m_row).astype(BF16))
        acc_refs[hh][...] += _dot(v_ref[0, hh, :, ks], p)

    @pl.when(pl.program_id(2) == 0)
    def _():
        m = neg
        for c in range(ncc):
            k_ref, _, ks, ss = chunk(c)
            m = scores(k_ref, ks, ss, 0, q_ref[0, 0], m)

        def latent(c, m):
            ks = pl.ds(pl.multiple_of(c * tk, tk), tk)
            ss = pl.ds(pl.multiple_of((c + ncc) * tk, tk), tk)
            return scores(kl_ref, ks, ss, 0, q_ref[0, 0], m)

        m_ref[...] = lax.fori_loop(0, nchunks - ncc, latent, m)

    acc0_ref[...] = jnp.zeros(acc0_ref.shape, F32)
    acc1_ref[...] = jnp.zeros(acc1_ref.shape, F32)
    m0 = jnp.max(m_ref[...], axis=0, keepdims=True)
    m1 = neg
    for c in range(nchunks):
        k_ref, v_ref, ks, ss = chunk(c)
        weight(v_ref, ks, ss, 0, m0)
        m1 = scores(k_ref, ks, ss, 1, q_ref[0, 1], m1)
    m1 = jnp.max(m1, axis=0, keepdims=True)
    m0_next = neg
    for c in range(nchunks):
        k_ref, v_ref, ks, ss = chunk(c)
        weight(v_ref, ks, ss, 1, m1)
        m0_next = scores(k_ref, ks, ss, 0, qn_ref[0, 0], m0_next)
    m_ref[...] = m0_next
    o_t = jnp.concatenate([a[:MLA_V] / a[MLA_V:MLA_V + 1] for a in acc_refs], axis=0)
    o_ref[0] = o_t.T.astype(o_ref.dtype)


def _mla_attention(q_t, k_ctx, k_lat, v_ctx, v_lat, tq, tk):
    b, h, _, n = q_t.shape
    nc = k_ctx.shape[2]
    tq = min(tq, n)
    assert nc % tk == 0 and n % tk == 0
    k_spec = lambda rows: pl.BlockSpec((1, 2, rows, MLA_PAD), lambda bi, hp, i: (bi, hp, 0, 0))
    v_spec = lambda rows: pl.BlockSpec((1, 2, MLA_VROWS, rows), lambda bi, hp, i: (bi, hp, 0, 0))
    return pl.pallas_call(
        functools.partial(_mla_attn_kernel, tk=tk),
        grid=(b, h // 2, n // tq),
        in_specs=[
            pl.BlockSpec((1, 2, MLA_PAD, tq), lambda bi, hp, i: (bi, hp, 0, i)),
            pl.BlockSpec((1, 2, MLA_PAD, tq), lambda bi, hp, i: (bi, hp, 0, jnp.minimum(i + 1, n // tq - 1))),
            k_spec(nc), k_spec(n), v_spec(nc), v_spec(n),
        ],
        out_specs=pl.BlockSpec((1, tq, 2 * MLA_V), lambda bi, hp, i: (bi, i, hp)),
        out_shape=jax.ShapeDtypeStruct((b, n, h * MLA_V), BF16),
        scratch_shapes=[pltpu.VMEM((MLA_VROWS, tq), F32), pltpu.VMEM((MLA_VROWS, tq), F32),
                        pltpu.VMEM((nc + n, tq), F32), pltpu.VMEM((nc + n, tq), F32), pltpu.VMEM((8, tq), F32)],
        compiler_params=_cparams("parallel", "parallel", "arbitrary"),
        name="mla_attention",
    )(q_t, q_t, k_ctx, k_lat, v_ctx, v_lat)


def _na_kernel(q_ref, *refs):
    k_refs, v_refs = refs[:NA_KBLOCKS + 1], refs[NA_KBLOCKS + 1:2 * NA_KBLOCKS + 2]
    tab_ref, o_ref = refs[2 * NA_KBLOCKS + 2:]
    tq = q_ref.shape[1]
    nloc = tab_ref.shape[3]
    lane = lax.broadcasted_iota(jnp.int32, (tq, LANES), 1)
    q = q_ref[0]
    k_all = jnp.concatenate([r[0] for r in k_refs], axis=0)
    v_all = jnp.concatenate([r[0] for r in v_refs], axis=0)
    outs = []
    for hh in range(2):
        in_head = (lane >= hh * C_HEAD_DIM) & (lane < (hh + 1) * C_HEAD_DIM)
        qh = jnp.where(in_head, q, jnp.zeros_like(q))
        s = lax.dot_general(qh, k_all, NT, preferred_element_type=F32)
        s_loc = s[:, :nloc] + tab_ref[0, hh]
        s_ctx = s[:, nloc:]
        m = jnp.maximum(jnp.max(s_loc, axis=-1, keepdims=True), jnp.max(s_ctx, axis=-1, keepdims=True))
        p_loc = jnp.exp(s_loc - m)
        p_ctx = jnp.exp(s_ctx - m)
        l = jnp.sum(p_loc, axis=-1, keepdims=True) + jnp.sum(p_ctx, axis=-1, keepdims=True)
        o = _dot(p_loc.astype(BF16), v_all[:nloc]) + _dot(p_ctx.astype(BF16), v_all[nloc:])
        outs.append(o / l)
    o_ref[0] = jnp.where(lane < C_HEAD_DIM, outs[0], outs[1]).astype(o_ref.dtype)


def _na_tables(rpb, rows):
    h = rpb.shape[0]
    w = GRID_W
    kr_n = NA_QROWS + NA_ROWS
    qc = np.arange(w)
    kc = np.arange(w)
    cs = np.clip(qc - NA_COLS // 2, 0, w - NA_COLS)
    col_ok = (kc[None, :] >= cs[:, None]) & (kc[None, :] < cs[:, None] + NA_COLS)
    dc = np.clip(kc[None, :] - qc[:, None] + (NA_COLS - 1), 0, 2 * NA_COLS - 2)
    pick_dc = (dc.reshape(-1)[None, :] == np.arange(2 * NA_COLS - 1)[:, None]).astype(np.float32)
    base = jnp.einsum('hrd,dx->hrx', rpb.astype(F32), jnp.asarray(pick_dc), precision=lax.Precision.HIGHEST)
    nblk = rows // NA_QROWS
    tabs = []
    for m in (0, 1, nblk - 1):
        qr = NA_QROWS * m + np.arange(NA_QROWS)
        rs = np.clip(qr - NA_ROWS // 2, 0, rows - NA_ROWS)
        kr = NA_QROWS * m - NA_ROWS // 2 + np.arange(kr_n)
        row_ok = (kr[None, :] >= rs[:, None]) & (kr[None, :] < rs[:, None] + NA_ROWS)
        dr = np.clip(kr[None, :] - qr[:, None] + (NA_ROWS - 1), 0, 2 * NA_ROWS - 2)
        pick_dr = (dr.reshape(-1)[:, None] == np.arange(2 * NA_ROWS - 1)[None, :]).astype(np.float32)
        t = jnp.einsum('vr,hrx->hvx', jnp.asarray(pick_dr), base, precision=lax.Precision.HIGHEST)
        t = t.reshape(h, NA_QROWS, kr_n, w, w).transpose(0, 1, 3, 2, 4)
        ok = row_ok[:, None, :, None] & col_ok[None, :, None, :]
        tabs.append(jnp.where(jnp.asarray(ok)[None], t, NEG).reshape(h, NA_QROWS * w, kr_n * w))
    return jnp.stack(tabs)


def _na_attention(u_lat, u_ctx, tabs):
    b, n, _ = u_lat.shape
    nc = u_ctx.shape[1]
    tq = NA_QROWS * GRID_W
    tkb = NA_KBLOCK_ROWS * GRID_W
    nblk = n // tq
    nkb = n // tkb
    per_q = NA_QROWS // NA_KBLOCK_ROWS
    assert NA_KBLOCKS == per_q + 2 and NA_ROWS // 2 == NA_KBLOCK_ROWS and n % tq == 0 and n // tq >= 2
    pairs = C_HEADS // 2
    q_spec = pl.BlockSpec((1, tq, LANES), lambda bi, hp, i: (bi, i, hp))
    kblk = lambda col0, j: pl.BlockSpec(
        (1, tkb, LANES), lambda bi, hp, i: (bi, jnp.clip(i * per_q - 1 + j, 0, nkb - 1), col0 + hp))
    ctx = lambda col0: pl.BlockSpec((1, nc, LANES), lambda bi, hp, i: (bi, 0, col0 + hp))
    sel = lambda i: jnp.where(i == 0, 0, jnp.where(i == nblk - 1, 2, 1))
    kv_specs = [kblk(col0, j) for col0 in (pairs, 2 * pairs) for j in range(NA_KBLOCKS)]
    kv_specs = kv_specs[:NA_KBLOCKS] + [ctx(pairs)] + kv_specs[NA_KBLOCKS:] + [ctx(2 * pairs)]
    kv_args = [u_lat] * NA_KBLOCKS + [u_ctx]
    return pl.pallas_call(
        _na_kernel,
        grid=(b, pairs, nblk),
        in_specs=[q_spec] + kv_specs + [
            pl.BlockSpec((1, 2, tq, NA_KBLOCKS * tkb), lambda bi, hp, i: (sel(i), hp, 0, 0)),
        ],
        out_specs=pl.BlockSpec((1, tq, LANES), lambda bi, hp, i: (bi, i, hp)),
        out_shape=jax.ShapeDtypeStruct((b, n, C_WIDTH), BF16),
        compiler_params=_cparams("parallel", "parallel", "arbitrary"),
        name="na_attention",
    )(u_lat, *kv_args, *kv_args, tabs)


def _moe_params(w_rg, b_rg, w_re, b_re, w_gate, w_up, w_down, layer):
    d = w_rg.shape[0]
    w_r = jnp.zeros((d, LANES), F32).at[:, :MOE_GROUPS].set(w_rg).at[:, MOE_GROUPS:MOE_GROUPS + MOE_EXPERTS].set(w_re)
    b_r = jnp.zeros((1, LANES), F32).at[0, :MOE_GROUPS].set(b_rg).at[0, MOE_GROUPS:MOE_GROUPS + MOE_EXPERTS].set(b_re)
    w_hi = w_r.astype(BF16)
    w_lo = (w_r - w_hi.astype(F32)).astype(BF16)
    return jnp.concatenate([w_hi, w_lo], axis=1), b_r, w_gate, w_up, w_down, layer


def _rope_perm():
    j = np.arange(MLA_ROPE)
    half = MLA_ROPE // 2
    return (j // half) * half + (j % half + half // 2) % half


def _rope_tables(n):
    half = MLA_ROPE // 2
    nf = half // 2
    t = np.arange(n)
    inv = (np.float32(ROPE_THETA) ** (-np.arange(nf, dtype=np.float32) / np.float32(nf))).astype(np.float32)
    parts_c, parts_s = [], []
    for pos in ((t // GRID_W).astype(np.float32), (t % GRID_W).astype(np.float32)):
        ang = (pos[:, None] * inv[None, :]).astype(np.float32)
        c, s = np.cos(ang).astype(np.float32), np.sin(ang).astype(np.float32)
        parts_c += [c, c]
        parts_s += [-s, s]
    pad = MLA_PAD - MLA_NOPE - MLA_ROPE
    cos = np.concatenate([np.ones((n, MLA_NOPE), np.float32)] + parts_c + [np.zeros((n, pad), np.float32)], axis=1)
    sin = np.concatenate([np.zeros((n, MLA_NOPE), np.float32)] + parts_s + [np.zeros((n, pad), np.float32)], axis=1)
    return jnp.asarray(cos), jnp.asarray(sin)


def _identity_rope_tables(n):
    pad = MLA_PAD - MLA_NOPE - MLA_ROPE
    cos = jnp.concatenate([jnp.ones((n, MLA_NOPE + MLA_ROPE), F32), jnp.zeros((n, pad), F32)], axis=1)
    return cos, jnp.zeros((n, MLA_PAD), F32)


def _pad_heads(w, widths, src_cols, dst_off):
    rank = w.shape[0]
    out = jnp.zeros((rank, D_HEADS, MLA_PAD), F32)
    wh = w.reshape(rank, D_HEADS, widths)[:, :, src_cols]
    return out.at[:, :, dst_off:dst_off + len(src_cols)].set(wh).reshape(rank, D_HEADS * MLA_PAD)


def _cd_params(w_in, w_uq, w_ukv):
    d = w_in.shape[0]
    perm = _rope_perm()
    o = 3 * C_WIDTH
    q_scale = float(C_HEAD_DIM ** -0.5)
    kr = w_in[:, o + MLA_Q_RANK + MLA_KV_RANK:]
    pad_rope = lambda a: jnp.zeros((d, MLA_PAD), F32).at[:, MLA_NOPE:MLA_NOPE + MLA_ROPE].set(a)
    w_cat = jnp.concatenate([
        w_in[:, :C_WIDTH] * q_scale, w_in[:, C_WIDTH:o],
        w_in[:, o:o + MLA_Q_RANK + MLA_KV_RANK], pad_rope(kr), pad_rope(kr[:, perm]),
    ], axis=1).astype(BF16)
    qw = MLA_NOPE + MLA_ROPE
    nope = np.arange(MLA_NOPE)
    rope = MLA_NOPE + np.arange(MLA_ROPE)
    wq = (_pad_heads(w_uq, qw, nope, 0) + _pad_heads(w_uq, qw, rope, MLA_NOPE)).astype(BF16)
    wqp = _pad_heads(w_uq, qw, rope[perm], MLA_NOPE).astype(BF16)
    kvw = MLA_NOPE + MLA_V
    wk = _pad_heads(w_ukv, kvw, nope, 0).astype(BF16)
    wv = _pad_heads(w_ukv, kvw, MLA_NOPE + np.arange(MLA_V), 0).astype(BF16)
    return w_cat, wq, wqp, wk, wv


def kernel(x, c, ctx, c_ctx, ada_w, ada_b, norm1_g, norm2_g, ab_w_in, ab_w_out, hgrn_lb_logits, hgrn_onorm_g, pool_w,
           pool_scale, cd_w_in, cd_w_out, na_rpb, mla_q_norm_g, mla_w_uq, mla_kv_norm_g, mla_w_ukv, moe_w_rg, moe_b_rg,
           moe_w_re, moe_b_re, moe_w_gate, moe_w_up, moe_w_down, final_norm_g):
    b, n, d = x.shape
    n_ctx = ctx.shape[1]
    assert ada_w.shape[0] == 2 and ab_w_in.shape[0] == 1 and cd_w_in.shape[0] == 1 and b + 1 <= 8
    tm = TOKEN_TILE

    cc = jnp.zeros((8, d), F32).at[:b].set(c).at[b].set(c_ctx)
    mods = _ada(cc, b + 1, ada_w, ada_b).reshape(2, 8, 6, d)
    mods_lat = [mods[l, :b] for l in range(2)]
    mods_ctx = [jnp.broadcast_to(mods[l, b:b + 1], (b, 6, d)) for l in range(2)]
    lb = jnp.cumsum(jax.nn.softmax(hgrn_lb_logits.astype(F32), axis=1), axis=1)[:, 0]

    w_in0 = ab_w_in[0].astype(BF16)
    w_out0 = ab_w_out[0].astype(BF16)
    pw0 = pool_w[0].astype(BF16)
    ab_cols = w_in0.shape[1]
    (u_ctx,) = _in_proj(ctx, norm1_g[0], mods_ctx[0], w_in0, ((0, ab_cols),), (F32,), tm)
    (u_lat,) = _in_proj(x, norm1_g[0], mods_lat[0], w_in0, ((0, ab_cols),), (F32,), tm)
    s0 = jnp.zeros((b, 2, A_HEADS, A_HEAD_DIM, A_HEAD_DIM), F32)
    ocf, ocb, s_ctx = _hgrn_scan(u_ctx, lb, s0, HGRN_ROWS)
    olf, olb, _ = _hgrn_scan(u_lat, lb, s_ctx, HGRN_ROWS)
    t_lat, t_ctx = b * n, b * n_ctx
    assert n % tm == 0 and t_ctx == tm
    x_lat = _ab_out(olf, olb, u_lat, x, mods_lat[0], hgrn_onorm_g[0], pw0, pool_scale[0], w_out0, tm)
    x_ctx = _ab_out(ocf, ocb, u_ctx, ctx, mods_ctx[0], hgrn_onorm_g[0], pw0, pool_scale[0], w_out0, tm)
    moe0 = _moe_params(moe_w_rg[0], moe_b_rg[0], moe_w_re[0], moe_b_re[0], moe_w_gate, moe_w_up, moe_w_down, 0)
    mods_all = jnp.concatenate([mods_lat[0], mods[0, b:b + 1]], axis=0)
    xa = _moe(x_lat, x_ctx, norm2_g[0], mods_all, n // tm, moe0, final_norm_g, False, tm)

    w_cat, wq, wqp, wk, wv = _cd_params(cd_w_in[0], mla_w_uq[0], mla_w_ukv[0])
    na_w = 3 * C_WIDTH
    splits = ((0, na_w), (na_w, w_cat.shape[1]))
    ua_ctx, ub_ctx = _in_proj(xa, norm1_g[1], mods_ctx[1], w_cat, splits, (BF16, F32), tm, (b, n_ctx, t_lat))
    ua_lat, ub_lat = _in_proj(xa, norm1_g[1], mods_lat[1], w_cat, splits, (BF16, F32), tm, (b, n, 0))
    q_g = mla_q_norm_g[0].reshape(1, -1)
    kv_g = mla_kv_norm_g[0].reshape(1, -1)
    cos_l, sin_l = _rope_tables(n)
    cos_c, sin_c = _identity_rope_tables(n_ctx)
    k_c, v_c = _mla_proj(ub_ctx, cos_c, sin_c, q_g, kv_g, wq, wqp, wk, wv, False, tm)
    q_l, k_l, v_l = _mla_proj(ub_lat, cos_l, sin_l, q_g, kv_g, wq, wqp, wk, wv, True, tm)
    d_lat = _mla_attention(q_l, k_c, k_l, v_c, v_l, MLA_Q_TILE, MLA_K_CHUNK)
    c_lat = _na_attention(ua_lat, ua_ctx, _na_tables(na_rpb[0], n // GRID_W))
    moe1 = _moe_params(moe_w_rg[1], moe_b_rg[1], moe_w_re[1], moe_b_re[1], moe_w_gate, moe_w_up, moe_w_down, 1)
    mix = (c_lat.reshape(t_lat, -1), d_lat.reshape(t_lat, -1), cd_w_out[0].astype(BF16))
    out = _moe(xa, None, norm2_g[1], mods_lat[1], n // tm, moe1, final_norm_g, True, tm, mix)
    return out.reshape(b, n, d)
```

```python
import functools

import numpy as np
import jax
import jax.numpy as jnp
from jax import lax
from jax.experimental import pallas as pl
from jax.experimental.pallas import tpu as pltpu

F32 = jnp.float32
BF16 = jnp.bfloat16

EPS = 1e-6
NEG = -1e30

GRID_W = 64
A_HEADS = 4
A_HEAD_DIM = 128
A_WIDTH = A_HEADS * A_HEAD_DIM
POOL_WINDOWS = (2, 4, 8, 16)
B_GROUP = 128
B_WIDTH = B_GROUP * len(POOL_WINDOWS)
POOL_HALO = 16
C_HEADS = 8
C_HEAD_DIM = 64
C_WIDTH = C_HEADS * C_HEAD_DIM
NA_ROWS = 8
NA_COLS = 16
NA_QROWS = 4
NA_KBLOCK_ROWS = 4
NA_KBLOCKS = 3
D_HEADS = 8
MLA_Q_RANK = 256
MLA_KV_RANK = 128
MLA_NOPE = 64
MLA_ROPE = 32
MLA_V = 64
MLA_PAD = 128
MLA_VROWS = 80
ROPE_THETA = 10000.0
MOE_GROUPS = 4
MOE_EPG = 8
MOE_EXPERTS = MOE_GROUPS * MOE_EPG
MOE_HIDDEN = 256
LANES = 128
SLOT_BLOCK = 16
STEP_BLOCKS = 32
VMEM_LIMIT = 56 * 1024 * 1024
TOKEN_TILE = 512
HGRN_ROWS = 256
MLA_Q_TILE = 256
MLA_K_CHUNK = 256

NT = (((1,), (1,)), ((), ()))
TN = (((0,), (0,)), ((), ()))


def _cparams(*sem):
    return pltpu.CompilerParams(dimension_semantics=sem, vmem_limit_bytes=VMEM_LIMIT)


def _sigmoid(x):
    return 1.0 / (1.0 + jnp.exp(-x))


def _silu(x):
    return x * _sigmoid(x)


def _dot(a, b):
    return jnp.dot(a, b, preferred_element_type=F32)


def _rmsnorm(x, g):
    return x * lax.rsqrt(jnp.mean(x * x, axis=-1, keepdims=True) + EPS) * g


def _ada_kernel(ct_ref, w_ref, b_ref, o_ref, *, nrows):
    s = _silu(ct_ref[...])
    w = w_ref[0]
    rows = [jnp.sum(s[:, r:r + 1] * w, axis=0, keepdims=True) + b_ref[0] for r in range(nrows)]
    rows.append(jnp.zeros((o_ref.shape[1] - nrows, w.shape[1]), F32))
    o_ref[0] = jnp.concatenate(rows, axis=0)


def _ada(cc, nrows, ada_w, ada_b):
    depth, d, n6 = ada_w.shape
    tn = n6 // 8
    return pl.pallas_call(
        functools.partial(_ada_kernel, nrows=nrows),
        grid=(depth, n6 // tn),
        in_specs=[
            pl.BlockSpec((d, 8), lambda l, j: (0, 0)),
            pl.BlockSpec((1, d, tn), lambda l, j: (l, 0, j)),
            pl.BlockSpec((1, 1, tn), lambda l, j: (l, 0, j)),
        ],
        out_specs=pl.BlockSpec((1, 8, tn), lambda l, j: (l, 0, j)),
        out_shape=jax.ShapeDtypeStruct((depth, 8, n6), F32),
        compiler_params=_cparams("parallel", "parallel"),
        name="ada_mod",
    )(cc.T, ada_w, ada_b.reshape(depth, 1, n6))


def _in_kernel(x_ref, g_ref, m_ref, w_ref, *o_refs, splits):
    h = _rmsnorm(x_ref[0], g_ref[...]) * (1.0 + m_ref[0, 1:2, :]) + m_ref[0, 0:1, :]
    hb = h.astype(BF16)
    for o_ref, (a, b) in zip(o_refs, splits):
        o_ref[0] = _dot(hb, w_ref[:, a:b]).astype(o_ref.dtype)


def _in_proj(x, gain, mods, w, splits, dtypes, tm):
    b, n, d = x.shape
    tm = min(tm, n)
    outs = [jax.ShapeDtypeStruct((b, n, hi - lo), dt) for (lo, hi), dt in zip(splits, dtypes)]
    return pl.pallas_call(
        functools.partial(_in_kernel, splits=splits),
        grid=(b, n // tm),
        in_specs=[
            pl.BlockSpec((1, tm, d), lambda bi, i: (bi, i, 0)),
            pl.BlockSpec((1, d), lambda bi, i: (0, 0)),
            pl.BlockSpec((1, 6, d), lambda bi, i: (bi, 0, 0)),
            pl.BlockSpec(w.shape, lambda bi, i: (0, 0)),
        ],
        out_specs=[pl.BlockSpec((1, tm, hi - lo), lambda bi, i: (bi, i, 0)) for lo, hi in splits],
        out_shape=outs,
        compiler_params=_cparams("parallel", "parallel"),
        name="in_proj",
    )(x, gain.reshape(1, d), mods, w)


HG_SUB = 64


def _hgrn_direction(q_raw, fz, v, lb, st_ref, d, o_ref, reverse):
    rows = q_raw.shape[0]
    c = HG_SUB
    f = lb + (1.0 - lb) * _sigmoid(fz)
    k = 1.0 - f
    g = jnp.log(f)
    q = _silu(q_raw)
    r_i = lax.broadcasted_iota(jnp.int32, (c, c), 0)
    c_i = lax.broadcasted_iota(jnp.int32, (c, c), 1)
    keep = (c_i >= r_i) if reverse else (c_i <= r_i)
    tri = jnp.where(keep, 1.0, 0.0).astype(BF16)
    order = range(rows // c - 1, -1, -1) if reverse else range(rows // c)
    for ci in order:
        sl = slice(ci * c, (ci + 1) * c)
        gc = g[sl]
        g_hi = gc.astype(BF16)
        g_lo = (gc - g_hi.astype(F32)).astype(BF16)
        bc = _dot(tri, g_hi) + _dot(tri, g_lo)
        ref = bc[c // 2:c // 2 + 1]
        tot = bc[0:1] if reverse else bc[c - 1:c]
        qt = q[sl] * jnp.exp(bc - ref)
        kt = k[sl] * jnp.exp(ref - bc)
        qd = (qt * jnp.exp(ref)).astype(BF16)
        kd = (kt * jnp.exp(tot - ref)).astype(BF16)
        qt = qt.astype(BF16)
        kt = kt.astype(BF16)
        vb = v[sl].astype(BF16)
        dec = jnp.exp(tot)
        for h in range(A_HEADS):
            hs = slice(h * A_HEAD_DIM, (h + 1) * A_HEAD_DIM)
            att = lax.dot_general(qt[:, hs], kt[:, hs], NT, preferred_element_type=F32)
            att = jnp.where(keep, att, 0.0).astype(BF16)
            st = st_ref[d, h]
            o = _dot(att, vb[:, hs]) + lax.dot_general(qd[:, hs], st.astype(BF16), NT, preferred_element_type=F32)
            o_ref[0, sl, hs] = o
            st_ref[d, h] = st * dec[:, hs] + lax.dot_general(vb[:, hs], kd[:, hs], TN, preferred_element_type=F32)


def _hgrn_kernel(qf_ref, ff_ref, vf_ref, qb_ref, fb_ref, vb_ref, lb_ref, s0_ref, of_ref, ob_ref, sfin_ref, st_ref):
    j = pl.program_id(1)

    @pl.when(j == 0)
    def _():
        st_ref[...] = s0_ref[0]

    _hgrn_direction(qf_ref[0], ff_ref[0], vf_ref[0], lb_ref[0:1], st_ref, 0, of_ref, False)
    _hgrn_direction(qb_ref[0], fb_ref[0], vb_ref[0], lb_ref[1:2], st_ref, 1, ob_ref, True)

    @pl.when(j == pl.num_programs(1) - 1)
    def _():
        sfin_ref[0] = st_ref[...]


def _hgrn_scan(u, lb, s0, rows):
    b, n, _ = u.shape
    rows = min(rows, n)
    nb = n // rows
    w = A_WIDTH

    def fwd(col):
        return pl.BlockSpec((1, rows, w), lambda bi, j: (bi, j, col))

    def bwd(col):
        return pl.BlockSpec((1, rows, w), lambda bi, j: (bi, nb - 1 - j, col))

    st_spec = pl.BlockSpec((1, 2, A_HEADS, A_HEAD_DIM, A_HEAD_DIM), lambda bi, j: (bi, 0, 0, 0, 0))
    return pl.pallas_call(
        _hgrn_kernel,
        grid=(b, nb),
        in_specs=[fwd(0), fwd(1), fwd(3), bwd(0), bwd(2), bwd(3), pl.BlockSpec((2, w), lambda bi, j: (0, 0)), st_spec],
        out_specs=[
            pl.BlockSpec((1, rows, w), lambda bi, j: (bi, j, 0)),
            pl.BlockSpec((1, rows, w), lambda bi, j: (bi, nb - 1 - j, 0)),
            st_spec,
        ],
        out_shape=[
            jax.ShapeDtypeStruct((b, n, w), F32),
            jax.ShapeDtypeStruct((b, n, w), F32),
            jax.ShapeDtypeStruct(s0.shape, F32),
        ],
        scratch_shapes=[pltpu.VMEM((2, A_HEADS, A_HEAD_DIM, A_HEAD_DIM), F32)],
        compiler_params=_cparams("parallel", "arbitrary"),
        name="hgrn_scan",
    )(u, u, u, u, u, u, lb, s0)


def _ab_out_kernel(of_ref, ob_ref, ug_ref, up_ref, pprev_ref, pnext_ref, x_ref, m_ref, on_ref, pw_ref, ps_ref,
                   wo_ref, o_ref, *, n):
    i = pl.program_id(1)
    tm = x_ref.shape[1]
    o = of_ref[0] + ob_ref[0]
    gate = _silu(ug_ref[0])
    parts = []
    for h in range(A_HEADS):
        hs = slice(h * A_HEAD_DIM, (h + 1) * A_HEAD_DIM)
        parts.append(_rmsnorm(o[:, hs], on_ref[...]) * gate[:, hs])
    main = up_ref[0]
    prev = jnp.where(i > 0, pprev_ref[0], 0.0)
    nxt = jnp.where(i < pl.num_programs(1) - 1, pnext_ref[0], 0.0)
    ext = jnp.concatenate([prev, main, nxt], axis=0)
    ext_rows = tm + 2 * POOL_HALO
    t = i * tm + lax.broadcasted_iota(jnp.int32, (tm, 1), 0)
    for gi, win in enumerate(POOL_WINDOWS):
        gs = slice(gi * B_GROUP, (gi + 1) * B_GROUP)
        acc = ext[:, gs]
        acc = acc + pltpu.roll(acc, 1, 0)
        half = 1
        while 2 * half < win:
            acc = pltpu.roll(acc, half, 0) + pltpu.roll(acc, ext_rows - half, 0)
            half *= 2
        cnt = jnp.minimum(t + (win - win // 2), n) - jnp.maximum(t - win // 2, 0)
        mean = acc[POOL_HALO:POOL_HALO + tm] / cnt.astype(F32)
        pooled = _dot((mean - main[:, gs]).astype(BF16), pw_ref[gi])
        parts.append(pooled * ps_ref[:, gs])
    mix = jnp.concatenate(parts, axis=-1).astype(BF16)
    o_ref[0] = x_ref[0] + m_ref[0, 2:3, :] * _dot(mix, wo_ref[...])


def _ab_out(o_f, o_b, u, x, mods, onorm_g, pool_w, pool_scale, w_out, tm):
    b, n, d = x.shape
    tm = min(tm, n)
    nt = n // tm
    hb = tm // POOL_HALO
    last_halo = n // POOL_HALO - 1
    w = A_WIDTH
    tile = lambda col: pl.BlockSpec((1, tm, w), lambda bi, i: (bi, i, col))
    out = pl.pallas_call(
        functools.partial(_ab_out_kernel, n=n),
        grid=(b, nt),
        in_specs=[
            tile(0), tile(0), tile(4), tile(5),
            pl.BlockSpec((1, POOL_HALO, w), lambda bi, i: (bi, jnp.maximum(i * hb - 1, 0), 5)),
            pl.BlockSpec((1, POOL_HALO, w), lambda bi, i: (bi, jnp.minimum((i + 1) * hb, last_halo), 5)),
            pl.BlockSpec((1, tm, d), lambda bi, i: (bi, i, 0)),
            pl.BlockSpec((1, 6, d), lambda bi, i: (bi, 0, 0)),
            pl.BlockSpec((1, A_HEAD_DIM), lambda bi, i: (0, 0)),
            pl.BlockSpec(pool_w.shape, lambda bi, i: (0, 0, 0)),
            pl.BlockSpec((1, B_WIDTH), lambda bi, i: (0, 0)),
            pl.BlockSpec(w_out.shape, lambda bi, i: (0, 0)),
        ],
        out_specs=pl.BlockSpec((1, tm, d), lambda bi, i: (bi, i, 0)),
        out_shape=jax.ShapeDtypeStruct((b, n, d), F32),
        compiler_params=_cparams("parallel", "parallel"),
        name="ab_out",
    )(o_f, o_b, u, u, u, u, x, mods, onorm_g.reshape(1, A_HEAD_DIM), pool_w, pool_scale.reshape(1, B_WIDTH), w_out)
    return out.reshape(b * n, d)


def _slot_rows(tr):
    rows = 2 * tr + MOE_EXPERTS * (SLOT_BLOCK - 1)
    assert rows % SLOT_BLOCK == 0
    return rows


def _tile_tokens(x_ref, xt_ref):
    if xt_ref is None:
        return x_ref[...]
    return jnp.where(pl.program_id(0) < pl.num_programs(0) - 1, x_ref[...], xt_ref[...])


def _route_kernel(*refs, slot_rows, has_tail, has_mix):
    refs = list(refs)
    x_ref = refs.pop(0)
    xt_ref = refs.pop(0) if has_tail else None
    g_ref, m_ref, wr_ref, br_ref = refs[:4]
    refs = refs[4:]
    x = _tile_tokens(x_ref, xt_ref)
    if has_mix:
        c_ref, d_ref, wo_ref = refs[:3]
        xs_ref, info_ref, cnt_ref, x1_ref = refs[3:]
        wc = c_ref.shape[1]
        x = x + m_ref[0, 2:3, :] * (_dot(c_ref[...], wo_ref[:wc]) + _dot(d_ref[...], wo_ref[wc:]))
        x1_ref[...] = x
    else:
        xs_ref, info_ref, cnt_ref = refs
    tr = x_ref.shape[0]
    h = _rmsnorm(x, g_ref[...]) * (1.0 + m_ref[0, 4:5, :]) + m_ref[0, 3:4, :]
    hb = h.astype(BF16)
    hl = (h - hb.astype(F32)).astype(BF16)
    w_hi, w_lo = wr_ref[:, :LANES], wr_ref[:, LANES:]
    logits = _dot(hb, w_hi) + _dot(hb, w_lo) + _dot(hl, w_hi) + br_ref[...]
    lane = lax.broadcasted_iota(jnp.int32, (tr, LANES), 1)
    lanef = lane.astype(F32)
    lg = jnp.where(lane < MOE_GROUPS, logits, NEG)
    mg = jnp.max(lg, axis=-1, keepdims=True)
    g_p = 1.0 / jnp.sum(jnp.exp(lg - mg), axis=-1, keepdims=True)
    gidx = jnp.min(jnp.where(lg == mg, lanef, float(LANES)), axis=-1, keepdims=True)
    lo = MOE_GROUPS + MOE_EPG * gidx
    le = jnp.where((lanef >= lo) & (lanef < lo + MOE_EPG), logits, NEG)
    m1 = jnp.max(le, axis=-1, keepdims=True)
    i1 = jnp.min(jnp.where(le == m1, lanef, float(LANES)), axis=-1, keepdims=True)
    le2 = jnp.where(lanef == i1, NEG, le)
    m2 = jnp.max(le2, axis=-1, keepdims=True)
    i2 = jnp.min(jnp.where(le2 == m2, lanef, float(LANES)), axis=-1, keepdims=True)
    ratio = jnp.exp(m2 - m1)
    w1 = g_p / (1.0 + ratio)
    w2 = g_p * ratio / (1.0 + ratio)
    hot1 = lanef == i1
    hot2 = lanef == i2
    hot = jnp.where(hot1, 1.0, jnp.where(hot2, 1.0, 0.0))
    r_i = lax.broadcasted_iota(jnp.int32, (tr, tr), 0)
    c_i = lax.broadcasted_iota(jnp.int32, (tr, tr), 1)
    rank = _dot(jnp.where(c_i < r_i, 1.0, 0.0).astype(BF16), hot.astype(BF16))
    cnt = jnp.sum(hot, axis=0, keepdims=True)
    nblk = jnp.floor((cnt + (SLOT_BLOCK - 1)) * (1.0 / SLOT_BLOCK))
    l_r = lax.broadcasted_iota(jnp.int32, (LANES, LANES), 0)
    l_c = lax.broadcasted_iota(jnp.int32, (LANES, LANES), 1)
    before = jnp.where(l_r < l_c, 1.0, 0.0).astype(BF16)
    off = SLOT_BLOCK * _dot(jnp.broadcast_to(nblk, (8, LANES)).astype(BF16), before)[0:1]
    posm = off + rank
    pos1 = jnp.sum(jnp.where(hot1, posm, 0.0), axis=-1, keepdims=True)
    pos2 = jnp.sum(jnp.where(hot2, posm, 0.0), axis=-1, keepdims=True)
    info = jnp.where(lane == 0, pos1, jnp.where(lane == 1, pos2, jnp.where(lane == 2, w1, jnp.where(lane == 3, w2, 0.0))))
    info_ref[...] = info
    pos_t = info.T.astype(jnp.int32)
    row = lax.broadcasted_iota(jnp.int32, (slot_rows, tr), 0)
    sel = jnp.where(row == pos_t[0:1], 1.0, jnp.where(row == pos_t[1:2], 1.0, 0.0)).astype(BF16)
    xs_ref[...] = _dot(sel, hb).astype(BF16).reshape(xs_ref.shape)
    cnt_ref[0] = jnp.broadcast_to(cnt, (8, LANES))


def _token_specs(x2d, x_tail, tr, index, rows=None):
    d = x2d.shape[1]
    nt = (x2d.shape[0] if rows is None else rows) // tr
    if x_tail is None:
        return nt, [pl.BlockSpec((tr, d), index(lambda i: (i, 0)))], [x2d]
    assert x_tail.shape == (tr, d)
    return nt + 1, [pl.BlockSpec((tr, d), index(lambda i: (jnp.minimum(i, nt - 1), 0))),
                    pl.BlockSpec((tr, d), index(lambda i: (0, 0)))], [x2d, x_tail]


def _moe_route(x2d, x_tail, gain, mods, tiles_per_mod, w_r, b_r, tr, mix=None):
    d = x2d.shape[1]
    rows = None if mix is None else mix[0].shape[0]
    nt, x_specs, x_args = _token_specs(x2d, x_tail, tr, lambda f: f, rows)
    t = nt * tr
    sr = _slot_rows(tr)
    mix_specs, mix_args, mix_out_specs, mix_out_shapes = [], [], [], []
    if mix is not None:
        c2d, d2d, w_out = mix
        mix_specs = [pl.BlockSpec((tr, c2d.shape[1]), lambda i: (i, 0)), pl.BlockSpec((tr, d2d.shape[1]), lambda i: (i, 0)),
                     pl.BlockSpec(w_out.shape, lambda i: (0, 0))]
        mix_args = [c2d, d2d, w_out]
        mix_out_specs = [pl.BlockSpec((tr, d), lambda i: (i, 0))]
        mix_out_shapes = [jax.ShapeDtypeStruct((t, d), F32)]
    return pl.pallas_call(
        functools.partial(_route_kernel, slot_rows=sr, has_tail=x_tail is not None, has_mix=mix is not None),
        grid=(nt,),
        in_specs=x_specs + [
            pl.BlockSpec((1, d), lambda i: (0, 0)),
            pl.BlockSpec((1, 6, d), lambda i: (jnp.minimum(i // tiles_per_mod, mods.shape[0] - 1), 0, 0)),
            pl.BlockSpec(w_r.shape, lambda i: (0, 0)),
            pl.BlockSpec((1, LANES), lambda i: (0, 0)),
        ] + mix_specs,
        out_specs=[
            pl.BlockSpec((sr // SLOT_BLOCK, SLOT_BLOCK, d), lambda i: (i, 0, 0)),
            pl.BlockSpec((tr, LANES), lambda i: (i, 0)),
            pl.BlockSpec((1, 8, LANES), lambda i: (i, 0, 0)),
        ] + mix_out_specs,
        out_shape=[
            jax.ShapeDtypeStruct((nt * sr // SLOT_BLOCK, SLOT_BLOCK, d), BF16),
            jax.ShapeDtypeStruct((t, LANES), F32),
            jax.ShapeDtypeStruct((nt, 8, LANES), F32),
        ] + mix_out_shapes,
        compiler_params=_cparams("parallel"),
        name="moe_route",
    )(*x_args, gain.reshape(1, d), mods, w_r, b_r, *mix_args)


def _tables_kernel(cnt_ref, src_ref, inv_ref, exp_ref, valid_ref, *, ntiles, bpt):
    cnt = cnt_ref[...]
    nb = jnp.floor((cnt + (SLOT_BLOCK - 1)) * (1.0 / SLOT_BLOCK))
    i_r = lax.broadcasted_iota(jnp.int32, (LANES, LANES), 0)
    i_c = lax.broadcasted_iota(jnp.int32, (LANES, LANES), 1)
    before = jnp.where(i_r < i_c, 1.0, 0.0)
    upto = jnp.where(i_c <= i_r, 1.0, 0.0)
    first = _dot(nb, before)
    cum = _dot(upto, nb)
    tot = jnp.max(cum, axis=0, keepdims=True)
    steps = jnp.floor((tot + (STEP_BLOCKS - 1)) * (1.0 / STEP_BLOCKS))
    start = STEP_BLOCKS * _dot(jnp.broadcast_to(steps, (LANES, LANES)), before)[0:1]
    pos = start + cum - nb

    sub = 8 * (-(-(MOE_GROUPS + MOE_EXPERTS) // 8))
    first_t, nb_t, pos_t = first.T[:sub], nb.T[:sub], pos.T[:sub]
    as_col = lambda v: jnp.broadcast_to(v, (LANES, LANES)).T[:sub, 0:1]
    start_c, tot_c, span_c = as_col(start), as_col(tot), as_col(STEP_BLOCKS * steps)

    inv_ref[...] = jnp.zeros(inv_ref.shape, jnp.int32)
    local = lax.broadcasted_iota(jnp.int32, (sub, LANES), 1).astype(F32)
    nsrc = src_ref.shape[1]
    j = lax.broadcasted_iota(jnp.int32, (sub, nsrc), 1).astype(F32)
    acc = jnp.zeros((1, nsrc), F32)
    for i in range(ntiles):
        f_i, n_i, p_i = first_t[:, i:i + 1], nb_t[:, i:i + 1], pos_t[:, i:i + 1]
        own = (local >= f_i) & (local < f_i + n_i)
        inv_ref[i:i + 1, :] = jnp.sum(jnp.where(own, p_i + (local - f_i), 0.0), axis=0, keepdims=True).astype(jnp.int32)
        own = (j >= p_i) & (j < p_i + n_i)
        acc = acc + jnp.sum(jnp.where(own, (i * bpt) + f_i + (j - p_i), 0.0), axis=0, keepdims=True)
    src_ref[...] = acc.astype(jnp.int32)

    nst = exp_ref.shape[1]
    at = STEP_BLOCKS * lax.broadcasted_iota(jnp.int32, (sub, nst), 1).astype(F32)
    expert = (lax.broadcasted_iota(jnp.int32, (sub, nst), 0) - MOE_GROUPS).astype(F32)
    inside = (at >= start_c) & (at < start_c + span_c)
    exp_ref[...] = jnp.sum(jnp.where(inside, expert, 0.0), axis=0, keepdims=True).astype(jnp.int32)
    occupied = jnp.where(inside, jnp.where(at - start_c < tot_c, 1.0, 0.0), 0.0)
    valid_ref[...] = jnp.sum(occupied, axis=0, keepdims=True).astype(jnp.int32)


def _expert_tables(cnt, bpt, nsteps):
    ntiles = cnt.shape[0]
    assert bpt <= LANES and ntiles <= LANES
    cnt = jnp.pad(cnt, ((0, LANES - ntiles), (0, 0)))
    nsrc = -(-nsteps * STEP_BLOCKS // LANES) * LANES
    nst = -(-nsteps // LANES) * LANES
    i32 = lambda *s: jax.ShapeDtypeStruct(s, jnp.int32)
    src, inv, step_e, valid = pl.pallas_call(
        functools.partial(_tables_kernel, ntiles=ntiles, bpt=bpt),
        out_shape=[i32(1, nsrc), i32(LANES, LANES), i32(1, nst), i32(1, nst)],
        compiler_params=pltpu.CompilerParams(vmem_limit_bytes=VMEM_LIMIT),
        name="moe_tables",
    )(cnt)
    return src.reshape(-1), inv.reshape(-1), step_e.reshape(-1), valid.reshape(-1)


def _block_gather(table_ref, first, nblocks, src_hbm, buf_ref, slot, sem_ref):
    return [pltpu.make_async_copy(src_hbm.at[table_ref[first + kk]],
                                  buf_ref.at[slot, pl.ds(kk * SLOT_BLOCK, SLOT_BLOCK)], sem_ref.at[slot])
            for kk in range(nblocks)]


def _experts_kernel(src_ref, exp_ref, valid_ref, xs_hbm, wg_ref, wu_ref, wd_ref, y_ref, xbuf_ref, sem_ref, wgb_ref,
                    wub_ref, wdb_ref):
    s = pl.program_id(0)
    slot = s % 2

    def gather(step, to_slot):
        return _block_gather(src_ref, step * STEP_BLOCKS, STEP_BLOCKS, xs_hbm, xbuf_ref, to_slot, sem_ref)

    @pl.when((s == 0) & (valid_ref[0] > 0))
    def _():
        for cp in gather(0, 0):
            cp.start()

    nxt = jnp.minimum(s + 1, pl.num_programs(0) - 1)

    @pl.when((s + 1 < pl.num_programs(0)) & (valid_ref[nxt] > 0))
    def _():
        for cp in gather(s + 1, 1 - slot):
            cp.start()

    @pl.when((s == 0) | (exp_ref[s] != exp_ref[jnp.maximum(s - 1, 0)]))
    def _():
        wgb_ref[...] = wg_ref[0, 0, 0].astype(BF16)
        wub_ref[...] = wu_ref[0, 0, 0].astype(BF16)
        wdb_ref[...] = wd_ref[0, 0, 0].astype(BF16)

    @pl.when(valid_ref[s] > 0)
    def _():
        for cp in gather(s, slot):
            cp.wait()
        x = xbuf_ref[slot]
        a = _silu(_dot(x, wgb_ref[...])) * _dot(x, wub_ref[...])
        y_ref[...] = _dot(a.astype(BF16), wdb_ref[...]).astype(BF16).reshape(y_ref.shape)

    @pl.when(valid_ref[s] == 0)
    def _():
        y_ref[...] = jnp.zeros(y_ref.shape, y_ref.dtype)


def _moe_experts(xs, src, step_e, valid, w_gate, w_up, w_down, layer, nsteps):
    xs3 = xs
    d = xs3.shape[-1]
    f = w_gate.shape[-1]
    step_rows = STEP_BLOCKS * SLOT_BLOCK
    w_blk = lambda shape: pl.BlockSpec((1, 1, 1) + shape,
                                       lambda s, sr, ex, va: (layer, ex[s] // MOE_EPG, ex[s] % MOE_EPG, 0, 0))
    grid_spec = pltpu.PrefetchScalarGridSpec(
        num_scalar_prefetch=3,
        grid=(nsteps,),
        in_specs=[pl.BlockSpec(memory_space=pl.ANY), w_blk((d, f)), w_blk((d, f)), w_blk((f, d))],
        out_specs=pl.BlockSpec((STEP_BLOCKS, SLOT_BLOCK, d), lambda s, sr, ex, va: (s, 0, 0)),
        scratch_shapes=[pltpu.VMEM((2, step_rows, d), BF16), pltpu.SemaphoreType.DMA((2,)),
                        pltpu.VMEM((d, f), BF16), pltpu.VMEM((d, f), BF16), pltpu.VMEM((f, d), BF16)],
    )
    return pl.pallas_call(
        _experts_kernel,
        grid_spec=grid_spec,
        out_shape=jax.ShapeDtypeStruct((nsteps * STEP_BLOCKS, SLOT_BLOCK, d), BF16),
        compiler_params=_cparams("arbitrary"),
        name="moe_experts",
    )(src, step_e, valid, xs3, w_gate, w_up, w_down)


def _combine_kernel(inv_ref, x_ref, xt_ref, info_ref, m_ref, fg_ref, ys_hbm, o_ref, ybuf_ref, sem_ref, *, bpt, final):
    i = pl.program_id(0)
    slot = i % 2
    tr = x_ref.shape[0]

    def gather(tile, to_slot):
        return _block_gather(inv_ref, tile * LANES, bpt, ys_hbm, ybuf_ref, to_slot, sem_ref)

    @pl.when(i == 0)
    def _():
        for cp in gather(0, 0):
            cp.start()

    @pl.when(i + 1 < pl.num_programs(0))
    def _():
        for cp in gather(i + 1, 1 - slot):
            cp.start()

    info = info_ref[...]
    col = lax.broadcasted_iota(jnp.int32, (tr, bpt * SLOT_BLOCK), 1)
    wsel = jnp.where(col == info[:, 0:1].astype(jnp.int32), info[:, 2:3],
                     jnp.where(col == info[:, 1:2].astype(jnp.int32), info[:, 3:4], 0.0))
    for cp in gather(i, slot):
        cp.wait()
    y = _dot(wsel.astype(BF16), ybuf_ref[slot])
    out = _tile_tokens(x_ref, xt_ref) + m_ref[0, 5:6, :] * y
    if final:
        out = _rmsnorm(out, fg_ref[...])
    o_ref[...] = out


def _moe_combine(x2d, x_tail, ys, inv, info, mods, tiles_per_mod, final_g, tr, bpt, final):
    d = x2d.shape[1]
    nt, x_specs, x_args = _token_specs(x2d, x_tail, tr, lambda f: (lambda i, iv: f(i)))
    t = nt * tr
    ys3 = ys
    kern = _combine_kernel
    if x_tail is None:
        kern = lambda inv_ref, x_ref, *refs, **kw: _combine_kernel(inv_ref, x_ref, None, *refs, **kw)
    grid_spec = pltpu.PrefetchScalarGridSpec(
        num_scalar_prefetch=1,
        grid=(nt,),
        in_specs=x_specs + [
            pl.BlockSpec((tr, LANES), lambda i, iv: (i, 0)),
            pl.BlockSpec((1, 6, d), lambda i, iv: (jnp.minimum(i // tiles_per_mod, mods.shape[0] - 1), 0, 0)),
            pl.BlockSpec((1, d), lambda i, iv: (0, 0)),
            pl.BlockSpec(memory_space=pl.ANY),
        ],
        out_specs=pl.BlockSpec((tr, d), lambda i, iv: (i, 0)),
        scratch_shapes=[pltpu.VMEM((2, bpt * SLOT_BLOCK, d), BF16), pltpu.SemaphoreType.DMA((2,))],
    )
    return pl.pallas_call(
        functools.partial(kern, bpt=bpt, final=final),
        grid_spec=grid_spec,
        out_shape=jax.ShapeDtypeStruct((t, d), F32),
        compiler_params=_cparams("arbitrary"),
        name="moe_combine",
    )(inv, *x_args, info, mods, final_g.reshape(1, d), ys3)


def _moe(x2d, x_tail, gain, mods, tiles_per_mod, params, final_g, final, tr, mix=None):
    w_r, b_r, w_gate, w_up, w_down, layer = params
    if mix is None:
        xs, info, cnt = _moe_route(x2d, x_tail, gain, mods, tiles_per_mod, w_r, b_r, tr)
    else:
        xs, info, cnt, x2d = _moe_route(x2d, x_tail, gain, mods, tiles_per_mod, w_r, b_r, tr, mix)
    nt = x2d.shape[0] // tr + (x_tail is not None)
    bpt = _slot_rows(tr) // SLOT_BLOCK
    nsteps = -(-(nt * bpt + MOE_EXPERTS * (STEP_BLOCKS - 1)) // STEP_BLOCKS)
    src, inv, step_e, valid = _expert_tables(cnt[:, 0, :], bpt, nsteps)
    ys = _moe_experts(xs, src, step_e, valid, w_gate, w_up, w_down, layer, nsteps)
    return _moe_combine(x2d, x_tail, ys, inv, info, mods, tiles_per_mod, final_g, tr, bpt, final)


def _cd_in_kernel(x_ref, g_ref, m_ref, w_ref, cos_ref, sin_ref, qg_ref, kg_ref, wq_ref, wqp_ref, wk_ref, wv_ref,
                  vone_ref, ua_ref, *o_refs, need_q, q_scale):
    h = _rmsnorm(x_ref[0], g_ref[...]) * (1.0 + m_ref[0, 1:2, :]) + m_ref[0, 0:1, :]
    hb = h.astype(BF16)
    na_w = ua_ref.shape[2]
    ua_ref[0] = _dot(hb, w_ref[:, :na_w]).astype(ua_ref.dtype)
    ub = _dot(hb, w_ref[:, na_w:])
    o = MLA_Q_RANK + MLA_KV_RANK
    cq_raw, ckv_raw, kr, krp = ub[:, :MLA_Q_RANK], ub[:, MLA_Q_RANK:o], ub[:, o:o + MLA_PAD], ub[:, o + MLA_PAD:]
    cos = cos_ref[...]
    sin = sin_ref[...]
    ckv = _rmsnorm(ckv_raw, kg_ref[...]).astype(BF16)
    k_rope = kr * cos + krp * sin
    kn = _dot(ckv, wk_ref[...])
    if need_q:
        q_ref, k_ref, v_ref = o_refs
    else:
        k_ref, v_ref = o_refs
    vx = _dot(ckv, wv_ref[...]) + vone_ref[...]
    for h in range(D_HEADS):
        hs = slice(h * MLA_PAD, (h + 1) * MLA_PAD)
        k_ref[0, h] = (kn[:, hs] + k_rope).astype(BF16)
        v_ref[0, h] = vx[:, hs].T[:MLA_VROWS].astype(BF16)
    if need_q:
        cq = _rmsnorm(cq_raw, qg_ref[...]).astype(BF16)
        qm = _dot(cq, wq_ref[...])
        qp = _dot(cq, wqp_ref[...])
        for h in range(D_HEADS):
            hs = slice(h * MLA_PAD, (h + 1) * MLA_PAD)
            q_ref[0, h] = ((qm[:, hs] * cos + qp[:, hs] * sin) * q_scale).T.astype(BF16)


def _cd_in_proj(x2d, flat, gain, mods, w_cat, na_w, cos, sin, q_g, kv_g, wq, wqp, wk, wv, need_q, tm):
    b, n, row0 = flat
    d = x2d.shape[-1]
    tm = min(tm, n)
    row_major = (jax.ShapeDtypeStruct((b, D_HEADS, n, MLA_PAD), BF16),
                 pl.BlockSpec((1, D_HEADS, tm, MLA_PAD), lambda bi, i: (bi, 0, i, 0)))
    col_major = (jax.ShapeDtypeStruct((b, D_HEADS, MLA_PAD, n), BF16),
                 pl.BlockSpec((1, D_HEADS, MLA_PAD, tm), lambda bi, i: (bi, 0, 0, i)))
    v_major = (jax.ShapeDtypeStruct((b, D_HEADS, MLA_VROWS, n), BF16),
               pl.BlockSpec((1, D_HEADS, MLA_VROWS, tm), lambda bi, i: (bi, 0, 0, i)))
    ua = (jax.ShapeDtypeStruct((b, n, na_w), BF16), pl.BlockSpec((1, tm, na_w), lambda bi, i: (bi, i, 0)))
    outs, specs = zip(*([ua] + ([col_major] if need_q else []) + [row_major, v_major]))
    full = lambda a: pl.BlockSpec(a.shape, lambda bi, i: (0,) * a.ndim)
    vone = jnp.tile(jnp.concatenate([jnp.zeros((1, MLA_V), F32), jnp.ones((1, MLA_PAD - MLA_V), F32)], axis=1),
                    (1, D_HEADS))
    q_scale = float((MLA_NOPE + MLA_ROPE) ** -0.5 * np.log2(np.e))
    return pl.pallas_call(
        functools.partial(_cd_in_kernel, need_q=need_q, q_scale=q_scale),
        grid=(b, n // tm),
        in_specs=[
            pl.BlockSpec((1, tm, d), lambda bi, i: (0, row0 // tm + bi * (n // tm) + i, 0)),
            pl.BlockSpec((1, d), lambda bi, i: (0, 0)),
            pl.BlockSpec((1, 6, d), lambda bi, i: (bi, 0, 0)),
            full(w_cat),
            pl.BlockSpec((tm, MLA_PAD), lambda bi, i: (i, 0)),
            pl.BlockSpec((tm, MLA_PAD), lambda bi, i: (i, 0)),
            full(q_g), full(kv_g), full(wq), full(wqp), full(wk), full(wv), full(vone),
        ],
        out_specs=list(specs),
        out_shape=list(outs),
        compiler_params=_cparams("parallel", "parallel"),
        name="cd_in_proj",
    )(x2d.reshape(1, -1, d), gain.reshape(1, d), mods, w_cat, cos, sin, q_g, kv_g, wq, wqp, wk, wv, vone)


def _mla_attn_kernel(q_ref, qn_ref, kc_ref, kl_ref, vc_ref, vl_ref, o_ref, acc0_ref, acc1_ref, s0_ref, s1_ref, m_ref,
                     *, tk):
    tq = q_ref.shape[3]
    ncc = kc_ref.shape[2] // tk
    nchunks = ncc + kl_ref.shape[2] // tk
    neg = jnp.full((8, tq), NEG, F32)
    s_refs = (s0_ref, s1_ref)
    acc_refs = (acc0_ref, acc1_ref)

    def chunk(c):
        part = (kc_ref, vc_ref, c) if c < ncc else (kl_ref, vl_ref, c - ncc)
        return part[0], part[1], pl.ds(part[2] * tk, tk), pl.ds(c * tk, tk)

    def scores(k_ref, ks, ss, hh, q, m):
        s = _dot(k_ref[0, hh, ks, :], q)
        s_refs[hh][ss, :] = s
        return jnp.maximum(m, jnp.max(s.reshape(tk // 8, 8, tq), axis=0))

    def weight(v_ref, ks, ss, hh, m_row):
        p = jnp.exp2((s_refs[hh][ss, :] - m_row).astype(BF16))
        acc_refs[hh][...] += _dot(v_ref[0, hh, :, ks], p)

    @pl.when(pl.program_id(2) == 0)
    def _():
        m = neg
        for c in range(ncc):
            k_ref, _, ks, ss = chunk(c)
            m = scores(k_ref, ks, ss, 0, q_ref[0, 0], m)

        def latent(c, m):
            ks = pl.ds(pl.multiple_of(c * tk, tk), tk)
            ss = pl.ds(pl.multiple_of((c + ncc) * tk, tk), tk)
            return scores(kl_ref, ks, ss, 0, q_ref[0, 0], m)

        m_ref[...] = lax.fori_loop(0, nchunks - ncc, latent, m)

    acc0_ref[...] = jnp.zeros(acc0_ref.shape, F32)
    acc1_ref[...] = jnp.zeros(acc1_ref.shape, F32)
    m0 = jnp.max(m_ref[...], axis=0, keepdims=True)
    m1 = neg
    for c in range(nchunks):
        k_ref, v_ref, ks, ss = chunk(c)
        weight(v_ref, ks, ss, 0, m0)
        m1 = scores(k_ref, ks, ss, 1, q_ref[0, 1], m1)
    m1 = jnp.max(m1, axis=0, keepdims=True)
    m0_next = neg
    for c in range(nchunks):
        k_ref, v_ref, ks, ss = chunk(c)
        weight(v_ref, ks, ss, 1, m1)
        m0_next = scores(k_ref, ks, ss, 0, qn_ref[0, 0], m0_next)
    m_ref[...] = m0_next
    o_t = jnp.concatenate([a[:MLA_V] / a[MLA_V:MLA_V + 1] for a in acc_refs], axis=0)
    o_ref[0] = o_t.T.astype(o_ref.dtype)


def _mla_attention(q_t, k_ctx, k_lat, v_ctx, v_lat, tq, tk):
    b, h, _, n = q_t.shape
    nc = k_ctx.shape[2]
    tq = min(tq, n)
    assert nc % tk == 0 and n % tk == 0
    k_spec = lambda rows: pl.BlockSpec((1, 2, rows, MLA_PAD), lambda bi, hp, i: (bi, hp, 0, 0))
    v_spec = lambda rows: pl.BlockSpec((1, 2, MLA_VROWS, rows), lambda bi, hp, i: (bi, hp, 0, 0))
    return pl.pallas_call(
        functools.partial(_mla_attn_kernel, tk=tk),
        grid=(b, h // 2, n // tq),
        in_specs=[
            pl.BlockSpec((1, 2, MLA_PAD, tq), lambda bi, hp, i: (bi, hp, 0, i)),
            pl.BlockSpec((1, 2, MLA_PAD, tq), lambda bi, hp, i: (bi, hp, 0, jnp.minimum(i + 1, n // tq - 1))),
            k_spec(nc), k_spec(n), v_spec(nc), v_spec(n),
        ],
        out_specs=pl.BlockSpec((1, tq, 2 * MLA_V), lambda bi, hp, i: (bi, i, hp)),
        out_shape=jax.ShapeDtypeStruct((b, n, h * MLA_V), BF16),
        scratch_shapes=[pltpu.VMEM((MLA_VROWS, tq), F32), pltpu.VMEM((MLA_VROWS, tq), F32),
                        pltpu.VMEM((nc + n, tq), F32), pltpu.VMEM((nc + n, tq), F32), pltpu.VMEM((8, tq), F32)],
        compiler_params=_cparams("parallel", "parallel", "arbitrary"),
        name="mla_attention",
    )(q_t, q_t, k_ctx, k_lat, v_ctx, v_lat)


def _na_kernel(q_ref, *refs):
    k_refs, v_refs = refs[:NA_KBLOCKS + 1], refs[NA_KBLOCKS + 1:2 * NA_KBLOCKS + 2]
    tab_ref, o_ref = refs[2 * NA_KBLOCKS + 2:]
    tq = q_ref.shape[1]
    nloc = tab_ref.shape[3]
    lane = lax.broadcasted_iota(jnp.int32, (tq, LANES), 1)
    q = q_ref[0]
    k_all = jnp.concatenate([r[0] for r in k_refs], axis=0)
    v_all = jnp.concatenate([r[0] for r in v_refs], axis=0)
    outs = []
    for hh in range(2):
        in_head = (lane >= hh * C_HEAD_DIM) & (lane < (hh + 1) * C_HEAD_DIM)
        qh = jnp.where(in_head, q, jnp.zeros_like(q))
        s = lax.dot_general(qh, k_all, NT, preferred_element_type=F32)
        s_loc = s[:, :nloc] + tab_ref[0, hh]
        s_ctx = s[:, nloc:]
        m = jnp.maximum(jnp.max(s_loc, axis=-1, keepdims=True), jnp.max(s_ctx, axis=-1, keepdims=True))
        p_loc = jnp.exp(s_loc - m)
        p_ctx = jnp.exp(s_ctx - m)
        l = jnp.sum(p_loc, axis=-1, keepdims=True) + jnp.sum(p_ctx, axis=-1, keepdims=True)
        o = _dot(p_loc.astype(BF16), v_all[:nloc]) + _dot(p_ctx.astype(BF16), v_all[nloc:])
        outs.append(o / l)
    o_ref[0] = jnp.where(lane < C_HEAD_DIM, outs[0], outs[1]).astype(o_ref.dtype)


def _na_tables(rpb, rows):
    h = rpb.shape[0]
    w = GRID_W
    kr_n = NA_QROWS + NA_ROWS
    qc = np.arange(w)
    kc = np.arange(w)
    cs = np.clip(qc - NA_COLS // 2, 0, w - NA_COLS)
    col_ok = (kc[None, :] >= cs[:, None]) & (kc[None, :] < cs[:, None] + NA_COLS)
    dc = np.clip(kc[None, :] - qc[:, None] + (NA_COLS - 1), 0, 2 * NA_COLS - 2)
    pick_dc = (dc.reshape(-1)[None, :] == np.arange(2 * NA_COLS - 1)[:, None]).astype(np.float32)
    base = jnp.einsum('hrd,dx->hrx', rpb.astype(F32), jnp.asarray(pick_dc), precision=lax.Precision.HIGHEST)
    nblk = rows // NA_QROWS
    tabs = []
    for m in (0, 1, nblk - 1):
        qr = NA_QROWS * m + np.arange(NA_QROWS)
        rs = np.clip(qr - NA_ROWS // 2, 0, rows - NA_ROWS)
        kr = NA_QROWS * m - NA_ROWS // 2 + np.arange(kr_n)
        row_ok = (kr[None, :] >= rs[:, None]) & (kr[None, :] < rs[:, None] + NA_ROWS)
        dr = np.clip(kr[None, :] - qr[:, None] + (NA_ROWS - 1), 0, 2 * NA_ROWS - 2)
        pick_dr = (dr.reshape(-1)[:, None] == np.arange(2 * NA_ROWS - 1)[None, :]).astype(np.float32)
        t = jnp.einsum('vr,hrx->hvx', jnp.asarray(pick_dr), base, precision=lax.Precision.HIGHEST)
        t = t.reshape(h, NA_QROWS, kr_n, w, w).transpose(0, 1, 3, 2, 4)
        ok = row_ok[:, None, :, None] & col_ok[None, :, None, :]
        tabs.append(jnp.where(jnp.asarray(ok)[None], t, NEG).reshape(h, NA_QROWS * w, kr_n * w))
    return jnp.stack(tabs)


def _na_attention(u_lat, u_ctx, tabs):
    b, n, _ = u_lat.shape
    nc = u_ctx.shape[1]
    tq = NA_QROWS * GRID_W
    tkb = NA_KBLOCK_ROWS * GRID_W
    nblk = n // tq
    nkb = n // tkb
    per_q = NA_QROWS // NA_KBLOCK_ROWS
    assert NA_KBLOCKS == per_q + 2 and NA_ROWS // 2 == NA_KBLOCK_ROWS and n % tq == 0 and n // tq >= 2
    pairs = C_HEADS // 2
    q_spec = pl.BlockSpec((1, tq, LANES), lambda bi, hp, i: (bi, i, hp))
    kblk = lambda col0, j: pl.BlockSpec(
        (1, tkb, LANES), lambda bi, hp, i: (bi, jnp.clip(i * per_q - 1 + j, 0, nkb - 1), col0 + hp))
    ctx = lambda col0: pl.BlockSpec((1, nc, LANES), lambda bi, hp, i: (bi, 0, col0 + hp))
    sel = lambda i: jnp.where(i == 0, 0, jnp.where(i == nblk - 1, 2, 1))
    kv_specs = [kblk(col0, j) for col0 in (pairs, 2 * pairs) for j in range(NA_KBLOCKS)]
    kv_specs = kv_specs[:NA_KBLOCKS] + [ctx(pairs)] + kv_specs[NA_KBLOCKS:] + [ctx(2 * pairs)]
    kv_args = [u_lat] * NA_KBLOCKS + [u_ctx]
    return pl.pallas_call(
        _na_kernel,
        grid=(b, pairs, nblk),
        in_specs=[q_spec] + kv_specs + [
            pl.BlockSpec((1, 2, tq, NA_KBLOCKS * tkb), lambda bi, hp, i: (sel(i), hp, 0, 0)),
        ],
        out_specs=pl.BlockSpec((1, tq, LANES), lambda bi, hp, i: (bi, i, hp)),
        out_shape=jax.ShapeDtypeStruct((b, n, C_WIDTH), BF16),
        compiler_params=_cparams("parallel", "parallel", "arbitrary"),
        name="na_attention",
    )(u_lat, *kv_args, *kv_args, tabs)


def _moe_params(w_rg, b_rg, w_re, b_re, w_gate, w_up, w_down, layer):
    d = w_rg.shape[0]
    w_r = jnp.zeros((d, LANES), F32).at[:, :MOE_GROUPS].set(w_rg).at[:, MOE_GROUPS:MOE_GROUPS + MOE_EXPERTS].set(w_re)
    b_r = jnp.zeros((1, LANES), F32).at[0, :MOE_GROUPS].set(b_rg).at[0, MOE_GROUPS:MOE_GROUPS + MOE_EXPERTS].set(b_re)
    w_hi = w_r.astype(BF16)
    w_lo = (w_r - w_hi.astype(F32)).astype(BF16)
    return jnp.concatenate([w_hi, w_lo], axis=1), b_r, w_gate, w_up, w_down, layer


def _rope_perm():
    j = np.arange(MLA_ROPE)
    half = MLA_ROPE // 2
    return (j // half) * half + (j % half + half // 2) % half


def _rope_tables(n):
    half = MLA_ROPE // 2
    nf = half // 2
    t = np.arange(n)
    inv = (np.float32(ROPE_THETA) ** (-np.arange(nf, dtype=np.float32) / np.float32(nf))).astype(np.float32)
    parts_c, parts_s = [], []
    for pos in ((t // GRID_W).astype(np.float32), (t % GRID_W).astype(np.float32)):
        ang = (pos[:, None] * inv[None, :]).astype(np.float32)
        c, s = np.cos(ang).astype(np.float32), np.sin(ang).astype(np.float32)
        parts_c += [c, c]
        parts_s += [-s, s]
    pad = MLA_PAD - MLA_NOPE - MLA_ROPE
    cos = np.concatenate([np.ones((n, MLA_NOPE), np.float32)] + parts_c + [np.zeros((n, pad), np.float32)], axis=1)
    sin = np.concatenate([np.zeros((n, MLA_NOPE), np.float32)] + parts_s + [np.zeros((n, pad), np.float32)], axis=1)
    return jnp.asarray(cos), jnp.asarray(sin)


def _identity_rope_tables(n):
    pad = MLA_PAD - MLA_NOPE - MLA_ROPE
    cos = jnp.concatenate([jnp.ones((n, MLA_NOPE + MLA_ROPE), F32), jnp.zeros((n, pad), F32)], axis=1)
    return cos, jnp.zeros((n, MLA_PAD), F32)


def _pad_heads(w, widths, src_cols, dst_off):
    rank = w.shape[0]
    out = jnp.zeros((rank, D_HEADS, MLA_PAD), F32)
    wh = w.reshape(rank, D_HEADS, widths)[:, :, src_cols]
    return out.at[:, :, dst_off:dst_off + len(src_cols)].set(wh).reshape(rank, D_HEADS * MLA_PAD)


def _cd_params(w_in, w_uq, w_ukv):
    d = w_in.shape[0]
    perm = _rope_perm()
    o = 3 * C_WIDTH
    q_scale = float(C_HEAD_DIM ** -0.5)
    kr = w_in[:, o + MLA_Q_RANK + MLA_KV_RANK:]
    pad_rope = lambda a: jnp.zeros((d, MLA_PAD), F32).at[:, MLA_NOPE:MLA_NOPE + MLA_ROPE].set(a)
    w_cat = jnp.concatenate([
        w_in[:, :C_WIDTH] * q_scale, w_in[:, C_WIDTH:o],
        w_in[:, o:o + MLA_Q_RANK + MLA_KV_RANK], pad_rope(kr), pad_rope(kr[:, perm]),
    ], axis=1).astype(BF16)
    qw = MLA_NOPE + MLA_ROPE
    nope = np.arange(MLA_NOPE)
    rope = MLA_NOPE + np.arange(MLA_ROPE)
    wq = (_pad_heads(w_uq, qw, nope, 0) + _pad_heads(w_uq, qw, rope, MLA_NOPE)).astype(BF16)
    wqp = _pad_heads(w_uq, qw, rope[perm], MLA_NOPE).astype(BF16)
    kvw = MLA_NOPE + MLA_V
    wk = _pad_heads(w_ukv, kvw, nope, 0).astype(BF16)
    wv = _pad_heads(w_ukv, kvw, MLA_NOPE + np.arange(MLA_V), 0).astype(BF16)
    return w_cat, wq, wqp, wk, wv


def kernel(x, c, ctx, c_ctx, ada_w, ada_b, norm1_g, norm2_g, ab_w_in, ab_w_out, hgrn_lb_logits, hgrn_onorm_g, pool_w,
           pool_scale, cd_w_in, cd_w_out, na_rpb, mla_q_norm_g, mla_w_uq, mla_kv_norm_g, mla_w_ukv, moe_w_rg, moe_b_rg,
           moe_w_re, moe_b_re, moe_w_gate, moe_w_up, moe_w_down, final_norm_g):
    b, n, d = x.shape
    n_ctx = ctx.shape[1]
    assert ada_w.shape[0] == 2 and ab_w_in.shape[0] == 1 and cd_w_in.shape[0] == 1 and b + 1 <= 8
    tm = TOKEN_TILE

    cc = jnp.zeros((8, d), F32).at[:b].set(c).at[b].set(c_ctx)
    mods = _ada(cc, b + 1, ada_w, ada_b).reshape(2, 8, 6, d)
    mods_lat = [mods[l, :b] for l in range(2)]
    mods_ctx = [jnp.broadcast_to(mods[l, b:b + 1], (b, 6, d)) for l in range(2)]
    lb = jnp.cumsum(jax.nn.softmax(hgrn_lb_logits.astype(F32), axis=1), axis=1)[:, 0]

    w_in0 = ab_w_in[0].astype(BF16)
    w_out0 = ab_w_out[0].astype(BF16)
    pw0 = pool_w[0].astype(BF16)
    ab_cols = w_in0.shape[1]
    (u_ctx,) = _in_proj(ctx, norm1_g[0], mods_ctx[0], w_in0, ((0, ab_cols),), (F32,), tm)
    (u_lat,) = _in_proj(x, norm1_g[0], mods_lat[0], w_in0, ((0, ab_cols),), (F32,), tm)
    s0 = jnp.zeros((b, 2, A_HEADS, A_HEAD_DIM, A_HEAD_DIM), F32)
    ocf, ocb, s_ctx = _hgrn_scan(u_ctx, lb, s0, HGRN_ROWS)
    olf, olb, _ = _hgrn_scan(u_lat, lb, s_ctx, HGRN_ROWS)
    t_lat, t_ctx = b * n, b * n_ctx
    assert n % tm == 0 and t_ctx == tm
    x_lat = _ab_out(olf, olb, u_lat, x, mods_lat[0], hgrn_onorm_g[0], pw0, pool_scale[0], w_out0, tm)
    x_ctx = _ab_out(ocf, ocb, u_ctx, ctx, mods_ctx[0], hgrn_onorm_g[0], pw0, pool_scale[0], w_out0, tm)
    moe0 = _moe_params(moe_w_rg[0], moe_b_rg[0], moe_w_re[0], moe_b_re[0], moe_w_gate, moe_w_up, moe_w_down, 0)
    mods_all = jnp.concatenate([mods_lat[0], mods[0, b:b + 1]], axis=0)
    xa = _moe(x_lat, x_ctx, norm2_g[0], mods_all, n // tm, moe0, final_norm_g, False, tm)

    w_cat, wq, wqp, wk, wv = _cd_params(cd_w_in[0], mla_w_uq[0], mla_w_ukv[0])
    na_w = 3 * C_WIDTH
    q_g = mla_q_norm_g[0].reshape(1, -1)
    kv_g = mla_kv_norm_g[0].reshape(1, -1)
    cos_l, sin_l = _rope_tables(n)
    cos_c, sin_c = _identity_rope_tables(n_ctx)
    ua_ctx, k_c, v_c = _cd_in_proj(xa, (b, n_ctx, t_lat), norm1_g[1], mods_ctx[1], w_cat, na_w, cos_c, sin_c, q_g, kv_g,
                                   wq, wqp, wk, wv, False, tm)
    ua_lat, q_l, k_l, v_l = _cd_in_proj(xa, (b, n, 0), norm1_g[1], mods_lat[1], w_cat, na_w, cos_l, sin_l, q_g, kv_g,
                                        wq, wqp, wk, wv, True, tm)
    d_lat = _mla_attention(q_l, k_c, k_l, v_c, v_l, MLA_Q_TILE, MLA_K_CHUNK)
    c_lat = _na_attention(ua_lat, ua_ctx, _na_tables(na_rpb[0], n // GRID_W))
    moe1 = _moe_params(moe_w_rg[1], moe_b_rg[1], moe_w_re[1], moe_b_re[1], moe_w_gate, moe_w_up, moe_w_down, 1)
    mix = (c_lat.reshape(t_lat, -1), d_lat.reshape(t_lat, -1), cd_w_out[0].astype(BF16))
    out = _moe(xa, None, norm2_g[1], mods_lat[1], n // tm, moe1, final_norm_g, True, tm, mix)
    return out.reshape(b, n, d)
```

```python
import functools

import numpy as np
import jax
import jax.numpy as jnp
from jax import lax
from jax.experimental import pallas as pl
from jax.experimental.pallas import tpu as pltpu

F32 = jnp.float32
BF16 = jnp.bfloat16

EPS = 1e-6
NEG = -1e30

GRID_W = 64
A_HEADS = 4
A_HEAD_DIM = 128
A_WIDTH = A_HEADS * A_HEAD_DIM
POOL_WINDOWS = (2, 4, 8, 16)
B_GROUP = 128
B_WIDTH = B_GROUP * len(POOL_WINDOWS)
POOL_HALO = 16
C_HEADS = 8
C_HEAD_DIM = 64
C_WIDTH = C_HEADS * C_HEAD_DIM
NA_ROWS = 8
NA_COLS = 16
NA_QROWS = 4
NA_KBLOCK_ROWS = 4
NA_KBLOCKS = 3
D_HEADS = 8
MLA_Q_RANK = 256
MLA_KV_RANK = 128
MLA_NOPE = 64
MLA_ROPE = 32
MLA_V = 64
MLA_PAD = 128
MLA_VROWS = 80
ROPE_THETA = 10000.0
MOE_GROUPS = 4
MOE_EPG = 8
MOE_EXPERTS = MOE_GROUPS * MOE_EPG
MOE_HIDDEN = 256
LANES = 128
SLOT_BLOCK = 16
STEP_BLOCKS = 32
VMEM_LIMIT = 56 * 1024 * 1024
TOKEN_TILE = 512
HGRN_ROWS = 256
MLA_Q_TILE = 256
MLA_K_CHUNK = 256

NT = (((1,), (1,)), ((), ()))
TN = (((0,), (0,)), ((), ()))


def _cparams(*sem):
    return pltpu.CompilerParams(dimension_semantics=sem, vmem_limit_bytes=VMEM_LIMIT)


def _sigmoid(x):
    return 1.0 / (1.0 + jnp.exp(-x))


def _silu(x):
    return x * _sigmoid(x)


def _dot(a, b):
    return jnp.dot(a, b, preferred_element_type=F32)


def _rmsnorm(x, g):
    return x * lax.rsqrt(jnp.mean(x * x, axis=-1, keepdims=True) + EPS) * g


def _ada_kernel(ct_ref, w_ref, b_ref, o_ref, *, nrows):
    s = _silu(ct_ref[...])
    w = w_ref[0]
    rows = [jnp.sum(s[:, r:r + 1] * w, axis=0, keepdims=True) + b_ref[0] for r in range(nrows)]
    rows.append(jnp.zeros((o_ref.shape[1] - nrows, w.shape[1]), F32))
    o_ref[0] = jnp.concatenate(rows, axis=0)


def _ada(cc, nrows, ada_w, ada_b):
    depth, d, n6 = ada_w.shape
    tn = n6 // 8
    return pl.pallas_call(
        functools.partial(_ada_kernel, nrows=nrows),
        grid=(depth, n6 // tn),
        in_specs=[
            pl.BlockSpec((d, 8), lambda l, j: (0, 0)),
            pl.BlockSpec((1, d, tn), lambda l, j: (l, 0, j)),
            pl.BlockSpec((1, 1, tn), lambda l, j: (l, 0, j)),
        ],
        out_specs=pl.BlockSpec((1, 8, tn), lambda l, j: (l, 0, j)),
        out_shape=jax.ShapeDtypeStruct((depth, 8, n6), F32),
        compiler_params=_cparams("parallel", "parallel"),
        name="ada_mod",
    )(cc.T, ada_w, ada_b.reshape(depth, 1, n6))


def _in_kernel(x_ref, g_ref, m_ref, w_ref, *o_refs, splits):
    h = _rmsnorm(x_ref[0], g_ref[...]) * (1.0 + m_ref[0, 1:2, :]) + m_ref[0, 0:1, :]
    hb = h.astype(BF16)
    for o_ref, (a, b) in zip(o_refs, splits):
        o_ref[0] = _dot(hb, w_ref[:, a:b]).astype(o_ref.dtype)


def _in_proj(x, gain, mods, w, splits, dtypes, tm):
    b, n, d = x.shape
    tm = min(tm, n)
    outs = [jax.ShapeDtypeStruct((b, n, hi - lo), dt) for (lo, hi), dt in zip(splits, dtypes)]
    return pl.pallas_call(
        functools.partial(_in_kernel, splits=splits),
        grid=(b, n // tm),
        in_specs=[
            pl.BlockSpec((1, tm, d), lambda bi, i: (bi, i, 0)),
            pl.BlockSpec((1, d), lambda bi, i: (0, 0)),
            pl.BlockSpec((1, 6, d), lambda bi, i: (bi, 0, 0)),
            pl.BlockSpec(w.shape, lambda bi, i: (0, 0)),
        ],
        out_specs=[pl.BlockSpec((1, tm, hi - lo), lambda bi, i: (bi, i, 0)) for lo, hi in splits],
        out_shape=outs,
        compiler_params=_cparams("parallel", "parallel"),
        name="in_proj",
    )(x, gain.reshape(1, d), mods, w)


HG_SUB = 64


def _hgrn_direction(q_raw, fz, v, lb, st_ref, d, o_ref, reverse):
    rows = q_raw.shape[0]
    c = HG_SUB
    f = lb + (1.0 - lb) * _sigmoid(fz)
    k = 1.0 - f
    g = jnp.log(f)
    q = _silu(q_raw)
    r_i = lax.broadcasted_iota(jnp.int32, (c, c), 0)
    c_i = lax.broadcasted_iota(jnp.int32, (c, c), 1)
    keep = (c_i >= r_i) if reverse else (c_i <= r_i)
    tri = jnp.where(keep, 1.0, 0.0).astype(BF16)
    order = range(rows // c - 1, -1, -1) if reverse else range(rows // c)
    for ci in order:
        sl = slice(ci * c, (ci + 1) * c)
        gc = g[sl]
        g_hi = gc.astype(BF16)
        g_lo = (gc - g_hi.astype(F32)).astype(BF16)
        bc = _dot(tri, g_hi) + _dot(tri, g_lo)
        ref = bc[c // 2:c // 2 + 1]
        tot = bc[0:1] if reverse else bc[c - 1:c]
        qt = q[sl] * jnp.exp(bc - ref)
        kt = k[sl] * jnp.exp(ref - bc)
        qd = (qt * jnp.exp(ref)).astype(BF16)
        kd = (kt * jnp.exp(tot - ref)).astype(BF16)
        qt = qt.astype(BF16)
        kt = kt.astype(BF16)
        vb = v[sl].astype(BF16)
        dec = jnp.exp(tot)
        for h in range(A_HEADS):
            hs = slice(h * A_HEAD_DIM, (h + 1) * A_HEAD_DIM)
            att = lax.dot_general(qt[:, hs], kt[:, hs], NT, preferred_element_type=F32)
            att = jnp.where(keep, att, 0.0).astype(BF16)
            st = st_ref[d, h]
            o = _dot(att, vb[:, hs]) + lax.dot_general(qd[:, hs], st.astype(BF16), NT, preferred_element_type=F32)
            o_ref[0, sl, hs] = o
            st_ref[d, h] = st * dec[:, hs] + lax.dot_general(vb[:, hs], kd[:, hs], TN, preferred_element_type=F32)


def _hgrn_kernel(qf_ref, ff_ref, vf_ref, qb_ref, fb_ref, vb_ref, lb_ref, s0_ref, of_ref, ob_ref, sfin_ref, st_ref):
    j = pl.program_id(1)

    @pl.when(j == 0)
    def _():
        st_ref[...] = s0_ref[0]

    _hgrn_direction(qf_ref[0], ff_ref[0], vf_ref[0], lb_ref[0:1], st_ref, 0, of_ref, False)
    _hgrn_direction(qb_ref[0], fb_ref[0], vb_ref[0], lb_ref[1:2], st_ref, 1, ob_ref, True)

    @pl.when(j == pl.num_programs(1) - 1)
    def _():
        sfin_ref[0] = st_ref[...]


def _hgrn_scan(u, lb, s0, rows):
    b, n, _ = u.shape
    rows = min(rows, n)
    nb = n // rows
    w = A_WIDTH

    def fwd(col):
        return pl.BlockSpec((1, rows, w), lambda bi, j: (bi, j, col))

    def bwd(col):
        return pl.BlockSpec((1, rows, w), lambda bi, j: (bi, nb - 1 - j, col))

    st_spec = pl.BlockSpec((1, 2, A_HEADS, A_HEAD_DIM, A_HEAD_DIM), lambda bi, j: (bi, 0, 0, 0, 0))
    return pl.pallas_call(
        _hgrn_kernel,
        grid=(b, nb),
        in_specs=[fwd(0), fwd(1), fwd(3), bwd(0), bwd(2), bwd(3), pl.BlockSpec((2, w), lambda bi, j: (0, 0)), st_spec],
        out_specs=[
            pl.BlockSpec((1, rows, w), lambda bi, j: (bi, j, 0)),
            pl.BlockSpec((1, rows, w), lambda bi, j: (bi, nb - 1 - j, 0)),
            st_spec,
        ],
        out_shape=[
            jax.ShapeDtypeStruct((b, n, w), F32),
            jax.ShapeDtypeStruct((b, n, w), F32),
            jax.ShapeDtypeStruct(s0.shape, F32),
        ],
        scratch_shapes=[pltpu.VMEM((2, A_HEADS, A_HEAD_DIM, A_HEAD_DIM), F32)],
        compiler_params=_cparams("parallel", "arbitrary"),
        name="hgrn_scan",
    )(u, u, u, u, u, u, lb, s0)


def _ab_out_kernel(of_ref, ob_ref, ug_ref, up_ref, pprev_ref, pnext_ref, x_ref, m_ref, on_ref, pw_ref, ps_ref,
                   wo_ref, o_ref, *, n):
    i = pl.program_id(1)
    tm = x_ref.shape[1]
    o = of_ref[0] + ob_ref[0]
    gate = _silu(ug_ref[0])
    parts = []
    for h in range(A_HEADS):
        hs = slice(h * A_HEAD_DIM, (h + 1) * A_HEAD_DIM)
        parts.append(_rmsnorm(o[:, hs], on_ref[...]) * gate[:, hs])
    main = up_ref[0]
    prev = jnp.where(i > 0, pprev_ref[0], 0.0)
    nxt = jnp.where(i < pl.num_programs(1) - 1, pnext_ref[0], 0.0)
    ext = jnp.concatenate([prev, main, nxt], axis=0)
    ext_rows = tm + 2 * POOL_HALO
    t = i * tm + lax.broadcasted_iota(jnp.int32, (tm, 1), 0)
    for gi, win in enumerate(POOL_WINDOWS):
        gs = slice(gi * B_GROUP, (gi + 1) * B_GROUP)
        acc = ext[:, gs]
        acc = acc + pltpu.roll(acc, 1, 0)
        half = 1
        while 2 * half < win:
            acc = pltpu.roll(acc, half, 0) + pltpu.roll(acc, ext_rows - half, 0)
            half *= 2
        cnt = jnp.minimum(t + (win - win // 2), n) - jnp.maximum(t - win // 2, 0)
        mean = acc[POOL_HALO:POOL_HALO + tm] / cnt.astype(F32)
        pooled = _dot((mean - main[:, gs]).astype(BF16), pw_ref[gi])
        parts.append(pooled * ps_ref[:, gs])
    mix = jnp.concatenate(parts, axis=-1).astype(BF16)
    o_ref[0] = x_ref[0] + m_ref[0, 2:3, :] * _dot(mix, wo_ref[...])


def _ab_out(o_f, o_b, u, x, mods, onorm_g, pool_w, pool_scale, w_out, tm):
    b, n, d = x.shape
    tm = min(tm, n)
    nt = n // tm
    hb = tm // POOL_HALO
    last_halo = n // POOL_HALO - 1
    w = A_WIDTH
    tile = lambda col: pl.BlockSpec((1, tm, w), lambda bi, i: (bi, i, col))
    out = pl.pallas_call(
        functools.partial(_ab_out_kernel, n=n),
        grid=(b, nt),
        in_specs=[
            tile(0), tile(0), tile(4), tile(5),
            pl.BlockSpec((1, POOL_HALO, w), lambda bi, i: (bi, jnp.maximum(i * hb - 1, 0), 5)),
            pl.BlockSpec((1, POOL_HALO, w), lambda bi, i: (bi, jnp.minimum((i + 1) * hb, last_halo), 5)),
            pl.BlockSpec((1, tm, d), lambda bi, i: (bi, i, 0)),
            pl.BlockSpec((1, 6, d), lambda bi, i: (bi, 0, 0)),
            pl.BlockSpec((1, A_HEAD_DIM), lambda bi, i: (0, 0)),
            pl.BlockSpec(pool_w.shape, lambda bi, i: (0, 0, 0)),
            pl.BlockSpec((1, B_WIDTH), lambda bi, i: (0, 0)),
            pl.BlockSpec(w_out.shape, lambda bi, i: (0, 0)),
        ],
        out_specs=pl.BlockSpec((1, tm, d), lambda bi, i: (bi, i, 0)),
        out_shape=jax.ShapeDtypeStruct((b, n, d), F32),
        compiler_params=_cparams("parallel", "parallel"),
        name="ab_out",
    )(o_f, o_b, u, u, u, u, x, mods, onorm_g.reshape(1, A_HEAD_DIM), pool_w, pool_scale.reshape(1, B_WIDTH), w_out)
    return out.reshape(b * n, d)


def _slot_rows(tr):
    rows = 2 * tr + MOE_EXPERTS * (SLOT_BLOCK - 1)
    assert rows % SLOT_BLOCK == 0
    return rows


def _tile_tokens(x_ref, xt_ref):
    if xt_ref is None:
        return x_ref[...]
    return jnp.where(pl.program_id(0) < pl.num_programs(0) - 1, x_ref[...], xt_ref[...])


def _route_kernel(*refs, slot_rows, has_tail, has_mix):
    refs = list(refs)
    x_ref = refs.pop(0)
    xt_ref = refs.pop(0) if has_tail else None
    g_ref, m_ref, wr_ref, br_ref = refs[:4]
    refs = refs[4:]
    x = _tile_tokens(x_ref, xt_ref)
    if has_mix:
        c_ref, d_ref, wo_ref = refs[:3]
        xs_ref, info_ref, cnt_ref, x1_ref = refs[3:]
        wc = c_ref.shape[2]
        x = x + m_ref[0, 2:3, :] * (_dot(c_ref[0], wo_ref[:wc]) + _dot(d_ref[0], wo_ref[wc:]))
        x1_ref[...] = x
    else:
        xs_ref, info_ref, cnt_ref = refs
    tr = x_ref.shape[0]
    h = _rmsnorm(x, g_ref[...]) * (1.0 + m_ref[0, 4:5, :]) + m_ref[0, 3:4, :]
    hb = h.astype(BF16)
    hl = (h - hb.astype(F32)).astype(BF16)
    w_hi, w_lo = wr_ref[:, :LANES], wr_ref[:, LANES:]
    logits = _dot(hb, w_hi) + _dot(hb, w_lo) + _dot(hl, w_hi) + br_ref[...]
    lane = lax.broadcasted_iota(jnp.int32, (tr, LANES), 1)
    lanef = lane.astype(F32)
    lg = jnp.where(lane < MOE_GROUPS, logits, NEG)
    mg = jnp.max(lg, axis=-1, keepdims=True)
    g_p = 1.0 / jnp.sum(jnp.exp(lg - mg), axis=-1, keepdims=True)
    gidx = jnp.min(jnp.where(lg == mg, lanef, float(LANES)), axis=-1, keepdims=True)
    lo = MOE_GROUPS + MOE_EPG * gidx
    le = jnp.where((lanef >= lo) & (lanef < lo + MOE_EPG), logits, NEG)
    m1 = jnp.max(le, axis=-1, keepdims=True)
    i1 = jnp.min(jnp.where(le == m1, lanef, float(LANES)), axis=-1, keepdims=True)
    le2 = jnp.where(lanef == i1, NEG, le)
    m2 = jnp.max(le2, axis=-1, keepdims=True)
    i2 = jnp.min(jnp.where(le2 == m2, lanef, float(LANES)), axis=-1, keepdims=True)
    ratio = jnp.exp(m2 - m1)
    w1 = g_p / (1.0 + ratio)
    w2 = g_p * ratio / (1.0 + ratio)
    hot1 = lanef == i1
    hot2 = lanef == i2
    hot = jnp.where(hot1, 1.0, jnp.where(hot2, 1.0, 0.0))
    r_i = lax.broadcasted_iota(jnp.int32, (tr, tr), 0)
    c_i = lax.broadcasted_iota(jnp.int32, (tr, tr), 1)
    rank = _dot(jnp.where(c_i < r_i, 1.0, 0.0).astype(BF16), hot.astype(BF16))
    cnt = jnp.sum(hot, axis=0, keepdims=True)
    nblk = jnp.floor((cnt + (SLOT_BLOCK - 1)) * (1.0 / SLOT_BLOCK))
    l_r = lax.broadcasted_iota(jnp.int32, (LANES, LANES), 0)
    l_c = lax.broadcasted_iota(jnp.int32, (LANES, LANES), 1)
    before = jnp.where(l_r < l_c, 1.0, 0.0).astype(BF16)
    off = SLOT_BLOCK * _dot(jnp.broadcast_to(nblk, (8, LANES)).astype(BF16), before)[0:1]
    posm = off + rank
    pos1 = jnp.sum(jnp.where(hot1, posm, 0.0), axis=-1, keepdims=True)
    pos2 = jnp.sum(jnp.where(hot2, posm, 0.0), axis=-1, keepdims=True)
    info = jnp.where(lane == 0, pos1, jnp.where(lane == 1, pos2, jnp.where(lane == 2, w1, jnp.where(lane == 3, w2, 0.0))))
    info_ref[...] = info
    pos_t = info.T.astype(jnp.int32)
    row = lax.broadcasted_iota(jnp.int32, (slot_rows, tr), 0)
    sel = jnp.where(row == pos_t[0:1], 1.0, jnp.where(row == pos_t[1:2], 1.0, 0.0)).astype(BF16)
    xs_ref[...] = _dot(sel, hb).astype(BF16).reshape(xs_ref.shape)
    cnt_ref[0] = jnp.broadcast_to(cnt, (8, LANES))


def _token_specs(x2d, x_tail, tr, index, rows=None):
    d = x2d.shape[1]
    nt = (x2d.shape[0] if rows is None else rows) // tr
    if x_tail is None:
        return nt, [pl.BlockSpec((tr, d), index(lambda i: (i, 0)))], [x2d]
    assert x_tail.shape == (tr, d)
    return nt + 1, [pl.BlockSpec((tr, d), index(lambda i: (jnp.minimum(i, nt - 1), 0))),
                    pl.BlockSpec((tr, d), index(lambda i: (0, 0)))], [x2d, x_tail]


def _moe_route(x2d, x_tail, gain, mods, tiles_per_mod, w_r, b_r, tr, mix=None):
    d = x2d.shape[1]
    rows = None if mix is None else mix[0].shape[0] * mix[0].shape[1]
    nt, x_specs, x_args = _token_specs(x2d, x_tail, tr, lambda f: f, rows)
    t = nt * tr
    sr = _slot_rows(tr)
    mix_specs, mix_args, mix_out_specs, mix_out_shapes = [], [], [], []
    if mix is not None:
        c3d, d3d, w_out = mix
        tpb = c3d.shape[1] // tr
        half = lambda a: pl.BlockSpec((1, tr, a.shape[2]), lambda i: (i // tpb, i % tpb, 0))
        mix_specs = [half(c3d), half(d3d), pl.BlockSpec(w_out.shape, lambda i: (0, 0))]
        mix_args = [c3d, d3d, w_out]
        mix_out_specs = [pl.BlockSpec((tr, d), lambda i: (i, 0))]
        mix_out_shapes = [jax.ShapeDtypeStruct((t, d), F32)]
    return pl.pallas_call(
        functools.partial(_route_kernel, slot_rows=sr, has_tail=x_tail is not None, has_mix=mix is not None),
        grid=(nt,),
        in_specs=x_specs + [
            pl.BlockSpec((1, d), lambda i: (0, 0)),
            pl.BlockSpec((1, 6, d), lambda i: (jnp.minimum(i // tiles_per_mod, mods.shape[0] - 1), 0, 0)),
            pl.BlockSpec(w_r.shape, lambda i: (0, 0)),
            pl.BlockSpec((1, LANES), lambda i: (0, 0)),
        ] + mix_specs,
        out_specs=[
            pl.BlockSpec((sr // SLOT_BLOCK, SLOT_BLOCK, d), lambda i: (i, 0, 0)),
            pl.BlockSpec((tr, LANES), lambda i: (i, 0)),
            pl.BlockSpec((1, 8, LANES), lambda i: (i, 0, 0)),
        ] + mix_out_specs,
        out_shape=[
            jax.ShapeDtypeStruct((nt * sr // SLOT_BLOCK, SLOT_BLOCK, d), BF16),
            jax.ShapeDtypeStruct((t, LANES), F32),
            jax.ShapeDtypeStruct((nt, 8, LANES), F32),
        ] + mix_out_shapes,
        compiler_params=_cparams("parallel"),
        name="moe_route",
    )(*x_args, gain.reshape(1, d), mods, w_r, b_r, *mix_args)


def _tables_kernel(cnt_ref, src_ref, inv_ref, exp_ref, valid_ref, *, ntiles, bpt):
    cnt = cnt_ref[...]
    nb = jnp.floor((cnt + (SLOT_BLOCK - 1)) * (1.0 / SLOT_BLOCK))
    i_r = lax.broadcasted_iota(jnp.int32, (LANES, LANES), 0)
    i_c = lax.broadcasted_iota(jnp.int32, (LANES, LANES), 1)
    before = jnp.where(i_r < i_c, 1.0, 0.0)
    upto = jnp.where(i_c <= i_r, 1.0, 0.0)
    first = _dot(nb, before)
    cum = _dot(upto, nb)
    tot = jnp.max(cum, axis=0, keepdims=True)
    steps = jnp.floor((tot + (STEP_BLOCKS - 1)) * (1.0 / STEP_BLOCKS))
    start = STEP_BLOCKS * _dot(jnp.broadcast_to(steps, (LANES, LANES)), before)[0:1]
    pos = start + cum - nb

    sub = 8 * (-(-(MOE_GROUPS + MOE_EXPERTS) // 8))
    first_t, nb_t, pos_t = first.T[:sub], nb.T[:sub], pos.T[:sub]
    as_col = lambda v: jnp.broadcast_to(v, (LANES, LANES)).T[:sub, 0:1]
    start_c, tot_c, span_c = as_col(start), as_col(tot), as_col(STEP_BLOCKS * steps)

    inv_ref[...] = jnp.zeros(inv_ref.shape, jnp.int32)
    local = lax.broadcasted_iota(jnp.int32, (sub, LANES), 1).astype(F32)
    nsrc = src_ref.shape[1]
    j = lax.broadcasted_iota(jnp.int32, (sub, nsrc), 1).astype(F32)
    acc = jnp.zeros((1, nsrc), F32)
    for i in range(ntiles):
        f_i, n_i, p_i = first_t[:, i:i + 1], nb_t[:, i:i + 1], pos_t[:, i:i + 1]
        own = (local >= f_i) & (local < f_i + n_i)
        inv_ref[i:i + 1, :] = jnp.sum(jnp.where(own, p_i + (local - f_i), 0.0), axis=0, keepdims=True).astype(jnp.int32)
        own = (j >= p_i) & (j < p_i + n_i)
        acc = acc + jnp.sum(jnp.where(own, (i * bpt) + f_i + (j - p_i), 0.0), axis=0, keepdims=True)
    src_ref[...] = acc.astype(jnp.int32)

    nst = exp_ref.shape[1]
    at = STEP_BLOCKS * lax.broadcasted_iota(jnp.int32, (sub, nst), 1).astype(F32)
    expert = (lax.broadcasted_iota(jnp.int32, (sub, nst), 0) - MOE_GROUPS).astype(F32)
    inside = (at >= start_c) & (at < start_c + span_c)
    exp_ref[...] = jnp.sum(jnp.where(inside, expert, 0.0), axis=0, keepdims=True).astype(jnp.int32)
    occupied = jnp.where(inside, jnp.where(at - start_c < tot_c, 1.0, 0.0), 0.0)
    valid_ref[...] = jnp.sum(occupied, axis=0, keepdims=True).astype(jnp.int32)


def _expert_tables(cnt, bpt, nsteps):
    ntiles = cnt.shape[0]
    assert bpt <= LANES and ntiles <= LANES
    cnt = jnp.pad(cnt, ((0, LANES - ntiles), (0, 0)))
    nsrc = -(-nsteps * STEP_BLOCKS // LANES) * LANES
    nst = -(-nsteps // LANES) * LANES
    i32 = lambda *s: jax.ShapeDtypeStruct(s, jnp.int32)
    src, inv, step_e, valid = pl.pallas_call(
        functools.partial(_tables_kernel, ntiles=ntiles, bpt=bpt),
        out_shape=[i32(1, nsrc), i32(LANES, LANES), i32(1, nst), i32(1, nst)],
        compiler_params=pltpu.CompilerParams(vmem_limit_bytes=VMEM_LIMIT),
        name="moe_tables",
    )(cnt)
    return src.reshape(-1), inv.reshape(-1), step_e.reshape(-1), valid.reshape(-1)


def _block_gather(table_ref, first, nblocks, src_hbm, buf_ref, slot, sem_ref):
    return [pltpu.make_async_copy(src_hbm.at[table_ref[first + kk]],
                                  buf_ref.at[slot, pl.ds(kk * SLOT_BLOCK, SLOT_BLOCK)], sem_ref.at[slot])
            for kk in range(nblocks)]


def _experts_kernel(src_ref, exp_ref, valid_ref, xs_hbm, wg_ref, wu_ref, wd_ref, y_ref, xbuf_ref, sem_ref, wgb_ref,
                    wub_ref, wdb_ref):
    s = pl.program_id(0)
    slot = s % 2

    def gather(step, to_slot):
        return _block_gather(src_ref, step * STEP_BLOCKS, STEP_BLOCKS, xs_hbm, xbuf_ref, to_slot, sem_ref)

    @pl.when((s == 0) & (valid_ref[0] > 0))
    def _():
        for cp in gather(0, 0):
            cp.start()

    nxt = jnp.minimum(s + 1, pl.num_programs(0) - 1)

    @pl.when((s + 1 < pl.num_programs(0)) & (valid_ref[nxt] > 0))
    def _():
        for cp in gather(s + 1, 1 - slot):
            cp.start()

    @pl.when((s == 0) | (exp_ref[s] != exp_ref[jnp.maximum(s - 1, 0)]))
    def _():
        wgb_ref[...] = wg_ref[0, 0, 0].astype(BF16)
        wub_ref[...] = wu_ref[0, 0, 0].astype(BF16)
        wdb_ref[...] = wd_ref[0, 0, 0].astype(BF16)

    @pl.when(valid_ref[s] > 0)
    def _():
        for cp in gather(s, slot):
            cp.wait()
        x = xbuf_ref[slot]
        a = _silu(_dot(x, wgb_ref[...])) * _dot(x, wub_ref[...])
        y_ref[...] = _dot(a.astype(BF16), wdb_ref[...]).astype(BF16).reshape(y_ref.shape)

    @pl.when(valid_ref[s] == 0)
    def _():
        y_ref[...] = jnp.zeros(y_ref.shape, y_ref.dtype)


def _moe_experts(xs, src, step_e, valid, w_gate, w_up, w_down, layer, nsteps):
    xs3 = xs
    d = xs3.shape[-1]
    f = w_gate.shape[-1]
    step_rows = STEP_BLOCKS * SLOT_BLOCK
    w_blk = lambda shape: pl.BlockSpec((1, 1, 1) + shape,
                                       lambda s, sr, ex, va: (layer, ex[s] // MOE_EPG, ex[s] % MOE_EPG, 0, 0))
    grid_spec = pltpu.PrefetchScalarGridSpec(
        num_scalar_prefetch=3,
        grid=(nsteps,),
        in_specs=[pl.BlockSpec(memory_space=pl.ANY), w_blk((d, f)), w_blk((d, f)), w_blk((f, d))],
        out_specs=pl.BlockSpec((STEP_BLOCKS, SLOT_BLOCK, d), lambda s, sr, ex, va: (s, 0, 0)),
        scratch_shapes=[pltpu.VMEM((2, step_rows, d), BF16), pltpu.SemaphoreType.DMA((2,)),
                        pltpu.VMEM((d, f), BF16), pltpu.VMEM((d, f), BF16), pltpu.VMEM((f, d), BF16)],
    )
    return pl.pallas_call(
        _experts_kernel,
        grid_spec=grid_spec,
        out_shape=jax.ShapeDtypeStruct((nsteps * STEP_BLOCKS, SLOT_BLOCK, d), BF16),
        compiler_params=_cparams("arbitrary"),
        name="moe_experts",
    )(src, step_e, valid, xs3, w_gate, w_up, w_down)


def _combine_kernel(inv_ref, x_ref, xt_ref, info_ref, m_ref, fg_ref, ys_hbm, o_ref, ybuf_ref, sem_ref, *, bpt, final):
    i = pl.program_id(0)
    slot = i % 2
    tr = x_ref.shape[0]

    def gather(tile, to_slot):
        return _block_gather(inv_ref, tile * LANES, bpt, ys_hbm, ybuf_ref, to_slot, sem_ref)

    @pl.when(i == 0)
    def _():
        for cp in gather(0, 0):
            cp.start()

    @pl.when(i + 1 < pl.num_programs(0))
    def _():
        for cp in gather(i + 1, 1 - slot):
            cp.start()

    info = info_ref[...]
    col = lax.broadcasted_iota(jnp.int32, (tr, bpt * SLOT_BLOCK), 1)
    wsel = jnp.where(col == info[:, 0:1].astype(jnp.int32), info[:, 2:3],
                     jnp.where(col == info[:, 1:2].astype(jnp.int32), info[:, 3:4], 0.0))
    for cp in gather(i, slot):
        cp.wait()
    y = _dot(wsel.astype(BF16), ybuf_ref[slot])
    out = _tile_tokens(x_ref, xt_ref) + m_ref[0, 5:6, :] * y
    if final:
        out = _rmsnorm(out, fg_ref[...])
    o_ref[...] = out


def _moe_combine(x2d, x_tail, ys, inv, info, mods, tiles_per_mod, final_g, tr, bpt, final):
    d = x2d.shape[1]
    nt, x_specs, x_args = _token_specs(x2d, x_tail, tr, lambda f: (lambda i, iv: f(i)))
    t = nt * tr
    ys3 = ys
    kern = _combine_kernel
    if x_tail is None:
        kern = lambda inv_ref, x_ref, *refs, **kw: _combine_kernel(inv_ref, x_ref, None, *refs, **kw)
    grid_spec = pltpu.PrefetchScalarGridSpec(
        num_scalar_prefetch=1,
        grid=(nt,),
        in_specs=x_specs + [
            pl.BlockSpec((tr, LANES), lambda i, iv: (i, 0)),
            pl.BlockSpec((1, 6, d), lambda i, iv: (jnp.minimum(i // tiles_per_mod, mods.shape[0] - 1), 0, 0)),
            pl.BlockSpec((1, d), lambda i, iv: (0, 0)),
            pl.BlockSpec(memory_space=pl.ANY),
        ],
        out_specs=pl.BlockSpec((tr, d), lambda i, iv: (i, 0)),
        scratch_shapes=[pltpu.VMEM((2, bpt * SLOT_BLOCK, d), BF16), pltpu.SemaphoreType.DMA((2,))],
    )
    return pl.pallas_call(
        functools.partial(kern, bpt=bpt, final=final),
        grid_spec=grid_spec,
        out_shape=jax.ShapeDtypeStruct((t, d), F32),
        compiler_params=_cparams("arbitrary"),
        name="moe_combine",
    )(inv, *x_args, info, mods, final_g.reshape(1, d), ys3)


def _moe(x2d, x_tail, gain, mods, tiles_per_mod, params, final_g, final, tr, mix=None):
    w_r, b_r, w_gate, w_up, w_down, layer = params
    if mix is None:
        xs, info, cnt = _moe_route(x2d, x_tail, gain, mods, tiles_per_mod, w_r, b_r, tr)
    else:
        xs, info, cnt, x2d = _moe_route(x2d, x_tail, gain, mods, tiles_per_mod, w_r, b_r, tr, mix)
    nt = x2d.shape[0] // tr + (x_tail is not None)
    bpt = _slot_rows(tr) // SLOT_BLOCK
    nsteps = -(-(nt * bpt + MOE_EXPERTS * (STEP_BLOCKS - 1)) // STEP_BLOCKS)
    src, inv, step_e, valid = _expert_tables(cnt[:, 0, :], bpt, nsteps)
    ys = _moe_experts(xs, src, step_e, valid, w_gate, w_up, w_down, layer, nsteps)
    return _moe_combine(x2d, x_tail, ys, inv, info, mods, tiles_per_mod, final_g, tr, bpt, final)


def _cd_in_kernel(x_ref, g_ref, m_ref, w_ref, cos_ref, sin_ref, qg_ref, kg_ref, wq_ref, wqp_ref, wk_ref, wv_ref,
                  vone_ref, ua_ref, *o_refs, need_q, q_scale):
    h = _rmsnorm(x_ref[0], g_ref[...]) * (1.0 + m_ref[0, 1:2, :]) + m_ref[0, 0:1, :]
    hb = h.astype(BF16)
    na_w = ua_ref.shape[2]
    ua_ref[0] = _dot(hb, w_ref[:, :na_w]).astype(ua_ref.dtype)
    ub = _dot(hb, w_ref[:, na_w:])
    o = MLA_Q_RANK + MLA_KV_RANK
    cq_raw, ckv_raw, kr, krp = ub[:, :MLA_Q_RANK], ub[:, MLA_Q_RANK:o], ub[:, o:o + MLA_PAD], ub[:, o + MLA_PAD:]
    cos = cos_ref[...]
    sin = sin_ref[...]
    ckv = _rmsnorm(ckv_raw, kg_ref[...]).astype(BF16)
    k_rope = kr * cos + krp * sin
    kn = _dot(ckv, wk_ref[...])
    if need_q:
        q_ref, k_ref, v_ref = o_refs
    else:
        k_ref, v_ref = o_refs
    vx = _dot(ckv, wv_ref[...]) + vone_ref[...]
    for h in range(D_HEADS):
        hs = slice(h * MLA_PAD, (h + 1) * MLA_PAD)
        k_ref[0, h] = (kn[:, hs] + k_rope).astype(BF16)
        v_ref[0, h] = vx[:, hs].T[:MLA_VROWS].astype(BF16)
    if need_q:
        cq = _rmsnorm(cq_raw, qg_ref[...]).astype(BF16)
        qm = _dot(cq, wq_ref[...])
        qp = _dot(cq, wqp_ref[...])
        for h in range(D_HEADS):
            hs = slice(h * MLA_PAD, (h + 1) * MLA_PAD)
            q_ref[0, h] = ((qm[:, hs] * cos + qp[:, hs] * sin) * q_scale).T.astype(BF16)


def _cd_in_proj(x2d, flat, gain, mods, w_cat, na_w, cos, sin, q_g, kv_g, wq, wqp, wk, wv, need_q, tm):
    b, n, row0 = flat
    d = x2d.shape[-1]
    tm = min(tm, n)
    row_major = (jax.ShapeDtypeStruct((b, D_HEADS, n, MLA_PAD), BF16),
                 pl.BlockSpec((1, D_HEADS, tm, MLA_PAD), lambda bi, i: (bi, 0, i, 0)))
    col_major = (jax.ShapeDtypeStruct((b, D_HEADS, MLA_PAD, n), BF16),
                 pl.BlockSpec((1, D_HEADS, MLA_PAD, tm), lambda bi, i: (bi, 0, 0, i)))
    v_major = (jax.ShapeDtypeStruct((b, D_HEADS, MLA_VROWS, n), BF16),
               pl.BlockSpec((1, D_HEADS, MLA_VROWS, tm), lambda bi, i: (bi, 0, 0, i)))
    ua = (jax.ShapeDtypeStruct((b, n, na_w), BF16), pl.BlockSpec((1, tm, na_w), lambda bi, i: (bi, i, 0)))
    outs, specs = zip(*([ua] + ([col_major] if need_q else []) + [row_major, v_major]))
    full = lambda a: pl.BlockSpec(a.shape, lambda bi, i: (0,) * a.ndim)
    vone = jnp.tile(jnp.concatenate([jnp.zeros((1, MLA_V), F32), jnp.ones((1, MLA_PAD - MLA_V), F32)], axis=1),
                    (1, D_HEADS))
    q_scale = float((MLA_NOPE + MLA_ROPE) ** -0.5 * np.log2(np.e))
    return pl.pallas_call(
        functools.partial(_cd_in_kernel, need_q=need_q, q_scale=q_scale),
        grid=(b, n // tm),
        in_specs=[
            pl.BlockSpec((1, tm, d), lambda bi, i: (0, row0 // tm + bi * (n // tm) + i, 0)),
            pl.BlockSpec((1, d), lambda bi, i: (0, 0)),
            pl.BlockSpec((1, 6, d), lambda bi, i: (bi, 0, 0)),
            full(w_cat),
            pl.BlockSpec((tm, MLA_PAD), lambda bi, i: (i, 0)),
            pl.BlockSpec((tm, MLA_PAD), lambda bi, i: (i, 0)),
            full(q_g), full(kv_g), full(wq), full(wqp), full(wk), full(wv), full(vone),
        ],
        out_specs=list(specs),
        out_shape=list(outs),
        compiler_params=_cparams("parallel", "parallel"),
        name="cd_in_proj",
    )(x2d.reshape(1, -1, d), gain.reshape(1, d), mods, w_cat, cos, sin, q_g, kv_g, wq, wqp, wk, wv, vone)


def _mla_attn_kernel(q_ref, qn_ref, kc_ref, kl_ref, vc_ref, vl_ref, o_ref, acc0_ref, acc1_ref, s0_ref, s1_ref, m_ref,
                     *, tk):
    tq = q_ref.shape[3]
    ncc = kc_ref.shape[2] // tk
    nchunks = ncc + kl_ref.shape[2] // tk
    neg = jnp.full((8, tq), NEG, F32)
    s_refs = (s0_ref, s1_ref)
    acc_refs = (acc0_ref, acc1_ref)

    def chunk(c):
        part = (kc_ref, vc_ref, c) if c < ncc else (kl_ref, vl_ref, c - ncc)
        return part[0], part[1], pl.ds(part[2] * tk, tk), pl.ds(c * tk, tk)

    def scores(k_ref, ks, ss, hh, q, m):
        s = _dot(k_ref[0, hh, ks, :], q)
        s_refs[hh][ss, :] = s
        return jnp.maximum(m, jnp.max(s.reshape(tk // 8, 8, tq), axis=0))

    def weight(v_ref, ks, ss, hh, m_row):
        p = jnp.exp2((s_refs[hh][ss, :] - m_row).astype(BF16))
        acc_refs[hh][...] += _dot(v_ref[0, hh, :, ks], p)

    @pl.when(pl.program_id(2) == 0)
    def _():
        m = neg
        for c in range(ncc):
            k_ref, _, ks, ss = chunk(c)
            m = scores(k_ref, ks, ss, 0, q_ref[0, 0], m)

        def latent(c, m):
            ks = pl.ds(pl.multiple_of(c * tk, tk), tk)
            ss = pl.ds(pl.multiple_of((c + ncc) * tk, tk), tk)
            return scores(kl_ref, ks, ss, 0, q_ref[0, 0], m)

        m_ref[...] = lax.fori_loop(0, nchunks - ncc, latent, m)

    acc0_ref[...] = jnp.zeros(acc0_ref.shape, F32)
    acc1_ref[...] = jnp.zeros(acc1_ref.shape, F32)
    m0 = jnp.max(m_ref[...], axis=0, keepdims=True)
    m1 = neg
    for c in range(nchunks):
        k_ref, v_ref, ks, ss = chunk(c)
        weight(v_ref, ks, ss, 0, m0)
        m1 = scores(k_ref, ks, ss, 1, q_ref[0, 1], m1)
    m1 = jnp.max(m1, axis=0, keepdims=True)
    m0_next = neg
    for c in range(nchunks):
        k_ref, v_ref, ks, ss = chunk(c)
        weight(v_ref, ks, ss, 1, m1)
        m0_next = scores(k_ref, ks, ss, 0, qn_ref[0, 0], m0_next)
    m_ref[...] = m0_next
    o_t = jnp.concatenate([a[:MLA_V] / a[MLA_V:MLA_V + 1] for a in acc_refs], axis=0)
    o_ref[0] = o_t.T.astype(o_ref.dtype)


def _mla_attention(q_t, k_ctx, k_lat, v_ctx, v_lat, tq, tk):
    b, h, _, n = q_t.shape
    nc = k_ctx.shape[2]
    tq = min(tq, n)
    assert nc % tk == 0 and n % tk == 0
    k_spec = lambda rows: pl.BlockSpec((1, 2, rows, MLA_PAD), lambda bi, hp, i: (bi, hp, 0, 0))
    v_spec = lambda rows: pl.BlockSpec((1, 2, MLA_VROWS, rows), lambda bi, hp, i: (bi, hp, 0, 0))
    return pl.pallas_call(
        functools.partial(_mla_attn_kernel, tk=tk),
        grid=(b, h // 2, n // tq),
        in_specs=[
            pl.BlockSpec((1, 2, MLA_PAD, tq), lambda bi, hp, i: (bi, hp, 0, i)),
            pl.BlockSpec((1, 2, MLA_PAD, tq), lambda bi, hp, i: (bi, hp, 0, jnp.minimum(i + 1, n // tq - 1))),
            k_spec(nc), k_spec(n), v_spec(nc), v_spec(n),
        ],
        out_specs=pl.BlockSpec((1, tq, 2 * MLA_V), lambda bi, hp, i: (bi, i, hp)),
        out_shape=jax.ShapeDtypeStruct((b, n, h * MLA_V), BF16),
        scratch_shapes=[pltpu.VMEM((MLA_VROWS, tq), F32), pltpu.VMEM((MLA_VROWS, tq), F32),
                        pltpu.VMEM((nc + n, tq), F32), pltpu.VMEM((nc + n, tq), F32), pltpu.VMEM((8, tq), F32)],
        compiler_params=_cparams("parallel", "parallel", "arbitrary"),
        name="mla_attention",
    )(q_t, q_t, k_ctx, k_lat, v_ctx, v_lat)


def _na_kernel(q_ref, *refs):
    k_refs, v_refs = refs[:NA_KBLOCKS + 1], refs[NA_KBLOCKS + 1:2 * NA_KBLOCKS + 2]
    tab_ref, o_ref = refs[2 * NA_KBLOCKS + 2:]
    tq = q_ref.shape[1]
    nloc = tab_ref.shape[3]
    lane = lax.broadcasted_iota(jnp.int32, (tq, LANES), 1)
    q = q_ref[0]
    k_all = jnp.concatenate([r[0] for r in k_refs], axis=0)
    v_all = jnp.concatenate([r[0] for r in v_refs], axis=0)
    outs = []
    for hh in range(2):
        in_head = (lane >= hh * C_HEAD_DIM) & (lane < (hh + 1) * C_HEAD_DIM)
        qh = jnp.where(in_head, q, jnp.zeros_like(q))
        s = lax.dot_general(qh, k_all, NT, preferred_element_type=F32)
        s_loc = s[:, :nloc] + tab_ref[0, hh]
        s_ctx = s[:, nloc:]
        m = jnp.maximum(jnp.max(s_loc, axis=-1, keepdims=True), jnp.max(s_ctx, axis=-1, keepdims=True))
        p_loc = jnp.exp(s_loc - m)
        p_ctx = jnp.exp(s_ctx - m)
        l = jnp.sum(p_loc, axis=-1, keepdims=True) + jnp.sum(p_ctx, axis=-1, keepdims=True)
        o = _dot(p_loc.astype(BF16), v_all[:nloc]) + _dot(p_ctx.astype(BF16), v_all[nloc:])
        outs.append(o / l)
    o_ref[0] = jnp.where(lane < C_HEAD_DIM, outs[0], outs[1]).astype(o_ref.dtype)


def _na_tables(rpb, rows):
    h = rpb.shape[0]
    w = GRID_W
    kr_n = NA_QROWS + NA_ROWS
    qc = np.arange(w)
    kc = np.arange(w)
    cs = np.clip(qc - NA_COLS // 2, 0, w - NA_COLS)
    col_ok = (kc[None, :] >= cs[:, None]) & (kc[None, :] < cs[:, None] + NA_COLS)
    dc = np.clip(kc[None, :] - qc[:, None] + (NA_COLS - 1), 0, 2 * NA_COLS - 2)
    pick_dc = (dc.reshape(-1)[None, :] == np.arange(2 * NA_COLS - 1)[:, None]).astype(np.float32)
    base = jnp.einsum('hrd,dx->hrx', rpb.astype(F32), jnp.asarray(pick_dc), precision=lax.Precision.HIGHEST)
    nblk = rows // NA_QROWS
    picks, oks = [], []
    for m in (0, 1, nblk - 1):
        qr = NA_QROWS * m + np.arange(NA_QROWS)
        rs = np.clip(qr - NA_ROWS // 2, 0, rows - NA_ROWS)
        kr = NA_QROWS * m - NA_ROWS // 2 + np.arange(kr_n)
        row_ok = (kr[None, :] >= rs[:, None]) & (kr[None, :] < rs[:, None] + NA_ROWS)
        dr = np.clip(kr[None, :] - qr[:, None] + (NA_ROWS - 1), 0, 2 * NA_ROWS - 2)
        picks.append((dr.reshape(-1)[:, None] == np.arange(2 * NA_ROWS - 1)[None, :]).astype(np.float32))
        oks.append(row_ok[:, None, :, None] & col_ok[None, :, None, :])
    t = jnp.einsum('vr,hrx->hvx', jnp.asarray(np.concatenate(picks)), base, precision=lax.Precision.HIGHEST)
    t = t.reshape(h, 3, NA_QROWS, kr_n, w, w).transpose(1, 0, 2, 4, 3, 5)
    t = jnp.where(jnp.asarray(np.stack(oks))[:, None], t, NEG)
    return t.reshape(3, h, NA_QROWS * w, kr_n * w)


def _na_attention(u_lat, u_ctx, tabs):
    b, n, _ = u_lat.shape
    nc = u_ctx.shape[1]
    tq = NA_QROWS * GRID_W
    tkb = NA_KBLOCK_ROWS * GRID_W
    nblk = n // tq
    nkb = n // tkb
    per_q = NA_QROWS // NA_KBLOCK_ROWS
    assert NA_KBLOCKS == per_q + 2 and NA_ROWS // 2 == NA_KBLOCK_ROWS and n % tq == 0 and n // tq >= 2
    pairs = C_HEADS // 2
    q_spec = pl.BlockSpec((1, tq, LANES), lambda bi, hp, i: (bi, i, hp))
    kblk = lambda col0, j: pl.BlockSpec(
        (1, tkb, LANES), lambda bi, hp, i: (bi, jnp.clip(i * per_q - 1 + j, 0, nkb - 1), col0 + hp))
    ctx = lambda col0: pl.BlockSpec((1, nc, LANES), lambda bi, hp, i: (bi, 0, col0 + hp))
    sel = lambda i: jnp.where(i == 0, 0, jnp.where(i == nblk - 1, 2, 1))
    kv_specs = [kblk(col0, j) for col0 in (pairs, 2 * pairs) for j in range(NA_KBLOCKS)]
    kv_specs = kv_specs[:NA_KBLOCKS] + [ctx(pairs)] + kv_specs[NA_KBLOCKS:] + [ctx(2 * pairs)]
    kv_args = [u_lat] * NA_KBLOCKS + [u_ctx]
    return pl.pallas_call(
        _na_kernel,
        grid=(b, pairs, nblk),
        in_specs=[q_spec] + kv_specs + [
            pl.BlockSpec((1, 2, tq, NA_KBLOCKS * tkb), lambda bi, hp, i: (sel(i), hp, 0, 0)),
        ],
        out_specs=pl.BlockSpec((1, tq, LANES), lambda bi, hp, i: (bi, i, hp)),
        out_shape=jax.ShapeDtypeStruct((b, n, C_WIDTH), BF16),
        compiler_params=_cparams("parallel", "parallel", "arbitrary"),
        name="na_attention",
    )(u_lat, *kv_args, *kv_args, tabs)


def _moe_params(w_rg, b_rg, w_re, b_re, w_gate, w_up, w_down, layer):
    d = w_rg.shape[0]
    w_r = jnp.zeros((d, LANES), F32).at[:, :MOE_GROUPS].set(w_rg).at[:, MOE_GROUPS:MOE_GROUPS + MOE_EXPERTS].set(w_re)
    b_r = jnp.zeros((1, LANES), F32).at[0, :MOE_GROUPS].set(b_rg).at[0, MOE_GROUPS:MOE_GROUPS + MOE_EXPERTS].set(b_re)
    w_hi = w_r.astype(BF16)
    w_lo = (w_r - w_hi.astype(F32)).astype(BF16)
    return jnp.concatenate([w_hi, w_lo], axis=1), b_r, w_gate, w_up, w_down, layer


def _rope_perm():
    j = np.arange(MLA_ROPE)
    half = MLA_ROPE // 2
    return (j // half) * half + (j % half + half // 2) % half


def _rope_tables(n):
    half = MLA_ROPE // 2
    nf = half // 2
    t = np.arange(n)
    inv = (np.float32(ROPE_THETA) ** (-np.arange(nf, dtype=np.float32) / np.float32(nf))).astype(np.float32)
    parts_c, parts_s = [], []
    for pos in ((t // GRID_W).astype(np.float32), (t % GRID_W).astype(np.float32)):
        ang = (pos[:, None] * inv[None, :]).astype(np.float32)
        c, s = np.cos(ang).astype(np.float32), np.sin(ang).astype(np.float32)
        parts_c += [c, c]
        parts_s += [-s, s]
    pad = MLA_PAD - MLA_NOPE - MLA_ROPE
    cos = np.concatenate([np.ones((n, MLA_NOPE), np.float32)] + parts_c + [np.zeros((n, pad), np.float32)], axis=1)
    sin = np.concatenate([np.zeros((n, MLA_NOPE), np.float32)] + parts_s + [np.zeros((n, pad), np.float32)], axis=1)
    return jnp.asarray(cos), jnp.asarray(sin)


def _identity_rope_tables(n):
    pad = MLA_PAD - MLA_NOPE - MLA_ROPE
    cos = jnp.concatenate([jnp.ones((n, MLA_NOPE + MLA_ROPE), F32), jnp.zeros((n, pad), F32)], axis=1)
    return cos, jnp.zeros((n, MLA_PAD), F32)


def _pad_heads(w, widths, src_cols, dst_off):
    rank = w.shape[0]
    out = jnp.zeros((rank, D_HEADS, MLA_PAD), F32)
    wh = w.reshape(rank, D_HEADS, widths)[:, :, src_cols]
    return out.at[:, :, dst_off:dst_off + len(src_cols)].set(wh).reshape(rank, D_HEADS * MLA_PAD)


def _cd_params(w_in, w_uq, w_ukv):
    d = w_in.shape[0]
    perm = _rope_perm()
    o = 3 * C_WIDTH
    q_scale = float(C_HEAD_DIM ** -0.5)
    kr = w_in[:, o + MLA_Q_RANK + MLA_KV_RANK:]
    pad_rope = lambda a: jnp.zeros((d, MLA_PAD), F32).at[:, MLA_NOPE:MLA_NOPE + MLA_ROPE].set(a)
    w_cat = jnp.concatenate([
        w_in[:, :C_WIDTH] * q_scale, w_in[:, C_WIDTH:o],
        w_in[:, o:o + MLA_Q_RANK + MLA_KV_RANK], pad_rope(kr), pad_rope(kr[:, perm]),
    ], axis=1).astype(BF16)
    qw = MLA_NOPE + MLA_ROPE
    nope = np.arange(MLA_NOPE)
    rope = MLA_NOPE + np.arange(MLA_ROPE)
    wq = (_pad_heads(w_uq, qw, nope, 0) + _pad_heads(w_uq, qw, rope, MLA_NOPE)).astype(BF16)
    wqp = _pad_heads(w_uq, qw, rope[perm], MLA_NOPE).astype(BF16)
    kvw = MLA_NOPE + MLA_V
    wk = _pad_heads(w_ukv, kvw, nope, 0).astype(BF16)
    wv = _pad_heads(w_ukv, kvw, MLA_NOPE + np.arange(MLA_V), 0).astype(BF16)
    return w_cat, wq, wqp, wk, wv


def kernel(x, c, ctx, c_ctx, ada_w, ada_b, norm1_g, norm2_g, ab_w_in, ab_w_out, hgrn_lb_logits, hgrn_onorm_g, pool_w,
           pool_scale, cd_w_in, cd_w_out, na_rpb, mla_q_norm_g, mla_w_uq, mla_kv_norm_g, mla_w_ukv, moe_w_rg, moe_b_rg,
           moe_w_re, moe_b_re, moe_w_gate, moe_w_up, moe_w_down, final_norm_g):
    b, n, d = x.shape
    n_ctx = ctx.shape[1]
    assert ada_w.shape[0] == 2 and ab_w_in.shape[0] == 1 and cd_w_in.shape[0] == 1 and b + 1 <= 8
    tm = TOKEN_TILE

    cc = jnp.zeros((8, d), F32).at[:b].set(c).at[b].set(c_ctx)
    mods = _ada(cc, b + 1, ada_w, ada_b).reshape(2, 8, 6, d)
    mods_lat = [mods[l, :b] for l in range(2)]
    mods_ctx = [jnp.broadcast_to(mods[l, b:b + 1], (b, 6, d)) for l in range(2)]
    lb = jnp.cumsum(jax.nn.softmax(hgrn_lb_logits.astype(F32), axis=1), axis=1)[:, 0]

    w_in0 = ab_w_in[0].astype(BF16)
    w_out0 = ab_w_out[0].astype(BF16)
    pw0 = pool_w[0].astype(BF16)
    ab_cols = w_in0.shape[1]
    (u_ctx,) = _in_proj(ctx, norm1_g[0], mods_ctx[0], w_in0, ((0, ab_cols),), (F32,), tm)
    (u_lat,) = _in_proj(x, norm1_g[0], mods_lat[0], w_in0, ((0, ab_cols),), (F32,), tm)
    s0 = jnp.zeros((b, 2, A_HEADS, A_HEAD_DIM, A_HEAD_DIM), F32)
    ocf, ocb, s_ctx = _hgrn_scan(u_ctx, lb, s0, HGRN_ROWS)
    olf, olb, _ = _hgrn_scan(u_lat, lb, s_ctx, HGRN_ROWS)
    t_lat, t_ctx = b * n, b * n_ctx
    assert n % tm == 0 and t_ctx == tm
    x_lat = _ab_out(olf, olb, u_lat, x, mods_lat[0], hgrn_onorm_g[0], pw0, pool_scale[0], w_out0, tm)
    x_ctx = _ab_out(ocf, ocb, u_ctx, ctx, mods_ctx[0], hgrn_onorm_g[0], pw0, pool_scale[0], w_out0, tm)
    moe0 = _moe_params(moe_w_rg[0], moe_b_rg[0], moe_w_re[0], moe_b_re[0], moe_w_gate, moe_w_up, moe_w_down, 0)
    mods_all = jnp.concatenate([mods_lat[0], mods[0, b:b + 1]], axis=0)
    xa = _moe(x_lat, x_ctx, norm2_g[0], mods_all, n // tm, moe0, final_norm_g, False, tm)

    w_cat, wq, wqp, wk, wv = _cd_params(cd_w_in[0], mla_w_uq[0], mla_w_ukv[0])
    na_w = 3 * C_WIDTH
    q_g = mla_q_norm_g[0].reshape(1, -1)
    kv_g = mla_kv_norm_g[0].reshape(1, -1)
    cos_l, sin_l = _rope_tables(n)
    cos_c, sin_c = _identity_rope_tables(n_ctx)
    ua_ctx, k_c, v_c = _cd_in_proj(xa, (b, n_ctx, t_lat), norm1_g[1], mods_ctx[1], w_cat, na_w, cos_c, sin_c, q_g, kv_g,
                                   wq, wqp, wk, wv, False, tm)
    ua_lat, q_l, k_l, v_l = _cd_in_proj(xa, (b, n, 0), norm1_g[1], mods_lat[1], w_cat, na_w, cos_l, sin_l, q_g, kv_g,
                                        wq, wqp, wk, wv, True, tm)
    d_lat = _mla_attention(q_l, k_c, k_l, v_c, v_l, MLA_Q_TILE, MLA_K_CHUNK)
    c_lat = _na_attention(ua_lat, ua_ctx, _na_tables(na_rpb[0], n // GRID_W))
    moe1 = _moe_params(moe_w_rg[1], moe_b_rg[1], moe_w_re[1], moe_b_re[1], moe_w_gate, moe_w_up, moe_w_down, 1)
    mix = (c_lat, d_lat, cd_w_out[0].astype(BF16))
    out = _moe(xa, None, norm2_g[1], mods_lat[1], n // tm, moe1, final_norm_g, True, tm, mix)
    return out.reshape(b, n, d)
```

```python
import functools

import numpy as np
import jax
import jax.numpy as jnp
from jax import lax
from jax.experimental import pallas as pl
from jax.experimental.pallas import tpu as pltpu

F32 = jnp.float32
BF16 = jnp.bfloat16

EPS = 1e-6
NEG = -1e30

GRID_W = 64
A_HEADS = 4
A_HEAD_DIM = 128
A_WIDTH = A_HEADS * A_HEAD_DIM
POOL_WINDOWS = (2, 4, 8, 16)
B_GROUP = 128
B_WIDTH = B_GROUP * len(POOL_WINDOWS)
POOL_HALO = 16
C_HEADS = 8
C_HEAD_DIM = 64
C_WIDTH = C_HEADS * C_HEAD_DIM
NA_ROWS = 8
NA_COLS = 16
NA_QROWS = 4
NA_KBLOCK_ROWS = 4
NA_KBLOCKS = 3
D_HEADS = 8
MLA_Q_RANK = 256
MLA_KV_RANK = 128
MLA_NOPE = 64
MLA_ROPE = 32
MLA_V = 64
MLA_PAD = 128
MLA_VROWS = 80
ROPE_THETA = 10000.0
MOE_GROUPS = 4
MOE_EPG = 8
MOE_EXPERTS = MOE_GROUPS * MOE_EPG
MOE_HIDDEN = 256
LANES = 128
SLOT_BLOCK = 16
STEP_BLOCKS = 32
VMEM_LIMIT = 56 * 1024 * 1024
TOKEN_TILE = 512
HGRN_ROWS = 256
MLA_Q_TILE = 256
MLA_K_CHUNK = 256

NT = (((1,), (1,)), ((), ()))
TN = (((0,), (0,)), ((), ()))


def _cparams(*sem):
    return pltpu.CompilerParams(dimension_semantics=sem, vmem_limit_bytes=VMEM_LIMIT)


def _sigmoid(x):
    return 1.0 / (1.0 + jnp.exp(-x))


def _silu(x):
    return x * _sigmoid(x)


def _dot(a, b):
    return jnp.dot(a, b, preferred_element_type=F32)


def _rmsnorm(x, g):
    return x * lax.rsqrt(jnp.mean(x * x, axis=-1, keepdims=True) + EPS) * g


def _ada_kernel(ct_ref, w_ref, b_ref, o_ref, *, nrows):
    s = _silu(ct_ref[...])
    w = w_ref[0]
    rows = [jnp.sum(s[:, r:r + 1] * w, axis=0, keepdims=True) + b_ref[0] for r in range(nrows)]
    rows.append(jnp.zeros((o_ref.shape[1] - nrows, w.shape[1]), F32))
    o_ref[0] = jnp.concatenate(rows, axis=0)


def _ada(cc, nrows, ada_w, ada_b):
    depth, d, n6 = ada_w.shape
    tn = n6 // 8
    return pl.pallas_call(
        functools.partial(_ada_kernel, nrows=nrows),
        grid=(depth, n6 // tn),
        in_specs=[
            pl.BlockSpec((d, 8), lambda l, j: (0, 0)),
            pl.BlockSpec((1, d, tn), lambda l, j: (l, 0, j)),
            pl.BlockSpec((1, 1, tn), lambda l, j: (l, 0, j)),
        ],
        out_specs=pl.BlockSpec((1, 8, tn), lambda l, j: (l, 0, j)),
        out_shape=jax.ShapeDtypeStruct((depth, 8, n6), F32),
        compiler_params=_cparams("parallel", "parallel"),
        name="ada_mod",
    )(cc.T, ada_w, ada_b.reshape(depth, 1, n6))


def _in_kernel(x_ref, g_ref, m_ref, w_ref, *o_refs, splits):
    h = _rmsnorm(x_ref[0], g_ref[...]) * (1.0 + m_ref[0, 1:2, :]) + m_ref[0, 0:1, :]
    hb = h.astype(BF16)
    for o_ref, (a, b) in zip(o_refs, splits):
        o_ref[0] = _dot(hb, w_ref[:, a:b]).astype(o_ref.dtype)


def _in_proj(x, gain, mods, w, splits, dtypes, tm):
    b, n, d = x.shape
    tm = min(tm, n)
    outs = [jax.ShapeDtypeStruct((b, n, hi - lo), dt) for (lo, hi), dt in zip(splits, dtypes)]
    return pl.pallas_call(
        functools.partial(_in_kernel, splits=splits),
        grid=(b, n // tm),
        in_specs=[
            pl.BlockSpec((1, tm, d), lambda bi, i: (bi, i, 0)),
            pl.BlockSpec((1, d), lambda bi, i: (0, 0)),
            pl.BlockSpec((1, 6, d), lambda bi, i: (bi, 0, 0)),
            pl.BlockSpec(w.shape, lambda bi, i: (0, 0)),
        ],
        out_specs=[pl.BlockSpec((1, tm, hi - lo), lambda bi, i: (bi, i, 0)) for lo, hi in splits],
        out_shape=outs,
        compiler_params=_cparams("parallel", "parallel"),
        name="in_proj",
    )(x, gain.reshape(1, d), mods, w)


HG_SUB = 64


def _hgrn_direction(q_raw, fz, v, lb, st_ref, d, o_ref, reverse):
    rows = q_raw.shape[0]
    c = HG_SUB
    f = lb + (1.0 - lb) * _sigmoid(fz)
    k = 1.0 - f
    g = jnp.log(f)
    q = _silu(q_raw)
    r_i = lax.broadcasted_iota(jnp.int32, (c, c), 0)
    c_i = lax.broadcasted_iota(jnp.int32, (c, c), 1)
    keep = (c_i >= r_i) if reverse else (c_i <= r_i)
    tri = jnp.where(keep, 1.0, 0.0).astype(BF16)
    order = range(rows // c - 1, -1, -1) if reverse else range(rows // c)
    for ci in order:
        sl = slice(ci * c, (ci + 1) * c)
        gc = g[sl]
        g_hi = gc.astype(BF16)
        g_lo = (gc - g_hi.astype(F32)).astype(BF16)
        bc = _dot(tri, g_hi) + _dot(tri, g_lo)
        ref = bc[c // 2:c // 2 + 1]
        tot = bc[0:1] if reverse else bc[c - 1:c]
        qt = q[sl] * jnp.exp(bc - ref)
        kt = k[sl] * jnp.exp(ref - bc)
        qd = (qt * jnp.exp(ref)).astype(BF16)
        kd = (kt * jnp.exp(tot - ref)).astype(BF16)
        qt = qt.astype(BF16)
        kt = kt.astype(BF16)
        vb = v[sl].astype(BF16)
        dec = jnp.exp(tot)
        for h in range(A_HEADS):
            hs = slice(h * A_HEAD_DIM, (h + 1) * A_HEAD_DIM)
            att = lax.dot_general(qt[:, hs], kt[:, hs], NT, preferred_element_type=F32)
            att = jnp.where(keep, att, 0.0).astype(BF16)
            st = st_ref[d, h]
            o = _dot(att, vb[:, hs]) + lax.dot_general(qd[:, hs], st.astype(BF16), NT, preferred_element_type=F32)
            o_ref[0, sl, hs] = o
            st_ref[d, h] = st * dec[:, hs] + lax.dot_general(vb[:, hs], kd[:, hs], TN, preferred_element_type=F32)


def _hgrn_kernel(qf_ref, ff_ref, vf_ref, qb_ref, fb_ref, vb_ref, lb_ref, s0_ref, of_ref, ob_ref, sfin_ref, st_ref):
    j = pl.program_id(1)

    @pl.when(j == 0)
    def _():
        st_ref[...] = s0_ref[0]

    _hgrn_direction(qf_ref[0], ff_ref[0], vf_ref[0], lb_ref[0:1], st_ref, 0, of_ref, False)
    _hgrn_direction(qb_ref[0], fb_ref[0], vb_ref[0], lb_ref[1:2], st_ref, 1, ob_ref, True)

    @pl.when(j == pl.num_programs(1) - 1)
    def _():
        sfin_ref[0] = st_ref[...]


def _hgrn_scan(u, lb, s0, rows):
    b, n, _ = u.shape
    rows = min(rows, n)
    nb = n // rows
    w = A_WIDTH

    def fwd(col):
        return pl.BlockSpec((1, rows, w), lambda bi, j: (bi, j, col))

    def bwd(col):
        return pl.BlockSpec((1, rows, w), lambda bi, j: (bi, nb - 1 - j, col))

    st_spec = pl.BlockSpec((1, 2, A_HEADS, A_HEAD_DIM, A_HEAD_DIM), lambda bi, j: (bi, 0, 0, 0, 0))
    return pl.pallas_call(
        _hgrn_kernel,
        grid=(b, nb),
        in_specs=[fwd(0), fwd(1), fwd(3), bwd(0), bwd(2), bwd(3), pl.BlockSpec((2, w), lambda bi, j: (0, 0)), st_spec],
        out_specs=[
            pl.BlockSpec((1, rows, w), lambda bi, j: (bi, j, 0)),
            pl.BlockSpec((1, rows, w), lambda bi, j: (bi, nb - 1 - j, 0)),
            st_spec,
        ],
        out_shape=[
            jax.ShapeDtypeStruct((b, n, w), F32),
            jax.ShapeDtypeStruct((b, n, w), F32),
            jax.ShapeDtypeStruct(s0.shape, F32),
        ],
        scratch_shapes=[pltpu.VMEM((2, A_HEADS, A_HEAD_DIM, A_HEAD_DIM), F32)],
        compiler_params=_cparams("parallel", "arbitrary"),
        name="hgrn_scan",
    )(u, u, u, u, u, u, lb, s0)


def _ab_out_kernel(of_ref, ob_ref, ug_ref, up_ref, pprev_ref, pnext_ref, x_ref, m_ref, on_ref, pw_ref, ps_ref,
                   wo_ref, o_ref, *, n):
    i = pl.program_id(1)
    tm = x_ref.shape[1]
    o = of_ref[0] + ob_ref[0]
    gate = _silu(ug_ref[0])
    parts = []
    for h in range(A_HEADS):
        hs = slice(h * A_HEAD_DIM, (h + 1) * A_HEAD_DIM)
        parts.append(_rmsnorm(o[:, hs], on_ref[...]) * gate[:, hs])
    main = up_ref[0]
    prev = jnp.where(i > 0, pprev_ref[0], 0.0)
    nxt = jnp.where(i < pl.num_programs(1) - 1, pnext_ref[0], 0.0)
    ext = jnp.concatenate([prev, main, nxt], axis=0)
    ext_rows = tm + 2 * POOL_HALO
    t = i * tm + lax.broadcasted_iota(jnp.int32, (tm, 1), 0)
    for gi, win in enumerate(POOL_WINDOWS):
        gs = slice(gi * B_GROUP, (gi + 1) * B_GROUP)
        acc = ext[:, gs]
        acc = acc + pltpu.roll(acc, 1, 0)
        half = 1
        while 2 * half < win:
            acc = pltpu.roll(acc, half, 0) + pltpu.roll(acc, ext_rows - half, 0)
            half *= 2
        cnt = jnp.minimum(t + (win - win // 2), n) - jnp.maximum(t - win // 2, 0)
        mean = acc[POOL_HALO:POOL_HALO + tm] / cnt.astype(F32)
        pooled = _dot((mean - main[:, gs]).astype(BF16), pw_ref[gi])
        parts.append(pooled * ps_ref[:, gs])
    mix = jnp.concatenate(parts, axis=-1).astype(BF16)
    o_ref[0] = x_ref[0] + m_ref[0, 2:3, :] * _dot(mix, wo_ref[...])


def _ab_out(o_f, o_b, u, x, mods, onorm_g, pool_w, pool_scale, w_out, tm):
    b, n, d = x.shape
    tm = min(tm, n)
    nt = n // tm
    hb = tm // POOL_HALO
    last_halo = n // POOL_HALO - 1
    w = A_WIDTH
    tile = lambda col: pl.BlockSpec((1, tm, w), lambda bi, i: (bi, i, col))
    out = pl.pallas_call(
        functools.partial(_ab_out_kernel, n=n),
        grid=(b, nt),
        in_specs=[
            tile(0), tile(0), tile(4), tile(5),
            pl.BlockSpec((1, POOL_HALO, w), lambda bi, i: (bi, jnp.maximum(i * hb - 1, 0), 5)),
            pl.BlockSpec((1, POOL_HALO, w), lambda bi, i: (bi, jnp.minimum((i + 1) * hb, last_halo), 5)),
            pl.BlockSpec((1, tm, d), lambda bi, i: (bi, i, 0)),
            pl.BlockSpec((1, 6, d), lambda bi, i: (bi, 0, 0)),
            pl.BlockSpec((1, A_HEAD_DIM), lambda bi, i: (0, 0)),
            pl.BlockSpec(pool_w.shape, lambda bi, i: (0, 0, 0)),
            pl.BlockSpec((1, B_WIDTH), lambda bi, i: (0, 0)),
            pl.BlockSpec(w_out.shape, lambda bi, i: (0, 0)),
        ],
        out_specs=pl.BlockSpec((1, tm, d), lambda bi, i: (bi, i, 0)),
        out_shape=jax.ShapeDtypeStruct((b, n, d), F32),
        compiler_params=_cparams("parallel", "parallel"),
        name="ab_out",
    )(o_f, o_b, u, u, u, u, x, mods, onorm_g.reshape(1, A_HEAD_DIM), pool_w, pool_scale.reshape(1, B_WIDTH), w_out)
    return out.reshape(b * n, d)


def _slot_rows(tr):
    rows = 2 * tr + MOE_EXPERTS * (SLOT_BLOCK - 1)
    assert rows % SLOT_BLOCK == 0
    return rows


def _tile_tokens(x_ref, xt_ref):
    if xt_ref is None:
        return x_ref[...]
    return jnp.where(pl.program_id(0) < pl.num_programs(0) - 1, x_ref[...], xt_ref[...])


def _route_kernel(*refs, slot_rows, has_tail, has_mix):
    refs = list(refs)
    x_ref = refs.pop(0)
    xt_ref = refs.pop(0) if has_tail else None
    g_ref, m_ref, wr_ref, br_ref = refs[:4]
    refs = refs[4:]
    x = _tile_tokens(x_ref, xt_ref)
    if has_mix:
        c_ref, d_ref, wo_ref = refs[:3]
        xs_ref, info_ref, cnt_ref, x1_ref = refs[3:]
        wc = c_ref.shape[2]
        x = x + m_ref[0, 2:3, :] * (_dot(c_ref[0], wo_ref[:wc]) + _dot(d_ref[0], wo_ref[wc:]))
        x1_ref[...] = x
    else:
        xs_ref, info_ref, cnt_ref = refs
    tr = x_ref.shape[0]
    h = _rmsnorm(x, g_ref[...]) * (1.0 + m_ref[0, 4:5, :]) + m_ref[0, 3:4, :]
    hb = h.astype(BF16)
    hl = (h - hb.astype(F32)).astype(BF16)
    w_hi, w_lo = wr_ref[:, :LANES], wr_ref[:, LANES:]
    logits = _dot(hb, w_hi) + _dot(hb, w_lo) + _dot(hl, w_hi) + br_ref[...]
    lane = lax.broadcasted_iota(jnp.int32, (tr, LANES), 1)
    lanef = lane.astype(F32)
    lg = jnp.where(lane < MOE_GROUPS, logits, NEG)
    mg = jnp.max(lg, axis=-1, keepdims=True)
    g_p = 1.0 / jnp.sum(jnp.exp(lg - mg), axis=-1, keepdims=True)
    gidx = jnp.min(jnp.where(lg == mg, lanef, float(LANES)), axis=-1, keepdims=True)
    lo = MOE_GROUPS + MOE_EPG * gidx
    le = jnp.where((lanef >= lo) & (lanef < lo + MOE_EPG), logits, NEG)
    m1 = jnp.max(le, axis=-1, keepdims=True)
    i1 = jnp.min(jnp.where(le == m1, lanef, float(LANES)), axis=-1, keepdims=True)
    le2 = jnp.where(lanef == i1, NEG, le)
    m2 = jnp.max(le2, axis=-1, keepdims=True)
    i2 = jnp.min(jnp.where(le2 == m2, lanef, float(LANES)), axis=-1, keepdims=True)
    ratio = jnp.exp(m2 - m1)
    w1 = g_p / (1.0 + ratio)
    w2 = g_p * ratio / (1.0 + ratio)
    hot1 = lanef == i1
    hot2 = lanef == i2
    hot = jnp.where(hot1, 1.0, jnp.where(hot2, 1.0, 0.0))
    r_i = lax.broadcasted_iota(jnp.int32, (tr, tr), 0)
    c_i = lax.broadcasted_iota(jnp.int32, (tr, tr), 1)
    rank = _dot(jnp.where(c_i < r_i, 1.0, 0.0).astype(BF16), hot.astype(BF16))
    cnt = jnp.sum(hot, axis=0, keepdims=True)
    nblk = jnp.floor((cnt + (SLOT_BLOCK - 1)) * (1.0 / SLOT_BLOCK))
    l_r = lax.broadcasted_iota(jnp.int32, (LANES, LANES), 0)
    l_c = lax.broadcasted_iota(jnp.int32, (LANES, LANES), 1)
    before = jnp.where(l_r < l_c, 1.0, 0.0).astype(BF16)
    off = SLOT_BLOCK * _dot(jnp.broadcast_to(nblk, (8, LANES)).astype(BF16), before)[0:1]
    posm = off + rank
    pos1 = jnp.sum(jnp.where(hot1, posm, 0.0), axis=-1, keepdims=True)
    pos2 = jnp.sum(jnp.where(hot2, posm, 0.0), axis=-1, keepdims=True)
    info = jnp.where(lane == 0, pos1, jnp.where(lane == 1, pos2, jnp.where(lane == 2, w1, jnp.where(lane == 3, w2, 0.0))))
    info_ref[...] = info
    pos_t = info.T.astype(jnp.int32)
    row = lax.broadcasted_iota(jnp.int32, (slot_rows, tr), 0)
    sel = jnp.where(row == pos_t[0:1], 1.0, jnp.where(row == pos_t[1:2], 1.0, 0.0)).astype(BF16)
    xs_ref[...] = _dot(sel, hb).astype(BF16).reshape(xs_ref.shape)
    cnt_ref[0] = jnp.broadcast_to(cnt, (8, LANES))


def _token_specs(x2d, x_tail, tr, index, rows=None):
    d = x2d.shape[1]
    nt = (x2d.shape[0] if rows is None else rows) // tr
    if x_tail is None:
        return nt, [pl.BlockSpec((tr, d), index(lambda i: (i, 0)))], [x2d]
    assert x_tail.shape == (tr, d)
    return nt + 1, [pl.BlockSpec((tr, d), index(lambda i: (jnp.minimum(i, nt - 1), 0))),
                    pl.BlockSpec((tr, d), index(lambda i: (0, 0)))], [x2d, x_tail]


def _moe_route(x2d, x_tail, gain, mods, tiles_per_mod, w_r, b_r, tr, mix=None):
    d = x2d.shape[1]
    rows = None if mix is None else mix[0].shape[0] * mix[0].shape[1]
    nt, x_specs, x_args = _token_specs(x2d, x_tail, tr, lambda f: f, rows)
    t = nt * tr
    sr = _slot_rows(tr)
    mix_specs, mix_args, mix_out_specs, mix_out_shapes = [], [], [], []
    if mix is not None:
        c3d, d3d, w_out = mix
        tpb = c3d.shape[1] // tr
        half = lambda a: pl.BlockSpec((1, tr, a.shape[2]), lambda i: (i // tpb, i % tpb, 0))
        mix_specs = [half(c3d), half(d3d), pl.BlockSpec(w_out.shape, lambda i: (0, 0))]
        mix_args = [c3d, d3d, w_out]
        mix_out_specs = [pl.BlockSpec((tr, d), lambda i: (i, 0))]
        mix_out_shapes = [jax.ShapeDtypeStruct((t, d), F32)]
    return pl.pallas_call(
        functools.partial(_route_kernel, slot_rows=sr, has_tail=x_tail is not None, has_mix=mix is not None),
        grid=(nt,),
        in_specs=x_specs + [
            pl.BlockSpec((1, d), lambda i: (0, 0)),
            pl.BlockSpec((1, 6, d), lambda i: (jnp.minimum(i // tiles_per_mod, mods.shape[0] - 1), 0, 0)),
            pl.BlockSpec(w_r.shape, lambda i: (0, 0)),
            pl.BlockSpec((1, LANES), lambda i: (0, 0)),
        ] + mix_specs,
        out_specs=[
            pl.BlockSpec((sr // SLOT_BLOCK, SLOT_BLOCK, d), lambda i: (i, 0, 0)),
            pl.BlockSpec((tr, LANES), lambda i: (i, 0)),
            pl.BlockSpec((1, 8, LANES), lambda i: (i, 0, 0)),
        ] + mix_out_specs,
        out_shape=[
            jax.ShapeDtypeStruct((nt * sr // SLOT_BLOCK, SLOT_BLOCK, d), BF16),
            jax.ShapeDtypeStruct((t, LANES), F32),
            jax.ShapeDtypeStruct((nt, 8, LANES), F32),
        ] + mix_out_shapes,
        compiler_params=_cparams("parallel"),
        name="moe_route",
    )(*x_args, gain.reshape(1, d), mods, w_r, b_r, *mix_args)


def _tables_kernel(cnt_ref, src_ref, inv_ref, exp_ref, valid_ref, *, ntiles, bpt):
    cnt = cnt_ref[...]
    nb = jnp.floor((cnt + (SLOT_BLOCK - 1)) * (1.0 / SLOT_BLOCK))
    i_r = lax.broadcasted_iota(jnp.int32, (LANES, LANES), 0)
    i_c = lax.broadcasted_iota(jnp.int32, (LANES, LANES), 1)
    before = jnp.where(i_r < i_c, 1.0, 0.0)
    upto = jnp.where(i_c <= i_r, 1.0, 0.0)
    first = _dot(nb, before)
    cum = _dot(upto, nb)
    tot = jnp.max(cum, axis=0, keepdims=True)
    steps = jnp.floor((tot + (STEP_BLOCKS - 1)) * (1.0 / STEP_BLOCKS))
    start = STEP_BLOCKS * _dot(jnp.broadcast_to(steps, (LANES, LANES)), before)[0:1]
    pos = start + cum - nb

    sub = 8 * (-(-(MOE_GROUPS + MOE_EXPERTS) // 8))
    first_t, nb_t, pos_t = first.T[:sub], nb.T[:sub], pos.T[:sub]
    as_col = lambda v: jnp.broadcast_to(v, (LANES, LANES)).T[:sub, 0:1]
    start_c, tot_c, span_c = as_col(start), as_col(tot), as_col(STEP_BLOCKS * steps)

    inv_ref[...] = jnp.zeros(inv_ref.shape, jnp.int32)
    local = lax.broadcasted_iota(jnp.int32, (sub, LANES), 1).astype(F32)
    nsrc = src_ref.shape[1]
    j = lax.broadcasted_iota(jnp.int32, (sub, nsrc), 1).astype(F32)
    acc = jnp.zeros((1, nsrc), F32)
    for i in range(ntiles):
        f_i, n_i, p_i = first_t[:, i:i + 1], nb_t[:, i:i + 1], pos_t[:, i:i + 1]
        own = (local >= f_i) & (local < f_i + n_i)
        inv_ref[i:i + 1, :] = jnp.sum(jnp.where(own, p_i + (local - f_i), 0.0), axis=0, keepdims=True).astype(jnp.int32)
        own = (j >= p_i) & (j < p_i + n_i)
        acc = acc + jnp.sum(jnp.where(own, (i * bpt) + f_i + (j - p_i), 0.0), axis=0, keepdims=True)
    src_ref[...] = acc.astype(jnp.int32)

    nst = exp_ref.shape[1]
    at = STEP_BLOCKS * lax.broadcasted_iota(jnp.int32, (sub, nst), 1).astype(F32)
    expert = (lax.broadcasted_iota(jnp.int32, (sub, nst), 0) - MOE_GROUPS).astype(F32)
    inside = (at >= start_c) & (at < start_c + span_c)
    exp_ref[...] = jnp.sum(jnp.where(inside, expert, 0.0), axis=0, keepdims=True).astype(jnp.int32)
    occupied = jnp.where(inside, jnp.where(at - start_c < tot_c, 1.0, 0.0), 0.0)
    valid_ref[...] = jnp.sum(occupied, axis=0, keepdims=True).astype(jnp.int32)


def _expert_tables(cnt, bpt, nsteps):
    ntiles = cnt.shape[0]
    assert bpt <= LANES and ntiles <= LANES
    cnt = jnp.pad(cnt, ((0, LANES - ntiles), (0, 0)))
    nsrc = -(-nsteps * STEP_BLOCKS // LANES) * LANES
    nst = -(-nsteps // LANES) * LANES
    i32 = lambda *s: jax.ShapeDtypeStruct(s, jnp.int32)
    src, inv, step_e, valid = pl.pallas_call(
        functools.partial(_tables_kernel, ntiles=ntiles, bpt=bpt),
        out_shape=[i32(1, nsrc), i32(LANES, LANES), i32(1, nst), i32(1, nst)],
        compiler_params=pltpu.CompilerParams(vmem_limit_bytes=VMEM_LIMIT),
        name="moe_tables",
    )(cnt)
    return src.reshape(-1), inv.reshape(-1), step_e.reshape(-1), valid.reshape(-1)


def _block_gather(table_ref, first, nblocks, src_hbm, buf_ref, slot, sem_ref):
    return [pltpu.make_async_copy(src_hbm.at[table_ref[first + kk]],
                                  buf_ref.at[slot, pl.ds(kk * SLOT_BLOCK, SLOT_BLOCK)], sem_ref.at[slot])
            for kk in range(nblocks)]


def _experts_kernel(src_ref, exp_ref, valid_ref, xs_hbm, wg_ref, wu_ref, wd_ref, y_ref, xbuf_ref, sem_ref, wgb_ref,
                    wub_ref, wdb_ref):
    s = pl.program_id(0)
    slot = s % 2

    def gather(step, to_slot):
        return _block_gather(src_ref, step * STEP_BLOCKS, STEP_BLOCKS, xs_hbm, xbuf_ref, to_slot, sem_ref)

    @pl.when((s == 0) & (valid_ref[0] > 0))
    def _():
        for cp in gather(0, 0):
            cp.start()

    nxt = jnp.minimum(s + 1, pl.num_programs(0) - 1)

    @pl.when((s + 1 < pl.num_programs(0)) & (valid_ref[nxt] > 0))
    def _():
        for cp in gather(s + 1, 1 - slot):
            cp.start()

    @pl.when((s == 0) | (exp_ref[s] != exp_ref[jnp.maximum(s - 1, 0)]))
    def _():
        wgb_ref[...] = wg_ref[0, 0, 0].astype(BF16)
        wub_ref[...] = wu_ref[0, 0, 0].astype(BF16)
        wdb_ref[...] = wd_ref[0, 0, 0].astype(BF16)

    @pl.when(valid_ref[s] > 0)
    def _():
        for cp in gather(s, slot):
            cp.wait()
        x = xbuf_ref[slot]
        a = _silu(_dot(x, wgb_ref[...])) * _dot(x, wub_ref[...])
        y_ref[...] = _dot(a.astype(BF16), wdb_ref[...]).astype(BF16).reshape(y_ref.shape)

    @pl.when(valid_ref[s] == 0)
    def _():
        y_ref[...] = jnp.zeros(y_ref.shape, y_ref.dtype)


def _moe_experts(xs, src, step_e, valid, w_gate, w_up, w_down, layer, nsteps):
    xs3 = xs
    d = xs3.shape[-1]
    f = w_gate.shape[-1]
    step_rows = STEP_BLOCKS * SLOT_BLOCK
    w_blk = lambda shape: pl.BlockSpec((1, 1, 1) + shape,
                                       lambda s, sr, ex, va: (layer, ex[s] // MOE_EPG, ex[s] % MOE_EPG, 0, 0))
    grid_spec = pltpu.PrefetchScalarGridSpec(
        num_scalar_prefetch=3,
        grid=(nsteps,),
        in_specs=[pl.BlockSpec(memory_space=pl.ANY), w_blk((d, f)), w_blk((d, f)), w_blk((f, d))],
        out_specs=pl.BlockSpec((STEP_BLOCKS, SLOT_BLOCK, d), lambda s, sr, ex, va: (s, 0, 0)),
        scratch_shapes=[pltpu.VMEM((2, step_rows, d), BF16), pltpu.SemaphoreType.DMA((2,)),
                        pltpu.VMEM((d, f), BF16), pltpu.VMEM((d, f), BF16), pltpu.VMEM((f, d), BF16)],
    )
    return pl.pallas_call(
        _experts_kernel,
        grid_spec=grid_spec,
        out_shape=jax.ShapeDtypeStruct((nsteps * STEP_BLOCKS, SLOT_BLOCK, d), BF16),
        compiler_params=_cparams("arbitrary"),
        name="moe_experts",
    )(src, step_e, valid, xs3, w_gate, w_up, w_down)


def _combine_kernel(inv_ref, x_ref, xt_ref, info_ref, m_ref, fg_ref, ys_hbm, o_ref, ybuf_ref, sem_ref, *, bpt, final):
    i = pl.program_id(0)
    slot = i % 2
    tr = x_ref.shape[0]

    def gather(tile, to_slot):
        return _block_gather(inv_ref, tile * LANES, bpt, ys_hbm, ybuf_ref, to_slot, sem_ref)

    @pl.when(i == 0)
    def _():
        for cp in gather(0, 0):
            cp.start()

    @pl.when(i + 1 < pl.num_programs(0))
    def _():
        for cp in gather(i + 1, 1 - slot):
            cp.start()

    info = info_ref[...]
    col = lax.broadcasted_iota(jnp.int32, (tr, bpt * SLOT_BLOCK), 1)
    wsel = jnp.where(col == info[:, 0:1].astype(jnp.int32), info[:, 2:3],
                     jnp.where(col == info[:, 1:2].astype(jnp.int32), info[:, 3:4], 0.0))
    for cp in gather(i, slot):
        cp.wait()
    y = _dot(wsel.astype(BF16), ybuf_ref[slot])
    out = _tile_tokens(x_ref, xt_ref) + m_ref[0, 5:6, :] * y
    if final:
        out = _rmsnorm(out, fg_ref[...])
    o_ref[...] = out


def _moe_combine(x2d, x_tail, ys, inv, info, mods, tiles_per_mod, final_g, tr, bpt, final):
    d = x2d.shape[1]
    nt, x_specs, x_args = _token_specs(x2d, x_tail, tr, lambda f: (lambda i, iv: f(i)))
    t = nt * tr
    ys3 = ys
    kern = _combine_kernel
    if x_tail is None:
        kern = lambda inv_ref, x_ref, *refs, **kw: _combine_kernel(inv_ref, x_ref, None, *refs, **kw)
    grid_spec = pltpu.PrefetchScalarGridSpec(
        num_scalar_prefetch=1,
        grid=(nt,),
        in_specs=x_specs + [
            pl.BlockSpec((tr, LANES), lambda i, iv: (i, 0)),
            pl.BlockSpec((1, 6, d), lambda i, iv: (jnp.minimum(i // tiles_per_mod, mods.shape[0] - 1), 0, 0)),
            pl.BlockSpec((1, d), lambda i, iv: (0, 0)),
            pl.BlockSpec(memory_space=pl.ANY),
        ],
        out_specs=pl.BlockSpec((tr, d), lambda i, iv: (i, 0)),
        scratch_shapes=[pltpu.VMEM((2, bpt * SLOT_BLOCK, d), BF16), pltpu.SemaphoreType.DMA((2,))],
    )
    return pl.pallas_call(
        functools.partial(kern, bpt=bpt, final=final),
        grid_spec=grid_spec,
        out_shape=jax.ShapeDtypeStruct((t, d), F32),
        compiler_params=_cparams("arbitrary"),
        name="moe_combine",
    )(inv, *x_args, info, mods, final_g.reshape(1, d), ys3)


def _moe(x2d, x_tail, gain, mods, tiles_per_mod, params, final_g, final, tr, mix=None):
    w_r, b_r, w_gate, w_up, w_down, layer = params
    if mix is None:
        xs, info, cnt = _moe_route(x2d, x_tail, gain, mods, tiles_per_mod, w_r, b_r, tr)
    else:
        xs, info, cnt, x2d = _moe_route(x2d, x_tail, gain, mods, tiles_per_mod, w_r, b_r, tr, mix)
    nt = x2d.shape[0] // tr + (x_tail is not None)
    bpt = _slot_rows(tr) // SLOT_BLOCK
    nsteps = -(-(nt * bpt + MOE_EXPERTS * (STEP_BLOCKS - 1)) // STEP_BLOCKS)
    src, inv, step_e, valid = _expert_tables(cnt[:, 0, :], bpt, nsteps)
    ys = _moe_experts(xs, src, step_e, valid, w_gate, w_up, w_down, layer, nsteps)
    return _moe_combine(x2d, x_tail, ys, inv, info, mods, tiles_per_mod, final_g, tr, bpt, final)


def _cd_in_kernel(x_ref, g_ref, m_ref, w_ref, cos_ref, sin_ref, qg_ref, kg_ref, wq_ref, wqp_ref, wk_ref, wv_ref,
                  vone_ref, ua_ref, *o_refs, need_q, q_scale):
    h = _rmsnorm(x_ref[0], g_ref[...]) * (1.0 + m_ref[0, 1:2, :]) + m_ref[0, 0:1, :]
    hb = h.astype(BF16)
    na_w = ua_ref.shape[2]
    ua_ref[0] = _dot(hb, w_ref[:, :na_w]).astype(ua_ref.dtype)
    ub = _dot(hb, w_ref[:, na_w:])
    o = MLA_Q_RANK + MLA_KV_RANK
    cq_raw, ckv_raw, kr, krp = ub[:, :MLA_Q_RANK], ub[:, MLA_Q_RANK:o], ub[:, o:o + MLA_PAD], ub[:, o + MLA_PAD:]
    cos = cos_ref[...]
    sin = sin_ref[...]
    ckv = _rmsnorm(ckv_raw, kg_ref[...]).astype(BF16)
    k_rope = kr * cos + krp * sin
    kn = _dot(ckv, wk_ref[...])
    if need_q:
        q_ref, k_ref, v_ref = o_refs
    else:
        k_ref, v_ref = o_refs
    vx = _dot(ckv, wv_ref[...]) + vone_ref[...]
    for h in range(D_HEADS):
        hs = slice(h * MLA_PAD, (h + 1) * MLA_PAD)
        k_ref[0, h] = (kn[:, hs] + k_rope).astype(BF16)
        v_ref[0, h] = vx[:, hs].T[:MLA_VROWS].astype(BF16)
    if need_q:
        cq = _rmsnorm(cq_raw, qg_ref[...]).astype(BF16)
        qm = _dot(cq, wq_ref[...])
        qp = _dot(cq, wqp_ref[...])
        for h in range(D_HEADS):
            hs = slice(h * MLA_PAD, (h + 1) * MLA_PAD)
            q_ref[0, h] = ((qm[:, hs] * cos + qp[:, hs] * sin) * q_scale).T.astype(BF16)


def _cd_in_proj(x2d, flat, gain, mods, w_cat, na_w, cos, sin, q_g, kv_g, wq, wqp, wk, wv, need_q, tm):
    b, n, row0 = flat
    d = x2d.shape[-1]
    tm = min(tm, n)
    row_major = (jax.ShapeDtypeStruct((b, D_HEADS, n, MLA_PAD), BF16),
                 pl.BlockSpec((1, D_HEADS, tm, MLA_PAD), lambda bi, i: (bi, 0, i, 0)))
    col_major = (jax.ShapeDtypeStruct((b, D_HEADS, MLA_PAD, n), BF16),
                 pl.BlockSpec((1, D_HEADS, MLA_PAD, tm), lambda bi, i: (bi, 0, 0, i)))
    v_major = (jax.ShapeDtypeStruct((b, D_HEADS, MLA_VROWS, n), BF16),
               pl.BlockSpec((1, D_HEADS, MLA_VROWS, tm), lambda bi, i: (bi, 0, 0, i)))
    ua = (jax.ShapeDtypeStruct((b, n, na_w), BF16), pl.BlockSpec((1, tm, na_w), lambda bi, i: (bi, i, 0)))
    outs, specs = zip(*([ua] + ([col_major] if need_q else []) + [row_major, v_major]))
    full = lambda a: pl.BlockSpec(a.shape, lambda bi, i: (0,) * a.ndim)
    vone = jnp.tile(jnp.concatenate([jnp.zeros((1, MLA_V), F32), jnp.ones((1, MLA_PAD - MLA_V), F32)], axis=1),
                    (1, D_HEADS))
    q_scale = float((MLA_NOPE + MLA_ROPE) ** -0.5 * np.log2(np.e))
    return pl.pallas_call(
        functools.partial(_cd_in_kernel, need_q=need_q, q_scale=q_scale),
        grid=(b, n // tm),
        in_specs=[
            pl.BlockSpec((1, tm, d), lambda bi, i: (0, row0 // tm + bi * (n // tm) + i, 0)),
            pl.BlockSpec((1, d), lambda bi, i: (0, 0)),
            pl.BlockSpec((1, 6, d), lambda bi, i: (bi, 0, 0)),
            full(w_cat),
            pl.BlockSpec((tm, MLA_PAD), lambda bi, i: (i, 0)),
            pl.BlockSpec((tm, MLA_PAD), lambda bi, i: (i, 0)),
            full(q_g), full(kv_g), full(wq), full(wqp), full(wk), full(wv), full(vone),
        ],
        out_specs=list(specs),
        out_shape=list(outs),
        compiler_params=_cparams("parallel", "parallel"),
        name="cd_in_proj",
    )(x2d.reshape(1, -1, d), gain.reshape(1, d), mods, w_cat, cos, sin, q_g, kv_g, wq, wqp, wk, wv, vone)


def _mla_attn_kernel(q_ref, qn_ref, kc_ref, kl_ref, vc_ref, vl_ref, o_ref, acc0_ref, acc1_ref, s0_ref, s1_ref, m_ref,
                     *, tk):
    tq = q_ref.shape[3]
    ncc = kc_ref.shape[2] // tk
    nchunks = ncc + kl_ref.shape[2] // tk
    neg = jnp.full((8, tq), NEG, F32)
    s_refs = (s0_ref, s1_ref)
    acc_refs = (acc0_ref, acc1_ref)

    def chunk(c):
        part = (kc_ref, vc_ref, c) if c < ncc else (kl_ref, vl_ref, c - ncc)
        return part[0], part[1], pl.ds(part[2] * tk, tk), pl.ds(c * tk, tk)

    def scores(k_ref, ks, ss, hh, q, m):
        s = _dot(k_ref[0, hh, ks, :], q)
        s_refs[hh][ss, :] = s
        return jnp.maximum(m, jnp.max(s.reshape(tk // 8, 8, tq), axis=0))

    def weight(v_ref, ks, ss, hh, m_row):
        p = jnp.exp2((s_refs[hh][ss, :] - m_row).astype(BF16))
        acc_refs[hh][...] += _dot(v_ref[0, hh, :, ks], p)

    @pl.when(pl.program_id(2) == 0)
    def _():
        m = neg
        for c in range(ncc):
            k_ref, _, ks, ss = chunk(c)
            m = scores(k_ref, ks, ss, 0, q_ref[0, 0], m)

        def latent(c, m):
            ks = pl.ds(pl.multiple_of(c * tk, tk), tk)
            ss = pl.ds(pl.multiple_of((c + ncc) * tk, tk), tk)
            return scores(kl_ref, ks, ss, 0, q_ref[0, 0], m)

        m_ref[...] = lax.fori_loop(0, nchunks - ncc, latent, m)

    acc0_ref[...] = jnp.zeros(acc0_ref.shape, F32)
    acc1_ref[...] = jnp.zeros(acc1_ref.shape, F32)
    m0 = jnp.max(m_ref[...], axis=0, keepdims=True)
    m1 = neg
    for c in range(nchunks):
        k_ref, v_ref, ks, ss = chunk(c)
        weight(v_ref, ks, ss, 0, m0)
        m1 = scores(k_ref, ks, ss, 1, q_ref[0, 1], m1)
    m1 = jnp.max(m1, axis=0, keepdims=True)
    m0_next = neg
    for c in range(nchunks):
        k_ref, v_ref, ks, ss = chunk(c)
        weight(v_ref, ks, ss, 1, m1)
        m0_next = scores(k_ref, ks, ss, 0, qn_ref[0, 0], m0_next)
    m_ref[...] = m0_next
    o_t = jnp.concatenate([a[:MLA_V] / a[MLA_V:MLA_V + 1] for a in acc_refs], axis=0)
    o_ref[0] = o_t.T.astype(o_ref.dtype)


def _mla_attention(q_t, k_ctx, k_lat, v_ctx, v_lat, tq, tk):
    b, h, _, n = q_t.shape
    nc = k_ctx.shape[2]
    tq = min(tq, n)
    assert nc % tk == 0 and n % tk == 0
    k_spec = lambda rows: pl.BlockSpec((1, 2, rows, MLA_PAD), lambda bi, hp, i: (bi, hp, 0, 0))
    v_spec = lambda rows: pl.BlockSpec((1, 2, MLA_VROWS, rows), lambda bi, hp, i: (bi, hp, 0, 0))
    return pl.pallas_call(
        functools.partial(_mla_attn_kernel, tk=tk),
        grid=(b, h // 2, n // tq),
        in_specs=[
            pl.BlockSpec((1, 2, MLA_PAD, tq), lambda bi, hp, i: (bi, hp, 0, i)),
            pl.BlockSpec((1, 2, MLA_PAD, tq), lambda bi, hp, i: (bi, hp, 0, jnp.minimum(i + 1, n // tq - 1))),
            k_spec(nc), k_spec(n), v_spec(nc), v_spec(n),
        ],
        out_specs=pl.BlockSpec((1, tq, 2 * MLA_V), lambda bi, hp, i: (bi, i, hp)),
        out_shape=jax.ShapeDtypeStruct((b, n, h * MLA_V), BF16),
        scratch_shapes=[pltpu.VMEM((MLA_VROWS, tq), F32), pltpu.VMEM((MLA_VROWS, tq), F32),
                        pltpu.VMEM((nc + n, tq), F32), pltpu.VMEM((nc + n, tq), F32), pltpu.VMEM((8, tq), F32)],
        compiler_params=_cparams("parallel", "parallel", "arbitrary"),
        name="mla_attention",
    )(q_t, q_t, k_ctx, k_lat, v_ctx, v_lat)


def _na_kernel(q_ref, *refs):
    k_refs, v_refs = refs[:NA_KBLOCKS + 1], refs[NA_KBLOCKS + 1:2 * NA_KBLOCKS + 2]
    tab_ref, o_ref = refs[2 * NA_KBLOCKS + 2:]
    tq = q_ref.shape[1]
    nloc = tab_ref.shape[3]
    lane = lax.broadcasted_iota(jnp.int32, (tq, LANES), 1)
    q = q_ref[0]
    k_all = jnp.concatenate([r[0] for r in k_refs], axis=0)
    v_all = jnp.concatenate([r[0] for r in v_refs], axis=0)
    outs = []
    for hh in range(2):
        in_head = (lane >= hh * C_HEAD_DIM) & (lane < (hh + 1) * C_HEAD_DIM)
        qh = jnp.where(in_head, q, jnp.zeros_like(q))
        s = lax.dot_general(qh, k_all, NT, preferred_element_type=F32)
        s_loc = s[:, :nloc] + tab_ref[0, hh]
        s_ctx = s[:, nloc:]
        m = jnp.maximum(jnp.max(s_loc, axis=-1, keepdims=True), jnp.max(s_ctx, axis=-1, keepdims=True))
        p_loc = jnp.exp(s_loc - m)
        p_ctx = jnp.exp(s_ctx - m)
        l = jnp.sum(p_loc, axis=-1, keepdims=True) + jnp.sum(p_ctx, axis=-1, keepdims=True)
        o = _dot(p_loc.astype(BF16), v_all[:nloc]) + _dot(p_ctx.astype(BF16), v_all[nloc:])
        outs.append(o / l)
    o_ref[0] = jnp.where(lane < C_HEAD_DIM, outs[0], outs[1]).astype(o_ref.dtype)


def _na_tables(rpb, rows):
    h = rpb.shape[0]
    w = GRID_W
    kr_n = NA_QROWS + NA_ROWS
    qc = np.arange(w)
    kc = np.arange(w)
    cs = np.clip(qc - NA_COLS // 2, 0, w - NA_COLS)
    col_ok = (kc[None, :] >= cs[:, None]) & (kc[None, :] < cs[:, None] + NA_COLS)
    dc = np.clip(kc[None, :] - qc[:, None] + (NA_COLS - 1), 0, 2 * NA_COLS - 2)
    pick_dc = (dc.reshape(-1)[None, :] == np.arange(2 * NA_COLS - 1)[:, None]).astype(np.float32)
    base = jnp.einsum('hrd,dx->hrx', rpb.astype(F32), jnp.asarray(pick_dc), precision=lax.Precision.HIGHEST)
    nblk = rows // NA_QROWS
    tabs = []
    for m in (0, 1, nblk - 1):
        qr = NA_QROWS * m + np.arange(NA_QROWS)
        rs = np.clip(qr - NA_ROWS // 2, 0, rows - NA_ROWS)
        kr = NA_QROWS * m - NA_ROWS // 2 + np.arange(kr_n)
        row_ok = (kr[None, :] >= rs[:, None]) & (kr[None, :] < rs[:, None] + NA_ROWS)
        dr = np.clip(kr[None, :] - qr[:, None] + (NA_ROWS - 1), 0, 2 * NA_ROWS - 2)
        pick_dr = (dr.reshape(-1)[:, None] == np.arange(2 * NA_ROWS - 1)[None, :]).astype(np.float32)
        t = jnp.einsum('vr,hrx->hvx', jnp.asarray(pick_dr), base, precision=lax.Precision.HIGHEST)
        t = t.reshape(h, NA_QROWS, kr_n, w, w).transpose(0, 1, 3, 2, 4)
        ok = row_ok[:, None, :, None] & col_ok[None, :, None, :]
        tabs.append(jnp.where(jnp.asarray(ok)[None], t, NEG).reshape(h, NA_QROWS * w, kr_n * w))
    return jnp.stack(tabs)


def _na_attention(u_lat, u_ctx, tabs):
    b, n, _ = u_lat.shape
    nc = u_ctx.shape[1]
    tq = NA_QROWS * GRID_W
    tkb = NA_KBLOCK_ROWS * GRID_W
    nblk = n // tq
    nkb = n // tkb
    per_q = NA_QROWS // NA_KBLOCK_ROWS
    assert NA_KBLOCKS == per_q + 2 and NA_ROWS // 2 == NA_KBLOCK_ROWS and n % tq == 0 and n // tq >= 2
    pairs = C_HEADS // 2
    q_spec = pl.BlockSpec((1, tq, LANES), lambda bi, hp, i: (bi, i, hp))
    kblk = lambda col0, j: pl.BlockSpec(
        (1, tkb, LANES), lambda bi, hp, i: (bi, jnp.clip(i * per_q - 1 + j, 0, nkb - 1), col0 + hp))
    ctx = lambda col0: pl.BlockSpec((1, nc, LANES), lambda bi, hp, i: (bi, 0, col0 + hp))
    sel = lambda i: jnp.where(i == 0, 0, jnp.where(i == nblk - 1, 2, 1))
    kv_specs = [kblk(col0, j) for col0 in (pairs, 2 * pairs) for j in range(NA_KBLOCKS)]
    kv_specs = kv_specs[:NA_KBLOCKS] + [ctx(pairs)] + kv_specs[NA_KBLOCKS:] + [ctx(2 * pairs)]
    kv_args = [u_lat] * NA_KBLOCKS + [u_ctx]
    return pl.pallas_call(
        _na_kernel,
        grid=(b, pairs, nblk),
        in_specs=[q_spec] + kv_specs + [
            pl.BlockSpec((1, 2, tq, NA_KBLOCKS * tkb), lambda bi, hp, i: (sel(i), hp, 0, 0)),
        ],
        out_specs=pl.BlockSpec((1, tq, LANES), lambda bi, hp, i: (bi, i, hp)),
        out_shape=jax.ShapeDtypeStruct((b, n, C_WIDTH), BF16),
        compiler_params=_cparams("parallel", "parallel", "arbitrary"),
        name="na_attention",
    )(u_lat, *kv_args, *kv_args, tabs)


def _moe_params(w_rg, b_rg, w_re, b_re, w_gate, w_up, w_down, layer):
    d = w_rg.shape[0]
    w_r = jnp.zeros((d, LANES), F32).at[:, :MOE_GROUPS].set(w_rg).at[:, MOE_GROUPS:MOE_GROUPS + MOE_EXPERTS].set(w_re)
    b_r = jnp.zeros((1, LANES), F32).at[0, :MOE_GROUPS].set(b_rg).at[0, MOE_GROUPS:MOE_GROUPS + MOE_EXPERTS].set(b_re)
    w_hi = w_r.astype(BF16)
    w_lo = (w_r - w_hi.astype(F32)).astype(BF16)
    return jnp.concatenate([w_hi, w_lo], axis=1), b_r, w_gate, w_up, w_down, layer


def _rope_perm():
    j = np.arange(MLA_ROPE)
    half = MLA_ROPE // 2
    return (j // half) * half + (j % half + half // 2) % half


def _rope_tables(n):
    half = MLA_ROPE // 2
    nf = half // 2
    t = np.arange(n)
    inv = (np.float32(ROPE_THETA) ** (-np.arange(nf, dtype=np.float32) / np.float32(nf))).astype(np.float32)
    parts_c, parts_s = [], []
    for pos in ((t // GRID_W).astype(np.float32), (t % GRID_W).astype(np.float32)):
        ang = (pos[:, None] * inv[None, :]).astype(np.float32)
        c, s = np.cos(ang).astype(np.float32), np.sin(ang).astype(np.float32)
        parts_c += [c, c]
        parts_s += [-s, s]
    pad = MLA_PAD - MLA_NOPE - MLA_ROPE
    cos = np.concatenate([np.ones((n, MLA_NOPE), np.float32)] + parts_c + [np.zeros((n, pad), np.float32)], axis=1)
    sin = np.concatenate([np.zeros((n, MLA_NOPE), np.float32)] + parts_s + [np.zeros((n, pad), np.float32)], axis=1)
    return jnp.asarray(cos), jnp.asarray(sin)


def _identity_rope_tables(n):
    pad = MLA_PAD - MLA_NOPE - MLA_ROPE
    cos = jnp.concatenate([jnp.ones((n, MLA_NOPE + MLA_ROPE), F32), jnp.zeros((n, pad), F32)], axis=1)
    return cos, jnp.zeros((n, MLA_PAD), F32)


def _pad_heads(w, widths, src_cols, dst_off):
    rank = w.shape[0]
    out = jnp.zeros((rank, D_HEADS, MLA_PAD), F32)
    wh = w.reshape(rank, D_HEADS, widths)[:, :, src_cols]
    return out.at[:, :, dst_off:dst_off + len(src_cols)].set(wh).reshape(rank, D_HEADS * MLA_PAD)


def _cd_params(w_in, w_uq, w_ukv):
    d = w_in.shape[0]
    perm = _rope_perm()
    o = 3 * C_WIDTH
    q_scale = float(C_HEAD_DIM ** -0.5)
    kr = w_in[:, o + MLA_Q_RANK + MLA_KV_RANK:]
    pad_rope = lambda a: jnp.zeros((d, MLA_PAD), F32).at[:, MLA_NOPE:MLA_NOPE + MLA_ROPE].set(a)
    w_cat = jnp.concatenate([
        w_in[:, :C_WIDTH] * q_scale, w_in[:, C_WIDTH:o],
        w_in[:, o:o + MLA_Q_RANK + MLA_KV_RANK], pad_rope(kr), pad_rope(kr[:, perm]),
    ], axis=1).astype(BF16)
    qw = MLA_NOPE + MLA_ROPE
    nope = np.arange(MLA_NOPE)
    rope = MLA_NOPE + np.arange(MLA_ROPE)
    wq = (_pad_heads(w_uq, qw, nope, 0) + _pad_heads(w_uq, qw, rope, MLA_NOPE)).astype(BF16)
    wqp = _pad_heads(w_uq, qw, rope[perm], MLA_NOPE).astype(BF16)
    kvw = MLA_NOPE + MLA_V
    wk = _pad_heads(w_ukv, kvw, nope, 0).astype(BF16)
    wv = _pad_heads(w_ukv, kvw, MLA_NOPE + np.arange(MLA_V), 0).astype(BF16)
    return w_cat, wq, wqp, wk, wv


def kernel(x, c, ctx, c_ctx, ada_w, ada_b, norm1_g, norm2_g, ab_w_in, ab_w_out, hgrn_lb_logits, hgrn_onorm_g, pool_w,
           pool_scale, cd_w_in, cd_w_out, na_rpb, mla_q_norm_g, mla_w_uq, mla_kv_norm_g, mla_w_ukv, moe_w_rg, moe_b_rg,
           moe_w_re, moe_b_re, moe_w_gate, moe_w_up, moe_w_down, final_norm_g):
    b, n, d = x.shape
    n_ctx = ctx.shape[1]
    assert ada_w.shape[0] == 2 and ab_w_in.shape[0] == 1 and cd_w_in.shape[0] == 1 and b + 1 <= 8
    tm = TOKEN_TILE

    cc = jnp.zeros((8, d), F32).at[:b].set(c).at[b].set(c_ctx)
    mods = _ada(cc, b + 1, ada_w, ada_b).reshape(2, 8, 6, d)
    mods_lat = [mods[l, :b] for l in range(2)]
    mods_ctx = [jnp.broadcast_to(mods[l, b:b + 1], (b, 6, d)) for l in range(2)]
    lb = jnp.cumsum(jax.nn.softmax(hgrn_lb_logits.astype(F32), axis=1), axis=1)[:, 0]

    w_in0 = ab_w_in[0].astype(BF16)
    w_out0 = ab_w_out[0].astype(BF16)
    pw0 = pool_w[0].astype(BF16)
    ab_cols = w_in0.shape[1]
    (u_ctx,) = _in_proj(ctx, norm1_g[0], mods_ctx[0], w_in0, ((0, ab_cols),), (F32,), tm)
    (u_lat,) = _in_proj(x, norm1_g[0], mods_lat[0], w_in0, ((0, ab_cols),), (F32,), tm)
    s0 = jnp.zeros((b, 2, A_HEADS, A_HEAD_DIM, A_HEAD_DIM), F32)
    ocf, ocb, s_ctx = _hgrn_scan(u_ctx, lb, s0, HGRN_ROWS)
    olf, olb, _ = _hgrn_scan(u_lat, lb, s_ctx, HGRN_ROWS)
    t_lat, t_ctx = b * n, b * n_ctx
    assert n % tm == 0 and t_ctx == tm
    x_lat = _ab_out(olf, olb, u_lat, x, mods_lat[0], hgrn_onorm_g[0], pw0, pool_scale[0], w_out0, tm)
    x_ctx = _ab_out(ocf, ocb, u_ctx, ctx, mods_ctx[0], hgrn_onorm_g[0], pw0, pool_scale[0], w_out0, tm)
    moe0 = _moe_params(moe_w_rg[0], moe_b_rg[0], moe_w_re[0], moe_b_re[0], moe_w_gate, moe_w_up, moe_w_down, 0)
    mods_all = jnp.concatenate([mods_lat[0], mods[0, b:b + 1]], axis=0)
    xa = _moe(x_lat, x_ctx, norm2_g[0], mods_all, n // tm, moe0, final_norm_g, False, tm)

    w_cat, wq, wqp, wk, wv = _cd_params(cd_w_in[0], mla_w_uq[0], mla_w_ukv[0])
    na_w = 3 * C_WIDTH
    q_g = mla_q_norm_g[0].reshape(1, -1)
    kv_g = mla_kv_norm_g[0].reshape(1, -1)
    cos_l, sin_l = _rope_tables(n)
    cos_c, sin_c = _identity_rope_tables(n_ctx)
    ua_ctx, k_c, v_c = _cd_in_proj(xa, (b, n_ctx, t_lat), norm1_g[1], mods_ctx[1], w_cat, na_w, cos_c, sin_c, q_g, kv_g,
                                   wq, wqp, wk, wv, False, tm)
    ua_lat, q_l, k_l, v_l = _cd_in_proj(xa, (b, n, 0), norm1_g[1], mods_lat[1], w_cat, na_w, cos_l, sin_l, q_g, kv_g,
                                        wq, wqp, wk, wv, True, tm)
    d_lat = _mla_attention(q_l, k_c, k_l, v_c, v_l, MLA_Q_TILE, MLA_K_CHUNK)
    c_lat = _na_attention(ua_lat, ua_ctx, _na_tables(na_rpb[0], n // GRID_W))
    moe1 = _moe_params(moe_w_rg[1], moe_b_rg[1], moe_w_re[1], moe_b_re[1], moe_w_gate, moe_w_up, moe_w_down, 1)
    mix = (c_lat, d_lat, cd_w_out[0].astype(BF16))
    out = _moe(xa, None, norm2_g[1], mods_lat[1], n // tm, moe1, final_norm_g, True, tm, mix)
    return out.reshape(b, n, d)
```

```python
import functools

import numpy as np
import jax
import jax.numpy as jnp
from jax import lax
from jax.experimental import pallas as pl
from jax.experimental.pallas import tpu as pltpu

F32 = jnp.float32
BF16 = jnp.bfloat16

EPS = 1e-6
NEG = -1e30

GRID_W = 64
A_HEADS = 4
A_HEAD_DIM = 128
A_WIDTH = A_HEADS * A_HEAD_DIM
POOL_WINDOWS = (2, 4, 8, 16)
B_GROUP = 128
B_WIDTH = B_GROUP * len(POOL_WINDOWS)
POOL_HALO = 16
C_HEADS = 8
C_HEAD_DIM = 64
C_WIDTH = C_HEADS * C_HEAD_DIM
NA_ROWS = 8
NA_COLS = 16
NA_QROWS = 4
NA_KBLOCK_ROWS = 4
NA_KBLOCKS = 3
D_HEADS = 8
MLA_Q_RANK = 256
MLA_KV_RANK = 128
MLA_NOPE = 64
MLA_ROPE = 32
MLA_V = 64
MLA_PAD = 128
MLA_VROWS = 80
ROPE_THETA = 10000.0
MOE_GROUPS = 4
MOE_EPG = 8
MOE_EXPERTS = MOE_GROUPS * MOE_EPG
MOE_HIDDEN = 256
LANES = 128
SLOT_BLOCK = 16
STEP_BLOCKS = 32
VMEM_LIMIT = 56 * 1024 * 1024
TOKEN_TILE = 512
HGRN_ROWS = 256
MLA_Q_TILE = 256
MLA_K_CHUNK = 256

NT = (((1,), (1,)), ((), ()))
TN = (((0,), (0,)), ((), ()))


def _cparams(*sem):
    return pltpu.CompilerParams(dimension_semantics=sem, vmem_limit_bytes=VMEM_LIMIT)


def _sigmoid(x):
    return 1.0 / (1.0 + jnp.exp(-x))


def _silu(x):
    return x * _sigmoid(x)


def _dot(a, b):
    return jnp.dot(a, b, preferred_element_type=F32)


def _rmsnorm(x, g):
    return x * lax.rsqrt(jnp.mean(x * x, axis=-1, keepdims=True) + EPS) * g


def _ada_kernel(ct_ref, w_ref, b_ref, o_ref, *, nrows):
    s = _silu(ct_ref[...])
    w = w_ref[0]
    rows = [jnp.sum(s[:, r:r + 1] * w, axis=0, keepdims=True) + b_ref[0] for r in range(nrows)]
    rows.append(jnp.zeros((o_ref.shape[1] - nrows, w.shape[1]), F32))
    o_ref[0] = jnp.concatenate(rows, axis=0)


def _ada(cc, nrows, ada_w, ada_b):
    depth, d, n6 = ada_w.shape
    tn = n6 // 8
    return pl.pallas_call(
        functools.partial(_ada_kernel, nrows=nrows),
        grid=(depth, n6 // tn),
        in_specs=[
            pl.BlockSpec((d, 8), lambda l, j: (0, 0)),
            pl.BlockSpec((1, d, tn), lambda l, j: (l, 0, j)),
            pl.BlockSpec((1, 1, tn), lambda l, j: (l, 0, j)),
        ],
        out_specs=pl.BlockSpec((1, 8, tn), lambda l, j: (l, 0, j)),
        out_shape=jax.ShapeDtypeStruct((depth, 8, n6), F32),
        compiler_params=_cparams("parallel", "parallel"),
        name="ada_mod",
    )(cc.T, ada_w, ada_b.reshape(depth, 1, n6))


def _in_kernel(x_ref, g_ref, m_ref, w_ref, *o_refs, splits):
    h = _rmsnorm(x_ref[0], g_ref[...]) * (1.0 + m_ref[0, 1:2, :]) + m_ref[0, 0:1, :]
    hb = h.astype(BF16)
    for o_ref, (a, b) in zip(o_refs, splits):
        o_ref[0] = _dot(hb, w_ref[:, a:b]).astype(o_ref.dtype)


def _in_proj(x, gain, mods, w, splits, dtypes, tm):
    b, n, d = x.shape
    tm = min(tm, n)
    outs = [jax.ShapeDtypeStruct((b, n, hi - lo), dt) for (lo, hi), dt in zip(splits, dtypes)]
    return pl.pallas_call(
        functools.partial(_in_kernel, splits=splits),
        grid=(b, n // tm),
        in_specs=[
            pl.BlockSpec((1, tm, d), lambda bi, i: (bi, i, 0)),
            pl.BlockSpec((1, d), lambda bi, i: (0, 0)),
            pl.BlockSpec((1, 6, d), lambda bi, i: (bi, 0, 0)),
            pl.BlockSpec(w.shape, lambda bi, i: (0, 0)),
        ],
        out_specs=[pl.BlockSpec((1, tm, hi - lo), lambda bi, i: (bi, i, 0)) for lo, hi in splits],
        out_shape=outs,
        compiler_params=_cparams("parallel", "parallel"),
        name="in_proj",
    )(x, gain.reshape(1, d), mods, w)


HG_SUB = 64


def _hgrn_direction(q_raw, fz, v, lb, st_ref, d, o_ref, reverse):
    rows = q_raw.shape[0]
    c = HG_SUB
    f = lb + (1.0 - lb) * _sigmoid(fz)
    k = 1.0 - f
    g = jnp.log(f)
    q = _silu(q_raw)
    r_i = lax.broadcasted_iota(jnp.int32, (c, c), 0)
    c_i = lax.broadcasted_iota(jnp.int32, (c, c), 1)
    keep = (c_i >= r_i) if reverse else (c_i <= r_i)
    tri = jnp.where(keep, 1.0, 0.0).astype(BF16)
    order = range(rows // c - 1, -1, -1) if reverse else range(rows // c)
    for ci in order:
        sl = slice(ci * c, (ci + 1) * c)
        gc = g[sl]
        g_hi = gc.astype(BF16)
        g_lo = (gc - g_hi.astype(F32)).astype(BF16)
        bc = _dot(tri, g_hi) + _dot(tri, g_lo)
        ref = bc[c // 2:c // 2 + 1]
        tot = bc[0:1] if reverse else bc[c - 1:c]
        qt = q[sl] * jnp.exp(bc - ref)
        kt = k[sl] * jnp.exp(ref - bc)
        qd = (qt * jnp.exp(ref)).astype(BF16)
        kd = (kt * jnp.exp(tot - ref)).astype(BF16)
        qt = qt.astype(BF16)
        kt = kt.astype(BF16)
        vb = v[sl].astype(BF16)
        dec = jnp.exp(tot)
        for h in range(A_HEADS):
            hs = slice(h * A_HEAD_DIM, (h + 1) * A_HEAD_DIM)
            att = lax.dot_general(qt[:, hs], kt[:, hs], NT, preferred_element_type=F32)
            att = jnp.where(keep, att, 0.0).astype(BF16)
            st = st_ref[d, h]
            o = _dot(att, vb[:, hs]) + lax.dot_general(qd[:, hs], st.astype(BF16), NT, preferred_element_type=F32)
            o_ref[0, sl, hs] = o
            st_ref[d, h] = st * dec[:, hs] + lax.dot_general(vb[:, hs], kd[:, hs], TN, preferred_element_type=F32)


def _hgrn_kernel(qf_ref, ff_ref, vf_ref, qb_ref, fb_ref, vb_ref, lb_ref, s0_ref, of_ref, ob_ref, sfin_ref, st_ref):
    j = pl.program_id(1)

    @pl.when(j == 0)
    def _():
        st_ref[...] = s0_ref[0]

    _hgrn_direction(qf_ref[0], ff_ref[0], vf_ref[0], lb_ref[0:1], st_ref, 0, of_ref, False)
    _hgrn_direction(qb_ref[0], fb_ref[0], vb_ref[0], lb_ref[1:2], st_ref, 1, ob_ref, True)

    @pl.when(j == pl.num_programs(1) - 1)
    def _():
        sfin_ref[0] = st_ref[...]


def _hgrn_scan(u, lb, s0, rows):
    b, n, _ = u.shape
    rows = min(rows, n)
    nb = n // rows
    w = A_WIDTH

    def fwd(col):
        return pl.BlockSpec((1, rows, w), lambda bi, j: (bi, j, col))

    def bwd(col):
        return pl.BlockSpec((1, rows, w), lambda bi, j: (bi, nb - 1 - j, col))

    st_spec = pl.BlockSpec((1, 2, A_HEADS, A_HEAD_DIM, A_HEAD_DIM), lambda bi, j: (bi, 0, 0, 0, 0))
    return pl.pallas_call(
        _hgrn_kernel,
        grid=(b, nb),
        in_specs=[fwd(0), fwd(1), fwd(3), bwd(0), bwd(2), bwd(3), pl.BlockSpec((2, w), lambda bi, j: (0, 0)), st_spec],
        out_specs=[
            pl.BlockSpec((1, rows, w), lambda bi, j: (bi, j, 0)),
            pl.BlockSpec((1, rows, w), lambda bi, j: (bi, nb - 1 - j, 0)),
            st_spec,
        ],
        out_shape=[
            jax.ShapeDtypeStruct((b, n, w), F32),
            jax.ShapeDtypeStruct((b, n, w), F32),
            jax.ShapeDtypeStruct(s0.shape, F32),
        ],
        scratch_shapes=[pltpu.VMEM((2, A_HEADS, A_HEAD_DIM, A_HEAD_DIM), F32)],
        compiler_params=_cparams("parallel", "arbitrary"),
        name="hgrn_scan",
    )(u, u, u, u, u, u, lb, s0)


def _ab_out_kernel(of_ref, ob_ref, ug_ref, up_ref, pprev_ref, pnext_ref, x_ref, m_ref, on_ref, pw_ref, ps_ref,
                   wo_ref, o_ref, *, n):
    i = pl.program_id(1)
    tm = x_ref.shape[1]
    o = of_ref[0] + ob_ref[0]
    gate = _silu(ug_ref[0])
    parts = []
    for h in range(A_HEADS):
        hs = slice(h * A_HEAD_DIM, (h + 1) * A_HEAD_DIM)
        parts.append(_rmsnorm(o[:, hs], on_ref[...]) * gate[:, hs])
    main = up_ref[0]
    prev = jnp.where(i > 0, pprev_ref[0], 0.0)
    nxt = jnp.where(i < pl.num_programs(1) - 1, pnext_ref[0], 0.0)
    ext = jnp.concatenate([prev, main, nxt], axis=0)
    ext_rows = tm + 2 * POOL_HALO
    t = i * tm + lax.broadcasted_iota(jnp.int32, (tm, 1), 0)
    for gi, win in enumerate(POOL_WINDOWS):
        gs = slice(gi * B_GROUP, (gi + 1) * B_GROUP)
        acc = ext[:, gs]
        acc = acc + pltpu.roll(acc, 1, 0)
        half = 1
        while 2 * half < win:
            acc = pltpu.roll(acc, half, 0) + pltpu.roll(acc, ext_rows - half, 0)
            half *= 2
        cnt = jnp.minimum(t + (win - win // 2), n) - jnp.maximum(t - win // 2, 0)
        mean = acc[POOL_HALO:POOL_HALO + tm] / cnt.astype(F32)
        pooled = _dot((mean - main[:, gs]).astype(BF16), pw_ref[gi])
        parts.append(pooled * ps_ref[:, gs])
    mix = jnp.concatenate(parts, axis=-1).astype(BF16)
    o_ref[0] = x_ref[0] + m_ref[0, 2:3, :] * _dot(mix, wo_ref[...])


def _ab_out(o_f, o_b, u, x, mods, onorm_g, pool_w, pool_scale, w_out, tm):
    b, n, d = x.shape
    tm = min(tm, n)
    nt = n // tm
    hb = tm // POOL_HALO
    last_halo = n // POOL_HALO - 1
    w = A_WIDTH
    tile = lambda col: pl.BlockSpec((1, tm, w), lambda bi, i: (bi, i, col))
    out = pl.pallas_call(
        functools.partial(_ab_out_kernel, n=n),
        grid=(b, nt),
        in_specs=[
            tile(0), tile(0), tile(4), tile(5),
            pl.BlockSpec((1, POOL_HALO, w), lambda bi, i: (bi, jnp.maximum(i * hb - 1, 0), 5)),
            pl.BlockSpec((1, POOL_HALO, w), lambda bi, i: (bi, jnp.minimum((i + 1) * hb, last_halo), 5)),
            pl.BlockSpec((1, tm, d), lambda bi, i: (bi, i, 0)),
            pl.BlockSpec((1, 6, d), lambda bi, i: (bi, 0, 0)),
            pl.BlockSpec((1, A_HEAD_DIM), lambda bi, i: (0, 0)),
            pl.BlockSpec(pool_w.shape, lambda bi, i: (0, 0, 0)),
            pl.BlockSpec((1, B_WIDTH), lambda bi, i: (0, 0)),
            pl.BlockSpec(w_out.shape, lambda bi, i: (0, 0)),
        ],
        out_specs=pl.BlockSpec((1, tm, d), lambda bi, i: (bi, i, 0)),
        out_shape=jax.ShapeDtypeStruct((b, n, d), F32),
        compiler_params=_cparams("parallel", "parallel"),
        name="ab_out",
    )(o_f, o_b, u, u, u, u, x, mods, onorm_g.reshape(1, A_HEAD_DIM), pool_w, pool_scale.reshape(1, B_WIDTH), w_out)
    return out.reshape(b * n, d)


def _slot_rows(tr):
    rows = 2 * tr + MOE_EXPERTS * (SLOT_BLOCK - 1)
    assert rows % SLOT_BLOCK == 0
    return rows


def _tile_tokens(x_ref, xt_ref):
    if xt_ref is None:
        return x_ref[...]
    return jnp.where(pl.program_id(0) < pl.num_programs(0) - 1, x_ref[...], xt_ref[...])


def _route_kernel(*refs, slot_rows, has_tail, has_mix):
    refs = list(refs)
    x_ref = refs.pop(0)
    xt_ref = refs.pop(0) if has_tail else None
    g_ref, m_ref, wr_ref, br_ref = refs[:4]
    refs = refs[4:]
    x = _tile_tokens(x_ref, xt_ref)
    if has_mix:
        c_ref, d_ref, wo_ref = refs[:3]
        xs_ref, info_ref, cnt_ref, x1_ref = refs[3:]
        wc = c_ref.shape[2]
        x = x + m_ref[0, 2:3, :] * (_dot(c_ref[0], wo_ref[:wc]) + _dot(d_ref[0], wo_ref[wc:]))
        x1_ref[...] = x
    else:
        xs_ref, info_ref, cnt_ref = refs
    tr = x_ref.shape[0]
    h = _rmsnorm(x, g_ref[...]) * (1.0 + m_ref[0, 4:5, :]) + m_ref[0, 3:4, :]
    hb = h.astype(BF16)
    hl = (h - hb.astype(F32)).astype(BF16)
    w_hi, w_lo = wr_ref[:, :LANES], wr_ref[:, LANES:]
    logits = _dot(hb, w_hi) + _dot(hb, w_lo) + _dot(hl, w_hi) + br_ref[...]
    lane = lax.broadcasted_iota(jnp.int32, (tr, LANES), 1)
    lanef = lane.astype(F32)
    lg = jnp.where(lane < MOE_GROUPS, logits, NEG)
    mg = jnp.max(lg, axis=-1, keepdims=True)
    g_p = 1.0 / jnp.sum(jnp.exp(lg - mg), axis=-1, keepdims=True)
    gidx = jnp.min(jnp.where(lg == mg, lanef, float(LANES)), axis=-1, keepdims=True)
    lo = MOE_GROUPS + MOE_EPG * gidx
    le = jnp.where((lanef >= lo) & (lanef < lo + MOE_EPG), logits, NEG)
    m1 = jnp.max(le, axis=-1, keepdims=True)
    i1 = jnp.min(jnp.where(le == m1, lanef, float(LANES)), axis=-1, keepdims=True)
    le2 = jnp.where(lanef == i1, NEG, le)
    m2 = jnp.max(le2, axis=-1, keepdims=True)
    i2 = jnp.min(jnp.where(le2 == m2, lanef, float(LANES)), axis=-1, keepdims=True)
    ratio = jnp.exp(m2 - m1)
    w1 = g_p / (1.0 + ratio)
    w2 = g_p * ratio / (1.0 + ratio)
    hot1 = lanef == i1
    hot2 = lanef == i2
    hot = jnp.where(hot1, 1.0, jnp.where(hot2, 1.0, 0.0))
    r_i = lax.broadcasted_iota(jnp.int32, (tr, tr), 0)
    c_i = lax.broadcasted_iota(jnp.int32, (tr, tr), 1)
    rank = _dot(jnp.where(c_i < r_i, 1.0, 0.0).astype(BF16), hot.astype(BF16))
    cnt = jnp.sum(hot, axis=0, keepdims=True)
    nblk = jnp.floor((cnt + (SLOT_BLOCK - 1)) * (1.0 / SLOT_BLOCK))
    l_r = lax.broadcasted_iota(jnp.int32, (LANES, LANES), 0)
    l_c = lax.broadcasted_iota(jnp.int32, (LANES, LANES), 1)
    before = jnp.where(l_r < l_c, 1.0, 0.0).astype(BF16)
    off = SLOT_BLOCK * _dot(jnp.broadcast_to(nblk, (8, LANES)).astype(BF16), before)[0:1]
    posm = off + rank
    pos1 = jnp.sum(jnp.where(hot1, posm, 0.0), axis=-1, keepdims=True)
    pos2 = jnp.sum(jnp.where(hot2, posm, 0.0), axis=-1, keepdims=True)
    info = jnp.where(lane == 0, pos1, jnp.where(lane == 1, pos2, jnp.where(lane == 2, w1, jnp.where(lane == 3, w2, 0.0))))
    info_ref[...] = info
    pos_t = info.T.astype(jnp.int32)
    row = lax.broadcasted_iota(jnp.int32, (slot_rows, tr), 0)
    sel = jnp.where(row == pos_t[0:1], 1.0, jnp.where(row == pos_t[1:2], 1.0, 0.0)).astype(BF16)
    xs_ref[...] = _dot(sel, hb).astype(BF16).reshape(xs_ref.shape)
    cnt_ref[0] = jnp.broadcast_to(cnt, (8, LANES))


def _token_specs(x2d, x_tail, tr, index, rows=None):
    d = x2d.shape[1]
    nt = (x2d.shape[0] if rows is None else rows) // tr
    if x_tail is None:
        return nt, [pl.BlockSpec((tr, d), index(lambda i: (i, 0)))], [x2d]
    assert x_tail.shape == (tr, d)
    return nt + 1, [pl.BlockSpec((tr, d), index(lambda i: (jnp.minimum(i, nt - 1), 0))),
                    pl.BlockSpec((tr, d), index(lambda i: (0, 0)))], [x2d, x_tail]


def _moe_route(x2d, x_tail, gain, mods, tiles_per_mod, w_r, b_r, tr, mix=None):
    d = x2d.shape[1]
    rows = None if mix is None else mix[0].shape[0] * mix[0].shape[1]
    nt, x_specs, x_args = _token_specs(x2d, x_tail, tr, lambda f: f, rows)
    t = nt * tr
    sr = _slot_rows(tr)
    mix_specs, mix_args, mix_out_specs, mix_out_shapes = [], [], [], []
    if mix is not None:
        c3d, d3d, w_out = mix
        tpb = c3d.shape[1] // tr
        half = lambda a: pl.BlockSpec((1, tr, a.shape[2]), lambda i: (i // tpb, i % tpb, 0))
        mix_specs = [half(c3d), half(d3d), pl.BlockSpec(w_out.shape, lambda i: (0, 0))]
        mix_args = [c3d, d3d, w_out]
        mix_out_specs = [pl.BlockSpec((tr, d), lambda i: (i, 0))]
        mix_out_shapes = [jax.ShapeDtypeStruct((t, d), F32)]
    return pl.pallas_call(
        functools.partial(_route_kernel, slot_rows=sr, has_tail=x_tail is not None, has_mix=mix is not None),
        grid=(nt,),
        in_specs=x_specs + [
            pl.BlockSpec((1, d), lambda i: (0, 0)),
            pl.BlockSpec((1, 6, d), lambda i: (jnp.minimum(i // tiles_per_mod, mods.shape[0] - 1), 0, 0)),
            pl.BlockSpec(w_r.shape, lambda i: (0, 0)),
            pl.BlockSpec((1, LANES), lambda i: (0, 0)),
        ] + mix_specs,
        out_specs=[
            pl.BlockSpec((sr // SLOT_BLOCK, SLOT_BLOCK, d), lambda i: (i, 0, 0)),
            pl.BlockSpec((tr, LANES), lambda i: (i, 0)),
            pl.BlockSpec((1, 8, LANES), lambda i: (i, 0, 0)),
        ] + mix_out_specs,
        out_shape=[
            jax.ShapeDtypeStruct((nt * sr // SLOT_BLOCK, SLOT_BLOCK, d), BF16),
            jax.ShapeDtypeStruct((t, LANES), F32),
            jax.ShapeDtypeStruct((nt, 8, LANES), F32),
        ] + mix_out_shapes,
        compiler_params=_cparams("parallel"),
        name="moe_route",
    )(*x_args, gain.reshape(1, d), mods, w_r, b_r, *mix_args)


def _tables_kernel(cnt_ref, src_ref, inv_ref, exp_ref, valid_ref, *, ntiles, bpt):
    cnt = cnt_ref[...]
    nb = jnp.floor((cnt + (SLOT_BLOCK - 1)) * (1.0 / SLOT_BLOCK))
    i_r = lax.broadcasted_iota(jnp.int32, (LANES, LANES), 0)
    i_c = lax.broadcasted_iota(jnp.int32, (LANES, LANES), 1)
    before = jnp.where(i_r < i_c, 1.0, 0.0)
    upto = jnp.where(i_c <= i_r, 1.0, 0.0)
    first = _dot(nb, before)
    cum = _dot(upto, nb)
    tot = jnp.max(cum, axis=0, keepdims=True)
    steps = jnp.floor((tot + (STEP_BLOCKS - 1)) * (1.0 / STEP_BLOCKS))
    start = STEP_BLOCKS * _dot(jnp.broadcast_to(steps, (LANES, LANES)), before)[0:1]
    pos = start + cum - nb

    sub = 8 * (-(-(MOE_GROUPS + MOE_EXPERTS) // 8))
    first_t, nb_t, pos_t = first.T[:sub], nb.T[:sub], pos.T[:sub]
    as_col = lambda v: jnp.broadcast_to(v, (LANES, LANES)).T[:sub, 0:1]
    start_c, tot_c, span_c = as_col(start), as_col(tot), as_col(STEP_BLOCKS * steps)

    inv_ref[...] = jnp.zeros(inv_ref.shape, jnp.int32)
    local = lax.broadcasted_iota(jnp.int32, (sub, LANES), 1).astype(F32)
    nsrc = src_ref.shape[1]
    j = lax.broadcasted_iota(jnp.int32, (sub, nsrc), 1).astype(F32)
    acc = jnp.zeros((1, nsrc), F32)
    for i in range(ntiles):
        f_i, n_i, p_i = first_t[:, i:i + 1], nb_t[:, i:i + 1], pos_t[:, i:i + 1]
        own = (local >= f_i) & (local < f_i + n_i)
        inv_ref[i:i + 1, :] = jnp.sum(jnp.where(own, p_i + (local - f_i), 0.0), axis=0, keepdims=True).astype(jnp.int32)
        own = (j >= p_i) & (j < p_i + n_i)
        acc = acc + jnp.sum(jnp.where(own, (i * bpt) + f_i + (j - p_i), 0.0), axis=0, keepdims=True)
    src_ref[...] = acc.astype(jnp.int32)

    nst = exp_ref.shape[1]
    at = STEP_BLOCKS * lax.broadcasted_iota(jnp.int32, (sub, nst), 1).astype(F32)
    expert = (lax.broadcasted_iota(jnp.int32, (sub, nst), 0) - MOE_GROUPS).astype(F32)
    inside = (at >= start_c) & (at < start_c + span_c)
    exp_ref[...] = jnp.sum(jnp.where(inside, expert, 0.0), axis=0, keepdims=True).astype(jnp.int32)
    occupied = jnp.where(inside, jnp.where(at - start_c < tot_c, 1.0, 0.0), 0.0)
    valid_ref[...] = jnp.sum(occupied, axis=0, keepdims=True).astype(jnp.int32)


def _expert_tables(cnt, bpt, nsteps):
    ntiles = cnt.shape[0]
    assert bpt <= LANES and ntiles <= LANES
    cnt = jnp.pad(cnt, ((0, LANES - ntiles), (0, 0)))
    nsrc = -(-nsteps * STEP_BLOCKS // LANES) * LANES
    nst = -(-nsteps // LANES) * LANES
    i32 = lambda *s: jax.ShapeDtypeStruct(s, jnp.int32)
    src, inv, step_e, valid = pl.pallas_call(
        functools.partial(_tables_kernel, ntiles=ntiles, bpt=bpt),
        out_shape=[i32(1, nsrc), i32(LANES, LANES), i32(1, nst), i32(1, nst)],
        compiler_params=pltpu.CompilerParams(vmem_limit_bytes=VMEM_LIMIT),
        name="moe_tables",
    )(cnt)
    return src.reshape(-1), inv.reshape(-1), step_e.reshape(-1), valid.reshape(-1)


def _block_gather(table_ref, first, nblocks, src_hbm, buf_ref, slot, sem_ref):
    return [pltpu.make_async_copy(src_hbm.at[table_ref[first + kk]],
                                  buf_ref.at[slot, pl.ds(kk * SLOT_BLOCK, SLOT_BLOCK)], sem_ref.at[slot])
            for kk in range(nblocks)]


def _experts_kernel(src_ref, exp_ref, valid_ref, xs_hbm, wg_ref, wu_ref, wd_ref, y_ref, xbuf_ref, sem_ref, wgb_ref,
                    wub_ref, wdb_ref):
    s = pl.program_id(0)
    slot = s % 2

    def gather(step, to_slot):
        return _block_gather(src_ref, step * STEP_BLOCKS, STEP_BLOCKS, xs_hbm, xbuf_ref, to_slot, sem_ref)

    @pl.when((s == 0) & (valid_ref[0] > 0))
    def _():
        for kk, cp in enumerate(gather(0, 0)):
            cp.start(priority=kk % 2)

    nxt = jnp.minimum(s + 1, pl.num_programs(0) - 1)

    @pl.when((s + 1 < pl.num_programs(0)) & (valid_ref[nxt] > 0))
    def _():
        for kk, cp in enumerate(gather(s + 1, 1 - slot)):
            cp.start(priority=kk % 2)

    @pl.when((s == 0) | (exp_ref[s] != exp_ref[jnp.maximum(s - 1, 0)]))
    def _():
        wgb_ref[...] = wg_ref[0, 0, 0].astype(BF16)
        wub_ref[...] = wu_ref[0, 0, 0].astype(BF16)
        wdb_ref[...] = wd_ref[0, 0, 0].astype(BF16)

    @pl.when(valid_ref[s] > 0)
    def _():
        for cp in gather(s, slot):
            cp.wait()
        x = xbuf_ref[slot]
        a = _silu(_dot(x, wgb_ref[...])) * _dot(x, wub_ref[...])
        y_ref[...] = _dot(a.astype(BF16), wdb_ref[...]).astype(BF16).reshape(y_ref.shape)

    @pl.when(valid_ref[s] == 0)
    def _():
        y_ref[...] = jnp.zeros(y_ref.shape, y_ref.dtype)


def _moe_experts(xs, src, step_e, valid, w_gate, w_up, w_down, layer, nsteps):
    xs3 = xs
    d = xs3.shape[-1]
    f = w_gate.shape[-1]
    step_rows = STEP_BLOCKS * SLOT_BLOCK
    w_blk = lambda shape: pl.BlockSpec((1, 1, 1) + shape,
                                       lambda s, sr, ex, va: (layer, ex[s] // MOE_EPG, ex[s] % MOE_EPG, 0, 0))
    grid_spec = pltpu.PrefetchScalarGridSpec(
        num_scalar_prefetch=3,
        grid=(nsteps,),
        in_specs=[pl.BlockSpec(memory_space=pl.ANY), w_blk((d, f)), w_blk((d, f)), w_blk((f, d))],
        out_specs=pl.BlockSpec((STEP_BLOCKS, SLOT_BLOCK, d), lambda s, sr, ex, va: (s, 0, 0)),
        scratch_shapes=[pltpu.VMEM((2, step_rows, d), BF16), pltpu.SemaphoreType.DMA((2,)),
                        pltpu.VMEM((d, f), BF16), pltpu.VMEM((d, f), BF16), pltpu.VMEM((f, d), BF16)],
    )
    return pl.pallas_call(
        _experts_kernel,
        grid_spec=grid_spec,
        out_shape=jax.ShapeDtypeStruct((nsteps * STEP_BLOCKS, SLOT_BLOCK, d), BF16),
        compiler_params=_cparams("arbitrary"),
        name="moe_experts",
    )(src, step_e, valid, xs3, w_gate, w_up, w_down)


def _combine_kernel(inv_ref, x_ref, xt_ref, info_ref, m_ref, fg_ref, ys_hbm, o_ref, ybuf_ref, sem_ref, *, bpt, final):
    i = pl.program_id(0)
    slot = i % 2
    tr = x_ref.shape[0]

    def gather(tile, to_slot):
        return _block_gather(inv_ref, tile * LANES, bpt, ys_hbm, ybuf_ref, to_slot, sem_ref)

    @pl.when(i == 0)
    def _():
        for kk, cp in enumerate(gather(0, 0)):
            cp.start(priority=kk % 2)

    @pl.when(i + 1 < pl.num_programs(0))
    def _():
        for kk, cp in enumerate(gather(i + 1, 1 - slot)):
            cp.start(priority=kk % 2)

    info = info_ref[...]
    col = lax.broadcasted_iota(jnp.int32, (tr, bpt * SLOT_BLOCK), 1)
    wsel = jnp.where(col == info[:, 0:1].astype(jnp.int32), info[:, 2:3],
                     jnp.where(col == info[:, 1:2].astype(jnp.int32), info[:, 3:4], 0.0))
    for cp in gather(i, slot):
        cp.wait()
    y = _dot(wsel.astype(BF16), ybuf_ref[slot])
    out = _tile_tokens(x_ref, xt_ref) + m_ref[0, 5:6, :] * y
    if final:
        out = _rmsnorm(out, fg_ref[...])
    o_ref[...] = out


def _moe_combine(x2d, x_tail, ys, inv, info, mods, tiles_per_mod, final_g, tr, bpt, final):
    d = x2d.shape[1]
    nt, x_specs, x_args = _token_specs(x2d, x_tail, tr, lambda f: (lambda i, iv: f(i)))
    t = nt * tr
    ys3 = ys
    kern = _combine_kernel
    if x_tail is None:
        kern = lambda inv_ref, x_ref, *refs, **kw: _combine_kernel(inv_ref, x_ref, None, *refs, **kw)
    grid_spec = pltpu.PrefetchScalarGridSpec(
        num_scalar_prefetch=1,
        grid=(nt,),
        in_specs=x_specs + [
            pl.BlockSpec((tr, LANES), lambda i, iv: (i, 0)),
            pl.BlockSpec((1, 6, d), lambda i, iv: (jnp.minimum(i // tiles_per_mod, mods.shape[0] - 1), 0, 0)),
            pl.BlockSpec((1, d), lambda i, iv: (0, 0)),
            pl.BlockSpec(memory_space=pl.ANY),
        ],
        out_specs=pl.BlockSpec((tr, d), lambda i, iv: (i, 0)),
        scratch_shapes=[pltpu.VMEM((2, bpt * SLOT_BLOCK, d), BF16), pltpu.SemaphoreType.DMA((2,))],
    )
    return pl.pallas_call(
        functools.partial(kern, bpt=bpt, final=final),
        grid_spec=grid_spec,
        out_shape=jax.ShapeDtypeStruct((t, d), F32),
        compiler_params=_cparams("arbitrary"),
        name="moe_combine",
    )(inv, *x_args, info, mods, final_g.reshape(1, d), ys3)


def _moe(x2d, x_tail, gain, mods, tiles_per_mod, params, final_g, final, tr, mix=None):
    w_r, b_r, w_gate, w_up, w_down, layer = params
    if mix is None:
        xs, info, cnt = _moe_route(x2d, x_tail, gain, mods, tiles_per_mod, w_r, b_r, tr)
    else:
        xs, info, cnt, x2d = _moe_route(x2d, x_tail, gain, mods, tiles_per_mod, w_r, b_r, tr, mix)
    nt = x2d.shape[0] // tr + (x_tail is not None)
    bpt = _slot_rows(tr) // SLOT_BLOCK
    nsteps = -(-(nt * bpt + MOE_EXPERTS * (STEP_BLOCKS - 1)) // STEP_BLOCKS)
    src, inv, step_e, valid = _expert_tables(cnt[:, 0, :], bpt, nsteps)
    ys = _moe_experts(xs, src, step_e, valid, w_gate, w_up, w_down, layer, nsteps)
    return _moe_combine(x2d, x_tail, ys, inv, info, mods, tiles_per_mod, final_g, tr, bpt, final)


def _cd_in_kernel(x_ref, g_ref, m_ref, w_ref, cos_ref, sin_ref, qg_ref, kg_ref, wq_ref, wqp_ref, wk_ref, wv_ref,
                  vone_ref, ua_ref, *o_refs, need_q, q_scale):
    h = _rmsnorm(x_ref[0], g_ref[...]) * (1.0 + m_ref[0, 1:2, :]) + m_ref[0, 0:1, :]
    hb = h.astype(BF16)
    na_w = ua_ref.shape[2]
    ua_ref[0] = _dot(hb, w_ref[:, :na_w]).astype(ua_ref.dtype)
    ub = _dot(hb, w_ref[:, na_w:])
    o = MLA_Q_RANK + MLA_KV_RANK
    cq_raw, ckv_raw, kr, krp = ub[:, :MLA_Q_RANK], ub[:, MLA_Q_RANK:o], ub[:, o:o + MLA_PAD], ub[:, o + MLA_PAD:]
    cos = cos_ref[...]
    sin = sin_ref[...]
    ckv = _rmsnorm(ckv_raw, kg_ref[...]).astype(BF16)
    k_rope = kr * cos + krp * sin
    kn = _dot(ckv, wk_ref[...])
    if need_q:
        q_ref, k_ref, v_ref = o_refs
    else:
        k_ref, v_ref = o_refs
    vx = _dot(ckv, wv_ref[...]) + vone_ref[...]
    for h in range(D_HEADS):
        hs = slice(h * MLA_PAD, (h + 1) * MLA_PAD)
        k_ref[0, h] = (kn[:, hs] + k_rope).astype(BF16)
        v_ref[0, h] = vx[:, hs].T[:MLA_VROWS].astype(BF16)
    if need_q:
        cq = _rmsnorm(cq_raw, qg_ref[...]).astype(BF16)
        qm = _dot(cq, wq_ref[...])
        qp = _dot(cq, wqp_ref[...])
        for h in range(D_HEADS):
            hs = slice(h * MLA_PAD, (h + 1) * MLA_PAD)
            q_ref[0, h] = ((qm[:, hs] * cos + qp[:, hs] * sin) * q_scale).T.astype(BF16)


def _cd_in_proj(x2d, flat, gain, mods, w_cat, na_w, cos, sin, q_g, kv_g, wq, wqp, wk, wv, need_q, tm):
    b, n, row0 = flat
    d = x2d.shape[-1]
    tm = min(tm, n)
    row_major = (jax.ShapeDtypeStruct((b, D_HEADS, n, MLA_PAD), BF16),
                 pl.BlockSpec((1, D_HEADS, tm, MLA_PAD), lambda bi, i: (bi, 0, i, 0)))
    col_major = (jax.ShapeDtypeStruct((b, D_HEADS, MLA_PAD, n), BF16),
                 pl.BlockSpec((1, D_HEADS, MLA_PAD, tm), lambda bi, i: (bi, 0, 0, i)))
    v_major = (jax.ShapeDtypeStruct((b, D_HEADS, MLA_VROWS, n), BF16),
               pl.BlockSpec((1, D_HEADS, MLA_VROWS, tm), lambda bi, i: (bi, 0, 0, i)))
    ua = (jax.ShapeDtypeStruct((b, n, na_w), BF16), pl.BlockSpec((1, tm, na_w), lambda bi, i: (bi, i, 0)))
    outs, specs = zip(*([ua] + ([col_major] if need_q else []) + [row_major, v_major]))
    full = lambda a: pl.BlockSpec(a.shape, lambda bi, i: (0,) * a.ndim)
    vone = jnp.tile(jnp.concatenate([jnp.zeros((1, MLA_V), F32), jnp.ones((1, MLA_PAD - MLA_V), F32)], axis=1),
                    (1, D_HEADS))
    q_scale = float((MLA_NOPE + MLA_ROPE) ** -0.5 * np.log2(np.e))
    return pl.pallas_call(
        functools.partial(_cd_in_kernel, need_q=need_q, q_scale=q_scale),
        grid=(b, n // tm),
        in_specs=[
            pl.BlockSpec((1, tm, d), lambda bi, i: (0, row0 // tm + bi * (n // tm) + i, 0)),
            pl.BlockSpec((1, d), lambda bi, i: (0, 0)),
            pl.BlockSpec((1, 6, d), lambda bi, i: (bi, 0, 0)),
            full(w_cat),
            pl.BlockSpec((tm, MLA_PAD), lambda bi, i: (i, 0)),
            pl.BlockSpec((tm, MLA_PAD), lambda bi, i: (i, 0)),
            full(q_g), full(kv_g), full(wq), full(wqp), full(wk), full(wv), full(vone),
        ],
        out_specs=list(specs),
        out_shape=list(outs),
        compiler_params=_cparams("parallel", "parallel"),
        name="cd_in_proj",
    )(x2d.reshape(1, -1, d), gain.reshape(1, d), mods, w_cat, cos, sin, q_g, kv_g, wq, wqp, wk, wv, vone)


def _mla_attn_kernel(q_ref, qn_ref, kc_ref, kl_ref, vc_ref, vl_ref, o_ref, acc0_ref, acc1_ref, s0_ref, s1_ref, m_ref,
                     *, tk):
    tq = q_ref.shape[3]
    ncc = kc_ref.shape[2] // tk
    nchunks = ncc + kl_ref.shape[2] // tk
    neg = jnp.full((8, tq), NEG, F32)
    s_refs = (s0_ref, s1_ref)
    acc_refs = (acc0_ref, acc1_ref)

    def chunk(c):
        part = (kc_ref, vc_ref, c) if c < ncc else (kl_ref, vl_ref, c - ncc)
        return part[0], part[1], pl.ds(part[2] * tk, tk), pl.ds(c * tk, tk)

    def scores(k_ref, ks, ss, hh, q, m):
        s = _dot(k_ref[0, hh, ks, :], q)
        s_refs[hh][ss, :] = s
        return jnp.maximum(m, jnp.max(s.reshape(tk // 8, 8, tq), axis=0))

    def weight(v_ref, ks, ss, hh, m_row):
        p = jnp.exp2((s_refs[hh][ss, :] - m_row).astype(BF16))
        acc_refs[hh][...] += _dot(v_ref[0, hh, :, ks], p)

    @pl.when(pl.program_id(2) == 0)
    def _():
        m = neg
        for c in range(ncc):
            k_ref, _, ks, ss = chunk(c)
            m = scores(k_ref, ks, ss, 0, q_ref[0, 0], m)

        def latent(c, m):
            ks = pl.ds(pl.multiple_of(c * tk, tk), tk)
            ss = pl.ds(pl.multiple_of((c + ncc) * tk, tk), tk)
            return scores(kl_ref, ks, ss, 0, q_ref[0, 0], m)

        m_ref[...] = lax.fori_loop(0, nchunks - ncc, latent, m)

    acc0_ref[...] = jnp.zeros(acc0_ref.shape, F32)
    acc1_ref[...] = jnp.zeros(acc1_ref.shape, F32)
    m0 = jnp.max(m_ref[...], axis=0, keepdims=True)
    m1 = neg
    for c in range(nchunks):
        k_ref, v_ref, ks, ss = chunk(c)
        weight(v_ref, ks, ss, 0, m0)
        m1 = scores(k_ref, ks, ss, 1, q_ref[0, 1], m1)
    m1 = jnp.max(m1, axis=0, keepdims=True)
    m0_next = neg
    for c in range(nchunks):
        k_ref, v_ref, ks, ss = chunk(c)
        weight(v_ref, ks, ss, 1, m1)
        m0_next = scores(k_ref, ks, ss, 0, qn_ref[0, 0], m0_next)
    m_ref[...] = m0_next
    o_t = jnp.concatenate([a[:MLA_V] / a[MLA_V:MLA_V + 1] for a in acc_refs], axis=0)
    o_ref[0] = o_t.T.astype(o_ref.dtype)


def _mla_attention(q_t, k_ctx, k_lat, v_ctx, v_lat, tq, tk):
    b, h, _, n = q_t.shape
    nc = k_ctx.shape[2]
    tq = min(tq, n)
    assert nc % tk == 0 and n % tk == 0
    k_spec = lambda rows: pl.BlockSpec((1, 2, rows, MLA_PAD), lambda bi, hp, i: (bi, hp, 0, 0))
    v_spec = lambda rows: pl.BlockSpec((1, 2, MLA_VROWS, rows), lambda bi, hp, i: (bi, hp, 0, 0))
    return pl.pallas_call(
        functools.partial(_mla_attn_kernel, tk=tk),
        grid=(b, h // 2, n // tq),
        in_specs=[
            pl.BlockSpec((1, 2, MLA_PAD, tq), lambda bi, hp, i: (bi, hp, 0, i)),
            pl.BlockSpec((1, 2, MLA_PAD, tq), lambda bi, hp, i: (bi, hp, 0, jnp.minimum(i + 1, n // tq - 1))),
            k_spec(nc), k_spec(n), v_spec(nc), v_spec(n),
        ],
        out_specs=pl.BlockSpec((1, tq, 2 * MLA_V), lambda bi, hp, i: (bi, i, hp)),
        out_shape=jax.ShapeDtypeStruct((b, n, h * MLA_V), BF16),
        scratch_shapes=[pltpu.VMEM((MLA_VROWS, tq), F32), pltpu.VMEM((MLA_VROWS, tq), F32),
                        pltpu.VMEM((nc + n, tq), F32), pltpu.VMEM((nc + n, tq), F32), pltpu.VMEM((8, tq), F32)],
        compiler_params=_cparams("parallel", "parallel", "arbitrary"),
        name="mla_attention",
    )(q_t, q_t, k_ctx, k_lat, v_ctx, v_lat)


def _na_kernel(q_ref, *refs):
    k_refs, v_refs = refs[:NA_KBLOCKS + 1], refs[NA_KBLOCKS + 1:2 * NA_KBLOCKS + 2]
    tab_ref, o_ref = refs[2 * NA_KBLOCKS + 2:]
    tq = q_ref.shape[1]
    nloc = tab_ref.shape[3]
    lane = lax.broadcasted_iota(jnp.int32, (tq, LANES), 1)
    q = q_ref[0]
    k_all = jnp.concatenate([r[0] for r in k_refs], axis=0)
    v_all = jnp.concatenate([r[0] for r in v_refs], axis=0)
    outs = []
    for hh in range(2):
        in_head = (lane >= hh * C_HEAD_DIM) & (lane < (hh + 1) * C_HEAD_DIM)
        qh = jnp.where(in_head, q, jnp.zeros_like(q))
        s = lax.dot_general(qh, k_all, NT, preferred_element_type=F32)
        s_loc = s[:, :nloc] + tab_ref[0, hh]
        s_ctx = s[:, nloc:]
        m = jnp.maximum(jnp.max(s_loc, axis=-1, keepdims=True), jnp.max(s_ctx, axis=-1, keepdims=True))
        p_loc = jnp.exp(s_loc - m)
        p_ctx = jnp.exp(s_ctx - m)
        l = jnp.sum(p_loc, axis=-1, keepdims=True) + jnp.sum(p_ctx, axis=-1, keepdims=True)
        o = _dot(p_loc.astype(BF16), v_all[:nloc]) + _dot(p_ctx.astype(BF16), v_all[nloc:])
        outs.append(o / l)
    o_ref[0] = jnp.where(lane < C_HEAD_DIM, outs[0], outs[1]).astype(o_ref.dtype)


def _na_tables(rpb, rows):
    h = rpb.shape[0]
    w = GRID_W
    kr_n = NA_QROWS + NA_ROWS
    qc = np.arange(w)
    kc = np.arange(w)
    cs = np.clip(qc - NA_COLS // 2, 0, w - NA_COLS)
    col_ok = (kc[None, :] >= cs[:, None]) & (kc[None, :] < cs[:, None] + NA_COLS)
    dc = np.clip(kc[None, :] - qc[:, None] + (NA_COLS - 1), 0, 2 * NA_COLS - 2)
    pick_dc = (dc.reshape(-1)[None, :] == np.arange(2 * NA_COLS - 1)[:, None]).astype(np.float32)
    base = jnp.einsum('hrd,dx->hrx', rpb.astype(F32), jnp.asarray(pick_dc), precision=lax.Precision.HIGHEST)
    nblk = rows // NA_QROWS
    picks, oks = [], []
    for m in (0, 1, nblk - 1):
        qr = NA_QROWS * m + np.arange(NA_QROWS)
        rs = np.clip(qr - NA_ROWS // 2, 0, rows - NA_ROWS)
        kr = NA_QROWS * m - NA_ROWS // 2 + np.arange(kr_n)
        row_ok = (kr[None, :] >= rs[:, None]) & (kr[None, :] < rs[:, None] + NA_ROWS)
        dr = np.clip(kr[None, :] - qr[:, None] + (NA_ROWS - 1), 0, 2 * NA_ROWS - 2)
        picks.append((dr.reshape(-1)[:, None] == np.arange(2 * NA_ROWS - 1)[None, :]).astype(np.float32))
        oks.append(row_ok[:, None, :, None] & col_ok[None, :, None, :])
    t = jnp.einsum('vr,hrx->hvx', jnp.asarray(np.concatenate(picks)), base, precision=lax.Precision.HIGHEST)
    t = t.reshape(h, 3, NA_QROWS, kr_n, w, w).transpose(1, 0, 2, 4, 3, 5)
    t = jnp.where(jnp.asarray(np.stack(oks))[:, None], t, NEG)
    return t.reshape(3, h, NA_QROWS * w, kr_n * w)


def _na_attention(u_lat, u_ctx, tabs):
    b, n, _ = u_lat.shape
    nc = u_ctx.shape[1]
    tq = NA_QROWS * GRID_W
    tkb = NA_KBLOCK_ROWS * GRID_W
    nblk = n // tq
    nkb = n // tkb
    per_q = NA_QROWS // NA_KBLOCK_ROWS
    assert NA_KBLOCKS == per_q + 2 and NA_ROWS // 2 == NA_KBLOCK_ROWS and n % tq == 0 and n // tq >= 2
    pairs = C_HEADS // 2
    q_spec = pl.BlockSpec((1, tq, LANES), lambda bi, hp, i: (bi, i, hp))
    kblk = lambda col0, j: pl.BlockSpec(
        (1, tkb, LANES), lambda bi, hp, i: (bi, jnp.clip(i * per_q - 1 + j, 0, nkb - 1), col0 + hp))
    ctx = lambda col0: pl.BlockSpec((1, nc, LANES), lambda bi, hp, i: (bi, 0, col0 + hp))
    sel = lambda i: jnp.where(i == 0, 0, jnp.where(i == nblk - 1, 2, 1))
    kv_specs = [kblk(col0, j) for col0 in (pairs, 2 * pairs) for j in range(NA_KBLOCKS)]
    kv_specs = kv_specs[:NA_KBLOCKS] + [ctx(pairs)] + kv_specs[NA_KBLOCKS:] + [ctx(2 * pairs)]
    kv_args = [u_lat] * NA_KBLOCKS + [u_ctx]
    return pl.pallas_call(
        _na_kernel,
        grid=(b, pairs, nblk),
        in_specs=[q_spec] + kv_specs + [
            pl.BlockSpec((1, 2, tq, NA_KBLOCKS * tkb), lambda bi, hp, i: (sel(i), hp, 0, 0)),
        ],
        out_specs=pl.BlockSpec((1, tq, LANES), lambda bi, hp, i: (bi, i, hp)),
        out_shape=jax.ShapeDtypeStruct((b, n, C_WIDTH), BF16),
        compiler_params=_cparams("parallel", "parallel", "arbitrary"),
        name="na_attention",
    )(u_lat, *kv_args, *kv_args, tabs)


def _moe_params(w_rg, b_rg, w_re, b_re, w_gate, w_up, w_down, layer):
    d = w_rg.shape[0]
    w_r = jnp.zeros((d, LANES), F32).at[:, :MOE_GROUPS].set(w_rg).at[:, MOE_GROUPS:MOE_GROUPS + MOE_EXPERTS].set(w_re)
    b_r = jnp.zeros((1, LANES), F32).at[0, :MOE_GROUPS].set(b_rg).at[0, MOE_GROUPS:MOE_GROUPS + MOE_EXPERTS].set(b_re)
    w_hi = w_r.astype(BF16)
    w_lo = (w_r - w_hi.astype(F32)).astype(BF16)
    return jnp.concatenate([w_hi, w_lo], axis=1), b_r, w_gate, w_up, w_down, layer


def _rope_perm():
    j = np.arange(MLA_ROPE)
    half = MLA_ROPE // 2
    return (j // half) * half + (j % half + half // 2) % half


def _rope_tables(n):
    half = MLA_ROPE // 2
    nf = half // 2
    t = np.arange(n)
    inv = (np.float32(ROPE_THETA) ** (-np.arange(nf, dtype=np.float32) / np.float32(nf))).astype(np.float32)
    parts_c, parts_s = [], []
    for pos in ((t // GRID_W).astype(np.float32), (t % GRID_W).astype(np.float32)):
        ang = (pos[:, None] * inv[None, :]).astype(np.float32)
        c, s = np.cos(ang).astype(np.float32), np.sin(ang).astype(np.float32)
        parts_c += [c, c]
        parts_s += [-s, s]
    pad = MLA_PAD - MLA_NOPE - MLA_ROPE
    cos = np.concatenate([np.ones((n, MLA_NOPE), np.float32)] + parts_c + [np.zeros((n, pad), np.float32)], axis=1)
    sin = np.concatenate([np.zeros((n, MLA_NOPE), np.float32)] + parts_s + [np.zeros((n, pad), np.float32)], axis=1)
    return jnp.asarray(cos), jnp.asarray(sin)


def _identity_rope_tables(n):
    pad = MLA_PAD - MLA_NOPE - MLA_ROPE
    cos = jnp.concatenate([jnp.ones((n, MLA_NOPE + MLA_ROPE), F32), jnp.zeros((n, pad), F32)], axis=1)
    return cos, jnp.zeros((n, MLA_PAD), F32)


def _pad_heads(w, widths, src_cols, dst_off):
    rank = w.shape[0]
    out = jnp.zeros((rank, D_HEADS, MLA_PAD), F32)
    wh = w.reshape(rank, D_HEADS, widths)[:, :, src_cols]
    return out.at[:, :, dst_off:dst_off + len(src_cols)].set(wh).reshape(rank, D_HEADS * MLA_PAD)


def _cd_params(w_in, w_uq, w_ukv):
    d = w_in.shape[0]
    perm = _rope_perm()
    o = 3 * C_WIDTH
    q_scale = float(C_HEAD_DIM ** -0.5)
    kr = w_in[:, o + MLA_Q_RANK + MLA_KV_RANK:]
    pad_rope = lambda a: jnp.zeros((d, MLA_PAD), F32).at[:, MLA_NOPE:MLA_NOPE + MLA_ROPE].set(a)
    w_cat = jnp.concatenate([
        w_in[:, :C_WIDTH] * q_scale, w_in[:, C_WIDTH:o],
        w_in[:, o:o + MLA_Q_RANK + MLA_KV_RANK], pad_rope(kr), pad_rope(kr[:, perm]),
    ], axis=1).astype(BF16)
    qw = MLA_NOPE + MLA_ROPE
    nope = np.arange(MLA_NOPE)
    rope = MLA_NOPE + np.arange(MLA_ROPE)
    wq = (_pad_heads(w_uq, qw, nope, 0) + _pad_heads(w_uq, qw, rope, MLA_NOPE)).astype(BF16)
    wqp = _pad_heads(w_uq, qw, rope[perm], MLA_NOPE).astype(BF16)
    kvw = MLA_NOPE + MLA_V
    wk = _pad_heads(w_ukv, kvw, nope, 0).astype(BF16)
    wv = _pad_heads(w_ukv, kvw, MLA_NOPE + np.arange(MLA_V), 0).astype(BF16)
    return w_cat, wq, wqp, wk, wv


def kernel(x, c, ctx, c_ctx, ada_w, ada_b, norm1_g, norm2_g, ab_w_in, ab_w_out, hgrn_lb_logits, hgrn_onorm_g, pool_w,
           pool_scale, cd_w_in, cd_w_out, na_rpb, mla_q_norm_g, mla_w_uq, mla_kv_norm_g, mla_w_ukv, moe_w_rg, moe_b_rg,
           moe_w_re, moe_b_re, moe_w_gate, moe_w_up, moe_w_down, final_norm_g):
    b, n, d = x.shape
    n_ctx = ctx.shape[1]
    assert ada_w.shape[0] == 2 and ab_w_in.shape[0] == 1 and cd_w_in.shape[0] == 1 and b + 1 <= 8
    tm = TOKEN_TILE

    cc = jnp.zeros((8, d), F32).at[:b].set(c).at[b].set(c_ctx)
    mods = _ada(cc, b + 1, ada_w, ada_b).reshape(2, 8, 6, d)
    mods_lat = [mods[l, :b] for l in range(2)]
    mods_ctx = [jnp.broadcast_to(mods[l, b:b + 1], (b, 6, d)) for l in range(2)]
    lb = jnp.cumsum(jax.nn.softmax(hgrn_lb_logits.astype(F32), axis=1), axis=1)[:, 0]

    w_in0 = ab_w_in[0].astype(BF16)
    w_out0 = ab_w_out[0].astype(BF16)
    pw0 = pool_w[0].astype(BF16)
    ab_cols = w_in0.shape[1]
    (u_ctx,) = _in_proj(ctx, norm1_g[0], mods_ctx[0], w_in0, ((0, ab_cols),), (F32,), tm)
    (u_lat,) = _in_proj(x, norm1_g[0], mods_lat[0], w_in0, ((0, ab_cols),), (F32,), tm)
    s0 = jnp.zeros((b, 2, A_HEADS, A_HEAD_DIM, A_HEAD_DIM), F32)
    ocf, ocb, s_ctx = _hgrn_scan(u_ctx, lb, s0, HGRN_ROWS)
    olf, olb, _ = _hgrn_scan(u_lat, lb, s_ctx, HGRN_ROWS)
    t_lat, t_ctx = b * n, b * n_ctx
    assert n % tm == 0 and t_ctx == tm
    x_lat = _ab_out(olf, olb, u_lat, x, mods_lat[0], hgrn_onorm_g[0], pw0, pool_scale[0], w_out0, tm)
    x_ctx = _ab_out(ocf, ocb, u_ctx, ctx, mods_ctx[0], hgrn_onorm_g[0], pw0, pool_scale[0], w_out0, tm)
    moe0 = _moe_params(moe_w_rg[0], moe_b_rg[0], moe_w_re[0], moe_b_re[0], moe_w_gate, moe_w_up, moe_w_down, 0)
    mods_all = jnp.concatenate([mods_lat[0], mods[0, b:b + 1]], axis=0)
    xa = _moe(x_lat, x_ctx, norm2_g[0], mods_all, n // tm, moe0, final_norm_g, False, tm)

    w_cat, wq, wqp, wk, wv = _cd_params(cd_w_in[0], mla_w_uq[0], mla_w_ukv[0])
    na_w = 3 * C_WIDTH
    q_g = mla_q_norm_g[0].reshape(1, -1)
    kv_g = mla_kv_norm_g[0].reshape(1, -1)
    cos_l, sin_l = _rope_tables(n)
    cos_c, sin_c = _identity_rope_tables(n_ctx)
    ua_ctx, k_c, v_c = _cd_in_proj(xa, (b, n_ctx, t_lat), norm1_g[1], mods_ctx[1], w_cat, na_w, cos_c, sin_c, q_g, kv_g,
                                   wq, wqp, wk, wv, False, tm)
    ua_lat, q_l, k_l, v_l = _cd_in_proj(xa, (b, n, 0), norm1_g[1], mods_lat[1], w_cat, na_w, cos_l, sin_l, q_g, kv_g,
                                        wq, wqp, wk, wv, True, tm)
    d_lat = _mla_attention(q_l, k_c, k_l, v_c, v_l, MLA_Q_TILE, MLA_K_CHUNK)
    c_lat = _na_attention(ua_lat, ua_ctx, _na_tables(na_rpb[0], n // GRID_W))
    moe1 = _moe_params(moe_w_rg[1], moe_b_rg[1], moe_w_re[1], moe_b_re[1], moe_w_gate, moe_w_up, moe_w_down, 1)
    mix = (c_lat, d_lat, cd_w_out[0].astype(BF16))
    out = _moe(xa, None, norm2_g[1], mods_lat[1], n // tm, moe1, final_norm_g, True, tm, mix)
    return out.reshape(b, n, d)
```
